```python
import jax, jax.numpy as jnp
from jax import lax
import numpy as np

D_MODEL = 1024
BATCH = 8
SEQ = 8192
DEPTH = 4

A_W = D_MODEL // 2
CONV_A = 31
B_W = D_MODEL // 2
CONV_B = 3
C_HEADS = 8
C_HEAD_DIM = 64
C_W = C_HEADS * C_HEAD_DIM
Q_BLOCK = 128
M_HEADS = 4
M_HEAD_DIM = 64
M_W = M_HEADS * M_HEAD_DIM
MEM_LEN = 256
N_BRANCH = 4
ALPHA = (2.0 * DEPTH) ** 0.25
BETA = (8.0 * DEPTH) ** -0.25
LN_EPS = 1e-5

SPLITS = (
    2 * A_W, A_W,
    B_W, B_W, B_W, B_W,
    C_W, C_W, C_W, C_HEADS, C_W,
    M_W, M_W,
    N_BRANCH * D_MODEL,
)
IN_COLS = sum(SPLITS)
SPLIT_CUTS = tuple(int(c) for c in np.cumsum(SPLITS)[:-1])

kernel_name = 'hybrid_gated_conformer_shortconv_fox_memxattn'


def _layer_norm(x, g, b):
    xf = x.astype(jnp.float32)
    mu = jnp.mean(xf, axis=-1, keepdims=True)
    var = jnp.mean(jnp.square(xf - mu), axis=-1, keepdims=True)
    y = ((xf - mu) * lax.rsqrt(var + LN_EPS)).astype(x.dtype)
    return y * g + b


def _causal_dwconv(x, w):
    k_width, ch = w.shape
    return lax.conv_general_dilated(
        x, w[:, None, :].astype(x.dtype), window_strides=(1,),
        padding=[(k_width - 1, 0)],
        dimension_numbers=('NWC', 'WIO', 'NWC'),
        feature_group_count=ch)


def _forgetting_attention(q, k, v, logf):
    bsz, seq, heads, dh = q.shape
    scale = dh ** -0.5
    cum = jnp.cumsum(logf, axis=1).transpose(0, 2, 1)
    outs = []
    for i in range(seq // Q_BLOCK):
        q0, end = i * Q_BLOCK, (i + 1) * Q_BLOCK
        s = jnp.einsum('bqhd,bkhd->bhqk', q[:, q0:end], k[:, :end]).astype(jnp.float32) * scale
        decay = cum[:, :, q0:end, None] - cum[:, :, None, :end]
        q_pos = q0 + jnp.arange(Q_BLOCK)
        k_pos = jnp.arange(end)
        causal = q_pos[:, None] >= k_pos[None, :]
        s = jnp.where(causal, s + decay, -jnp.inf)
        p = jax.nn.softmax(s, axis=-1).astype(v.dtype)
        outs.append(jnp.einsum('bhqk,bkhd->bqhd', p, v[:, :end]))
    return jnp.concatenate(outs, axis=1)


def _cross_attention(q, k, v):
    dh = q.shape[-1]
    s = jnp.einsum('bshd,bmhd->bhsm', q, k).astype(jnp.float32) * (dh ** -0.5)
    p = jax.nn.softmax(s, axis=-1).astype(v.dtype)
    return jnp.einsum('bhsm,bmhd->bshd', p, v)


def _fwd_setup_inputs(seed: int = 0) -> dict:
    key = jax.random.key(seed)
    ks = jax.random.split(key, 20)
    f32 = jnp.float32
    nrm = lambda k, shape, scale: jax.random.normal(k, shape, f32) * scale
    return {
        'x': nrm(ks[0], (BATCH, SEQ, D_MODEL), 1.0),
        'mem': nrm(ks[1], (BATCH, MEM_LEN, D_MODEL), 1.0),
        'w_in': nrm(ks[2], (DEPTH, D_MODEL, IN_COLS), D_MODEL ** -0.5),
        'b_forget': jax.random.uniform(ks[3], (DEPTH, C_HEADS), f32, 1.0, 5.0),
        'conv_a_w': nrm(ks[4], (DEPTH, CONV_A, A_W), CONV_A ** -0.5),
        'conv_a_b': nrm(ks[5], (DEPTH, A_W), 0.02),
        'ln_a_g': 1.0 + nrm(ks[6], (DEPTH, A_W), 0.05),
        'ln_a_b': nrm(ks[7], (DEPTH, A_W), 0.02),
        'conv_b_w': nrm(ks[8], (DEPTH, CONV_B, B_W), CONV_B ** -0.5),
        'w_kv_mem': nrm(ks[9], (DEPTH, D_MODEL, 2 * M_W), D_MODEL ** -0.5),
        'mem_ln_g': 1.0 + nrm(ks[10], (D_MODEL,), 0.05),
        'mem_ln_b': nrm(ks[11], (D_MODEL,), 0.02),
        'p_a': nrm(ks[12], (DEPTH, A_W, D_MODEL), BETA * A_W ** -0.5),
        'p_b': nrm(ks[13], (DEPTH, B_W, D_MODEL), BETA * B_W ** -0.5),
        'p_c': nrm(ks[14], (DEPTH, C_W, D_MODEL), BETA * C_W ** -0.5),
        'p_m': nrm(ks[15], (DEPTH, M_W, D_MODEL), BETA * M_W ** -0.5),
        'w_out': nrm(ks[16], (DEPTH, D_MODEL, D_MODEL), BETA * D_MODEL ** -0.5),
        'ln_g': 1.0 + nrm(ks[17], (DEPTH, D_MODEL), 0.05),
        'ln_b': nrm(ks[18], (DEPTH, D_MODEL), 0.02),
    }


def _fwd_reference(x, mem, w_in, b_forget, conv_a_w, conv_a_b, ln_a_g, ln_a_b, conv_b_w,
              w_kv_mem, mem_ln_g, mem_ln_b, p_a, p_b, p_c, p_m, w_out, ln_g, ln_b):
    bsz, seq, _ = x.shape
    mem_n = _layer_norm(mem, mem_ln_g, mem_ln_b)
    for l in range(DEPTH):
        proj = x @ w_in[l]
        (a_glu, a_gate, b_h, b_b, b_c, b_gate, c_q, c_k, c_v, c_f, c_gate,
         m_q, m_gate, g) = jnp.split(proj, SPLIT_CUTS, axis=-1)

        a_u, a_v = jnp.split(a_glu, 2, axis=-1)
        a = _causal_dwconv(a_u * jax.nn.sigmoid(a_v), conv_a_w[l]) + conv_a_b[l]
        a = jax.nn.silu(_layer_norm(a, ln_a_g[l], ln_a_b[l]))
        y_a = (a * jax.nn.silu(a_gate)) @ p_a[l]

        hb = b_b * _causal_dwconv(b_c * b_h, conv_b_w[l])
        y_b = (hb * jax.nn.silu(b_gate)) @ p_b[l]

        q = c_q.reshape(bsz, seq, C_HEADS, C_HEAD_DIM)
        k = c_k.reshape(bsz, seq, C_HEADS, C_HEAD_DIM)
        v = c_v.reshape(bsz, seq, C_HEADS, C_HEAD_DIM)
        logf = jax.nn.log_sigmoid((c_f + b_forget[l]).astype(jnp.float32))
        o_c = _forgetting_attention(q, k, v, logf).reshape(bsz, seq, C_W)
        y_c = (o_c * jax.nn.silu(c_gate)) @ p_c[l]

        mk, mv = jnp.split(mem_n @ w_kv_mem[l], 2, axis=-1)
        o_m = _cross_attention(
            m_q.reshape(bsz, seq, M_HEADS, M_HEAD_DIM),
            mk.reshape(bsz, MEM_LEN, M_HEADS, M_HEAD_DIM),
            mv.reshape(bsz, MEM_LEN, M_HEADS, M_HEAD_DIM)).reshape(bsz, seq, M_W)
        y_m = (o_m * jax.nn.silu(m_gate)) @ p_m[l]

        gates = jax.nn.sigmoid(g.reshape(bsz, seq, N_BRANCH, D_MODEL))
        merged = (gates[:, :, 0] * y_a + gates[:, :, 1] * y_b
                  + gates[:, :, 2] * y_c + gates[:, :, 3] * y_m)
        out = merged @ w_out[l]

        x = _layer_norm(ALPHA * x + out, ln_g[l], ln_b[l])
    return x


import jax as _jax
import jax.numpy as _jnp

TWIN_FORMAT = 'train_step'
FWD_PARAMS = ['x', 'mem', 'w_in', 'b_forget', 'conv_a_w', 'conv_a_b', 'ln_a_g', 'ln_a_b', 'conv_b_w', 'w_kv_mem', 'mem_ln_g', 'mem_ln_b', 'p_a', 'p_b', 'p_c', 'p_m', 'w_out', 'ln_g', 'ln_b']
TWIN_WEIGHTS = ['w_in', 'b_forget', 'conv_a_w', 'conv_a_b', 'ln_a_g', 'ln_a_b', 'conv_b_w', 'w_kv_mem', 'mem_ln_g', 'mem_ln_b', 'p_a', 'p_b', 'p_c', 'p_m', 'w_out', 'ln_g', 'ln_b']
TWIN_DIFF_INPUT = 'x'
TWIN_INPUTS = ['x', 'mem', 'w_in', 'b_forget', 'conv_a_w', 'conv_a_b', 'ln_a_g', 'ln_a_b', 'conv_b_w', 'w_kv_mem', 'mem_ln_g', 'mem_ln_b', 'p_a', 'p_b', 'p_c', 'p_m', 'w_out', 'ln_g', 'ln_b', 'loss_target', 'm_w_in', 'm_b_forget', 'm_conv_a_w', 'm_conv_a_b', 'm_ln_a_g', 'm_ln_a_b', 'm_conv_b_w', 'm_w_kv_mem', 'm_mem_ln_g', 'm_mem_ln_b', 'm_p_a', 'm_p_b', 'm_p_c', 'm_p_m', 'm_w_out', 'm_ln_g', 'm_ln_b', 'v_w_in', 'v_b_forget', 'v_conv_a_w', 'v_conv_a_b', 'v_ln_a_g', 'v_ln_a_b', 'v_conv_b_w', 'v_w_kv_mem', 'v_mem_ln_g', 'v_mem_ln_b', 'v_p_a', 'v_p_b', 'v_p_c', 'v_p_m', 'v_w_out', 'v_ln_g', 'v_ln_b']
TWIN_OUTPUTS = ['loss', 'grad_x', 'grad_w_in', 'grad_b_forget', 'grad_conv_a_w', 'grad_conv_a_b', 'grad_ln_a_g', 'grad_ln_a_b', 'grad_conv_b_w', 'grad_w_kv_mem', 'grad_mem_ln_g', 'grad_mem_ln_b', 'grad_p_a', 'grad_p_b', 'grad_p_c', 'grad_p_m', 'grad_w_out', 'grad_ln_g', 'grad_ln_b', 'delta_w_in', 'delta_b_forget', 'delta_conv_a_w', 'delta_conv_a_b', 'delta_ln_a_g', 'delta_ln_a_b', 'delta_conv_b_w', 'delta_w_kv_mem', 'delta_mem_ln_g', 'delta_mem_ln_b', 'delta_p_a', 'delta_p_b', 'delta_p_c', 'delta_p_m', 'delta_w_out', 'delta_ln_g', 'delta_ln_b', 'new_m_w_in', 'new_m_b_forget', 'new_m_conv_a_w', 'new_m_conv_a_b', 'new_m_ln_a_g', 'new_m_ln_a_b', 'new_m_conv_b_w', 'new_m_w_kv_mem', 'new_m_mem_ln_g', 'new_m_mem_ln_b', 'new_m_p_a', 'new_m_p_b', 'new_m_p_c', 'new_m_p_m', 'new_m_w_out', 'new_m_ln_g', 'new_m_ln_b', 'new_v_w_in', 'new_v_b_forget', 'new_v_conv_a_w', 'new_v_conv_a_b', 'new_v_ln_a_g', 'new_v_ln_a_b', 'new_v_conv_b_w', 'new_v_w_kv_mem', 'new_v_mem_ln_g', 'new_v_mem_ln_b', 'new_v_p_a', 'new_v_p_b', 'new_v_p_c', 'new_v_p_m', 'new_v_w_out', 'new_v_ln_g', 'new_v_ln_b']
TWIN_LEAF_KINDS = {'loss': 'loss', 'grad_x': 'grad_x', 'grad_w_in': 'grad_w', 'grad_b_forget': 'grad_w', 'grad_conv_a_w': 'grad_w', 'grad_conv_a_b': 'grad_w', 'grad_ln_a_g': 'grad_w', 'grad_ln_a_b': 'grad_w', 'grad_conv_b_w': 'grad_w', 'grad_w_kv_mem': 'grad_w', 'grad_mem_ln_g': 'grad_w', 'grad_mem_ln_b': 'grad_w', 'grad_p_a': 'grad_w', 'grad_p_b': 'grad_w', 'grad_p_c': 'grad_w', 'grad_p_m': 'grad_w', 'grad_w_out': 'grad_w', 'grad_ln_g': 'grad_w', 'grad_ln_b': 'grad_w', 'delta_w_in': 'delta_w', 'delta_b_forget': 'delta_w', 'delta_conv_a_w': 'delta_w', 'delta_conv_a_b': 'delta_w', 'delta_ln_a_g': 'delta_w', 'delta_ln_a_b': 'delta_w', 'delta_conv_b_w': 'delta_w', 'delta_w_kv_mem': 'delta_w', 'delta_mem_ln_g': 'delta_w', 'delta_mem_ln_b': 'delta_w', 'delta_p_a': 'delta_w', 'delta_p_b': 'delta_w', 'delta_p_c': 'delta_w', 'delta_p_m': 'delta_w', 'delta_w_out': 'delta_w', 'delta_ln_g': 'delta_w', 'delta_ln_b': 'delta_w', 'new_m_w_in': 'new_m', 'new_m_b_forget': 'new_m', 'new_m_conv_a_w': 'new_m', 'new_m_conv_a_b': 'new_m', 'new_m_ln_a_g': 'new_m', 'new_m_ln_a_b': 'new_m', 'new_m_conv_b_w': 'new_m', 'new_m_w_kv_mem': 'new_m', 'new_m_mem_ln_g': 'new_m', 'new_m_mem_ln_b': 'new_m', 'new_m_p_a': 'new_m', 'new_m_p_b': 'new_m', 'new_m_p_c': 'new_m', 'new_m_p_m': 'new_m', 'new_m_w_out': 'new_m', 'new_m_ln_g': 'new_m', 'new_m_ln_b': 'new_m', 'new_v_w_in': 'new_v', 'new_v_b_forget': 'new_v', 'new_v_conv_a_w': 'new_v', 'new_v_conv_a_b': 'new_v', 'new_v_ln_a_g': 'new_v', 'new_v_ln_a_b': 'new_v', 'new_v_conv_b_w': 'new_v', 'new_v_w_kv_mem': 'new_v', 'new_v_mem_ln_g': 'new_v', 'new_v_mem_ln_b': 'new_v', 'new_v_p_a': 'new_v', 'new_v_p_b': 'new_v', 'new_v_p_c': 'new_v', 'new_v_p_m': 'new_v', 'new_v_w_out': 'new_v', 'new_v_ln_g': 'new_v', 'new_v_ln_b': 'new_v'}


def _forward(args):
    return _fwd_reference(*[args[k] for k in FWD_PARAMS])


def _output_shape():
    out = _jax.eval_shape(lambda: _forward(_fwd_setup_inputs(0)))
    return out.shape, out.dtype

N_MICROBATCH = 1
ADAM_LR = 0.001
ADAM_B1 = 0.9
ADAM_B2 = 0.999
ADAM_EPS = 1e-08
ADAM_WD = 0.01
ADAM_STEP = 10
PER_EXAMPLE_BATCH_AXIS = {'x': 0, 'mem': 0, 'loss_target': 0}
SHARED_INPUTS = []
_WEIGHT_DTYPES = {'w_in': _jnp.float32, 'b_forget': _jnp.float32, 'conv_a_w': _jnp.float32, 'conv_a_b': _jnp.float32, 'ln_a_g': _jnp.float32, 'ln_a_b': _jnp.float32, 'conv_b_w': _jnp.float32, 'w_kv_mem': _jnp.float32, 'mem_ln_g': _jnp.float32, 'mem_ln_b': _jnp.float32, 'p_a': _jnp.float32, 'p_b': _jnp.float32, 'p_c': _jnp.float32, 'p_m': _jnp.float32, 'w_out': _jnp.float32, 'ln_g': _jnp.float32, 'ln_b': _jnp.float32}
MOMENT_SCALE = {'w_in': 6.357566e-03, 'b_forget': 1.669327e-02, 'conv_a_w': 7.527407e-03, 'conv_a_b': 1.628895e-02, 'ln_a_g': 8.946975e-03, 'ln_a_b': 8.565916e-03, 'conv_b_w': 1.294407e-02, 'w_kv_mem': 1.849802e-03, 'mem_ln_g': 2.705764e-03, 'mem_ln_b': 2.201411e-02, 'p_a': 1.253581e-02, 'p_b': 2.037494e-02, 'p_c': 7.649329e-03, 'p_m': 2.236667e-03, 'w_out': 2.490597e-02, 'ln_g': 3.278477e+01, 'ln_b': 1.063050e+00}


def _to_microbatches(a, axis):
    t = _jnp.moveaxis(a, axis, 0)
    t = t.reshape((N_MICROBATCH, t.shape[0] // N_MICROBATCH) + t.shape[1:])
    return _jnp.moveaxis(t, 1, axis + 1)


def setup_inputs(seed: int = 0) -> dict:
    inp = _fwd_setup_inputs(seed)
    key = _jax.random.fold_in(_jax.random.key(seed), 7919)
    shape, _ = _output_shape()
    out = dict(inp)
    out["loss_target"] = _jax.random.normal(_jax.random.fold_in(key, 0), shape, _jnp.float32)
    for i, name in enumerate(TWIN_WEIGHTS):
        w = inp[name].astype(_jnp.float32)
        if MOMENT_SCALE is None:
            s = _jnp.sqrt(_jnp.mean(_jnp.square(w)) + 1e-30)
        else:
            s = MOMENT_SCALE[name]
        km, kv = _jax.random.split(_jax.random.fold_in(key, i + 1))
        out[name] = w
        out["m_" + name] = s * _jax.random.normal(km, w.shape, _jnp.float32)
        out["v_" + name] = (s * s) * _jax.random.uniform(kv, w.shape, _jnp.float32, 0.5, 1.5)
    if N_MICROBATCH > 1:
        for name, axis in PER_EXAMPLE_BATCH_AXIS.items():
            out[name] = _to_microbatches(out[name], axis)
    return {'x': out['x'], 'mem': out['mem'], 'w_in': out['w_in'], 'b_forget': out['b_forget'], 'conv_a_w': out['conv_a_w'], 'conv_a_b': out['conv_a_b'], 'ln_a_g': out['ln_a_g'], 'ln_a_b': out['ln_a_b'], 'conv_b_w': out['conv_b_w'], 'w_kv_mem': out['w_kv_mem'], 'mem_ln_g': out['mem_ln_g'], 'mem_ln_b': out['mem_ln_b'], 'p_a': out['p_a'], 'p_b': out['p_b'], 'p_c': out['p_c'], 'p_m': out['p_m'], 'w_out': out['w_out'], 'ln_g': out['ln_g'], 'ln_b': out['ln_b'], 'loss_target': out['loss_target'], 'm_w_in': out['m_w_in'], 'm_b_forget': out['m_b_forget'], 'm_conv_a_w': out['m_conv_a_w'], 'm_conv_a_b': out['m_conv_a_b'], 'm_ln_a_g': out['m_ln_a_g'], 'm_ln_a_b': out['m_ln_a_b'], 'm_conv_b_w': out['m_conv_b_w'], 'm_w_kv_mem': out['m_w_kv_mem'], 'm_mem_ln_g': out['m_mem_ln_g'], 'm_mem_ln_b': out['m_mem_ln_b'], 'm_p_a': out['m_p_a'], 'm_p_b': out['m_p_b'], 'm_p_c': out['m_p_c'], 'm_p_m': out['m_p_m'], 'm_w_out': out['m_w_out'], 'm_ln_g': out['m_ln_g'], 'm_ln_b': out['m_ln_b'], 'v_w_in': out['v_w_in'], 'v_b_forget': out['v_b_forget'], 'v_conv_a_w': out['v_conv_a_w'], 'v_conv_a_b': out['v_conv_a_b'], 'v_ln_a_g': out['v_ln_a_g'], 'v_ln_a_b': out['v_ln_a_b'], 'v_conv_b_w': out['v_conv_b_w'], 'v_w_kv_mem': out['v_w_kv_mem'], 'v_mem_ln_g': out['v_mem_ln_g'], 'v_mem_ln_b': out['v_mem_ln_b'], 'v_p_a': out['v_p_a'], 'v_p_b': out['v_p_b'], 'v_p_c': out['v_p_c'], 'v_p_m': out['v_p_m'], 'v_w_out': out['v_w_out'], 'v_ln_g': out['v_ln_g'], 'v_ln_b': out['v_ln_b']}


def _loss(weights, diff, rest, loss_target):
    with _jax.named_scope("forward"):
        args = {**rest, TWIN_DIFF_INPUT: diff, **{k: w.astype(_WEIGHT_DTYPES[k]) for k, w in weights.items()}}
        y = _forward(args)
    with _jax.named_scope("loss_head"):
        err = _jnp.square(y.astype(_jnp.float32) - loss_target)
        return 0.5 * _jnp.sum(_jnp.mean(err, axis=-1)) if err.ndim else 0.5 * err


def _adamw(w, g, m, v):
    m = ADAM_B1 * m + (1.0 - ADAM_B1) * g
    v = ADAM_B2 * v + (1.0 - ADAM_B2) * _jnp.square(g)
    m_hat = m / (1.0 - ADAM_B1 ** ADAM_STEP)
    v_hat = v / (1.0 - ADAM_B2 ** ADAM_STEP)
    delta = -ADAM_LR * (m_hat / (_jnp.sqrt(v_hat) + ADAM_EPS) + ADAM_WD * w)
    return delta, m, v


def reference(x, mem, w_in, b_forget, conv_a_w, conv_a_b, ln_a_g, ln_a_b, conv_b_w, w_kv_mem, mem_ln_g, mem_ln_b, p_a, p_b, p_c, p_m, w_out, ln_g, ln_b, loss_target, m_w_in, m_b_forget, m_conv_a_w, m_conv_a_b, m_ln_a_g, m_ln_a_b, m_conv_b_w, m_w_kv_mem, m_mem_ln_g, m_mem_ln_b, m_p_a, m_p_b, m_p_c, m_p_m, m_w_out, m_ln_g, m_ln_b, v_w_in, v_b_forget, v_conv_a_w, v_conv_a_b, v_ln_a_g, v_ln_a_b, v_conv_b_w, v_w_kv_mem, v_mem_ln_g, v_mem_ln_b, v_p_a, v_p_b, v_p_c, v_p_m, v_w_out, v_ln_g, v_ln_b):
    given = dict(x=x, mem=mem, w_in=w_in, b_forget=b_forget, conv_a_w=conv_a_w, conv_a_b=conv_a_b, ln_a_g=ln_a_g, ln_a_b=ln_a_b, conv_b_w=conv_b_w, w_kv_mem=w_kv_mem, mem_ln_g=mem_ln_g, mem_ln_b=mem_ln_b, p_a=p_a, p_b=p_b, p_c=p_c, p_m=p_m, w_out=w_out, ln_g=ln_g, ln_b=ln_b, loss_target=loss_target, m_w_in=m_w_in, m_b_forget=m_b_forget, m_conv_a_w=m_conv_a_w, m_conv_a_b=m_conv_a_b, m_ln_a_g=m_ln_a_g, m_ln_a_b=m_ln_a_b, m_conv_b_w=m_conv_b_w, m_w_kv_mem=m_w_kv_mem, m_mem_ln_g=m_mem_ln_g, m_mem_ln_b=m_mem_ln_b, m_p_a=m_p_a, m_p_b=m_p_b, m_p_c=m_p_c, m_p_m=m_p_m, m_w_out=m_w_out, m_ln_g=m_ln_g, m_ln_b=m_ln_b, v_w_in=v_w_in, v_b_forget=v_b_forget, v_conv_a_w=v_conv_a_w, v_conv_a_b=v_conv_a_b, v_ln_a_g=v_ln_a_g, v_ln_a_b=v_ln_a_b, v_conv_b_w=v_conv_b_w, v_w_kv_mem=v_w_kv_mem, v_mem_ln_g=v_mem_ln_g, v_mem_ln_b=v_mem_ln_b, v_p_a=v_p_a, v_p_b=v_p_b, v_p_c=v_p_c, v_p_m=v_p_m, v_w_out=v_w_out, v_ln_g=v_ln_g, v_ln_b=v_ln_b)
    weights = {n: given[n] for n in TWIN_WEIGHTS}
    shared = {n: given[n] for n in SHARED_INPUTS}
    per_example = {n: given[n] for n in ['x', 'mem']}
    grad_fn = _jax.value_and_grad(_loss, argnums=(0, 1))

    def one_microbatch(ex, loss_target):
        ex = dict(ex)
        diff = ex.pop(TWIN_DIFF_INPUT)
        return grad_fn(weights, diff, {**shared, **ex}, loss_target)

    if N_MICROBATCH == 1:
        loss, (grad_w, grad_x) = one_microbatch(per_example, given["loss_target"])
    else:
        def body(carry, xs):
            loss_sum, grad_sum = carry
            l_k, (gw_k, gx_k) = one_microbatch(xs[0], xs[1])
            with _jax.named_scope("update"):
                return (loss_sum + l_k, _jax.tree.map(_jnp.add, grad_sum, gw_k)), gx_k

        init = (_jnp.zeros((), _jnp.float32), _jax.tree.map(_jnp.zeros_like, weights))
        (loss, grad_w), grad_x = _jax.lax.scan(body, init, (per_example, given["loss_target"]))
    with _jax.named_scope("update"):
        delta_w, new_m, new_v = {}, {}, {}
        for n in TWIN_WEIGHTS:
            delta_w[n], new_m[n], new_v[n] = _adamw(weights[n], grad_w[n], given["m_" + n], given["v_" + n])
    return (loss, grad_x, *[grad_w[n] for n in TWIN_WEIGHTS], *[delta_w[n] for n in TWIN_WEIGHTS],
            *[new_m[n] for n in TWIN_WEIGHTS], *[new_v[n] for n in TWIN_WEIGHTS])
```

```python
import functools
import math

import numpy as np
import jax
import jax.numpy as jnp
from jax import lax
from jax.experimental import pallas as pl
from jax.experimental.pallas import tpu as pltpu

F32 = jnp.float32
BF16 = jnp.bfloat16
MESH = pl.DeviceIdType.MESH
ANY = pl.BlockSpec(memory_space=pl.ANY)

LN_EPS = 1e-5
NEG_BIG = -1e30
HEAD_DIM = 64
LANES = 128
CONV_A = 31
CONV_B = 3
HALO_A = 32
HALO_B = 8
CHUNK = 32
VMEM_LIMIT = 60 * 1024 * 1024

ADAM_LR, ADAM_B1, ADAM_B2, ADAM_EPS, ADAM_WD, ADAM_STEP = 0.001, 0.9, 0.999, 1e-08, 0.01, 10

T_MM = 512
T_A = 128
T_B = 256
T_ATT = 512
T_CUM = 512
T_M = 512
T_MERGE = 256
T_ELEM = 512


def _pick(n, prefs):
    for p in prefs:
        if n % p == 0:
            return p
    return n


def _params(sem=None):
    return pltpu.CompilerParams(dimension_semantics=sem, vmem_limit_bytes=VMEM_LIMIT)


def _sigmoid(x):
    return jax.nn.sigmoid(x)


def _silu(x):
    return x * _sigmoid(x)


def _dsilu(x):
    s = _sigmoid(x)
    return s * (1.0 + x * (1.0 - s))


def _dot(a, b, dims, precision=None):
    return lax.dot_general(a, b, (dims, ((), ())), preferred_element_type=F32, precision=precision)


NN = ((1,), (0,))
NT = ((1,), (1,))
TN = ((0,), (0,))


def _iota(shape, dim):
    return lax.broadcasted_iota(jnp.int32, shape, dim)


def _mm(a, b, *, ta=False, tb=False, add=None, out_dtype=F32, name):
    m = a.shape[1] if ta else a.shape[0]
    k = a.shape[0] if ta else a.shape[1]
    n = b.shape[0] if tb else b.shape[1]
    tm = _pick(m, (1024, 512, 256)) if ta else _pick(m, (T_MM, 256))
    tn = _pick(n, (1152, 1024, 768, 512, 384, 256, 128))
    tk = _pick(k, (1024, 512, 256)) if not ta else _pick(k, (512, 256))
    nk = k // tk
    dims = ((0,) if ta else (1,), (1,) if tb else (0,))

    def body(*refs):
        if add is None:
            a_ref, b_ref, o_ref, acc_ref = refs
        else:
            a_ref, b_ref, add_ref, o_ref, acc_ref = refs
        kk = pl.program_id(2)
        p = _dot(a_ref[...].astype(BF16), b_ref[...].astype(BF16), dims)

        @pl.when(kk == 0)
        def _():
            acc_ref[...] = p

        @pl.when(kk > 0)
        def _():
            acc_ref[...] += p

        @pl.when(kk == nk - 1)
        def _():
            r = acc_ref[...]
            if add is not None:
                r = r + add_ref[...]
            o_ref[...] = r.astype(out_dtype)

    a_spec = (pl.BlockSpec((tk, tm), lambda j, i, kk: (kk, i)) if ta
              else pl.BlockSpec((tm, tk), lambda j, i, kk: (i, kk)))
    b_spec = (pl.BlockSpec((tn, tk), lambda j, i, kk: (j, kk)) if tb
              else pl.BlockSpec((tk, tn), lambda j, i, kk: (kk, j)))
    o_spec = pl.BlockSpec((tm, tn), lambda j, i, kk: (i, j))
    in_specs = [a_spec, b_spec] + ([o_spec] if add is not None else [])
    args = (a, b) + ((add,) if add is not None else ())
    return pl.pallas_call(
        body, name=name, grid=(n // tn, m // tm, nk), in_specs=in_specs, out_specs=o_spec,
        out_shape=jax.ShapeDtypeStruct((m, n), out_dtype),
        scratch_shapes=[pltpu.VMEM((tm, tn), F32)],
        compiler_params=_params(("parallel", "parallel", "arbitrary")),
    )(*args)


def _ln_rows(x, g, b, name):
    r, d = x.shape
    t = _pick(r, (256,))

    def body(x_ref, g_ref, b_ref, o_ref):
        xv = x_ref[...]
        mu = jnp.mean(xv, axis=-1, keepdims=True)
        dv = xv - mu
        var = jnp.mean(dv * dv, axis=-1, keepdims=True)
        o_ref[...] = dv * lax.rsqrt(var + LN_EPS) * g_ref[...] + b_ref[...]

    row = pl.BlockSpec((t, d), lambda i: (i, 0))
    vec = pl.BlockSpec((1, d), lambda i: (0, 0))
    return pl.pallas_call(body, name=name, grid=(r // t,), in_specs=[row, vec, vec], out_specs=row,
                          out_shape=jax.ShapeDtypeStruct((r, d), F32), compiler_params=_params(("parallel",)))(x, g, b)


def _ln_rows_param_grads(x, dy, name):
    r, d = x.shape

    def body(x_ref, dy_ref, dg_ref, db_ref):
        xv = x_ref[...]
        mu = jnp.mean(xv, axis=-1, keepdims=True)
        dv = xv - mu
        var = jnp.mean(dv * dv, axis=-1, keepdims=True)
        xh = dv * lax.rsqrt(var + LN_EPS)
        dg_ref[...] = jnp.sum(dy_ref[...] * xh, axis=0, keepdims=True)
        db_ref[...] = jnp.sum(dy_ref[...], axis=0, keepdims=True)

    full = pl.BlockSpec((r, d), lambda i: (0, 0))
    vec = pl.BlockSpec((1, d), lambda i: (0, 0))
    o = jax.ShapeDtypeStruct((1, d), F32)
    return pl.pallas_call(body, name=name, grid=(1,), in_specs=[full, full], out_specs=[vec, vec],
                          out_shape=[o, o], compiler_params=_params(("arbitrary",)))(x, dy)


def _conv_a_chunk(glu_ref, cw_ref, r0):
    acc = cw_ref[0:1, :] * glu_ref[pl.ds(r0 + 2, CHUNK), :]
    for k in range(1, CONV_A):
        acc = acc + cw_ref[k:k + 1, :] * glu_ref[pl.ds(r0 + 2 + k, CHUNK), :]
    return acc


def _ln_stats(c):
    mu = jnp.mean(c, axis=-1, keepdims=True)
    d = c - mu
    var = jnp.mean(d * d, axis=-1, keepdims=True)
    rstd = lax.rsqrt(var + LN_EPS)
    return d * rstd, rstd


def _branch_a_fwd(proj, cw, cb, lg, lb, name):
    s = proj.shape[0]
    t = T_A
    w = cw.shape[1]
    r = t // HALO_A

    def body(u_ref, v_ref, gt_ref, hu_ref, hv_ref, cw_ref, cb_ref, lg_ref, lb_ref, z_ref, glu):
        i = pl.program_id(0)
        hglu = hu_ref[...] * _sigmoid(hv_ref[...])
        glu[0:HALO_A, :] = jnp.where(i > 0, hglu, 0.0)
        glu[HALO_A:HALO_A + t, :] = u_ref[...] * _sigmoid(v_ref[...])
        for c in range(t // CHUNK):
            r0 = c * CHUNK
            conv = _conv_a_chunk(glu, cw_ref, r0) + cb_ref[...]
            xh, _ = _ln_stats(conv)
            a3 = _silu(xh * lg_ref[...] + lb_ref[...])
            z_ref[r0:r0 + CHUNK, :] = (a3 * _silu(gt_ref[r0:r0 + CHUNK, :])).astype(BF16)

    def cur(col):
        return pl.BlockSpec((t, w), lambda i, col=col: (i, col))

    def prev(col):
        return pl.BlockSpec((HALO_A, w), lambda i, col=col: (jnp.maximum(i * r - 1, 0), col))

    vec = pl.BlockSpec((1, w), lambda i: (0, 0))
    return pl.pallas_call(
        body, name=name, grid=(s // t,),
        in_specs=[cur(0), cur(1), cur(2), prev(0), prev(1), pl.BlockSpec((HALO_A, w), lambda i: (0, 0)), vec, vec, vec],
        out_specs=pl.BlockSpec((t, w), lambda i: (i, 0)),
        out_shape=jax.ShapeDtypeStruct((s, w), BF16),
        scratch_shapes=[pltpu.VMEM((HALO_A + t, w), F32)],
        compiler_params=_params(("parallel",)),
    )(proj, proj, proj, proj, proj, cw, cb, lg, lb)


def _branch_a_bwd(proj, dz, cw, cb, lg, lb, name):
    s = proj.shape[0]
    t = T_A
    w = cw.shape[1]
    r = t // HALO_A
    n = s // t
    nblk = s // HALO_A
    ext = t + HALO_A

    def body(u_ref, v_ref, gt_ref, dz_ref, pu_ref, pv_ref, nu_ref, nv_ref, ngt_ref, ndz_ref,
             cw_ref, cb_ref, lg_ref, lb_ref, da_ref, dw_ref, dcb_ref, dlg_ref, dlb_ref, glu, dc, dw8):
        i = pl.program_id(0)

        @pl.when(i == 0)
        def _():
            dw8[...] = jnp.zeros_like(dw8)
            dcb_ref[...] = jnp.zeros_like(dcb_ref)
            dlg_ref[...] = jnp.zeros_like(dlg_ref)
            dlb_ref[...] = jnp.zeros_like(dlb_ref)

        glu[0:HALO_A, :] = jnp.where(i > 0, pu_ref[...] * _sigmoid(pv_ref[...]), 0.0)
        glu[HALO_A:HALO_A + t, :] = u_ref[...] * _sigmoid(v_ref[...])
        glu[HALO_A + t:HALO_A + t + HALO_A, :] = nu_ref[...] * _sigmoid(nv_ref[...])
        has_next = i < n - 1
        dcb = jnp.zeros((1, w), F32)
        dlg = jnp.zeros((1, w), F32)
        dlb = jnp.zeros((1, w), F32)
        for c in range(ext // CHUNK):
            r0 = c * CHUNK
            own = r0 < t
            conv = _conv_a_chunk(glu, cw_ref, r0) + cb_ref[...]
            xh, rstd = _ln_stats(conv)
            a2 = xh * lg_ref[...] + lb_ref[...]
            if own:
                gt = gt_ref[r0:r0 + CHUNK, :]
                dzc = dz_ref[r0:r0 + CHUNK, :]
            else:
                gt = ngt_ref[...]
                dzc = ndz_ref[...]
            da2 = dzc * _silu(gt) * _dsilu(a2)
            dxh = da2 * lg_ref[...]
            dconv = rstd * (dxh - jnp.mean(dxh, axis=-1, keepdims=True)
                            - xh * jnp.mean(dxh * xh, axis=-1, keepdims=True))
            if own:
                dc[r0:r0 + CHUNK, :] = dconv
                da_ref[r0:r0 + CHUNK, 2 * w:3 * w] = (dzc * _silu(a2) * _dsilu(gt)).astype(BF16)
                dcb = dcb + jnp.sum(dconv, axis=0, keepdims=True)
                dlg = dlg + jnp.sum(da2 * xh, axis=0, keepdims=True)
                dlb = dlb + jnp.sum(da2, axis=0, keepdims=True)
            else:
                dc[r0:r0 + CHUNK, :] = jnp.where(has_next, dconv, 0.0)
        dcb_ref[...] += dcb
        dlg_ref[...] += dlg
        dlb_ref[...] += dlb
        for c in range(t // CHUNK):
            r0 = c * CHUNK
            dcc = dc[r0:r0 + CHUNK, :]
            dglu = cw_ref[0:1, :] * dc[pl.ds(r0 + CONV_A - 1, CHUNK), :]
            for k in range(1, CONV_A):
                dglu = dglu + cw_ref[k:k + 1, :] * dc[pl.ds(r0 + CONV_A - 1 - k, CHUNK), :]
            for k in range(CONV_A):
                prod = dcc * glu[pl.ds(r0 + 2 + k, CHUNK), :]
                dw8[k] += jnp.sum(prod.reshape(CHUNK // 8, 8, w), axis=0)
            sv = _sigmoid(v_ref[r0:r0 + CHUNK, :])
            da_ref[r0:r0 + CHUNK, 0:w] = (dglu * sv).astype(BF16)
            da_ref[r0:r0 + CHUNK, w:2 * w] = (dglu * u_ref[r0:r0 + CHUNK, :] * sv * (1.0 - sv)).astype(BF16)

        @pl.when(i == n - 1)
        def _():
            dw_ref[...] = jnp.sum(dw8[...], axis=1)

    def cur(col):
        return pl.BlockSpec((t, w), lambda i, col=col: (i, col))

    def prev(col):
        return pl.BlockSpec((HALO_A, w), lambda i, col=col: (jnp.maximum(i * r - 1, 0), col))

    def nxt(col):
        return pl.BlockSpec((HALO_A, w), lambda i, col=col: (jnp.minimum((i + 1) * r, nblk - 1), col))

    vec = pl.BlockSpec((1, w), lambda i: (0, 0))
    vo = jax.ShapeDtypeStruct((1, w), F32)
    return pl.pallas_call(
        body, name=name, grid=(n,),
        in_specs=[cur(0), cur(1), cur(2), pl.BlockSpec((t, w), lambda i: (i, 0)),
                  prev(0), prev(1), nxt(0), nxt(1), nxt(2),
                  pl.BlockSpec((HALO_A, w), lambda i: (jnp.minimum((i + 1) * r, nblk - 1), 0)),
                  pl.BlockSpec((HALO_A, w), lambda i: (0, 0)), vec, vec, vec],
        out_specs=[pl.BlockSpec((t, 3 * w), lambda i: (i, 0)), pl.BlockSpec((HALO_A, w), lambda i: (0, 0)), vec, vec, vec],
        out_shape=[jax.ShapeDtypeStruct((s, 3 * w), BF16), jax.ShapeDtypeStruct((HALO_A, w), F32), vo, vo, vo],
        scratch_shapes=[pltpu.VMEM((t + 2 * HALO_A, w), F32), pltpu.VMEM((ext, w), F32),
                        pltpu.VMEM((HALO_A, 8, w), F32)],
        compiler_params=_params(("arbitrary",)),
    )(proj, proj, proj, dz, proj, proj, proj, proj, proj, dz, cw, cb, lg, lb)


def _conv_b(u_ext, cw_ref, t):
    acc = cw_ref[0:1, :] * u_ext[pl.ds(HALO_B - 2, t), :]
    for k in range(1, CONV_B):
        acc = acc + cw_ref[k:k + 1, :] * u_ext[pl.ds(HALO_B - 2 + k, t), :]
    return acc


def _branch_b_fwd(proj, cw, name):
    s = proj.shape[0]
    t = T_B
    w = cw.shape[1]
    r = t // HALO_B

    def body(h_ref, b_ref, c_ref, gt_ref, ph_ref, pc_ref, cw_ref, z_ref, u_ext):
        i = pl.program_id(0)
        u_ext[0:HALO_B, :] = jnp.where(i > 0, pc_ref[...] * ph_ref[...], 0.0)
        u_ext[HALO_B:HALO_B + t, :] = c_ref[...] * h_ref[...]
        cv = _conv_b(u_ext, cw_ref, t)
        z_ref[...] = (b_ref[...] * cv * _silu(gt_ref[...])).astype(BF16)

    def cur(col):
        return pl.BlockSpec((t, w), lambda i, col=col: (i, col))

    def prev(col):
        return pl.BlockSpec((HALO_B, w), lambda i, col=col: (jnp.maximum(i * r - 1, 0), col))

    return pl.pallas_call(
        body, name=name, grid=(s // t,),
        in_specs=[cur(3), cur(4), cur(5), cur(6), prev(3), prev(5), pl.BlockSpec((HALO_B, w), lambda i: (0, 0))],
        out_specs=pl.BlockSpec((t, w), lambda i: (i, 0)),
        out_shape=jax.ShapeDtypeStruct((s, w), BF16),
        scratch_shapes=[pltpu.VMEM((HALO_B + t, w), F32)],
        compiler_params=_params(("parallel",)),
    )(proj, proj, proj, proj, proj, proj, cw)


def _branch_b_bwd(proj, dz, cw, name):
    s = proj.shape[0]
    t = T_B
    w = cw.shape[1]
    r = t // HALO_B
    n = s // t
    nblk = s // HALO_B

    def body(h_ref, b_ref, c_ref, gt_ref, dz_ref, ph_ref, pc_ref, nb_ref, ngt_ref, ndz_ref, cw_ref,
             db_ref, dw_ref, u_ext, dcv_ext, dw8):
        i = pl.program_id(0)

        @pl.when(i == 0)
        def _():
            dw8[...] = jnp.zeros_like(dw8)

        u_ext[0:HALO_B, :] = jnp.where(i > 0, pc_ref[...] * ph_ref[...], 0.0)
        u_ext[HALO_B:HALO_B + t, :] = c_ref[...] * h_ref[...]
        cv = _conv_b(u_ext, cw_ref, t)
        gt = gt_ref[...]
        dhb = dz_ref[...] * _silu(gt)
        db_ref[:, 3 * w:4 * w] = (dz_ref[...] * b_ref[...] * cv * _dsilu(gt)).astype(BF16)
        db_ref[:, w:2 * w] = (dhb * cv).astype(BF16)
        dcv = dhb * b_ref[...]
        dcv_ext[0:t, :] = dcv
        ndcv = ndz_ref[...] * _silu(ngt_ref[...]) * nb_ref[...]
        dcv_ext[t:t + HALO_B, :] = jnp.where(i < n - 1, ndcv, 0.0)
        du = cw_ref[0:1, :] * dcv_ext[pl.ds(2, t), :]
        for k in range(1, CONV_B):
            du = du + cw_ref[k:k + 1, :] * dcv_ext[pl.ds(2 - k, t), :]
        db_ref[:, 2 * w:3 * w] = (du * h_ref[...]).astype(BF16)
        db_ref[:, 0:w] = (du * c_ref[...]).astype(BF16)
        for k in range(CONV_B):
            prod = dcv * u_ext[pl.ds(HALO_B - 2 + k, t), :]
            dw8[k] += jnp.sum(prod.reshape(t // 8, 8, w), axis=0)

        @pl.when(i == n - 1)
        def _():
            dw_ref[...] = jnp.sum(dw8[...], axis=1)

    def cur(col):
        return pl.BlockSpec((t, w), lambda i, col=col: (i, col))

    def prev(col):
        return pl.BlockSpec((HALO_B, w), lambda i, col=col: (jnp.maximum(i * r - 1, 0), col))

    def nxt(col):
        return pl.BlockSpec((HALO_B, w), lambda i, col=col: (jnp.minimum((i + 1) * r, nblk - 1), col))

    return pl.pallas_call(
        body, name=name, grid=(n,),
        in_specs=[cur(3), cur(4), cur(5), cur(6), pl.BlockSpec((t, w), lambda i: (i, 0)),
                  prev(3), prev(5), nxt(4), nxt(6),
                  pl.BlockSpec((HALO_B, w), lambda i: (jnp.minimum((i + 1) * r, nblk - 1), 0)),
                  pl.BlockSpec((HALO_B, w), lambda i: (0, 0))],
        out_specs=[pl.BlockSpec((t, 4 * w), lambda i: (i, 0)), pl.BlockSpec((HALO_B, w), lambda i: (0, 0))],
        out_shape=[jax.ShapeDtypeStruct((s, 4 * w), BF16), jax.ShapeDtypeStruct((HALO_B, w), F32)],
        scratch_shapes=[pltpu.VMEM((HALO_B + t, w), F32), pltpu.VMEM((t + HALO_B, w), F32),
                        pltpu.VMEM((HALO_B, 8, w), F32)],
        compiler_params=_params(("arbitrary",)),
    )(proj, proj, proj, proj, dz, proj, proj, proj, proj, dz, cw)


def _forget_prep(proj, bf, fblk, name):
    s = proj.shape[0]
    t = T_CUM

    def body(f_ref, bf_ref, cum_ref, cumt_ref, sgt_ref, carry):
        i = pl.program_id(0)

        @pl.when(i == 0)
        def _():
            carry[...] = jnp.zeros_like(carry)

        z = f_ref[...] + bf_ref[...]
        logf = jnp.minimum(z, 0.0) - jnp.log1p(jnp.exp(-jnp.abs(z)))
        tri = (_iota((t, t), 0) >= _iota((t, t), 1)).astype(F32)
        cum = _dot(tri, logf, NN, precision=lax.Precision.HIGHEST) + carry[0:1, :]
        carry[0:1, :] = cum[t - 1:t, :]
        cum_ref[...] = cum
        cumt_ref[...] = cum.T[0:8, :]
        sgt_ref[...] = _sigmoid(-z).T[0:8, :]

    return pl.pallas_call(
        body, name=name, grid=(s // t,),
        in_specs=[pl.BlockSpec((t, LANES), lambda i: (i, fblk)), pl.BlockSpec((1, LANES), lambda i: (0, 0))],
        out_specs=[pl.BlockSpec((t, LANES), lambda i: (i, 0)), pl.BlockSpec((8, t), lambda i: (0, i)),
                   pl.BlockSpec((8, t), lambda i: (0, i))],
        out_shape=[jax.ShapeDtypeStruct((s, LANES), F32), jax.ShapeDtypeStruct((8, s), F32),
                   jax.ShapeDtypeStruct((8, s), F32)],
        scratch_shapes=[pltpu.VMEM((8, LANES), F32)],
        compiler_params=_params(("arbitrary",)),
    )(proj, bf)


def _lane_pick(x, lane):
    return jnp.sum(jnp.where(_iota(x.shape, 1) == lane, x, 0.0), axis=1, keepdims=True)


def _sublane_pick(x, row):
    return jnp.sum(jnp.where(_iota(x.shape, 0) == row, x, 0.0), axis=0, keepdims=True)


def _head_mask(hh):
    lane = _iota((1, LANES), 1)
    return (lane >= HEAD_DIM * hh) & (lane < HEAD_DIM * (hh + 1))


def _causal_pairs(n, kv_major):
    if kv_major:
        pairs = [(q, k) for k in range(n) for q in range(k, n)]
    else:
        pairs = [(q, k) for q in range(n) for k in range(q + 1)]
    return (np.asarray([p[0] for p in pairs], np.int32), np.asarray([p[1] for p in pairs], np.int32))


def _fox_fwd(proj, cum, cumt, qblk, kblk, vblk, name):
    s = proj.shape[0]
    t = T_ATT
    n = s // t
    n_pair = 4
    qi_np, ki_np = _causal_pairs(n, kv_major=False)
    scale = HEAD_DIM ** -0.5

    def body(qi_ref, ki_ref, q_ref, k_ref, v_ref, cum_ref, cumt_ref, o_ref, lse_ref, m_s, l_s, acc_s):
        hp = pl.program_id(0)
        step = pl.program_id(1)
        qi = qi_ref[step]
        ki = ki_ref[step]

        @pl.when(ki == 0)
        def _():
            m_s[...] = jnp.full_like(m_s, NEG_BIG)
            l_s[...] = jnp.zeros_like(l_s)
            acc_s[...] = jnp.zeros_like(acc_s)

        q = q_ref[...]
        kb = k_ref[...].astype(BF16)
        vb = v_ref[...].astype(BF16)
        causal = _iota((t, t), 0) >= _iota((t, t), 1)
        for hh in range(2):
            h = 2 * hp + hh
            qm = jnp.where(_head_mask(hh), q, 0.0).astype(BF16)
            sc = _dot(qm, kb, NT) * scale
            sc = sc + (_lane_pick(cum_ref[...], h) - _sublane_pick(cumt_ref[...], h))
            sc = jnp.where((ki < qi) | causal, sc, NEG_BIG)
            m_old = m_s[hh]
            m_new = jnp.maximum(m_old, jnp.max(sc, axis=1, keepdims=True))
            p = jnp.exp(sc - m_new)
            alpha = jnp.exp(m_old - m_new)
            l_s[hh] = alpha * l_s[hh] + jnp.sum(p, axis=1, keepdims=True)
            acc_s[hh] = alpha * acc_s[hh] + _dot(p.astype(BF16), vb, NN)
            m_s[hh] = m_new

        @pl.when(ki == qi)
        def _():
            lane = _iota((t, LANES), 1)
            o_ref[...] = jnp.where(lane < HEAD_DIM, acc_s[0] / l_s[0], acc_s[1] / l_s[1])
            lse0 = m_s[0] + jnp.log(l_s[0])
            lse1 = m_s[1] + jnp.log(l_s[1])
            lse_ref[0] = jnp.where(lane == 0, lse0, jnp.where(lane == 1, lse1, 0.0))

    grid_spec = pltpu.PrefetchScalarGridSpec(
        num_scalar_prefetch=2, grid=(n_pair, len(qi_np)),
        in_specs=[pl.BlockSpec((t, LANES), lambda hp, st, qi, ki: (qi[st], qblk + hp)),
                  pl.BlockSpec((t, LANES), lambda hp, st, qi, ki: (ki[st], kblk + hp)),
                  pl.BlockSpec((t, LANES), lambda hp, st, qi, ki: (ki[st], vblk + hp)),
                  pl.BlockSpec((t, LANES), lambda hp, st, qi, ki: (qi[st], 0)),
                  pl.BlockSpec((8, t), lambda hp, st, qi, ki: (0, ki[st]))],
        out_specs=[pl.BlockSpec((t, LANES), lambda hp, st, qi, ki: (qi[st], hp)),
                   pl.BlockSpec((1, t, LANES), lambda hp, st, qi, ki: (hp, qi[st], 0))],
        scratch_shapes=[pltpu.VMEM((2, t, 1), F32), pltpu.VMEM((2, t, 1), F32), pltpu.VMEM((2, t, LANES), F32)])
    return pl.pallas_call(
        body, name=name, grid_spec=grid_spec,
        out_shape=[jax.ShapeDtypeStruct((s, n_pair * LANES), F32), jax.ShapeDtypeStruct((n_pair, s, LANES), F32)],
        compiler_params=_params(("parallel", "arbitrary")),
    )(jnp.asarray(qi_np), jnp.asarray(ki_np), proj, proj, proj, cum, cumt)


def _fox_bwd_prep(proj, dz, o, gblk, name):
    s, w = o.shape
    t = T_ELEM
    n_head = w // HEAD_DIM

    def body(gt_ref, dz_ref, o_ref, do_ref, dg_ref, dl_ref):
        gt = gt_ref[...]
        do = dz_ref[...] * _silu(gt)
        do_ref[...] = do
        dg_ref[...] = (dz_ref[...] * o_ref[...] * _dsilu(gt)).astype(BF16)
        sel = (_iota((w, LANES), 0) // HEAD_DIM == _iota((w, LANES), 1)).astype(F32)
        dl_ref[...] = _dot(do * o_ref[...], sel, NN, precision=lax.Precision.HIGHEST)

    assert n_head <= LANES
    row = pl.BlockSpec((t, w), lambda i: (i, 0))
    return pl.pallas_call(
        body, name=name, grid=(s // t,),
        in_specs=[pl.BlockSpec((t, w), lambda i: (i, gblk)), row, row],
        out_specs=[row, row, pl.BlockSpec((t, LANES), lambda i: (i, 0))],
        out_shape=[jax.ShapeDtypeStruct((s, w), F32), jax.ShapeDtypeStruct((s, w), BF16),
                   jax.ShapeDtypeStruct((s, LANES), F32)],
        compiler_params=_params(("parallel",)),
    )(proj, dz, o)


def _fox_bwd(proj, do, lse, delta, cum, cumt, qblk, kblk, vblk, name):
    s = proj.shape[0]
    t = T_ATT
    n = s // t
    n_pair = 4
    qi_np, ki_np = _causal_pairs(n, kv_major=True)
    n_step = len(qi_np)
    scale = HEAD_DIM ** -0.5

    def body(qi_ref, ki_ref, q_ref, k_ref, v_ref, do_ref, lse_ref, dl_ref, cum_ref, cumt_ref,
             dq_ref, dk_ref, dv_ref, dct_ref, dcq_ref, dq_s, dk_s, dv_s, dc_s, dcq_s):
        hp = pl.program_id(0)
        step = pl.program_id(1)
        qi = qi_ref[step]
        ki = ki_ref[step]

        @pl.when(step == 0)
        def _():
            dq_s[...] = jnp.zeros_like(dq_s)
            dcq_s[...] = jnp.zeros_like(dcq_s)

        @pl.when(qi == ki)
        def _():
            dk_s[...] = jnp.zeros_like(dk_s)
            dv_s[...] = jnp.zeros_like(dv_s)
            dc_s[...] = jnp.zeros_like(dc_s)

        q = q_ref[...]
        do = do_ref[...]
        kb = k_ref[...].astype(BF16)
        vb = v_ref[...].astype(BF16)
        causal = _iota((t, t), 0) >= _iota((t, t), 1)
        sub = _iota((8, t), 0)
        dq_new = jnp.zeros((t, LANES), F32)
        dcq_new = jnp.zeros((t, LANES), F32)
        lane = _iota((t, LANES), 1)
        for hh in range(2):
            h = 2 * hp + hh
            hm = _head_mask(hh)
            qm = jnp.where(hm, q, 0.0).astype(BF16)
            dom = jnp.where(hm, do, 0.0).astype(BF16)
            sc = _dot(qm, kb, NT) * scale
            sc = sc + (_lane_pick(cum_ref[...], h) - _sublane_pick(cumt_ref[...], h))
            sc = jnp.where((ki < qi) | causal, sc, NEG_BIG)
            p = jnp.exp(sc - _lane_pick(lse_ref[0], hh))
            dv_s[...] += _dot(p.astype(BF16), dom, TN)
            dp = _dot(dom, vb, NT)
            ds = p * (dp - _lane_pick(dl_ref[...], h))
            dc_s[...] += jnp.where(sub == h, -jnp.sum(ds, axis=0, keepdims=True), 0.0)
            dcq_new = dcq_new + jnp.where(lane == h, jnp.sum(ds, axis=1, keepdims=True), 0.0)
            dsb = (ds * scale).astype(BF16)
            dk_s[...] += _dot(dsb, qm, TN)
            dq_new = dq_new + jnp.where(hm, _dot(dsb, kb, NN), 0.0)
        row0 = pl.multiple_of(qi * t, t)
        dq_s[pl.ds(row0, t), :] += dq_new
        dcq_s[pl.ds(row0, t), :] += dcq_new

        @pl.when(qi == n - 1)
        def _():
            dk_ref[...] = dk_s[...].astype(BF16)
            dv_ref[...] = dv_s[...].astype(BF16)
            dct_ref[0] = dc_s[...]

        @pl.when(step == n_step - 1)
        def _():
            dq_ref[...] = dq_s[...].astype(BF16)
            dcq_ref[0] = dcq_s[...]

    grid_spec = pltpu.PrefetchScalarGridSpec(
        num_scalar_prefetch=2, grid=(n_pair, len(qi_np)),
        in_specs=[pl.BlockSpec((t, LANES), lambda hp, st, qi, ki: (qi[st], qblk + hp)),
                  pl.BlockSpec((t, LANES), lambda hp, st, qi, ki: (ki[st], kblk + hp)),
                  pl.BlockSpec((t, LANES), lambda hp, st, qi, ki: (ki[st], vblk + hp)),
                  pl.BlockSpec((t, LANES), lambda hp, st, qi, ki: (qi[st], hp)),
                  pl.BlockSpec((1, t, LANES), lambda hp, st, qi, ki: (hp, qi[st], 0)),
                  pl.BlockSpec((t, LANES), lambda hp, st, qi, ki: (qi[st], 0)),
                  pl.BlockSpec((t, LANES), lambda hp, st, qi, ki: (qi[st], 0)),
                  pl.BlockSpec((8, t), lambda hp, st, qi, ki: (0, ki[st]))],
        out_specs=[pl.BlockSpec((s, LANES), lambda hp, st, qi, ki: (0, hp)),
                   pl.BlockSpec((t, LANES), lambda hp, st, qi, ki: (ki[st], hp)),
                   pl.BlockSpec((t, LANES), lambda hp, st, qi, ki: (ki[st], hp)),
                   pl.BlockSpec((1, 8, t), lambda hp, st, qi, ki: (hp, 0, ki[st])),
                   pl.BlockSpec((1, s, LANES), lambda hp, st, qi, ki: (hp, 0, 0))],
        scratch_shapes=[pltpu.VMEM((s, LANES), F32), pltpu.VMEM((t, LANES), F32), pltpu.VMEM((t, LANES), F32),
                        pltpu.VMEM((8, t), F32), pltpu.VMEM((s, LANES), F32)])
    w = n_pair * LANES
    return pl.pallas_call(
        body, name=name, grid_spec=grid_spec,
        out_shape=[jax.ShapeDtypeStruct((s, w), BF16), jax.ShapeDtypeStruct((s, w), BF16),
                   jax.ShapeDtypeStruct((s, w), BF16), jax.ShapeDtypeStruct((n_pair, 8, s), F32),
                   jax.ShapeDtypeStruct((n_pair, s, LANES), F32)],
        compiler_params=_params(("parallel", "arbitrary")),
    )(jnp.asarray(qi_np), jnp.asarray(ki_np), proj, proj, proj, do, lse, delta, cum, cumt)


def _forget_bwd(dcumt4, dcumq4, sgt, name):
    s = sgt.shape[1]
    t = T_CUM
    n = s // t

    def body(d_ref, dq_ref, sg_ref, df_ref, dbf_ref, carry):
        i = pl.program_id(0)

        @pl.when(i == 0)
        def _():
            carry[...] = jnp.zeros_like(carry)
            dbf_ref[...] = jnp.zeros_like(dbf_ref)

        dq = dq_ref[0] + dq_ref[1] + dq_ref[2] + dq_ref[3]
        d = d_ref[0] + d_ref[1] + d_ref[2] + d_ref[3] + dq.T[0:8, :]
        upper = (_iota((t, t), 0) >= _iota((t, t), 1)).astype(F32)
        dlog = _dot(d, upper, NN, precision=lax.Precision.HIGHEST) + carry[:, 0:1]
        carry[...] += jnp.sum(d, axis=1, keepdims=True)
        dzt = dlog * sg_ref[...]
        dbf_ref[...] += jnp.sum(dzt, axis=1, keepdims=True)
        padded = jnp.concatenate([dzt, jnp.zeros((LANES - 8, t), F32)], axis=0)
        df_ref[...] = padded.T.astype(BF16)

    return pl.pallas_call(
        body, name=name, grid=(n,),
        in_specs=[pl.BlockSpec((4, 8, t), lambda i: (0, 0, n - 1 - i)),
                  pl.BlockSpec((4, t, LANES), lambda i: (0, n - 1 - i, 0)),
                  pl.BlockSpec((8, t), lambda i: (0, n - 1 - i))],
        out_specs=[pl.BlockSpec((t, LANES), lambda i: (n - 1 - i, 0)), pl.BlockSpec((8, LANES), lambda i: (0, 0))],
        out_shape=[jax.ShapeDtypeStruct((s, LANES), BF16), jax.ShapeDtypeStruct((8, LANES), F32)],
        scratch_shapes=[pltpu.VMEM((8, LANES), F32)],
        compiler_params=_params(("arbitrary",)),
    )(dcumt4, dcumq4, sgt)


def _mem_softmax(qm, kp):
    sc = _dot(qm, kp, NT) * (HEAD_DIM ** -0.5)
    e = jnp.exp(sc - jnp.max(sc, axis=1, keepdims=True))
    return e / jnp.sum(e, axis=1, keepdims=True)


def _branch_m_fwd(proj, mkv, mblk, name):
    s = proj.shape[0]
    t = T_M
    mw = mkv.shape[1] // 2
    ml = mkv.shape[0]

    def body(m_ref, kv_ref, z_ref):
        outs = []
        for pr in range(mw // LANES):
            qp = m_ref[:, pr * LANES:(pr + 1) * LANES]
            kp = kv_ref[:, pr * LANES:(pr + 1) * LANES].astype(BF16)
            vp = kv_ref[:, mw + pr * LANES:mw + (pr + 1) * LANES].astype(BF16)
            oh = []
            for hh in range(2):
                qm = jnp.where(_head_mask(hh), qp, 0.0).astype(BF16)
                oh.append(_dot(_mem_softmax(qm, kp).astype(BF16), vp, NN))
            outs.append(jnp.where(_iota((t, LANES), 1) < HEAD_DIM, oh[0], oh[1]))
        o = jnp.concatenate(outs, axis=1)
        z_ref[...] = (o * _silu(m_ref[:, mw:2 * mw])).astype(BF16)

    return pl.pallas_call(
        body, name=name, grid=(s // t,),
        in_specs=[pl.BlockSpec((t, 2 * mw), lambda i: (i, mblk)), pl.BlockSpec((ml, 2 * mw), lambda i: (0, 0))],
        out_specs=pl.BlockSpec((t, mw), lambda i: (i, 0)),
        out_shape=jax.ShapeDtypeStruct((s, mw), BF16),
        compiler_params=_params(("parallel",)),
    )(proj, mkv)


def _branch_m_bwd(proj, mkv, dz, mblk, name):
    s = proj.shape[0]
    t = T_M
    mw = mkv.shape[1] // 2
    ml = mkv.shape[0]
    scale = HEAD_DIM ** -0.5

    def body(m_ref, kv_ref, dz_ref, dm_ref, dkv_ref):
        i = pl.program_id(0)

        @pl.when(i == 0)
        def _():
            dkv_ref[...] = jnp.zeros_like(dkv_ref)

        gt = m_ref[:, mw:2 * mw]
        dz = dz_ref[...]
        do = dz * _silu(gt)
        outs = []
        for pr in range(mw // LANES):
            cols = slice(pr * LANES, (pr + 1) * LANES)
            vcols = slice(mw + pr * LANES, mw + (pr + 1) * LANES)
            qp = m_ref[:, cols]
            kp = kv_ref[:, cols].astype(BF16)
            vp = kv_ref[:, vcols].astype(BF16)
            dop = do[:, cols]
            oh = []
            dq = jnp.zeros((t, LANES), F32)
            dk = jnp.zeros((ml, LANES), F32)
            dv = jnp.zeros((ml, LANES), F32)
            for hh in range(2):
                hm = _head_mask(hh)
                qm = jnp.where(hm, qp, 0.0).astype(BF16)
                dom = jnp.where(hm, dop, 0.0).astype(BF16)
                p = _mem_softmax(qm, kp)
                pb = p.astype(BF16)
                oh.append(_dot(pb, vp, NN))
                dv = dv + _dot(pb, dom, TN)
                dp = _dot(dom, vp, NT)
                ds = p * (dp - jnp.sum(dp * p, axis=1, keepdims=True))
                dsb = (ds * scale).astype(BF16)
                dq = dq + jnp.where(hm, _dot(dsb, kp, NN), 0.0)
                dk = dk + _dot(dsb, qm, TN)
            outs.append(jnp.where(_iota((t, LANES), 1) < HEAD_DIM, oh[0], oh[1]))
            dm_ref[:, cols] = dq.astype(BF16)
            dkv_ref[:, cols] += dk
            dkv_ref[:, vcols] += dv
        o = jnp.concatenate(outs, axis=1)
        dm_ref[:, mw:2 * mw] = (dz * o * _dsilu(gt)).astype(BF16)

    return pl.pallas_call(
        body, name=name, grid=(s // t,),
        in_specs=[pl.BlockSpec((t, 2 * mw), lambda i: (i, mblk)), pl.BlockSpec((ml, 2 * mw), lambda i: (0, 0)),
                  pl.BlockSpec((t, mw), lambda i: (i, 0))],
        out_specs=[pl.BlockSpec((t, 2 * mw), lambda i: (i, 0)), pl.BlockSpec((ml, 2 * mw), lambda i: (0, 0))],
        out_shape=[jax.ShapeDtypeStruct((s, 2 * mw), BF16), jax.ShapeDtypeStruct((ml, 2 * mw), F32)],
        compiler_params=_params(("arbitrary",)),
    )(proj, mkv, dz)


def _merge_parts(z_refs, g_refs, pcat, bounds):
    ys, sgs = [], []
    merged = None
    for zr, gr, (lo, hi) in zip(z_refs, g_refs, bounds):
        y = _dot(zr[...], pcat[lo:hi, :], NN)
        sg = _sigmoid(gr[...])
        ys.append(y)
        sgs.append(sg)
        merged = sg * y if merged is None else merged + sg * y
    return ys, sgs, merged


def _branch_bounds(zs):
    bounds, lo = [], 0
    for z in zs:
        bounds.append((lo, lo + z.shape[1]))
        lo += z.shape[1]
    return bounds


def _merge_fwd(zs, proj, gblk, pcat, wout, x, lng, lnb, alpha, name):
    s, d = x.shape
    t = T_MERGE
    bounds = _branch_bounds(zs)

    def body(*refs):
        z_refs, g_refs = refs[0:4], refs[4:8]
        pcat_hbm, wout_hbm, x_ref, lng_ref, lnb_ref, y_ref, pcat_v, wout_v = refs[8:]

        @pl.when(pl.program_id(0) == 0)
        def _():
            pltpu.sync_copy(pcat_hbm, pcat_v)
            pltpu.sync_copy(wout_hbm, wout_v)

        _, _, merged = _merge_parts(z_refs, g_refs, pcat_v, bounds)
        h = alpha * x_ref[...] + _dot(merged.astype(BF16), wout_v[...], NN)
        xh, _ = _ln_stats(h)
        y_ref[...] = xh * lng_ref[...] + lnb_ref[...]

    row = pl.BlockSpec((t, d), lambda i: (i, 0))
    vec = pl.BlockSpec((1, d), lambda i: (0, 0))
    in_specs = ([pl.BlockSpec((t, z.shape[1]), lambda i: (i, 0)) for z in zs]
                + [pl.BlockSpec((t, d), lambda i, k=k: (i, gblk + k)) for k in range(4)]
                + [ANY, ANY, row, vec, vec])
    return pl.pallas_call(
        body, name=name, grid=(s // t,), in_specs=in_specs, out_specs=row,
        out_shape=jax.ShapeDtypeStruct((s, d), F32),
        scratch_shapes=[pltpu.VMEM(pcat.shape, BF16), pltpu.VMEM(wout.shape, BF16)],
        compiler_params=_params(("arbitrary",)),
    )(*zs, proj, proj, proj, proj, pcat, wout, x, lng, lnb)


def _merge_bwd(zs, proj, gblk, pcat, wout, x, lng, lnb, dy, alpha, name):
    s, d = x.shape
    t = T_MERGE
    n = s // t
    bounds = _branch_bounds(zs)

    def body(*refs):
        z_refs, g_refs = refs[0:4], refs[4:8]
        pcat_hbm, wout_hbm, x_ref, lng_ref, lnb_ref, dy_ref = refs[8:14]
        dx_ref, dg_ref = refs[14:16]
        dz_refs = refs[16:20]
        dpcat_hbm, dwout_hbm, dlng_ref, dlnb_ref = refs[20:24]
        pcat_v, wout_v, dpcat_v, dwout_v = refs[24:]
        i = pl.program_id(0)

        @pl.when(i == 0)
        def _():
            pltpu.sync_copy(pcat_hbm, pcat_v)
            pltpu.sync_copy(wout_hbm, wout_v)
            dpcat_v[...] = jnp.zeros_like(dpcat_v)
            dwout_v[...] = jnp.zeros_like(dwout_v)
            dlng_ref[...] = jnp.zeros_like(dlng_ref)
            dlnb_ref[...] = jnp.zeros_like(dlnb_ref)

        ys, sgs, merged = _merge_parts(z_refs, g_refs, pcat_v, bounds)
        mb = merged.astype(BF16)
        h = alpha * x_ref[...] + _dot(mb, wout_v[...], NN)
        xh, rstd = _ln_stats(h)
        dyv = dy_ref[...]
        dlng_ref[...] += jnp.sum(dyv * xh, axis=0, keepdims=True)
        dlnb_ref[...] += jnp.sum(dyv, axis=0, keepdims=True)
        dxh = dyv * lng_ref[...]
        dh = rstd * (dxh - jnp.mean(dxh, axis=-1, keepdims=True) - xh * jnp.mean(dxh * xh, axis=-1, keepdims=True))
        dx_ref[...] = alpha * dh
        dhb = dh.astype(BF16)
        dwout_v[...] += _dot(mb, dhb, TN)
        dmerged = _dot(dhb, wout_v[...], NT)
        for k, (zr, (lo, hi)) in enumerate(zip(z_refs, bounds)):
            sg = sgs[k]
            dg_ref[:, k * d:(k + 1) * d] = (dmerged * ys[k] * sg * (1.0 - sg)).astype(BF16)
            dyk = (dmerged * sg).astype(BF16)
            dpcat_v[lo:hi, :] += _dot(zr[...], dyk, TN)
            dz_refs[k][...] = _dot(dyk, pcat_v[lo:hi, :], NT)

        @pl.when(i == n - 1)
        def _():
            pltpu.sync_copy(dpcat_v, dpcat_hbm)
            pltpu.sync_copy(dwout_v, dwout_hbm)

    row = pl.BlockSpec((t, d), lambda i: (i, 0))
    vec = pl.BlockSpec((1, d), lambda i: (0, 0))
    z_specs = [pl.BlockSpec((t, z.shape[1]), lambda i: (i, 0)) for z in zs]
    in_specs = (z_specs + [pl.BlockSpec((t, d), lambda i, k=k: (i, gblk + k)) for k in range(4)]
                + [ANY, ANY, row, vec, vec, row])
    out_specs = [row, pl.BlockSpec((t, 4 * d), lambda i: (i, 0))] + z_specs + [ANY, ANY, vec, vec]
    vo = jax.ShapeDtypeStruct((1, d), F32)
    out_shape = ([jax.ShapeDtypeStruct((s, d), F32), jax.ShapeDtypeStruct((s, 4 * d), BF16)]
                 + [jax.ShapeDtypeStruct(z.shape, F32) for z in zs]
                 + [jax.ShapeDtypeStruct(pcat.shape, F32), jax.ShapeDtypeStruct(wout.shape, F32), vo, vo])
    return pl.pallas_call(
        body, name=name, grid=(n,), in_specs=in_specs, out_specs=out_specs, out_shape=out_shape,
        scratch_shapes=[pltpu.VMEM(pcat.shape, BF16), pltpu.VMEM(wout.shape, BF16),
                        pltpu.VMEM(pcat.shape, F32), pltpu.VMEM(wout.shape, F32)],
        compiler_params=_params(("arbitrary",)),
    )(*zs, proj, proj, proj, proj, pcat, wout, x, lng, lnb, dy)


def _loss_head(y, target, name):
    s, d = y.shape
    t = T_ELEM

    def body(y_ref, t_ref, dy_ref, loss_ref):
        @pl.when(pl.program_id(0) == 0)
        def _():
            loss_ref[...] = jnp.zeros_like(loss_ref)

        e = y_ref[...] - t_ref[...]
        dy_ref[...] = e * (1.0 / d)
        loss_ref[...] += 0.5 * jnp.sum(jnp.mean(e * e, axis=-1, keepdims=True), axis=0, keepdims=True)

    row = pl.BlockSpec((t, d), lambda i: (i, 0))
    return pl.pallas_call(
        body, name=name, grid=(s // t,), in_specs=[row, row],
        out_specs=[row, pl.BlockSpec((8, LANES), lambda i: (0, 0))],
        out_shape=[jax.ShapeDtypeStruct((s, d), F32), jax.ShapeDtypeStruct((8, LANES), F32)],
        compiler_params=_params(("arbitrary",)),
    )(y, target)


def _adamw(w, g, m, v, name):
    r, c = w.shape
    t = _pick(r, (256, 128, 64, 32, 16, 8))

    def body(w_ref, g_ref, m_ref, v_ref, d_ref, nm_ref, nv_ref):
        gv = g_ref[...]
        nm = ADAM_B1 * m_ref[...] + (1.0 - ADAM_B1) * gv
        nv = ADAM_B2 * v_ref[...] + (1.0 - ADAM_B2) * (gv * gv)
        m_hat = nm / (1.0 - ADAM_B1 ** ADAM_STEP)
        v_hat = nv / (1.0 - ADAM_B2 ** ADAM_STEP)
        d_ref[...] = -ADAM_LR * (m_hat / (jnp.sqrt(v_hat) + ADAM_EPS) + ADAM_WD * w_ref[...])
        nm_ref[...] = nm
        nv_ref[...] = nv

    blk = pl.BlockSpec((t, c), lambda i: (i, 0))
    o = jax.ShapeDtypeStruct((r, c), F32)
    return pl.pallas_call(body, name=name, grid=(r // t,), in_specs=[blk] * 4, out_specs=[blk] * 3,
                          out_shape=[o, o, o], compiler_params=_params(("parallel",)))(w, g, m, v)


def _sum_leading(x, name):
    k, r, c = x.shape
    t = _pick(r, (256, 128, 64, 32, 16, 8))

    def body(x_ref, o_ref):
        acc = x_ref[0]
        for j in range(1, k):
            acc = acc + x_ref[j]
        o_ref[...] = acc

    return pl.pallas_call(body, name=name, grid=(r // t,),
                          in_specs=[pl.BlockSpec((k, t, c), lambda i: (0, i, 0))],
                          out_specs=pl.BlockSpec((t, c), lambda i: (i, 0)),
                          out_shape=jax.ShapeDtypeStruct((r, c), F32), compiler_params=_params(("parallel",)))(x)


def _position():
    return lax.axis_index("x"), lax.axis_index("y"), lax.axis_index("c")


def _chip_peers(x, y):
    return [(1 - x, y), (x, 1 - y), (1 - x, 1 - y)]


def _comm_call(body, n_in, out_shape, n_remote, n_local, name):
    return pl.pallas_call(
        body, name=name, in_specs=[ANY] * n_in, out_specs=[ANY] * len(out_shape), out_shape=out_shape,
        scratch_shapes=[pltpu.SemaphoreType.DMA((n_remote,)), pltpu.SemaphoreType.DMA((n_remote,)),
                        pltpu.SemaphoreType.DMA((max(n_local, 1),))])


def _allgather_chips(arrs, name):
    n = len(arrs)

    def body(*refs):
        ins, outs = refs[:n], refs[n:2 * n]
        send, recv, loc = refs[2 * n:]
        x, y, c = _position()
        me = 2 * x + y
        local = [pltpu.make_async_copy(ins[a], outs[a].at[me], loc.at[a]) for a in range(n)]
        for cp in local:
            cp.start()
        remote = []
        for a in range(n):
            for j, (px, py) in enumerate(_chip_peers(x, y)):
                remote.append(pltpu.make_async_remote_copy(
                    src_ref=ins[a], dst_ref=outs[a].at[me], send_sem=send.at[3 * a + j], recv_sem=recv.at[3 * a + j],
                    device_id=(px, py, c), device_id_type=MESH))
        for cp in remote:
            cp.start()
        for cp in remote:
            cp.wait()
        for cp in local:
            cp.wait()

    out_shape = [jax.ShapeDtypeStruct((4,) + a.shape, a.dtype) for a in arrs]
    return _comm_call(body, n, out_shape, 3 * n, n, name)(*arrs)


def _allgather_all(v, name):
    def body(v_ref, o_ref, send, recv, loc):
        x, y, c = _position()
        me = 4 * x + 2 * y + c
        local = pltpu.make_async_copy(v_ref, o_ref.at[me], loc.at[0])
        local.start()
        remote = []
        for k in range(1, 8):
            fx, fy, fc = (k >> 2) & 1, (k >> 1) & 1, k & 1
            peer = (x ^ fx, y ^ fy, c ^ fc)
            remote.append(pltpu.make_async_remote_copy(
                src_ref=v_ref, dst_ref=o_ref.at[me], send_sem=send.at[k - 1], recv_sem=recv.at[k - 1],
                device_id=peer, device_id_type=MESH))
        for cp in remote:
            cp.start()
        for cp in remote:
            cp.wait()
        local.wait()

    return _comm_call(body, 1, [jax.ShapeDtypeStruct((8,) + v.shape, v.dtype)], 7, 1, name)(v)[0]


def _pair_exchange(gs, name):
    n = len(gs)

    def body(*refs):
        ins, outs = refs[:n], refs[n:2 * n]
        send, recv, loc = refs[2 * n:]
        x, y, c = _position()
        copies = []
        for a in range(n):
            half = gs[a].shape[1] // 2
            mine = ins[a].at[:, pl.ds(pl.multiple_of(c * half, 8), half), :]
            theirs = ins[a].at[:, pl.ds(pl.multiple_of((1 - c) * half, 8), half), :]
            copies.append(pltpu.make_async_copy(mine, outs[a].at[c], loc.at[a]))
            copies.append(pltpu.make_async_remote_copy(
                src_ref=theirs, dst_ref=outs[a].at[c], send_sem=send.at[a], recv_sem=recv.at[a],
                device_id=(x, y, 1 - c), device_id_type=MESH))
        for cp in copies:
            cp.start()
        for cp in copies:
            cp.wait()

    out_shape = [jax.ShapeDtypeStruct((2, 4, g.shape[1] // 2, g.shape[2]), g.dtype) for g in gs]
    return _comm_call(body, n, out_shape, n, n, name)(*gs)


def _chip_exchange(ps, name):
    n = len(ps)

    def body(*refs):
        ins, outs = refs[:n], refs[n:2 * n]
        send, recv, loc = refs[2 * n:]
        x, y, c = _position()
        me = 2 * x + y
        copies = []
        for a in range(n):
            copies.append(pltpu.make_async_copy(ins[a].at[me], outs[a].at[me], loc.at[a]))
            for j, (px, py) in enumerate(_chip_peers(x, y)):
                copies.append(pltpu.make_async_remote_copy(
                    src_ref=ins[a].at[2 * px + py], dst_ref=outs[a].at[me],
                    send_sem=send.at[3 * a + j], recv_sem=recv.at[3 * a + j],
                    device_id=(px, py, c), device_id_type=MESH))
        for cp in copies:
            cp.start()
        for cp in copies:
            cp.wait()

    out_shape = [jax.ShapeDtypeStruct(p.shape, p.dtype) for p in ps]
    return _comm_call(body, n, out_shape, 3 * n, n, name)(*ps)


def _pair_allgather(rs, name):
    n = len(rs)

    def body(*refs):
        ins, outs = refs[:n], refs[n:2 * n]
        send, recv, loc = refs[2 * n:]
        x, y, c = _position()
        copies = []
        for a in range(n):
            half = rs[a].shape[0]
            dst = outs[a].at[pl.ds(pl.multiple_of(c * half, 8), half), :]
            copies.append(pltpu.make_async_copy(ins[a], dst, loc.at[a]))
            copies.append(pltpu.make_async_remote_copy(
                src_ref=ins[a], dst_ref=dst, send_sem=send.at[a], recv_sem=recv.at[a],
                device_id=(x, y, 1 - c), device_id_type=MESH))
        for cp in copies:
            cp.start()
        for cp in copies:
            cp.wait()

    out_shape = [jax.ShapeDtypeStruct((2 * r.shape[0], r.shape[1]), r.dtype) for r in rs]
    return _comm_call(body, n, out_shape, n, n, name)(*rs)


def _reduce_scatter(gs):
    pairs = _pair_exchange(gs, "rs_pair_exchange")
    ps = [_sum_leading(p.reshape(2, -1, p.shape[-1]), f"rs_pair_sum_{a}").reshape(p.shape[1:])
          for a, p in enumerate(pairs)]
    qs = _chip_exchange(ps, "rs_chip_exchange")
    rs = [_sum_leading(q, f"rs_chip_sum_{a}") for a, q in enumerate(qs)]
    return _pair_allgather(rs, "rs_pair_allgather")


def _pad_rows(a, rows):
    return jnp.pad(a, ((0, rows - a.shape[0]), (0, 0)))


def _shard_cols(a):
    r, c4 = a.shape
    return a.reshape(r, 4, c4 // 4).transpose(1, 0, 2)


def kernel(x, mem, w_in, b_forget, conv_a_w, conv_a_b, ln_a_g, ln_a_b, conv_b_w, w_kv_mem, mem_ln_g, mem_ln_b, p_a, p_b, p_c, p_m, w_out, ln_g, ln_b, loss_target, m_w_in, m_b_forget, m_conv_a_w, m_conv_a_b, m_ln_a_g, m_ln_a_b, m_conv_b_w, m_w_kv_mem, m_mem_ln_g, m_mem_ln_b, m_p_a, m_p_b, m_p_c, m_p_m, m_w_out, m_ln_g, m_ln_b, v_w_in, v_b_forget, v_conv_a_w, v_conv_a_b, v_ln_a_g, v_ln_a_b, v_conv_b_w, v_w_kv_mem, v_mem_ln_g, v_mem_ln_b, v_p_a, v_p_b, v_p_c, v_p_m, v_w_out, v_ln_g, v_ln_b):
    depth = w_in.shape[0]
    x0 = x[0]
    s, d = x0.shape
    aw = conv_a_w.shape[2] * 4
    mw = p_m.shape[1]
    n_head = b_forget.shape[1]
    alpha = (2.0 * depth) ** 0.25
    in_cols = w_in.shape[2] * 4
    assert aw == n_head * HEAD_DIM and mw % LANES == 0 and in_cols == 11 * aw + n_head + 2 * mw + 4 * d
    cf0 = 10 * aw
    n_main = in_cols - n_head
    fblk = n_main // LANES
    n_pad = n_main + LANES
    gblk = (11 * aw + 2 * mw) // d
    mblk = (11 * aw) // (2 * mw)
    qblk, kblk, vblk = 7 * aw // LANES, 8 * aw // LANES, 9 * aw // LANES
    assert (11 * aw + 2 * mw) % d == 0 and (11 * aw) % (2 * mw) == 0

    gathered = _allgather_chips(
        [w_in.astype(BF16), p_a.astype(BF16), p_b.astype(BF16), p_c.astype(BF16), p_m.astype(BF16),
         w_kv_mem.astype(BF16), w_out.astype(BF16), conv_a_w, conv_b_w], "gather_weights")
    w_in_g, p_a_g, p_b_g, p_c_g, p_m_g, w_kv_g, w_out_g, conv_a_g, conv_b_g = gathered

    def cols(g):
        return jnp.concatenate([g[j] for j in range(4)], axis=-1)

    def rows(g):
        return jnp.concatenate([g[j] for j in range(4)], axis=-2)

    w_full = cols(w_in_g)
    w_pad = jnp.concatenate([w_full[:, :, :cf0], w_full[:, :, cf0 + n_head:], w_full[:, :, cf0:cf0 + n_head],
                             jnp.zeros((depth, d, LANES - n_head), BF16)], axis=-1)
    pcat = jnp.concatenate([cols(p_a_g), cols(p_b_g), cols(p_c_g), cols(p_m_g)], axis=1)
    w_kv = rows(w_kv_g)
    wout = rows(w_out_g)
    conv_a = jnp.pad(cols(conv_a_g), ((0, 0), (0, HALO_A - CONV_A), (0, 0)))
    conv_b = jnp.pad(cols(conv_b_g), ((0, 0), (0, HALO_B - CONV_B), (0, 0)))
    bf_pad = jnp.pad(b_forget, ((0, 0), (0, LANES - n_head)))
    pieces = [(0, 3 * aw), (3 * aw, 7 * aw), (7 * aw, 10 * aw), (10 * aw, 11 * aw),
              (11 * aw, 11 * aw + 2 * mw), (11 * aw + 2 * mw, n_main), (n_main, n_pad)]

    mem_n = _ln_rows(mem[0], mem_ln_g[None], mem_ln_b[None], "mem_ln")

    xs, saved = [x0], []
    for l in range(depth):
        xl = xs[-1]
        proj = _mm(xl, w_pad[l], name=f"proj_{l}")
        za = _branch_a_fwd(proj, conv_a[l], conv_a_b[l][None], ln_a_g[l][None], ln_a_b[l][None], f"a_fwd_{l}")
        zb = _branch_b_fwd(proj, conv_b[l], f"b_fwd_{l}")
        cum, cumt, sgt = _forget_prep(proj, bf_pad[l][None], fblk, f"forget_prep_{l}")
        o_c, lse = _fox_fwd(proj, cum, cumt, qblk, kblk, vblk, f"fox_fwd_{l}")
        zc = _gate_mul(proj, o_c, 10, f"c_gate_{l}")
        mkv = _mm(mem_n, w_kv[l], name=f"mkv_{l}")
        zm = _branch_m_fwd(proj, mkv, mblk, f"m_fwd_{l}")
        zs = [za, zb, zc, zm]
        y = _merge_fwd(zs, proj, gblk, pcat[l], wout[l], xl, ln_g[l][None], ln_b[l][None], alpha, f"merge_fwd_{l}")
        xs.append(y)
        saved.append((proj, zs, cum, cumt, sgt, o_c, lse, mkv))

    dy, loss_part = _loss_head(xs[-1], loss_target[0], "loss_head")

    g_w_in, g_conv_a, g_conv_b, g_w_kv, g_pcat, g_wout = [], [], [], [], [], []
    small = []
    dmem_n = None
    for l in reversed(range(depth)):
        proj, zs, cum, cumt, sgt, o_c, lse, mkv = saved[l]
        xl = xs[l]
        (dx, d_g, dza, dzb, dzc, dzm, dpcat, dwout, dlng, dlnb) = _merge_bwd(
            zs, proj, gblk, pcat[l], wout[l], xl, ln_g[l][None], ln_b[l][None], dy, alpha, f"merge_bwd_{l}")
        d_m, dmkv = _branch_m_bwd(proj, mkv, dzm, mblk, f"m_bwd_{l}")
        g_w_kv.append(_mm(mem_n, dmkv, ta=True, name=f"dwkv_{l}"))
        dmem_n = _mm(dmkv, w_kv[l], tb=True, add=dmem_n, name=f"dmem_{l}")
        do, d_cg, delta = _fox_bwd_prep(proj, dzc, o_c, 10, f"fox_bwd_prep_{l}")
        dq, dk, dv, dcumt4, dcumq4 = _fox_bwd(proj, do, lse, delta, cum, cumt, qblk, kblk, vblk, f"fox_bwd_{l}")
        d_f, dbf = _forget_bwd(dcumt4, dcumq4, sgt, f"forget_bwd_{l}")
        d_b, dconv_b = _branch_b_bwd(proj, dzb, conv_b[l], f"b_bwd_{l}")
        d_a, dconv_a, dconv_ab, dlag, dlab = _branch_a_bwd(
            proj, dza, conv_a[l], conv_a_b[l][None], ln_a_g[l][None], ln_a_b[l][None], f"a_bwd_{l}")
        dparts = [d_a, d_b, jnp.concatenate([dq, dk, dv], axis=1), d_cg, d_m, d_g, d_f]
        dw_parts = []
        for k, (dp, (lo, hi)) in enumerate(zip(dparts, pieces)):
            wp = w_pad[l][:, lo:hi]
            dx = _mm(dp, wp, tb=True, add=dx, name=f"dx_{l}_{k}")
            dw_parts.append(_mm(xl, dp, ta=True, name=f"dw_{l}_{k}"))
        dw = jnp.concatenate(dw_parts[:3] + [dw_parts[6][:, :n_head]] + dw_parts[3:6], axis=1)
        g_w_in.append(dw)
        g_conv_a.append(dconv_a)
        g_conv_b.append(dconv_b)
        g_pcat.append(dpcat)
        g_wout.append(dwout)
        small.append([dbf[:, 0], dconv_ab[0], dlag[0], dlab[0], dlng[0], dlnb[0]])
        dy = dx
    grad_x = dy
    dmlg, dmlb = _ln_rows_param_grads(mem[0], dmem_n, "mem_ln_grads")
    for lst in (g_w_in, g_conv_a, g_conv_b, g_w_kv, g_pcat, g_wout, small):
        lst.reverse()

    pa_end, pb_end, pc_end = aw, 2 * aw, 3 * aw
    rs_in = [
        _shard_cols(jnp.concatenate(g_w_in, axis=0)),
        _shard_cols(jnp.concatenate(g_conv_a, axis=0)),
        _shard_cols(jnp.concatenate([_pad_rows(g, 2 * HALO_B) for g in g_conv_b], axis=0)),
        jnp.concatenate([g.reshape(4, g.shape[0] // 4, g.shape[1]) for g in g_w_kv], axis=1),
        _shard_cols(jnp.concatenate([g[:pa_end] for g in g_pcat], axis=0)),
        _shard_cols(jnp.concatenate([g[pa_end:pb_end] for g in g_pcat], axis=0)),
        _shard_cols(jnp.concatenate([g[pb_end:pc_end] for g in g_pcat], axis=0)),
        _shard_cols(jnp.concatenate([g[pc_end:] for g in g_pcat], axis=0)),
        jnp.concatenate([g.reshape(4, g.shape[0] // 4, g.shape[1]) for g in g_wout], axis=1),
    ]
    rs_out = _reduce_scatter(rs_in)
    gw_in = rs_out[0].reshape(depth, d, -1)
    g_ca = rs_out[1].reshape(depth, HALO_A, -1)[:, :CONV_A]
    g_cb = rs_out[2].reshape(depth, 2 * HALO_B, -1)[:, :CONV_B]
    gw_kv = rs_out[3].reshape(depth, -1, 2 * mw)
    gp_a = rs_out[4].reshape(depth, aw, -1)
    gp_b = rs_out[5].reshape(depth, aw, -1)
    gp_c = rs_out[6].reshape(depth, aw, -1)
    gp_m = rs_out[7].reshape(depth, mw, -1)
    gw_out = rs_out[8].reshape(depth, -1, d)

    flat = jnp.concatenate([jnp.concatenate(p) for p in small] + [dmlg[0], dmlb[0], loss_part[0, 0:1]])
    n_small = flat.shape[0]
    n_rows = -(-n_small // (8 * LANES)) * 8
    vec = jnp.pad(flat, (0, n_rows * LANES - n_small)).reshape(n_rows, LANES)
    tot = _sum_leading(_allgather_all(vec, "gather_small"), "sum_small").reshape(-1)
    per_layer = n_head + 3 * aw + 2 * d
    tl = tot[:depth * per_layer].reshape(depth, per_layer)
    offs = np.cumsum([0, n_head, aw, aw, aw, d, d])
    g_bf, g_cab, g_lag, g_lab, g_lg, g_lb = [tl[:, offs[k]:offs[k + 1]] for k in range(6)]
    base = depth * per_layer
    g_mlg, g_mlb = tot[base:base + d], tot[base + d:base + 2 * d]
    loss = tot[base + 2 * d]

    grads = [gw_in, g_bf, g_ca, g_cab, g_lag, g_lab, g_cb, gw_kv, g_mlg, g_mlb, gp_a, gp_b, gp_c, gp_m, gw_out, g_lg, g_lb]
    ws = [w_in, b_forget, conv_a_w, conv_a_b, ln_a_g, ln_a_b, conv_b_w, w_kv_mem, mem_ln_g, mem_ln_b, p_a, p_b, p_c, p_m, w_out, ln_g, ln_b]
    ms = [m_w_in, m_b_forget, m_conv_a_w, m_conv_a_b, m_ln_a_g, m_ln_a_b, m_conv_b_w, m_w_kv_mem, m_mem_ln_g, m_mem_ln_b, m_p_a, m_p_b, m_p_c, m_p_m, m_w_out, m_ln_g, m_ln_b]
    vs = [v_w_in, v_b_forget, v_conv_a_w, v_conv_a_b, v_ln_a_g, v_ln_a_b, v_conv_b_w, v_w_kv_mem, v_mem_ln_g, v_mem_ln_b, v_p_a, v_p_b, v_p_c, v_p_m, v_w_out, v_ln_g, v_ln_b]
    deltas, new_ms, new_vs = [], [], []
    for k, (wk, gk, mk, vk) in enumerate(zip(ws, grads, ms, vs)):
        shape = wk.shape
        two_d = (1, shape[0]) if wk.ndim == 1 else (int(np.prod(shape[:-1])), shape[-1])
        dk_, nm_, nv_ = _adamw(wk.reshape(two_d), gk.reshape(two_d), mk.reshape(two_d), vk.reshape(two_d), f"adamw_{k}")
        deltas.append(dk_.reshape(shape))
        new_ms.append(nm_.reshape(shape))
        new_vs.append(nv_.reshape(shape))
        grads[k] = gk.reshape(shape)
    return (loss, grad_x[None], *grads, *deltas, *new_ms, *new_vs)


def _gate_mul(proj, o, gblk, name):
    s, w = o.shape
    t = T_ELEM

    def body(g_ref, o_ref, z_ref):
        z_ref[...] = (o_ref[...] * _silu(g_ref[...])).astype(BF16)

    row = pl.BlockSpec((t, w), lambda i: (i, 0))
    return pl.pallas_call(body, name=name, grid=(s // t,), in_specs=[pl.BlockSpec((t, w), lambda i: (i, gblk)), row],
                          out_specs=row, out_shape=jax.ShapeDtypeStruct((s, w), BF16),
                          compiler_params=_params(("parallel",)))(proj, o)
```

```python
import functools
import math

import numpy as np
import jax
import jax.numpy as jnp
from jax import lax
from jax.experimental import pallas as pl
from jax.experimental.pallas import tpu as pltpu

F32 = jnp.float32
BF16 = jnp.bfloat16
MESH = pl.DeviceIdType.MESH
ANY = pl.BlockSpec(memory_space=pl.ANY)

LN_EPS = 1e-5
NEG_BIG = -1e30
HEAD_DIM = 64
LANES = 128
CONV_A = 31
CONV_B = 3
HALO_A = 32
HALO_B = 8
CHUNK = 32
VMEM_LIMIT = 60 * 1024 * 1024

ADAM_LR, ADAM_B1, ADAM_B2, ADAM_EPS, ADAM_WD, ADAM_STEP = 0.001, 0.9, 0.999, 1e-08, 0.01, 10

T_MM = 512
T_A = 128
T_B = 256
T_ATT_Q = 512
T_ATT_K = 1024
T_CUM = 512
T_M = 512
T_MERGE = 256
T_ELEM = 512


def _pick(n, prefs):
    for p in prefs:
        if n % p == 0:
            return p
    return n


def _params(sem=None):
    return pltpu.CompilerParams(dimension_semantics=sem, vmem_limit_bytes=VMEM_LIMIT)


def _sigmoid(x):
    return jax.nn.sigmoid(x)


def _silu(x):
    return x * _sigmoid(x)


def _dsilu(x):
    s = _sigmoid(x)
    return s * (1.0 + x * (1.0 - s))


def _dot(a, b, dims, precision=None):
    return lax.dot_general(a, b, (dims, ((), ())), preferred_element_type=F32, precision=precision)


NN = ((1,), (0,))
NT = ((1,), (1,))
TN = ((0,), (0,))


def _iota(shape, dim):
    return lax.broadcasted_iota(jnp.int32, shape, dim)


def _mm(a, b, *, ta=False, tb=False, add=None, out_dtype=F32, name):
    m = a.shape[1] if ta else a.shape[0]
    k = a.shape[0] if ta else a.shape[1]
    n = b.shape[0] if tb else b.shape[1]
    tm = _pick(m, (1024, 512, 256)) if ta else _pick(m, (T_MM, 256))
    tn = _pick(n, (1152, 1024, 768, 512, 384, 256, 128))
    tk = _pick(k, (1024, 512, 256)) if not ta else _pick(k, (512, 256))
    nk = k // tk
    dims = ((0,) if ta else (1,), (1,) if tb else (0,))

    def body(*refs):
        if add is None:
            a_ref, b_ref, o_ref, acc_ref = refs
        else:
            a_ref, b_ref, add_ref, o_ref, acc_ref = refs
        kk = pl.program_id(2)
        p = _dot(a_ref[...].astype(BF16), b_ref[...].astype(BF16), dims)

        @pl.when(kk == 0)
        def _():
            acc_ref[...] = p

        @pl.when(kk > 0)
        def _():
            acc_ref[...] += p

        @pl.when(kk == nk - 1)
        def _():
            r = acc_ref[...]
            if add is not None:
                r = r + add_ref[...]
            o_ref[...] = r.astype(out_dtype)

    a_spec = (pl.BlockSpec((tk, tm), lambda j, i, kk: (kk, i)) if ta
              else pl.BlockSpec((tm, tk), lambda j, i, kk: (i, kk)))
    b_spec = (pl.BlockSpec((tn, tk), lambda j, i, kk: (j, kk)) if tb
              else pl.BlockSpec((tk, tn), lambda j, i, kk: (kk, j)))
    o_spec = pl.BlockSpec((tm, tn), lambda j, i, kk: (i, j))
    in_specs = [a_spec, b_spec] + ([o_spec] if add is not None else [])
    args = (a, b) + ((add,) if add is not None else ())
    return pl.pallas_call(
        body, name=name, grid=(n // tn, m // tm, nk), in_specs=in_specs, out_specs=o_spec,
        out_shape=jax.ShapeDtypeStruct((m, n), out_dtype),
        scratch_shapes=[pltpu.VMEM((tm, tn), F32)],
        compiler_params=_params(("parallel", "parallel", "arbitrary")),
    )(*args)


def _ln_rows(x, g, b, name):
    r, d = x.shape
    t = _pick(r, (256,))

    def body(x_ref, g_ref, b_ref, o_ref):
        xv = x_ref[...]
        mu = jnp.mean(xv, axis=-1, keepdims=True)
        dv = xv - mu
        var = jnp.mean(dv * dv, axis=-1, keepdims=True)
        o_ref[...] = dv * lax.rsqrt(var + LN_EPS) * g_ref[...] + b_ref[...]

    row = pl.BlockSpec((t, d), lambda i: (i, 0))
    vec = pl.BlockSpec((1, d), lambda i: (0, 0))
    return pl.pallas_call(body, name=name, grid=(r // t,), in_specs=[row, vec, vec], out_specs=row,
                          out_shape=jax.ShapeDtypeStruct((r, d), F32), compiler_params=_params(("parallel",)))(x, g, b)


def _ln_rows_param_grads(x, dy, name):
    r, d = x.shape

    def body(x_ref, dy_ref, dg_ref, db_ref):
        xv = x_ref[...]
        mu = jnp.mean(xv, axis=-1, keepdims=True)
        dv = xv - mu
        var = jnp.mean(dv * dv, axis=-1, keepdims=True)
        xh = dv * lax.rsqrt(var + LN_EPS)
        dg_ref[...] = jnp.sum(dy_ref[...] * xh, axis=0, keepdims=True)
        db_ref[...] = jnp.sum(dy_ref[...], axis=0, keepdims=True)

    full = pl.BlockSpec((r, d), lambda i: (0, 0))
    vec = pl.BlockSpec((1, d), lambda i: (0, 0))
    o = jax.ShapeDtypeStruct((1, d), F32)
    return pl.pallas_call(body, name=name, grid=(1,), in_specs=[full, full], out_specs=[vec, vec],
                          out_shape=[o, o], compiler_params=_params(("arbitrary",)))(x, dy)


def _conv_a_chunk(glu_ref, cw_ref, r0):
    acc = cw_ref[0:1, :] * glu_ref[pl.ds(r0 + 2, CHUNK), :]
    for k in range(1, CONV_A):
        acc = acc + cw_ref[k:k + 1, :] * glu_ref[pl.ds(r0 + 2 + k, CHUNK), :]
    return acc


def _ln_stats(c):
    mu = jnp.mean(c, axis=-1, keepdims=True)
    d = c - mu
    var = jnp.mean(d * d, axis=-1, keepdims=True)
    rstd = lax.rsqrt(var + LN_EPS)
    return d * rstd, rstd


def _branch_a_fwd(proj, cw, cb, lg, lb, name):
    s = proj.shape[0]
    t = T_A
    w = cw.shape[1]
    r = t // HALO_A

    def body(u_ref, v_ref, gt_ref, hu_ref, hv_ref, cw_ref, cb_ref, lg_ref, lb_ref, z_ref, glu):
        i = pl.program_id(0)
        hglu = hu_ref[...] * _sigmoid(hv_ref[...])
        glu[0:HALO_A, :] = jnp.where(i > 0, hglu, 0.0)
        glu[HALO_A:HALO_A + t, :] = u_ref[...] * _sigmoid(v_ref[...])
        for c in range(t // CHUNK):
            r0 = c * CHUNK
            conv = _conv_a_chunk(glu, cw_ref, r0) + cb_ref[...]
            xh, _ = _ln_stats(conv)
            a3 = _silu(xh * lg_ref[...] + lb_ref[...])
            z_ref[r0:r0 + CHUNK, :] = (a3 * _silu(gt_ref[r0:r0 + CHUNK, :])).astype(BF16)

    def cur(col):
        return pl.BlockSpec((t, w), lambda i, col=col: (i, col))

    def prev(col):
        return pl.BlockSpec((HALO_A, w), lambda i, col=col: (jnp.maximum(i * r - 1, 0), col))

    vec = pl.BlockSpec((1, w), lambda i: (0, 0))
    return pl.pallas_call(
        body, name=name, grid=(s // t,),
        in_specs=[cur(0), cur(1), cur(2), prev(0), prev(1), pl.BlockSpec((HALO_A, w), lambda i: (0, 0)), vec, vec, vec],
        out_specs=pl.BlockSpec((t, w), lambda i: (i, 0)),
        out_shape=jax.ShapeDtypeStruct((s, w), BF16),
        scratch_shapes=[pltpu.VMEM((HALO_A + t, w), F32)],
        compiler_params=_params(("parallel",)),
    )(proj, proj, proj, proj, proj, cw, cb, lg, lb)


def _branch_a_bwd(proj, dz, cw, cb, lg, lb, name):
    s = proj.shape[0]
    t = T_A
    w = cw.shape[1]
    r = t // HALO_A
    n = s // t
    nblk = s // HALO_A
    ext = t + HALO_A

    def body(u_ref, v_ref, gt_ref, dz_ref, pu_ref, pv_ref, nu_ref, nv_ref, ngt_ref, ndz_ref,
             cw_ref, cb_ref, lg_ref, lb_ref, da_ref, dw_ref, dcb_ref, dlg_ref, dlb_ref, glu, dc, dw8):
        i = pl.program_id(0)

        @pl.when(i == 0)
        def _():
            dw8[...] = jnp.zeros_like(dw8)
            dcb_ref[...] = jnp.zeros_like(dcb_ref)
            dlg_ref[...] = jnp.zeros_like(dlg_ref)
            dlb_ref[...] = jnp.zeros_like(dlb_ref)

        glu[0:HALO_A, :] = jnp.where(i > 0, pu_ref[...] * _sigmoid(pv_ref[...]), 0.0)
        glu[HALO_A:HALO_A + t, :] = u_ref[...] * _sigmoid(v_ref[...])
        glu[HALO_A + t:HALO_A + t + HALO_A, :] = nu_ref[...] * _sigmoid(nv_ref[...])
        has_next = i < n - 1
        dcb = jnp.zeros((1, w), F32)
        dlg = jnp.zeros((1, w), F32)
        dlb = jnp.zeros((1, w), F32)
        for c in range(ext // CHUNK):
            r0 = c * CHUNK
            own = r0 < t
            conv = _conv_a_chunk(glu, cw_ref, r0) + cb_ref[...]
            xh, rstd = _ln_stats(conv)
            a2 = xh * lg_ref[...] + lb_ref[...]
            if own:
                gt = gt_ref[r0:r0 + CHUNK, :]
                dzc = dz_ref[r0:r0 + CHUNK, :]
            else:
                gt = ngt_ref[...]
                dzc = ndz_ref[...]
            da2 = dzc * _silu(gt) * _dsilu(a2)
            dxh = da2 * lg_ref[...]
            dconv = rstd * (dxh - jnp.mean(dxh, axis=-1, keepdims=True)
                            - xh * jnp.mean(dxh * xh, axis=-1, keepdims=True))
            if own:
                dc[r0:r0 + CHUNK, :] = dconv
                da_ref[r0:r0 + CHUNK, 2 * w:3 * w] = (dzc * _silu(a2) * _dsilu(gt)).astype(BF16)
                dcb = dcb + jnp.sum(dconv, axis=0, keepdims=True)
                dlg = dlg + jnp.sum(da2 * xh, axis=0, keepdims=True)
                dlb = dlb + jnp.sum(da2, axis=0, keepdims=True)
            else:
                dc[r0:r0 + CHUNK, :] = jnp.where(has_next, dconv, 0.0)
        dcb_ref[...] += dcb
        dlg_ref[...] += dlg
        dlb_ref[...] += dlb
        for c in range(t // CHUNK):
            r0 = c * CHUNK
            dcc = dc[r0:r0 + CHUNK, :]
            dglu = cw_ref[0:1, :] * dc[pl.ds(r0 + CONV_A - 1, CHUNK), :]
            for k in range(1, CONV_A):
                dglu = dglu + cw_ref[k:k + 1, :] * dc[pl.ds(r0 + CONV_A - 1 - k, CHUNK), :]
            for k in range(CONV_A):
                prod = dcc * glu[pl.ds(r0 + 2 + k, CHUNK), :]
                dw8[k] += jnp.sum(prod.reshape(CHUNK // 8, 8, w), axis=0)
            sv = _sigmoid(v_ref[r0:r0 + CHUNK, :])
            da_ref[r0:r0 + CHUNK, 0:w] = (dglu * sv).astype(BF16)
            da_ref[r0:r0 + CHUNK, w:2 * w] = (dglu * u_ref[r0:r0 + CHUNK, :] * sv * (1.0 - sv)).astype(BF16)

        @pl.when(i == n - 1)
        def _():
            dw_ref[...] = jnp.sum(dw8[...], axis=1)

    def cur(col):
        return pl.BlockSpec((t, w), lambda i, col=col: (i, col))

    def prev(col):
        return pl.BlockSpec((HALO_A, w), lambda i, col=col: (jnp.maximum(i * r - 1, 0), col))

    def nxt(col):
        return pl.BlockSpec((HALO_A, w), lambda i, col=col: (jnp.minimum((i + 1) * r, nblk - 1), col))

    vec = pl.BlockSpec((1, w), lambda i: (0, 0))
    vo = jax.ShapeDtypeStruct((1, w), F32)
    return pl.pallas_call(
        body, name=name, grid=(n,),
        in_specs=[cur(0), cur(1), cur(2), pl.BlockSpec((t, w), lambda i: (i, 0)),
                  prev(0), prev(1), nxt(0), nxt(1), nxt(2),
                  pl.BlockSpec((HALO_A, w), lambda i: (jnp.minimum((i + 1) * r, nblk - 1), 0)),
                  pl.BlockSpec((HALO_A, w), lambda i: (0, 0)), vec, vec, vec],
        out_specs=[pl.BlockSpec((t, 3 * w), lambda i: (i, 0)), pl.BlockSpec((HALO_A, w), lambda i: (0, 0)), vec, vec, vec],
        out_shape=[jax.ShapeDtypeStruct((s, 3 * w), BF16), jax.ShapeDtypeStruct((HALO_A, w), F32), vo, vo, vo],
        scratch_shapes=[pltpu.VMEM((t + 2 * HALO_A, w), F32), pltpu.VMEM((ext, w), F32),
                        pltpu.VMEM((HALO_A, 8, w), F32)],
        compiler_params=_params(("arbitrary",)),
    )(proj, proj, proj, dz, proj, proj, proj, proj, proj, dz, cw, cb, lg, lb)


def _conv_b(u_ext, cw_ref, t):
    acc = cw_ref[0:1, :] * u_ext[pl.ds(HALO_B - 2, t), :]
    for k in range(1, CONV_B):
        acc = acc + cw_ref[k:k + 1, :] * u_ext[pl.ds(HALO_B - 2 + k, t), :]
    return acc


def _branch_b_fwd(proj, cw, name):
    s = proj.shape[0]
    t = T_B
    w = cw.shape[1]
    r = t // HALO_B

    def body(h_ref, b_ref, c_ref, gt_ref, ph_ref, pc_ref, cw_ref, z_ref, u_ext):
        i = pl.program_id(0)
        u_ext[0:HALO_B, :] = jnp.where(i > 0, pc_ref[...] * ph_ref[...], 0.0)
        u_ext[HALO_B:HALO_B + t, :] = c_ref[...] * h_ref[...]
        cv = _conv_b(u_ext, cw_ref, t)
        z_ref[...] = (b_ref[...] * cv * _silu(gt_ref[...])).astype(BF16)

    def cur(col):
        return pl.BlockSpec((t, w), lambda i, col=col: (i, col))

    def prev(col):
        return pl.BlockSpec((HALO_B, w), lambda i, col=col: (jnp.maximum(i * r - 1, 0), col))

    return pl.pallas_call(
        body, name=name, grid=(s // t,),
        in_specs=[cur(3), cur(4), cur(5), cur(6), prev(3), prev(5), pl.BlockSpec((HALO_B, w), lambda i: (0, 0))],
        out_specs=pl.BlockSpec((t, w), lambda i: (i, 0)),
        out_shape=jax.ShapeDtypeStruct((s, w), BF16),
        scratch_shapes=[pltpu.VMEM((HALO_B + t, w), F32)],
        compiler_params=_params(("parallel",)),
    )(proj, proj, proj, proj, proj, proj, cw)


def _branch_b_bwd(proj, dz, cw, name):
    s = proj.shape[0]
    t = T_B
    w = cw.shape[1]
    r = t // HALO_B
    n = s // t
    nblk = s // HALO_B

    def body(h_ref, b_ref, c_ref, gt_ref, dz_ref, ph_ref, pc_ref, nb_ref, ngt_ref, ndz_ref, cw_ref,
             db_ref, dw_ref, u_ext, dcv_ext, dw8):
        i = pl.program_id(0)

        @pl.when(i == 0)
        def _():
            dw8[...] = jnp.zeros_like(dw8)

        u_ext[0:HALO_B, :] = jnp.where(i > 0, pc_ref[...] * ph_ref[...], 0.0)
        u_ext[HALO_B:HALO_B + t, :] = c_ref[...] * h_ref[...]
        cv = _conv_b(u_ext, cw_ref, t)
        gt = gt_ref[...]
        dhb = dz_ref[...] * _silu(gt)
        db_ref[:, 3 * w:4 * w] = (dz_ref[...] * b_ref[...] * cv * _dsilu(gt)).astype(BF16)
        db_ref[:, w:2 * w] = (dhb * cv).astype(BF16)
        dcv = dhb * b_ref[...]
        dcv_ext[0:t, :] = dcv
        ndcv = ndz_ref[...] * _silu(ngt_ref[...]) * nb_ref[...]
        dcv_ext[t:t + HALO_B, :] = jnp.where(i < n - 1, ndcv, 0.0)
        du = cw_ref[0:1, :] * dcv_ext[pl.ds(2, t), :]
        for k in range(1, CONV_B):
            du = du + cw_ref[k:k + 1, :] * dcv_ext[pl.ds(2 - k, t), :]
        db_ref[:, 2 * w:3 * w] = (du * h_ref[...]).astype(BF16)
        db_ref[:, 0:w] = (du * c_ref[...]).astype(BF16)
        for k in range(CONV_B):
            prod = dcv * u_ext[pl.ds(HALO_B - 2 + k, t), :]
            dw8[k] += jnp.sum(prod.reshape(t // 8, 8, w), axis=0)

        @pl.when(i == n - 1)
        def _():
            dw_ref[...] = jnp.sum(dw8[...], axis=1)

    def cur(col):
        return pl.BlockSpec((t, w), lambda i, col=col: (i, col))

    def prev(col):
        return pl.BlockSpec((HALO_B, w), lambda i, col=col: (jnp.maximum(i * r - 1, 0), col))

    def nxt(col):
        return pl.BlockSpec((HALO_B, w), lambda i, col=col: (jnp.minimum((i + 1) * r, nblk - 1), col))

    return pl.pallas_call(
        body, name=name, grid=(n,),
        in_specs=[cur(3), cur(4), cur(5), cur(6), pl.BlockSpec((t, w), lambda i: (i, 0)),
                  prev(3), prev(5), nxt(4), nxt(6),
                  pl.BlockSpec((HALO_B, w), lambda i: (jnp.minimum((i + 1) * r, nblk - 1), 0)),
                  pl.BlockSpec((HALO_B, w), lambda i: (0, 0))],
        out_specs=[pl.BlockSpec((t, 4 * w), lambda i: (i, 0)), pl.BlockSpec((HALO_B, w), lambda i: (0, 0))],
        out_shape=[jax.ShapeDtypeStruct((s, 4 * w), BF16), jax.ShapeDtypeStruct((HALO_B, w), F32)],
        scratch_shapes=[pltpu.VMEM((HALO_B + t, w), F32), pltpu.VMEM((t + HALO_B, w), F32),
                        pltpu.VMEM((HALO_B, 8, w), F32)],
        compiler_params=_params(("arbitrary",)),
    )(proj, proj, proj, proj, dz, proj, proj, proj, proj, dz, cw)


def _forget_prep(proj, bf, fblk, name):
    s = proj.shape[0]
    t = T_CUM

    def body(f_ref, bf_ref, cum_ref, cumt_ref, sgt_ref, carry):
        i = pl.program_id(0)

        @pl.when(i == 0)
        def _():
            carry[...] = jnp.zeros_like(carry)

        z = f_ref[...] + bf_ref[...]
        logf = jnp.minimum(z, 0.0) - jnp.log1p(jnp.exp(-jnp.abs(z)))
        tri = (_iota((t, t), 0) >= _iota((t, t), 1)).astype(F32)
        cum = _dot(tri, logf, NN, precision=lax.Precision.HIGHEST) + carry[0:1, :]
        carry[0:1, :] = cum[t - 1:t, :]
        cum_ref[...] = cum
        cumt_ref[...] = cum.T[0:8, :]
        sgt_ref[...] = _sigmoid(-z).T[0:8, :]

    return pl.pallas_call(
        body, name=name, grid=(s // t,),
        in_specs=[pl.BlockSpec((t, LANES), lambda i: (i, fblk)), pl.BlockSpec((1, LANES), lambda i: (0, 0))],
        out_specs=[pl.BlockSpec((t, LANES), lambda i: (i, 0)), pl.BlockSpec((8, t), lambda i: (0, i)),
                   pl.BlockSpec((8, t), lambda i: (0, i))],
        out_shape=[jax.ShapeDtypeStruct((s, LANES), F32), jax.ShapeDtypeStruct((8, s), F32),
                   jax.ShapeDtypeStruct((8, s), F32)],
        scratch_shapes=[pltpu.VMEM((8, LANES), F32)],
        compiler_params=_params(("arbitrary",)),
    )(proj, bf)


def _lane_pick(x, lane):
    return jnp.sum(jnp.where(_iota(x.shape, 1) == lane, x, 0.0), axis=1, keepdims=True)


def _sublane_pick(x, row):
    return jnp.sum(jnp.where(_iota(x.shape, 0) == row, x, 0.0), axis=0, keepdims=True)


def _head_mask(hh):
    lane = _iota((1, LANES), 1)
    return (lane >= HEAD_DIM * hh) & (lane < HEAD_DIM * (hh + 1))


def _causal_pairs(nq, ratio, kv_major):
    if kv_major:
        pairs = [(q, k) for k in range(nq // ratio) for q in range(k * ratio, nq)]
    else:
        pairs = [(q, k) for q in range(nq) for k in range(q // ratio + 1)]
    qs = np.asarray([p[0] for p in pairs], np.int32)
    ks = np.asarray([p[1] for p in pairs], np.int32)
    return qs, ks, (ks == qs // ratio).astype(np.int32)


def _fox_scores(qm, kb, cum_ref, cumt_ref, h, qi, ki, diag, tq, tk):
    cq0 = _lane_pick(cum_ref[0:1, :], h)
    sc = _dot(qm, kb, NT) + (cq0 - _sublane_pick(cumt_ref[...], h))
    causal = (qi * tq + _iota((tq, tk), 0)) >= (ki * tk + _iota((tq, tk), 1))
    return jnp.where(causal | (diag == 0), sc, NEG_BIG)


def _fox_fwd(proj, cum, cumt, qblk, kblk, vblk, name):
    s = proj.shape[0]
    tq, tk = T_ATT_Q, T_ATT_K
    n_pair = 4
    qi_np, ki_np, diag_np = _causal_pairs(s // tq, tk // tq, kv_major=False)
    scale = HEAD_DIM ** -0.5

    def body(qi_ref, ki_ref, diag_ref, q_ref, k_ref, v_ref, cum_ref, cumt_ref, o_ref, lse_ref, m_s, l_s, acc_s):
        hp = pl.program_id(0)
        step = pl.program_id(1)
        qi, ki, diag = qi_ref[step], ki_ref[step], diag_ref[step]

        @pl.when(ki == 0)
        def _():
            m_s[...] = jnp.full_like(m_s, NEG_BIG)
            l_s[...] = jnp.zeros_like(l_s)
            acc_s[...] = jnp.zeros_like(acc_s)

        q = q_ref[...] * scale
        kb = k_ref[...].astype(BF16)
        vb = v_ref[...].astype(BF16)
        for hh in range(2):
            qm = jnp.where(_head_mask(hh), q, 0.0).astype(BF16)
            sc = _fox_scores(qm, kb, cum_ref, cumt_ref, 2 * hp + hh, qi, ki, diag, tq, tk)
            m_old = m_s[hh]
            m_new = jnp.maximum(m_old, jnp.max(sc, axis=1, keepdims=True))
            p = jnp.exp(sc - m_new)
            alpha = jnp.exp(m_old - m_new)
            l_s[hh] = alpha * l_s[hh] + jnp.sum(p, axis=1, keepdims=True)
            acc_s[hh] = alpha * acc_s[hh] + _dot(p.astype(BF16), vb, NN)
            m_s[hh] = m_new

        @pl.when(diag == 1)
        def _():
            lane = _iota((tq, LANES), 1)
            o_ref[...] = jnp.where(lane < HEAD_DIM, acc_s[0] / l_s[0], acc_s[1] / l_s[1])
            lse0 = m_s[0] + jnp.log(l_s[0])
            lse1 = m_s[1] + jnp.log(l_s[1])
            lse_ref[0] = jnp.where(lane == 0, lse0, jnp.where(lane == 1, lse1, 0.0))

    grid_spec = pltpu.PrefetchScalarGridSpec(
        num_scalar_prefetch=3, grid=(n_pair, len(qi_np)),
        in_specs=[pl.BlockSpec((tq, LANES), lambda hp, st, qi, ki, dg: (qi[st], qblk + hp)),
                  pl.BlockSpec((tk, LANES), lambda hp, st, qi, ki, dg: (ki[st], kblk + hp)),
                  pl.BlockSpec((tk, LANES), lambda hp, st, qi, ki, dg: (ki[st], vblk + hp)),
                  pl.BlockSpec((tq, LANES), lambda hp, st, qi, ki, dg: (qi[st], 0)),
                  pl.BlockSpec((8, tk), lambda hp, st, qi, ki, dg: (0, ki[st]))],
        out_specs=[pl.BlockSpec((tq, LANES), lambda hp, st, qi, ki, dg: (qi[st], hp)),
                   pl.BlockSpec((1, tq, LANES), lambda hp, st, qi, ki, dg: (hp, qi[st], 0))],
        scratch_shapes=[pltpu.VMEM((2, tq, 1), F32), pltpu.VMEM((2, tq, 1), F32), pltpu.VMEM((2, tq, LANES), F32)])
    return pl.pallas_call(
        body, name=name, grid_spec=grid_spec,
        out_shape=[jax.ShapeDtypeStruct((s, n_pair * LANES), F32), jax.ShapeDtypeStruct((n_pair, s, LANES), F32)],
        compiler_params=_params(("parallel", "arbitrary")),
    )(jnp.asarray(qi_np), jnp.asarray(ki_np), jnp.asarray(diag_np), proj, proj, proj, cum, cumt)


def _fox_bwd_prep(proj, dz, o, gblk, name):
    s, w = o.shape
    t = T_ELEM
    n_head = w // HEAD_DIM

    def body(gt_ref, dz_ref, o_ref, do_ref, dg_ref, dl_ref):
        gt = gt_ref[...]
        do = dz_ref[...] * _silu(gt)
        do_ref[...] = do
        dg_ref[...] = (dz_ref[...] * o_ref[...] * _dsilu(gt)).astype(BF16)
        sel = (_iota((w, LANES), 0) // HEAD_DIM == _iota((w, LANES), 1)).astype(F32)
        dl_ref[...] = _dot(do * o_ref[...], sel, NN, precision=lax.Precision.HIGHEST)

    assert n_head <= LANES
    row = pl.BlockSpec((t, w), lambda i: (i, 0))
    return pl.pallas_call(
        body, name=name, grid=(s // t,),
        in_specs=[pl.BlockSpec((t, w), lambda i: (i, gblk)), row, row],
        out_specs=[row, row, pl.BlockSpec((t, LANES), lambda i: (i, 0))],
        out_shape=[jax.ShapeDtypeStruct((s, w), F32), jax.ShapeDtypeStruct((s, w), BF16),
                   jax.ShapeDtypeStruct((s, LANES), F32)],
        compiler_params=_params(("parallel",)),
    )(proj, dz, o)


def _fox_bwd(proj, do, lse, delta, cum, cumt, qblk, kblk, vblk, name):
    s = proj.shape[0]
    tq, tk = T_ATT_Q, T_ATT_K
    nq = s // tq
    n_pair = 4
    qi_np, ki_np, diag_np = _causal_pairs(nq, tk // tq, kv_major=True)
    n_step = len(qi_np)
    scale = HEAD_DIM ** -0.5

    def body(qi_ref, ki_ref, diag_ref, q_ref, k_ref, v_ref, do_ref, lse_ref, dl_ref, cum_ref, cumt_ref,
             dq_ref, dk_ref, dv_ref, dct_ref, dcq_ref, dq_s, dk_s, dv_s, dc_s, dcq_s):
        hp = pl.program_id(0)
        step = pl.program_id(1)
        qi, ki, diag = qi_ref[step], ki_ref[step], diag_ref[step]

        @pl.when(step == 0)
        def _():
            dq_s[...] = jnp.zeros_like(dq_s)
            dcq_s[...] = jnp.zeros_like(dcq_s)

        @pl.when(qi == ki * (tk // tq))
        def _():
            dk_s[...] = jnp.zeros_like(dk_s)
            dv_s[...] = jnp.zeros_like(dv_s)
            dc_s[...] = jnp.zeros_like(dc_s)

        q = q_ref[...] * scale
        do = do_ref[...]
        kb = k_ref[...].astype(BF16)
        vb = v_ref[...].astype(BF16)
        sub = _iota((8, tk), 0)
        dq_new = jnp.zeros((tq, LANES), F32)
        dcq_new = jnp.zeros((tq, LANES), F32)
        lane = _iota((tq, LANES), 1)
        for hh in range(2):
            h = 2 * hp + hh
            hm = _head_mask(hh)
            qm = jnp.where(hm, q, 0.0).astype(BF16)
            dom = jnp.where(hm, do, 0.0).astype(BF16)
            sc = _fox_scores(qm, kb, cum_ref, cumt_ref, h, qi, ki, diag, tq, tk)
            p = jnp.exp(sc - _lane_pick(lse_ref[0], hh))
            dv_s[...] += _dot(p.astype(BF16), dom, TN)
            dp = _dot(dom, vb, NT)
            ds = p * (dp - _lane_pick(dl_ref[...], h))
            dc_s[...] += jnp.where(sub == h, -jnp.sum(ds, axis=0, keepdims=True), 0.0)
            dcq_new = dcq_new + jnp.where(lane == h, jnp.sum(ds, axis=1, keepdims=True), 0.0)
            dsb = ds.astype(BF16)
            dk_s[...] += _dot(dsb, qm, TN)
            dq_new = dq_new + jnp.where(hm, _dot(dsb, kb, NN), 0.0)
        row0 = pl.multiple_of(qi * tq, tq)
        dq_s[pl.ds(row0, tq), :] += dq_new * scale
        dcq_s[pl.ds(row0, tq), :] += dcq_new

        @pl.when(qi == nq - 1)
        def _():
            dk_ref[...] = dk_s[...].astype(BF16)
            dv_ref[...] = dv_s[...].astype(BF16)
            dct_ref[0] = dc_s[...]

        @pl.when(step == n_step - 1)
        def _():
            dq_ref[...] = dq_s[...].astype(BF16)
            dcq_ref[0] = dcq_s[...]

    grid_spec = pltpu.PrefetchScalarGridSpec(
        num_scalar_prefetch=3, grid=(n_pair, n_step),
        in_specs=[pl.BlockSpec((tq, LANES), lambda hp, st, qi, ki, dg: (qi[st], qblk + hp)),
                  pl.BlockSpec((tk, LANES), lambda hp, st, qi, ki, dg: (ki[st], kblk + hp)),
                  pl.BlockSpec((tk, LANES), lambda hp, st, qi, ki, dg: (ki[st], vblk + hp)),
                  pl.BlockSpec((tq, LANES), lambda hp, st, qi, ki, dg: (qi[st], hp)),
                  pl.BlockSpec((1, tq, LANES), lambda hp, st, qi, ki, dg: (hp, qi[st], 0)),
                  pl.BlockSpec((tq, LANES), lambda hp, st, qi, ki, dg: (qi[st], 0)),
                  pl.BlockSpec((tq, LANES), lambda hp, st, qi, ki, dg: (qi[st], 0)),
                  pl.BlockSpec((8, tk), lambda hp, st, qi, ki, dg: (0, ki[st]))],
        out_specs=[pl.BlockSpec((s, LANES), lambda hp, st, qi, ki, dg: (0, hp)),
                   pl.BlockSpec((tk, LANES), lambda hp, st, qi, ki, dg: (ki[st], hp)),
                   pl.BlockSpec((tk, LANES), lambda hp, st, qi, ki, dg: (ki[st], hp)),
                   pl.BlockSpec((1, 8, tk), lambda hp, st, qi, ki, dg: (hp, 0, ki[st])),
                   pl.BlockSpec((1, s, LANES), lambda hp, st, qi, ki, dg: (hp, 0, 0))],
        scratch_shapes=[pltpu.VMEM((s, LANES), F32), pltpu.VMEM((tk, LANES), F32), pltpu.VMEM((tk, LANES), F32),
                        pltpu.VMEM((8, tk), F32), pltpu.VMEM((s, LANES), F32)])
    w = n_pair * LANES
    return pl.pallas_call(
        body, name=name, grid_spec=grid_spec,
        out_shape=[jax.ShapeDtypeStruct((s, w), BF16), jax.ShapeDtypeStruct((s, w), BF16),
                   jax.ShapeDtypeStruct((s, w), BF16), jax.ShapeDtypeStruct((n_pair, 8, s), F32),
                   jax.ShapeDtypeStruct((n_pair, s, LANES), F32)],
        compiler_params=_params(("parallel", "arbitrary")),
    )(jnp.asarray(qi_np), jnp.asarray(ki_np), jnp.asarray(diag_np), proj, proj, proj, do, lse, delta, cum, cumt)


def _forget_bwd(dcumt4, dcumq4, sgt, name):
    s = sgt.shape[1]
    t = T_CUM
    n = s // t

    def body(d_ref, dq_ref, sg_ref, df_ref, dbf_ref, carry):
        i = pl.program_id(0)

        @pl.when(i == 0)
        def _():
            carry[...] = jnp.zeros_like(carry)
            dbf_ref[...] = jnp.zeros_like(dbf_ref)

        dq = dq_ref[0] + dq_ref[1] + dq_ref[2] + dq_ref[3]
        d = d_ref[0] + d_ref[1] + d_ref[2] + d_ref[3] + dq.T[0:8, :]
        upper = (_iota((t, t), 0) >= _iota((t, t), 1)).astype(F32)
        dlog = _dot(d, upper, NN, precision=lax.Precision.HIGHEST) + carry[:, 0:1]
        carry[...] += jnp.sum(d, axis=1, keepdims=True)
        dzt = dlog * sg_ref[...]
        dbf_ref[...] += jnp.sum(dzt, axis=1, keepdims=True)
        padded = jnp.concatenate([dzt, jnp.zeros((LANES - 8, t), F32)], axis=0)
        df_ref[...] = padded.T.astype(BF16)

    return pl.pallas_call(
        body, name=name, grid=(n,),
        in_specs=[pl.BlockSpec((4, 8, t), lambda i: (0, 0, n - 1 - i)),
                  pl.BlockSpec((4, t, LANES), lambda i: (0, n - 1 - i, 0)),
                  pl.BlockSpec((8, t), lambda i: (0, n - 1 - i))],
        out_specs=[pl.BlockSpec((t, LANES), lambda i: (n - 1 - i, 0)), pl.BlockSpec((8, LANES), lambda i: (0, 0))],
        out_shape=[jax.ShapeDtypeStruct((s, LANES), BF16), jax.ShapeDtypeStruct((8, LANES), F32)],
        scratch_shapes=[pltpu.VMEM((8, LANES), F32)],
        compiler_params=_params(("arbitrary",)),
    )(dcumt4, dcumq4, sgt)


def _mem_softmax(qm, kp):
    sc = _dot(qm, kp, NT) * (HEAD_DIM ** -0.5)
    e = jnp.exp(sc - jnp.max(sc, axis=1, keepdims=True))
    return e / jnp.sum(e, axis=1, keepdims=True)


def _branch_m_fwd(proj, mkv, mblk, name):
    s = proj.shape[0]
    t = T_M
    mw = mkv.shape[1] // 2
    ml = mkv.shape[0]

    def body(m_ref, kv_ref, z_ref):
        outs = []
        for pr in range(mw // LANES):
            qp = m_ref[:, pr * LANES:(pr + 1) * LANES]
            kp = kv_ref[:, pr * LANES:(pr + 1) * LANES].astype(BF16)
            vp = kv_ref[:, mw + pr * LANES:mw + (pr + 1) * LANES].astype(BF16)
            oh = []
            for hh in range(2):
                qm = jnp.where(_head_mask(hh), qp, 0.0).astype(BF16)
                oh.append(_dot(_mem_softmax(qm, kp).astype(BF16), vp, NN))
            outs.append(jnp.where(_iota((t, LANES), 1) < HEAD_DIM, oh[0], oh[1]))
        o = jnp.concatenate(outs, axis=1)
        z_ref[...] = (o * _silu(m_ref[:, mw:2 * mw])).astype(BF16)

    return pl.pallas_call(
        body, name=name, grid=(s // t,),
        in_specs=[pl.BlockSpec((t, 2 * mw), lambda i: (i, mblk)), pl.BlockSpec((ml, 2 * mw), lambda i: (0, 0))],
        out_specs=pl.BlockSpec((t, mw), lambda i: (i, 0)),
        out_shape=jax.ShapeDtypeStruct((s, mw), BF16),
        compiler_params=_params(("parallel",)),
    )(proj, mkv)


def _branch_m_bwd(proj, mkv, dz, mblk, name):
    s = proj.shape[0]
    t = T_M
    mw = mkv.shape[1] // 2
    ml = mkv.shape[0]
    scale = HEAD_DIM ** -0.5

    def body(m_ref, kv_ref, dz_ref, dm_ref, dkv_ref):
        i = pl.program_id(0)

        @pl.when(i == 0)
        def _():
            dkv_ref[...] = jnp.zeros_like(dkv_ref)

        gt = m_ref[:, mw:2 * mw]
        dz = dz_ref[...]
        do = dz * _silu(gt)
        outs = []
        for pr in range(mw // LANES):
            cols = slice(pr * LANES, (pr + 1) * LANES)
            vcols = slice(mw + pr * LANES, mw + (pr + 1) * LANES)
            qp = m_ref[:, cols]
            kp = kv_ref[:, cols].astype(BF16)
            vp = kv_ref[:, vcols].astype(BF16)
            dop = do[:, cols]
            oh = []
            dq = jnp.zeros((t, LANES), F32)
            dk = jnp.zeros((ml, LANES), F32)
            dv = jnp.zeros((ml, LANES), F32)
            for hh in range(2):
                hm = _head_mask(hh)
                qm = jnp.where(hm, qp, 0.0).astype(BF16)
                dom = jnp.where(hm, dop, 0.0).astype(BF16)
                p = _mem_softmax(qm, kp)
                pb = p.astype(BF16)
                oh.append(_dot(pb, vp, NN))
                dv = dv + _dot(pb, dom, TN)
                dp = _dot(dom, vp, NT)
                ds = p * (dp - jnp.sum(dp * p, axis=1, keepdims=True))
                dsb = (ds * scale).astype(BF16)
                dq = dq + jnp.where(hm, _dot(dsb, kp, NN), 0.0)
                dk = dk + _dot(dsb, qm, TN)
            outs.append(jnp.where(_iota((t, LANES), 1) < HEAD_DIM, oh[0], oh[1]))
            dm_ref[:, cols] = dq.astype(BF16)
            dkv_ref[:, cols] += dk
            dkv_ref[:, vcols] += dv
        o = jnp.concatenate(outs, axis=1)
        dm_ref[:, mw:2 * mw] = (dz * o * _dsilu(gt)).astype(BF16)

    return pl.pallas_call(
        body, name=name, grid=(s // t,),
        in_specs=[pl.BlockSpec((t, 2 * mw), lambda i: (i, mblk)), pl.BlockSpec((ml, 2 * mw), lambda i: (0, 0)),
                  pl.BlockSpec((t, mw), lambda i: (i, 0))],
        out_specs=[pl.BlockSpec((t, 2 * mw), lambda i: (i, 0)), pl.BlockSpec((ml, 2 * mw), lambda i: (0, 0))],
        out_shape=[jax.ShapeDtypeStruct((s, 2 * mw), BF16), jax.ShapeDtypeStruct((ml, 2 * mw), F32)],
        compiler_params=_params(("arbitrary",)),
    )(proj, mkv, dz)


def _merge_parts(z_refs, g_refs, pcat, bounds):
    ys, sgs = [], []
    merged = None
    for zr, gr, (lo, hi) in zip(z_refs, g_refs, bounds):
        y = _dot(zr[...], pcat[lo:hi, :], NN)
        sg = _sigmoid(gr[...])
        ys.append(y)
        sgs.append(sg)
        merged = sg * y if merged is None else merged + sg * y
    return ys, sgs, merged


def _branch_bounds(zs):
    bounds, lo = [], 0
    for z in zs:
        bounds.append((lo, lo + z.shape[1]))
        lo += z.shape[1]
    return bounds


def _merge_fwd(zs, proj, gblk, pcat, wout, x, lng, lnb, alpha, name):
    s, d = x.shape
    t = T_MERGE
    bounds = _branch_bounds(zs)

    def body(*refs):
        z_refs, g_refs = refs[0:4], refs[4:8]
        pcat_hbm, wout_hbm, x_ref, lng_ref, lnb_ref, y_ref, pcat_v, wout_v = refs[8:]

        @pl.when(pl.program_id(0) == 0)
        def _():
            pltpu.sync_copy(pcat_hbm, pcat_v)
            pltpu.sync_copy(wout_hbm, wout_v)

        _, _, merged = _merge_parts(z_refs, g_refs, pcat_v, bounds)
        h = alpha * x_ref[...] + _dot(merged.astype(BF16), wout_v[...], NN)
        xh, _ = _ln_stats(h)
        y_ref[...] = xh * lng_ref[...] + lnb_ref[...]

    row = pl.BlockSpec((t, d), lambda i: (i, 0))
    vec = pl.BlockSpec((1, d), lambda i: (0, 0))
    in_specs = ([pl.BlockSpec((t, z.shape[1]), lambda i: (i, 0)) for z in zs]
                + [pl.BlockSpec((t, d), lambda i, k=k: (i, gblk + k)) for k in range(4)]
                + [ANY, ANY, row, vec, vec])
    return pl.pallas_call(
        body, name=name, grid=(s // t,), in_specs=in_specs, out_specs=row,
        out_shape=jax.ShapeDtypeStruct((s, d), F32),
        scratch_shapes=[pltpu.VMEM(pcat.shape, BF16), pltpu.VMEM(wout.shape, BF16)],
        compiler_params=_params(("arbitrary",)),
    )(*zs, proj, proj, proj, proj, pcat, wout, x, lng, lnb)


def _merge_bwd(zs, proj, gblk, pcat, wout, x, lng, lnb, dy, alpha, name):
    s, d = x.shape
    t = T_MERGE
    n = s // t
    bounds = _branch_bounds(zs)

    def body(*refs):
        z_refs, g_refs = refs[0:4], refs[4:8]
        pcat_hbm, wout_hbm, x_ref, lng_ref, lnb_ref, dy_ref = refs[8:14]
        dx_ref, dg_ref = refs[14:16]
        dz_refs = refs[16:20]
        dpcat_hbm, dwout_hbm, dlng_ref, dlnb_ref = refs[20:24]
        pcat_v, wout_v, dpcat_v, dwout_v = refs[24:]
        i = pl.program_id(0)

        @pl.when(i == 0)
        def _():
            pltpu.sync_copy(pcat_hbm, pcat_v)
            pltpu.sync_copy(wout_hbm, wout_v)
            dpcat_v[...] = jnp.zeros_like(dpcat_v)
            dwout_v[...] = jnp.zeros_like(dwout_v)
            dlng_ref[...] = jnp.zeros_like(dlng_ref)
            dlnb_ref[...] = jnp.zeros_like(dlnb_ref)

        ys, sgs, merged = _merge_parts(z_refs, g_refs, pcat_v, bounds)
        mb = merged.astype(BF16)
        h = alpha * x_ref[...] + _dot(mb, wout_v[...], NN)
        xh, rstd = _ln_stats(h)
        dyv = dy_ref[...]
        dlng_ref[...] += jnp.sum(dyv * xh, axis=0, keepdims=True)
        dlnb_ref[...] += jnp.sum(dyv, axis=0, keepdims=True)
        dxh = dyv * lng_ref[...]
        dh = rstd * (dxh - jnp.mean(dxh, axis=-1, keepdims=True) - xh * jnp.mean(dxh * xh, axis=-1, keepdims=True))
        dx_ref[...] = alpha * dh
        dhb = dh.astype(BF16)
        dwout_v[...] += _dot(mb, dhb, TN)
        dmerged = _dot(dhb, wout_v[...], NT)
        for k, (zr, (lo, hi)) in enumerate(zip(z_refs, bounds)):
            sg = sgs[k]
            dg_ref[:, k * d:(k + 1) * d] = (dmerged * ys[k] * sg * (1.0 - sg)).astype(BF16)
            dyk = (dmerged * sg).astype(BF16)
            dpcat_v[lo:hi, :] += _dot(zr[...], dyk, TN)
            dz_refs[k][...] = _dot(dyk, pcat_v[lo:hi, :], NT)

        @pl.when(i == n - 1)
        def _():
            pltpu.sync_copy(dpcat_v, dpcat_hbm)
            pltpu.sync_copy(dwout_v, dwout_hbm)

    row = pl.BlockSpec((t, d), lambda i: (i, 0))
    vec = pl.BlockSpec((1, d), lambda i: (0, 0))
    z_specs = [pl.BlockSpec((t, z.shape[1]), lambda i: (i, 0)) for z in zs]
    in_specs = (z_specs + [pl.BlockSpec((t, d), lambda i, k=k: (i, gblk + k)) for k in range(4)]
                + [ANY, ANY, row, vec, vec, row])
    out_specs = [row, pl.BlockSpec((t, 4 * d), lambda i: (i, 0))] + z_specs + [ANY, ANY, vec, vec]
    vo = jax.ShapeDtypeStruct((1, d), F32)
    out_shape = ([jax.ShapeDtypeStruct((s, d), F32), jax.ShapeDtypeStruct((s, 4 * d), BF16)]
                 + [jax.ShapeDtypeStruct(z.shape, F32) for z in zs]
                 + [jax.ShapeDtypeStruct(pcat.shape, F32), jax.ShapeDtypeStruct(wout.shape, F32), vo, vo])
    return pl.pallas_call(
        body, name=name, grid=(n,), in_specs=in_specs, out_specs=out_specs, out_shape=out_shape,
        scratch_shapes=[pltpu.VMEM(pcat.shape, BF16), pltpu.VMEM(wout.shape, BF16),
                        pltpu.VMEM(pcat.shape, F32), pltpu.VMEM(wout.shape, F32)],
        compiler_params=_params(("arbitrary",)),
    )(*zs, proj, proj, proj, proj, pcat, wout, x, lng, lnb, dy)


def _loss_head(y, target, name):
    s, d = y.shape
    t = T_ELEM

    def body(y_ref, t_ref, dy_ref, loss_ref):
        @pl.when(pl.program_id(0) == 0)
        def _():
            loss_ref[...] = jnp.zeros_like(loss_ref)

        e = y_ref[...] - t_ref[...]
        dy_ref[...] = e * (1.0 / d)
        loss_ref[...] += 0.5 * jnp.sum(jnp.mean(e * e, axis=-1, keepdims=True), axis=0, keepdims=True)

    row = pl.BlockSpec((t, d), lambda i: (i, 0))
    return pl.pallas_call(
        body, name=name, grid=(s // t,), in_specs=[row, row],
        out_specs=[row, pl.BlockSpec((8, LANES), lambda i: (0, 0))],
        out_shape=[jax.ShapeDtypeStruct((s, d), F32), jax.ShapeDtypeStruct((8, LANES), F32)],
        compiler_params=_params(("arbitrary",)),
    )(y, target)


def _adamw(w, g, m, v, name):
    r, c = w.shape
    t = _pick(r, (256, 128, 64, 32, 16, 8))

    def body(w_ref, g_ref, m_ref, v_ref, d_ref, nm_ref, nv_ref):
        gv = g_ref[...]
        nm = ADAM_B1 * m_ref[...] + (1.0 - ADAM_B1) * gv
        nv = ADAM_B2 * v_ref[...] + (1.0 - ADAM_B2) * (gv * gv)
        m_hat = nm / (1.0 - ADAM_B1 ** ADAM_STEP)
        v_hat = nv / (1.0 - ADAM_B2 ** ADAM_STEP)
        d_ref[...] = -ADAM_LR * (m_hat / (jnp.sqrt(v_hat) + ADAM_EPS) + ADAM_WD * w_ref[...])
        nm_ref[...] = nm
        nv_ref[...] = nv

    blk = pl.BlockSpec((t, c), lambda i: (i, 0))
    o = jax.ShapeDtypeStruct((r, c), F32)
    return pl.pallas_call(body, name=name, grid=(r // t,), in_specs=[blk] * 4, out_specs=[blk] * 3,
                          out_shape=[o, o, o], compiler_params=_params(("parallel",)))(w, g, m, v)


def _sum_leading(x, out_dtype, name):
    k, r, c = x.shape
    t = _pick(r, (256, 128, 64, 32, 16, 8))

    def body(x_ref, o_ref):
        acc = x_ref[0].astype(F32)
        for j in range(1, k):
            acc = acc + x_ref[j].astype(F32)
        o_ref[...] = acc.astype(out_dtype)

    return pl.pallas_call(body, name=name, grid=(r // t,),
                          in_specs=[pl.BlockSpec((k, t, c), lambda i: (0, i, 0))],
                          out_specs=pl.BlockSpec((t, c), lambda i: (i, 0)),
                          out_shape=jax.ShapeDtypeStruct((r, c), out_dtype), compiler_params=_params(("parallel",)))(x)


def _position():
    return lax.axis_index("x"), lax.axis_index("y"), lax.axis_index("c")


def _chip_peers(x, y):
    return [(1 - x, y), (x, 1 - y), (1 - x, 1 - y)]


def _comm_call(body, n_in, out_shape, n_remote, n_local, name):
    return pl.pallas_call(
        body, name=name, in_specs=[ANY] * n_in, out_specs=[ANY] * len(out_shape), out_shape=out_shape,
        scratch_shapes=[pltpu.SemaphoreType.DMA((n_remote,)), pltpu.SemaphoreType.DMA((n_remote,)),
                        pltpu.SemaphoreType.DMA((max(n_local, 1),))])


def _run_copies(local, remote, send, recv, loc):
    copies = [pltpu.make_async_copy(src, dst, loc.at[k]) for k, (src, dst) in enumerate(local)]
    copies += [pltpu.make_async_remote_copy(src_ref=src, dst_ref=dst, send_sem=send.at[k], recv_sem=recv.at[k],
                                            device_id=peer, device_id_type=MESH)
               for k, (src, dst, peer) in enumerate(remote)]
    for cp in copies:
        cp.start()
    for cp in copies:
        cp.wait()


COPY_BYTES = 2 * 1024 * 1024


def _n_copies(rows, row_bytes, align):
    n = 8
    while n > 1 and (rows % (n * align) or rows // n * row_bytes < COPY_BYTES):
        n //= 2
    return n


def _allgather_chips(arrs, name):
    n = len(arrs)
    per_layer = [a.size * a.dtype.itemsize // a.shape[0] >= COPY_BYTES for a in arrs]
    n_each = [a.shape[0] if pl_ else 1 for a, pl_ in zip(arrs, per_layer)]

    def body(*refs):
        ins, outs = refs[:n], refs[n:2 * n]
        send, recv, loc = refs[2 * n:]
        x, y, c = _position()
        me = 2 * x + y
        local, remote = [], []
        for a in range(n):
            if per_layer[a]:
                parts = [(ins[a].at[l], outs[a].at[me, l]) for l in range(arrs[a].shape[0])]
            else:
                parts = [(ins[a], outs[a].at[me])]
            local += parts
            for px, py in _chip_peers(x, y):
                remote += [(src, dst, (px, py, c)) for src, dst in parts]
        _run_copies(local, remote, send, recv, loc)

    out_shape = [jax.ShapeDtypeStruct((4,) + a.shape, a.dtype) for a in arrs]
    return _comm_call(body, n, out_shape, 3 * sum(n_each), sum(n_each), name)(*arrs)


def _allgather_all(v, name):
    def body(v_ref, o_ref, send, recv, loc):
        x, y, c = _position()
        me = 4 * x + 2 * y + c
        remote = []
        for k in range(1, 8):
            fx, fy, fc = (k >> 2) & 1, (k >> 1) & 1, k & 1
            remote.append((v_ref, o_ref.at[me], (x ^ fx, y ^ fy, c ^ fc)))
        _run_copies([(v_ref, o_ref.at[me])], remote, send, recv, loc)

    return _comm_call(body, 1, [jax.ShapeDtypeStruct((8,) + v.shape, v.dtype)], 7, 1, name)(v)[0]


def _pair_exchange(gs, name):
    n = len(gs)
    n_each = [4 * g.shape[2] for g in gs]

    def body(*refs):
        ins, outs = refs[:n], refs[n:2 * n]
        send, recv, loc = refs[2 * n:]
        x, y, c = _position()
        local, remote = [], []
        for a in range(n):
            for j in range(4):
                for k in range(gs[a].shape[2]):
                    local.append((ins[a].at[j, c, k], outs[a].at[c, j, k]))
                    remote.append((ins[a].at[j, 1 - c, k], outs[a].at[c, j, k], (x, y, 1 - c)))
        _run_copies(local, remote, send, recv, loc)

    out_shape = [jax.ShapeDtypeStruct((2, 4) + g.shape[2:], g.dtype) for g in gs]
    return _comm_call(body, n, out_shape, sum(n_each), sum(n_each), name)(*gs)


def _chip_exchange(ps, name):
    n = len(ps)
    n_each = [p.shape[1] for p in ps]

    def body(*refs):
        ins, outs = refs[:n], refs[n:2 * n]
        send, recv, loc = refs[2 * n:]
        x, y, c = _position()
        me = 2 * x + y
        local, remote = [], []
        for a in range(n):
            for k in range(ps[a].shape[1]):
                local.append((ins[a].at[me, k], outs[a].at[me, k]))
                for px, py in _chip_peers(x, y):
                    remote.append((ins[a].at[2 * px + py, k], outs[a].at[me, k], (px, py, c)))
        _run_copies(local, remote, send, recv, loc)

    out_shape = [jax.ShapeDtypeStruct(p.shape, p.dtype) for p in ps]
    return _comm_call(body, n, out_shape, 3 * sum(n_each), sum(n_each), name)(*ps)


def _pair_allgather(rs, name):
    n = len(rs)
    n_each = [r.shape[0] for r in rs]

    def body(*refs):
        ins, outs = refs[:n], refs[n:2 * n]
        send, recv, loc = refs[2 * n:]
        x, y, c = _position()
        local, remote = [], []
        for a in range(n):
            for k in range(rs[a].shape[0]):
                local.append((ins[a].at[k], outs[a].at[c, k]))
                remote.append((ins[a].at[k], outs[a].at[c, k], (x, y, 1 - c)))
        _run_copies(local, remote, send, recv, loc)

    out_shape = [jax.ShapeDtypeStruct((2,) + r.shape, r.dtype) for r in rs]
    return _comm_call(body, n, out_shape, sum(n_each), sum(n_each), name)(*rs)


def _reduce_scatter(gs):
    split = []
    for g in gs:
        _, rows, c = g.shape
        k = _n_copies(rows // 2, c * 4, 16)
        split.append(g.reshape(4, 2, k, rows // (2 * k), c))
    pairs = _pair_exchange(split, "rs_pair_exchange")
    ps = [_sum_leading(p.reshape(2, -1, p.shape[-1]), BF16, f"rs_pair_sum_{a}").reshape(p.shape[1:])
          for a, p in enumerate(pairs)]
    qs = _chip_exchange(ps, "rs_chip_exchange")
    rs = [_sum_leading(q.reshape(4, -1, q.shape[-1]), F32, f"rs_chip_sum_{a}").reshape(q.shape[1:])
          for a, q in enumerate(qs)]
    outs = _pair_allgather(rs, "rs_pair_allgather")
    return [o.reshape(-1, o.shape[-1]) for o in outs]


def _pad_rows(a, rows):
    return jnp.pad(a, ((0, rows - a.shape[0]), (0, 0)))


def _shard_cols(a):
    r, c4 = a.shape
    return a.reshape(r, 4, c4 // 4).transpose(1, 0, 2)


def kernel(x, mem, w_in, b_forget, conv_a_w, conv_a_b, ln_a_g, ln_a_b, conv_b_w, w_kv_mem, mem_ln_g, mem_ln_b, p_a, p_b, p_c, p_m, w_out, ln_g, ln_b, loss_target, m_w_in, m_b_forget, m_conv_a_w, m_conv_a_b, m_ln_a_g, m_ln_a_b, m_conv_b_w, m_w_kv_mem, m_mem_ln_g, m_mem_ln_b, m_p_a, m_p_b, m_p_c, m_p_m, m_w_out, m_ln_g, m_ln_b, v_w_in, v_b_forget, v_conv_a_w, v_conv_a_b, v_ln_a_g, v_ln_a_b, v_conv_b_w, v_w_kv_mem, v_mem_ln_g, v_mem_ln_b, v_p_a, v_p_b, v_p_c, v_p_m, v_w_out, v_ln_g, v_ln_b):
    depth = w_in.shape[0]
    x0 = x[0]
    s, d = x0.shape
    aw = conv_a_w.shape[2] * 4
    mw = p_m.shape[1]
    n_head = b_forget.shape[1]
    alpha = (2.0 * depth) ** 0.25
    in_cols = w_in.shape[2] * 4
    assert aw == n_head * HEAD_DIM and mw % LANES == 0 and in_cols == 11 * aw + n_head + 2 * mw + 4 * d
    cf0 = 10 * aw
    n_main = in_cols - n_head
    fblk = n_main // LANES
    n_pad = n_main + LANES
    gblk = (11 * aw + 2 * mw) // d
    mblk = (11 * aw) // (2 * mw)
    qblk, kblk, vblk = 7 * aw // LANES, 8 * aw // LANES, 9 * aw // LANES
    assert (11 * aw + 2 * mw) % d == 0 and (11 * aw) % (2 * mw) == 0

    gathered = _allgather_chips(
        [w_in.astype(BF16), p_a.astype(BF16), p_b.astype(BF16), p_c.astype(BF16), p_m.astype(BF16),
         w_kv_mem.astype(BF16), w_out.astype(BF16), conv_a_w, conv_b_w], "gather_weights")
    w_in_g, p_a_g, p_b_g, p_c_g, p_m_g, w_kv_g, w_out_g, conv_a_g, conv_b_g = gathered

    def cols(g):
        return jnp.concatenate([g[j] for j in range(4)], axis=-1)

    def rows(g):
        return jnp.concatenate([g[j] for j in range(4)], axis=-2)

    w_full = cols(w_in_g)
    w_pad = jnp.concatenate([w_full[:, :, :cf0], w_full[:, :, cf0 + n_head:], w_full[:, :, cf0:cf0 + n_head],
                             jnp.zeros((depth, d, LANES - n_head), BF16)], axis=-1)
    pcat = jnp.concatenate([cols(p_a_g), cols(p_b_g), cols(p_c_g), cols(p_m_g)], axis=1)
    w_kv = rows(w_kv_g)
    wout = rows(w_out_g)
    conv_a = jnp.pad(cols(conv_a_g), ((0, 0), (0, HALO_A - CONV_A), (0, 0)))
    conv_b = jnp.pad(cols(conv_b_g), ((0, 0), (0, HALO_B - CONV_B), (0, 0)))
    bf_pad = jnp.pad(b_forget, ((0, 0), (0, LANES - n_head)))
    pieces = [(0, 3 * aw), (3 * aw, 7 * aw), (7 * aw, 10 * aw), (10 * aw, 11 * aw),
              (11 * aw, 11 * aw + 2 * mw), (11 * aw + 2 * mw, n_main), (n_main, n_pad)]

    mem_n = _ln_rows(mem[0], mem_ln_g[None], mem_ln_b[None], "mem_ln")

    xs, saved = [x0], []
    for l in range(depth):
        xl = xs[-1]
        proj = _mm(xl, w_pad[l], name=f"proj_{l}")
        za = _branch_a_fwd(proj, conv_a[l], conv_a_b[l][None], ln_a_g[l][None], ln_a_b[l][None], f"a_fwd_{l}")
        zb = _branch_b_fwd(proj, conv_b[l], f"b_fwd_{l}")
        cum, cumt, sgt = _forget_prep(proj, bf_pad[l][None], fblk, f"forget_prep_{l}")
        o_c, lse = _fox_fwd(proj, cum, cumt, qblk, kblk, vblk, f"fox_fwd_{l}")
        zc = _gate_mul(proj, o_c, 10, f"c_gate_{l}")
        mkv = _mm(mem_n, w_kv[l], name=f"mkv_{l}")
        zm = _branch_m_fwd(proj, mkv, mblk, f"m_fwd_{l}")
        zs = [za, zb, zc, zm]
        y = _merge_fwd(zs, proj, gblk, pcat[l], wout[l], xl, ln_g[l][None], ln_b[l][None], alpha, f"merge_fwd_{l}")
        xs.append(y)
        saved.append((proj, zs, cum, cumt, sgt, o_c, lse, mkv))

    dy, loss_part = _loss_head(xs[-1], loss_target[0], "loss_head")

    g_w_in, g_conv_a, g_conv_b, g_w_kv, g_pcat, g_wout = [], [], [], [], [], []
    small = []
    dmem_n = None
    for l in reversed(range(depth)):
        proj, zs, cum, cumt, sgt, o_c, lse, mkv = saved[l]
        xl = xs[l]
        (dx, d_g, dza, dzb, dzc, dzm, dpcat, dwout, dlng, dlnb) = _merge_bwd(
            zs, proj, gblk, pcat[l], wout[l], xl, ln_g[l][None], ln_b[l][None], dy, alpha, f"merge_bwd_{l}")
        d_m, dmkv = _branch_m_bwd(proj, mkv, dzm, mblk, f"m_bwd_{l}")
        g_w_kv.append(_mm(mem_n, dmkv, ta=True, name=f"dwkv_{l}"))
        dmem_n = _mm(dmkv, w_kv[l], tb=True, add=dmem_n, name=f"dmem_{l}")
        do, d_cg, delta = _fox_bwd_prep(proj, dzc, o_c, 10, f"fox_bwd_prep_{l}")
        dq, dk, dv, dcumt4, dcumq4 = _fox_bwd(proj, do, lse, delta, cum, cumt, qblk, kblk, vblk, f"fox_bwd_{l}")
        d_f, dbf = _forget_bwd(dcumt4, dcumq4, sgt, f"forget_bwd_{l}")
        d_b, dconv_b = _branch_b_bwd(proj, dzb, conv_b[l], f"b_bwd_{l}")
        d_a, dconv_a, dconv_ab, dlag, dlab = _branch_a_bwd(
            proj, dza, conv_a[l], conv_a_b[l][None], ln_a_g[l][None], ln_a_b[l][None], f"a_bwd_{l}")
        dparts = [d_a, d_b, jnp.concatenate([dq, dk, dv], axis=1), d_cg, d_m, d_g, d_f]
        dw_parts = []
        for k, (dp, (lo, hi)) in enumerate(zip(dparts, pieces)):
            wp = w_pad[l][:, lo:hi]
            dx = _mm(dp, wp, tb=True, add=dx, name=f"dx_{l}_{k}")
            dw_parts.append(_mm(xl, dp, ta=True, name=f"dw_{l}_{k}"))
        dw = jnp.concatenate(dw_parts[:3] + [dw_parts[6][:, :n_head]] + dw_parts[3:6], axis=1)
        g_w_in.append(dw)
        g_conv_a.append(dconv_a)
        g_conv_b.append(dconv_b)
        g_pcat.append(dpcat)
        g_wout.append(dwout)
        small.append([dbf[:, 0], dconv_ab[0], dlag[0], dlab[0], dlng[0], dlnb[0]])
        dy = dx
    grad_x = dy
    dmlg, dmlb = _ln_rows_param_grads(mem[0], dmem_n, "mem_ln_grads")
    for lst in (g_w_in, g_conv_a, g_conv_b, g_w_kv, g_pcat, g_wout, small):
        lst.reverse()

    pa_end, pb_end, pc_end = aw, 2 * aw, 3 * aw
    rs_in = [
        _shard_cols(jnp.concatenate(g_w_in, axis=0)),
        _shard_cols(jnp.concatenate(g_conv_a, axis=0)),
        _shard_cols(jnp.concatenate([_pad_rows(g, 2 * HALO_B) for g in g_conv_b], axis=0)),
        jnp.concatenate([g.reshape(4, g.shape[0] // 4, g.shape[1]) for g in g_w_kv], axis=1),
        _shard_cols(jnp.concatenate([g[:pa_end] for g in g_pcat], axis=0)),
        _shard_cols(jnp.concatenate([g[pa_end:pb_end] for g in g_pcat], axis=0)),
        _shard_cols(jnp.concatenate([g[pb_end:pc_end] for g in g_pcat], axis=0)),
        _shard_cols(jnp.concatenate([g[pc_end:] for g in g_pcat], axis=0)),
        jnp.concatenate([g.reshape(4, g.shape[0] // 4, g.shape[1]) for g in g_wout], axis=1),
    ]
    rs_out = _reduce_scatter(rs_in)
    gw_in = rs_out[0].reshape(depth, d, -1)
    g_ca = rs_out[1].reshape(depth, HALO_A, -1)[:, :CONV_A]
    g_cb = rs_out[2].reshape(depth, 2 * HALO_B, -1)[:, :CONV_B]
    gw_kv = rs_out[3].reshape(depth, -1, 2 * mw)
    gp_a = rs_out[4].reshape(depth, aw, -1)
    gp_b = rs_out[5].reshape(depth, aw, -1)
    gp_c = rs_out[6].reshape(depth, aw, -1)
    gp_m = rs_out[7].reshape(depth, mw, -1)
    gw_out = rs_out[8].reshape(depth, -1, d)

    flat = jnp.concatenate([jnp.concatenate(p) for p in small] + [dmlg[0], dmlb[0], loss_part[0, 0:1]])
    n_small = flat.shape[0]
    n_rows = -(-n_small // (8 * LANES)) * 8
    vec = jnp.pad(flat, (0, n_rows * LANES - n_small)).reshape(n_rows, LANES)
    tot = _sum_leading(_allgather_all(vec, "gather_small"), F32, "sum_small").reshape(-1)
    per_layer = n_head + 3 * aw + 2 * d
    tl = tot[:depth * per_layer].reshape(depth, per_layer)
    offs = np.cumsum([0, n_head, aw, aw, aw, d, d])
    g_bf, g_cab, g_lag, g_lab, g_lg, g_lb = [tl[:, offs[k]:offs[k + 1]] for k in range(6)]
    base = depth * per_layer
    g_mlg, g_mlb = tot[base:base + d], tot[base + d:base + 2 * d]
    loss = tot[base + 2 * d]

    grads = [gw_in, g_bf, g_ca, g_cab, g_lag, g_lab, g_cb, gw_kv, g_mlg, g_mlb, gp_a, gp_b, gp_c, gp_m, gw_out, g_lg, g_lb]
    ws = [w_in, b_forget, conv_a_w, conv_a_b, ln_a_g, ln_a_b, conv_b_w, w_kv_mem, mem_ln_g, mem_ln_b, p_a, p_b, p_c, p_m, w_out, ln_g, ln_b]
    ms = [m_w_in, m_b_forget, m_conv_a_w, m_conv_a_b, m_ln_a_g, m_ln_a_b, m_conv_b_w, m_w_kv_mem, m_mem_ln_g, m_mem_ln_b, m_p_a, m_p_b, m_p_c, m_p_m, m_w_out, m_ln_g, m_ln_b]
    vs = [v_w_in, v_b_forget, v_conv_a_w, v_conv_a_b, v_ln_a_g, v_ln_a_b, v_conv_b_w, v_w_kv_mem, v_mem_ln_g, v_mem_ln_b, v_p_a, v_p_b, v_p_c, v_p_m, v_w_out, v_ln_g, v_ln_b]
    deltas, new_ms, new_vs = [], [], []
    for k, (wk, gk, mk, vk) in enumerate(zip(ws, grads, ms, vs)):
        shape = wk.shape
        two_d = (1, shape[0]) if wk.ndim == 1 else (int(np.prod(shape[:-1])), shape[-1])
        dk_, nm_, nv_ = _adamw(wk.reshape(two_d), gk.reshape(two_d), mk.reshape(two_d), vk.reshape(two_d), f"adamw_{k}")
        deltas.append(dk_.reshape(shape))
        new_ms.append(nm_.reshape(shape))
        new_vs.append(nv_.reshape(shape))
        grads[k] = gk.reshape(shape)
    return (loss, grad_x[None], *grads, *deltas, *new_ms, *new_vs)


def _gate_mul(proj, o, gblk, name):
    s, w = o.shape
    t = T_ELEM

    def body(g_ref, o_ref, z_ref):
        z_ref[...] = (o_ref[...] * _silu(g_ref[...])).astype(BF16)

    row = pl.BlockSpec((t, w), lambda i: (i, 0))
    return pl.pallas_call(body, name=name, grid=(s // t,), in_specs=[pl.BlockSpec((t, w), lambda i: (i, gblk)), row],
                          out_specs=row, out_shape=jax.ShapeDtypeStruct((s, w), BF16),
                          compiler_params=_params(("parallel",)))(proj, o)
```

```python
import functools
import math

import numpy as np
import jax
import jax.numpy as jnp
from jax import lax
from jax.experimental import pallas as pl
from jax.experimental.pallas import tpu as pltpu

F32 = jnp.float32
BF16 = jnp.bfloat16
MESH = pl.DeviceIdType.MESH
ANY = pl.BlockSpec(memory_space=pl.ANY)

LN_EPS = 1e-5
NEG_BIG = -1e30
HEAD_DIM = 64
LANES = 128
CONV_A = 31
CONV_B = 3
HALO_A = 32
HALO_B = 8
CHUNK = 32
VMEM_LIMIT = 60 * 1024 * 1024

ADAM_LR, ADAM_B1, ADAM_B2, ADAM_EPS, ADAM_WD, ADAM_STEP = 0.001, 0.9, 0.999, 1e-08, 0.01, 10

T_MM = 512
T_A = 128
T_B = 256
T_ATT_Q = 512
T_ATT_K = 1024
T_CUM = 512
T_M = 512
T_MERGE = 256
T_DX = 256
T_ELEM = 512


def _pick(n, prefs):
    for p in prefs:
        if n % p == 0:
            return p
    return n


def _params(sem=None):
    return pltpu.CompilerParams(dimension_semantics=sem, vmem_limit_bytes=VMEM_LIMIT)


def _sigmoid(x):
    return jax.nn.sigmoid(x)


def _silu(x):
    return x * _sigmoid(x)


def _dsilu(x):
    s = _sigmoid(x)
    return s * (1.0 + x * (1.0 - s))


def _dot(a, b, dims, precision=None):
    return lax.dot_general(a, b, (dims, ((), ())), preferred_element_type=F32, precision=precision)


NN = ((1,), (0,))
NT = ((1,), (1,))
TN = ((0,), (0,))


def _iota(shape, dim):
    return lax.broadcasted_iota(jnp.int32, shape, dim)


def _mm(a, b, *, ta=False, tb=False, add=None, out_dtype=F32, name):
    m = a.shape[1] if ta else a.shape[0]
    k = a.shape[0] if ta else a.shape[1]
    n = b.shape[0] if tb else b.shape[1]
    tm = _pick(m, (1024, 512, 256)) if ta else _pick(m, (T_MM, 256))
    tn = _pick(n, (1152, 1024, 768, 512, 384, 256, 128))
    tk = _pick(k, (1024, 512, 256))
    nk = k // tk
    dims = ((0,) if ta else (1,), (1,) if tb else (0,))

    def body(*refs):
        if add is None:
            a_ref, b_ref, o_ref, acc_ref = refs
        else:
            a_ref, b_ref, add_ref, o_ref, acc_ref = refs
        kk = pl.program_id(2)
        p = _dot(a_ref[...].astype(BF16), b_ref[...].astype(BF16), dims)

        @pl.when(kk == 0)
        def _():
            acc_ref[...] = p

        @pl.when(kk > 0)
        def _():
            acc_ref[...] += p

        @pl.when(kk == nk - 1)
        def _():
            r = acc_ref[...]
            if add is not None:
                r = r + add_ref[...]
            o_ref[...] = r.astype(out_dtype)

    a_spec = (pl.BlockSpec((tk, tm), lambda j, i, kk: (kk, i)) if ta
              else pl.BlockSpec((tm, tk), lambda j, i, kk: (i, kk)))
    b_spec = (pl.BlockSpec((tn, tk), lambda j, i, kk: (j, kk)) if tb
              else pl.BlockSpec((tk, tn), lambda j, i, kk: (kk, j)))
    o_spec = pl.BlockSpec((tm, tn), lambda j, i, kk: (i, j))
    in_specs = [a_spec, b_spec] + ([o_spec] if add is not None else [])
    args = (a, b) + ((add,) if add is not None else ())
    return pl.pallas_call(
        body, name=name, grid=(n // tn, m // tm, nk), in_specs=in_specs, out_specs=o_spec,
        out_shape=jax.ShapeDtypeStruct((m, n), out_dtype),
        scratch_shapes=[pltpu.VMEM((tm, tn), F32)],
        compiler_params=_params(("parallel", "parallel", "arbitrary")),
    )(*args)


def _input_grad(dparts, wparts, add, name):
    s, d = add.shape
    t = T_DX
    n_p = len(dparts)

    def body(*refs):
        d_refs, w_hbm = refs[:n_p], refs[n_p:2 * n_p]
        add_ref, o_ref = refs[2 * n_p], refs[2 * n_p + 1]
        w_v = refs[2 * n_p + 2:]

        @pl.when(pl.program_id(0) == 0)
        def _():
            for p in range(n_p):
                pltpu.sync_copy(w_hbm[p], w_v[p])

        acc = add_ref[...]
        for p in range(n_p):
            acc = acc + _dot(d_refs[p][...], w_v[p][...], NT)
        o_ref[...] = acc

    row = pl.BlockSpec((t, d), lambda i: (i, 0))
    in_specs = ([pl.BlockSpec((t, dp.shape[1]), lambda i: (i, 0)) for dp in dparts] + [ANY] * n_p + [row])
    return pl.pallas_call(
        body, name=name, grid=(s // t,), in_specs=in_specs, out_specs=row,
        out_shape=jax.ShapeDtypeStruct((s, d), F32),
        scratch_shapes=[pltpu.VMEM(w.shape, BF16) for w in wparts],
        compiler_params=_params(("arbitrary",)),
    )(*dparts, *wparts, add)


def _ln_rows(x, g, b, name):
    r, d = x.shape
    t = _pick(r, (256,))

    def body(x_ref, g_ref, b_ref, o_ref):
        xv = x_ref[...]
        mu = jnp.mean(xv, axis=-1, keepdims=True)
        dv = xv - mu
        var = jnp.mean(dv * dv, axis=-1, keepdims=True)
        o_ref[...] = dv * lax.rsqrt(var + LN_EPS) * g_ref[...] + b_ref[...]

    row = pl.BlockSpec((t, d), lambda i: (i, 0))
    vec = pl.BlockSpec((1, d), lambda i: (0, 0))
    return pl.pallas_call(body, name=name, grid=(r // t,), in_specs=[row, vec, vec], out_specs=row,
                          out_shape=jax.ShapeDtypeStruct((r, d), F32), compiler_params=_params(("parallel",)))(x, g, b)


def _ln_rows_param_grads(x, dy, name):
    r, d = x.shape

    def body(x_ref, dy_ref, dg_ref, db_ref):
        xv = x_ref[...]
        mu = jnp.mean(xv, axis=-1, keepdims=True)
        dv = xv - mu
        var = jnp.mean(dv * dv, axis=-1, keepdims=True)
        xh = dv * lax.rsqrt(var + LN_EPS)
        dg_ref[...] = jnp.sum(dy_ref[...] * xh, axis=0, keepdims=True)
        db_ref[...] = jnp.sum(dy_ref[...], axis=0, keepdims=True)

    full = pl.BlockSpec((r, d), lambda i: (0, 0))
    vec = pl.BlockSpec((1, d), lambda i: (0, 0))
    o = jax.ShapeDtypeStruct((1, d), F32)
    return pl.pallas_call(body, name=name, grid=(1,), in_specs=[full, full], out_specs=[vec, vec],
                          out_shape=[o, o], compiler_params=_params(("arbitrary",)))(x, dy)


def _conv_a_chunk(glu_ref, cw_ref, r0):
    acc = cw_ref[0:1, :] * glu_ref[pl.ds(r0 + 2, CHUNK), :]
    for k in range(1, CONV_A):
        acc = acc + cw_ref[k:k + 1, :] * glu_ref[pl.ds(r0 + 2 + k, CHUNK), :]
    return acc


def _ln_stats(c):
    mu = jnp.mean(c, axis=-1, keepdims=True)
    d = c - mu
    var = jnp.mean(d * d, axis=-1, keepdims=True)
    rstd = lax.rsqrt(var + LN_EPS)
    return d * rstd, rstd


def _branch_a_fwd(proj, cw, cb, lg, lb, name):
    s = proj.shape[0]
    t = T_A
    w = cw.shape[1]
    r = t // HALO_A

    def body(u_ref, v_ref, gt_ref, hu_ref, hv_ref, cw_ref, cb_ref, lg_ref, lb_ref, z_ref, glu):
        i = pl.program_id(0)
        hglu = hu_ref[...] * _sigmoid(hv_ref[...])
        glu[0:HALO_A, :] = jnp.where(i > 0, hglu, 0.0)
        glu[HALO_A:HALO_A + t, :] = u_ref[...] * _sigmoid(v_ref[...])
        for c in range(t // CHUNK):
            r0 = c * CHUNK
            conv = _conv_a_chunk(glu, cw_ref, r0) + cb_ref[...]
            xh, _ = _ln_stats(conv)
            a3 = _silu(xh * lg_ref[...] + lb_ref[...])
            z_ref[r0:r0 + CHUNK, :] = (a3 * _silu(gt_ref[r0:r0 + CHUNK, :])).astype(BF16)

    def cur(col):
        return pl.BlockSpec((t, w), lambda i, col=col: (i, col))

    def prev(col):
        return pl.BlockSpec((HALO_A, w), lambda i, col=col: (jnp.maximum(i * r - 1, 0), col))

    vec = pl.BlockSpec((1, w), lambda i: (0, 0))
    return pl.pallas_call(
        body, name=name, grid=(s // t,),
        in_specs=[cur(0), cur(1), cur(2), prev(0), prev(1), pl.BlockSpec((HALO_A, w), lambda i: (0, 0)), vec, vec, vec],
        out_specs=pl.BlockSpec((t, w), lambda i: (i, 0)),
        out_shape=jax.ShapeDtypeStruct((s, w), BF16),
        scratch_shapes=[pltpu.VMEM((HALO_A + t, w), F32)],
        compiler_params=_params(("parallel",)),
    )(proj, proj, proj, proj, proj, cw, cb, lg, lb)


def _branch_a_bwd(proj, dz, cw, cb, lg, lb, name):
    s = proj.shape[0]
    t = T_A
    w = cw.shape[1]
    r = t // HALO_A
    n = s // t
    nblk = s // HALO_A
    ext = t + HALO_A

    def body(u_ref, v_ref, gt_ref, dz_ref, pu_ref, pv_ref, nu_ref, nv_ref, ngt_ref, ndz_ref,
             cw_ref, cb_ref, lg_ref, lb_ref, da_ref, dw_ref, dcb_ref, dlg_ref, dlb_ref, glu, dc, dw8):
        i = pl.program_id(0)

        @pl.when(i == 0)
        def _():
            dw8[...] = jnp.zeros_like(dw8)
            dcb_ref[...] = jnp.zeros_like(dcb_ref)
            dlg_ref[...] = jnp.zeros_like(dlg_ref)
            dlb_ref[...] = jnp.zeros_like(dlb_ref)

        glu[0:HALO_A, :] = jnp.where(i > 0, pu_ref[...] * _sigmoid(pv_ref[...]), 0.0)
        glu[HALO_A:HALO_A + t, :] = u_ref[...] * _sigmoid(v_ref[...])
        glu[HALO_A + t:HALO_A + t + HALO_A, :] = nu_ref[...] * _sigmoid(nv_ref[...])
        has_next = i < n - 1
        dcb = jnp.zeros((1, w), F32)
        dlg = jnp.zeros((1, w), F32)
        dlb = jnp.zeros((1, w), F32)
        for c in range(ext // CHUNK):
            r0 = c * CHUNK
            own = r0 < t
            conv = _conv_a_chunk(glu, cw_ref, r0) + cb_ref[...]
            xh, rstd = _ln_stats(conv)
            a2 = xh * lg_ref[...] + lb_ref[...]
            if own:
                gt = gt_ref[r0:r0 + CHUNK, :]
                dzc = dz_ref[r0:r0 + CHUNK, :]
            else:
                gt = ngt_ref[...]
                dzc = ndz_ref[...]
            da2 = dzc * _silu(gt) * _dsilu(a2)
            dxh = da2 * lg_ref[...]
            dconv = rstd * (dxh - jnp.mean(dxh, axis=-1, keepdims=True)
                            - xh * jnp.mean(dxh * xh, axis=-1, keepdims=True))
            if own:
                dc[r0:r0 + CHUNK, :] = dconv
                da_ref[r0:r0 + CHUNK, 2 * w:3 * w] = (dzc * _silu(a2) * _dsilu(gt)).astype(BF16)
                dcb = dcb + jnp.sum(dconv, axis=0, keepdims=True)
                dlg = dlg + jnp.sum(da2 * xh, axis=0, keepdims=True)
                dlb = dlb + jnp.sum(da2, axis=0, keepdims=True)
            else:
                dc[r0:r0 + CHUNK, :] = jnp.where(has_next, dconv, 0.0)
        dcb_ref[...] += dcb
        dlg_ref[...] += dlg
        dlb_ref[...] += dlb
        for c in range(t // CHUNK):
            r0 = c * CHUNK
            dcc = dc[r0:r0 + CHUNK, :]
            dglu = cw_ref[0:1, :] * dc[pl.ds(r0 + CONV_A - 1, CHUNK), :]
            for k in range(1, CONV_A):
                dglu = dglu + cw_ref[k:k + 1, :] * dc[pl.ds(r0 + CONV_A - 1 - k, CHUNK), :]
            for k in range(CONV_A):
                prod = dcc * glu[pl.ds(r0 + 2 + k, CHUNK), :]
                dw8[k] += jnp.sum(prod.reshape(CHUNK // 8, 8, w), axis=0)
            sv = _sigmoid(v_ref[r0:r0 + CHUNK, :])
            da_ref[r0:r0 + CHUNK, 0:w] = (dglu * sv).astype(BF16)
            da_ref[r0:r0 + CHUNK, w:2 * w] = (dglu * u_ref[r0:r0 + CHUNK, :] * sv * (1.0 - sv)).astype(BF16)

        @pl.when(i == n - 1)
        def _():
            dw_ref[...] = jnp.sum(dw8[...], axis=1)

    def cur(col):
        return pl.BlockSpec((t, w), lambda i, col=col: (i, col))

    def prev(col):
        return pl.BlockSpec((HALO_A, w), lambda i, col=col: (jnp.maximum(i * r - 1, 0), col))

    def nxt(col):
        return pl.BlockSpec((HALO_A, w), lambda i, col=col: (jnp.minimum((i + 1) * r, nblk - 1), col))

    vec = pl.BlockSpec((1, w), lambda i: (0, 0))
    vo = jax.ShapeDtypeStruct((1, w), F32)
    return pl.pallas_call(
        body, name=name, grid=(n,),
        in_specs=[cur(0), cur(1), cur(2), pl.BlockSpec((t, w), lambda i: (i, 0)),
                  prev(0), prev(1), nxt(0), nxt(1), nxt(2),
                  pl.BlockSpec((HALO_A, w), lambda i: (jnp.minimum((i + 1) * r, nblk - 1), 0)),
                  pl.BlockSpec((HALO_A, w), lambda i: (0, 0)), vec, vec, vec],
        out_specs=[pl.BlockSpec((t, 3 * w), lambda i: (i, 0)), pl.BlockSpec((HALO_A, w), lambda i: (0, 0)), vec, vec, vec],
        out_shape=[jax.ShapeDtypeStruct((s, 3 * w), BF16), jax.ShapeDtypeStruct((HALO_A, w), F32), vo, vo, vo],
        scratch_shapes=[pltpu.VMEM((t + 2 * HALO_A, w), F32), pltpu.VMEM((ext, w), F32),
                        pltpu.VMEM((HALO_A, 8, w), F32)],
        compiler_params=_params(("arbitrary",)),
    )(proj, proj, proj, dz, proj, proj, proj, proj, proj, dz, cw, cb, lg, lb)


def _conv_b(u_ext, cw_ref, t):
    acc = cw_ref[0:1, :] * u_ext[pl.ds(HALO_B - 2, t), :]
    for k in range(1, CONV_B):
        acc = acc + cw_ref[k:k + 1, :] * u_ext[pl.ds(HALO_B - 2 + k, t), :]
    return acc


def _branch_b_fwd(proj, cw, name):
    s = proj.shape[0]
    t = T_B
    w = cw.shape[1]
    r = t // HALO_B

    def body(h_ref, b_ref, c_ref, gt_ref, ph_ref, pc_ref, cw_ref, z_ref, u_ext):
        i = pl.program_id(0)
        u_ext[0:HALO_B, :] = jnp.where(i > 0, pc_ref[...] * ph_ref[...], 0.0)
        u_ext[HALO_B:HALO_B + t, :] = c_ref[...] * h_ref[...]
        cv = _conv_b(u_ext, cw_ref, t)
        z_ref[...] = (b_ref[...] * cv * _silu(gt_ref[...])).astype(BF16)

    def cur(col):
        return pl.BlockSpec((t, w), lambda i, col=col: (i, col))

    def prev(col):
        return pl.BlockSpec((HALO_B, w), lambda i, col=col: (jnp.maximum(i * r - 1, 0), col))

    return pl.pallas_call(
        body, name=name, grid=(s // t,),
        in_specs=[cur(3), cur(4), cur(5), cur(6), prev(3), prev(5), pl.BlockSpec((HALO_B, w), lambda i: (0, 0))],
        out_specs=pl.BlockSpec((t, w), lambda i: (i, 0)),
        out_shape=jax.ShapeDtypeStruct((s, w), BF16),
        scratch_shapes=[pltpu.VMEM((HALO_B + t, w), F32)],
        compiler_params=_params(("parallel",)),
    )(proj, proj, proj, proj, proj, proj, cw)


def _branch_b_bwd(proj, dz, cw, name):
    s = proj.shape[0]
    t = T_B
    w = cw.shape[1]
    r = t // HALO_B
    n = s // t
    nblk = s // HALO_B

    def body(h_ref, b_ref, c_ref, gt_ref, dz_ref, ph_ref, pc_ref, nb_ref, ngt_ref, ndz_ref, cw_ref,
             db_ref, dw_ref, u_ext, dcv_ext, dw8):
        i = pl.program_id(0)

        @pl.when(i == 0)
        def _():
            dw8[...] = jnp.zeros_like(dw8)

        u_ext[0:HALO_B, :] = jnp.where(i > 0, pc_ref[...] * ph_ref[...], 0.0)
        u_ext[HALO_B:HALO_B + t, :] = c_ref[...] * h_ref[...]
        cv = _conv_b(u_ext, cw_ref, t)
        gt = gt_ref[...]
        dhb = dz_ref[...] * _silu(gt)
        db_ref[:, 3 * w:4 * w] = (dz_ref[...] * b_ref[...] * cv * _dsilu(gt)).astype(BF16)
        db_ref[:, w:2 * w] = (dhb * cv).astype(BF16)
        dcv = dhb * b_ref[...]
        dcv_ext[0:t, :] = dcv
        ndcv = ndz_ref[...] * _silu(ngt_ref[...]) * nb_ref[...]
        dcv_ext[t:t + HALO_B, :] = jnp.where(i < n - 1, ndcv, 0.0)
        du = cw_ref[0:1, :] * dcv_ext[pl.ds(2, t), :]
        for k in range(1, CONV_B):
            du = du + cw_ref[k:k + 1, :] * dcv_ext[pl.ds(2 - k, t), :]
        db_ref[:, 2 * w:3 * w] = (du * h_ref[...]).astype(BF16)
        db_ref[:, 0:w] = (du * c_ref[...]).astype(BF16)
        for k in range(CONV_B):
            prod = dcv * u_ext[pl.ds(HALO_B - 2 + k, t), :]
            dw8[k] += jnp.sum(prod.reshape(t // 8, 8, w), axis=0)

        @pl.when(i == n - 1)
        def _():
            dw_ref[...] = jnp.sum(dw8[...], axis=1)

    def cur(col):
        return pl.BlockSpec((t, w), lambda i, col=col: (i, col))

    def prev(col):
        return pl.BlockSpec((HALO_B, w), lambda i, col=col: (jnp.maximum(i * r - 1, 0), col))

    def nxt(col):
        return pl.BlockSpec((HALO_B, w), lambda i, col=col: (jnp.minimum((i + 1) * r, nblk - 1), col))

    return pl.pallas_call(
        body, name=name, grid=(n,),
        in_specs=[cur(3), cur(4), cur(5), cur(6), pl.BlockSpec((t, w), lambda i: (i, 0)),
                  prev(3), prev(5), nxt(4), nxt(6),
                  pl.BlockSpec((HALO_B, w), lambda i: (jnp.minimum((i + 1) * r, nblk - 1), 0)),
                  pl.BlockSpec((HALO_B, w), lambda i: (0, 0))],
        out_specs=[pl.BlockSpec((t, 4 * w), lambda i: (i, 0)), pl.BlockSpec((HALO_B, w), lambda i: (0, 0))],
        out_shape=[jax.ShapeDtypeStruct((s, 4 * w), BF16), jax.ShapeDtypeStruct((HALO_B, w), F32)],
        scratch_shapes=[pltpu.VMEM((HALO_B + t, w), F32), pltpu.VMEM((t + HALO_B, w), F32),
                        pltpu.VMEM((HALO_B, 8, w), F32)],
        compiler_params=_params(("arbitrary",)),
    )(proj, proj, proj, proj, dz, proj, proj, proj, proj, dz, cw)


def _forget_prep(proj, bf, fblk, name):
    s = proj.shape[0]
    t = T_CUM

    def body(f_ref, bf_ref, cum_ref, cumt_ref, sgt_ref, carry):
        i = pl.program_id(0)

        @pl.when(i == 0)
        def _():
            carry[...] = jnp.zeros_like(carry)

        z = f_ref[...] + bf_ref[...]
        logf = jnp.minimum(z, 0.0) - jnp.log1p(jnp.exp(-jnp.abs(z)))
        tri = (_iota((t, t), 0) >= _iota((t, t), 1)).astype(F32)
        cum = _dot(tri, logf, NN, precision=lax.Precision.HIGHEST) + carry[0:1, :]
        carry[0:1, :] = cum[t - 1:t, :]
        cum_ref[...] = cum
        cumt_ref[...] = cum.T[0:8, :]
        sgt_ref[...] = _sigmoid(-z).T[0:8, :]

    return pl.pallas_call(
        body, name=name, grid=(s // t,),
        in_specs=[pl.BlockSpec((t, LANES), lambda i: (i, fblk)), pl.BlockSpec((1, LANES), lambda i: (0, 0))],
        out_specs=[pl.BlockSpec((t, LANES), lambda i: (i, 0)), pl.BlockSpec((8, t), lambda i: (0, i)),
                   pl.BlockSpec((8, t), lambda i: (0, i))],
        out_shape=[jax.ShapeDtypeStruct((s, LANES), F32), jax.ShapeDtypeStruct((8, s), F32),
                   jax.ShapeDtypeStruct((8, s), F32)],
        scratch_shapes=[pltpu.VMEM((8, LANES), F32)],
        compiler_params=_params(("arbitrary",)),
    )(proj, bf)


def _lane_pick(x, lane):
    return jnp.sum(jnp.where(_iota(x.shape, 1) == lane, x, 0.0), axis=1, keepdims=True)


def _sublane_pick(x, row):
    return jnp.sum(jnp.where(_iota(x.shape, 0) == row, x, 0.0), axis=0, keepdims=True)


def _head_mask(hh):
    lane = _iota((1, LANES), 1)
    return (lane >= HEAD_DIM * hh) & (lane < HEAD_DIM * (hh + 1))


def _causal_pairs(nq, ratio, kv_major):
    if kv_major:
        pairs = [(q, k) for k in range(nq // ratio) for q in range(k * ratio, nq)]
    else:
        pairs = [(q, k) for q in range(nq) for k in range(q // ratio + 1)]
    qs = np.asarray([p[0] for p in pairs], np.int32)
    ks = np.asarray([p[1] for p in pairs], np.int32)
    return qs, ks, (ks == qs // ratio).astype(np.int32)


def _fox_scores(qm, kb, cum_ref, cumt_ref, h, qi, ki, diag, tq, tk):
    cq0 = _lane_pick(cum_ref[0:1, :], h)
    sc = _dot(qm, kb, NT) + (cq0 - _sublane_pick(cumt_ref[...], h))
    causal = (qi * tq + _iota((tq, tk), 0)) >= (ki * tk + _iota((tq, tk), 1))
    return jnp.where(causal | (diag == 0), sc, NEG_BIG)


def _fox_fwd(proj, cum, cumt, qblk, kblk, vblk, name):
    s = proj.shape[0]
    tq, tk = T_ATT_Q, T_ATT_K
    n_pair = 4
    qi_np, ki_np, diag_np = _causal_pairs(s // tq, tk // tq, kv_major=False)
    scale = HEAD_DIM ** -0.5

    def body(qi_ref, ki_ref, diag_ref, q_ref, k_ref, v_ref, cum_ref, cumt_ref, o_ref, lse_ref, m_s, l_s, acc_s):
        hp = pl.program_id(0)
        step = pl.program_id(1)
        qi, ki, diag = qi_ref[step], ki_ref[step], diag_ref[step]

        @pl.when(ki == 0)
        def _():
            m_s[...] = jnp.full_like(m_s, NEG_BIG)
            l_s[...] = jnp.zeros_like(l_s)
            acc_s[...] = jnp.zeros_like(acc_s)

        q = q_ref[...] * scale
        kb = k_ref[...].astype(BF16)
        vb = v_ref[...].astype(BF16)
        for hh in range(2):
            qm = jnp.where(_head_mask(hh), q, 0.0).astype(BF16)
            sc = _fox_scores(qm, kb, cum_ref, cumt_ref, 2 * hp + hh, qi, ki, diag, tq, tk)
            m_old = m_s[hh]
            m_new = jnp.maximum(m_old, jnp.max(sc, axis=1, keepdims=True))
            p = jnp.exp(sc - m_new)
            alpha = jnp.exp(m_old - m_new)
            l_s[hh] = alpha * l_s[hh] + jnp.sum(p, axis=1, keepdims=True)
            acc_s[hh] = alpha * acc_s[hh] + _dot(p.astype(BF16), vb, NN)
            m_s[hh] = m_new

        @pl.when(diag == 1)
        def _():
            lane = _iota((tq, LANES), 1)
            o_ref[...] = jnp.where(lane < HEAD_DIM, acc_s[0] / l_s[0], acc_s[1] / l_s[1])
            lse0 = m_s[0] + jnp.log(l_s[0])
            lse1 = m_s[1] + jnp.log(l_s[1])
            lse_ref[0] = jnp.where(lane == 0, lse0, jnp.where(lane == 1, lse1, 0.0))

    grid_spec = pltpu.PrefetchScalarGridSpec(
        num_scalar_prefetch=3, grid=(n_pair, len(qi_np)),
        in_specs=[pl.BlockSpec((tq, LANES), lambda hp, st, qi, ki, dg: (qi[st], qblk + hp)),
                  pl.BlockSpec((tk, LANES), lambda hp, st, qi, ki, dg: (ki[st], kblk + hp)),
                  pl.BlockSpec((tk, LANES), lambda hp, st, qi, ki, dg: (ki[st], vblk + hp)),
                  pl.BlockSpec((tq, LANES), lambda hp, st, qi, ki, dg: (qi[st], 0)),
                  pl.BlockSpec((8, tk), lambda hp, st, qi, ki, dg: (0, ki[st]))],
        out_specs=[pl.BlockSpec((tq, LANES), lambda hp, st, qi, ki, dg: (qi[st], hp)),
                   pl.BlockSpec((1, tq, LANES), lambda hp, st, qi, ki, dg: (hp, qi[st], 0))],
        scratch_shapes=[pltpu.VMEM((2, tq, 1), F32), pltpu.VMEM((2, tq, 1), F32), pltpu.VMEM((2, tq, LANES), F32)])
    return pl.pallas_call(
        body, name=name, grid_spec=grid_spec,
        out_shape=[jax.ShapeDtypeStruct((s, n_pair * LANES), F32), jax.ShapeDtypeStruct((n_pair, s, LANES), F32)],
        compiler_params=_params(("parallel", "arbitrary")),
    )(jnp.asarray(qi_np), jnp.asarray(ki_np), jnp.asarray(diag_np), proj, proj, proj, cum, cumt)


def _fox_bwd_prep(proj, dz, o, gblk, name):
    s, w = o.shape
    t = T_ELEM
    n_head = w // HEAD_DIM

    def body(gt_ref, dz_ref, o_ref, do_ref, dg_ref, dl_ref):
        gt = gt_ref[...]
        do = dz_ref[...] * _silu(gt)
        do_ref[...] = do
        dg_ref[...] = (dz_ref[...] * o_ref[...] * _dsilu(gt)).astype(BF16)
        sel = (_iota((w, LANES), 0) // HEAD_DIM == _iota((w, LANES), 1)).astype(F32)
        dl_ref[...] = _dot(do * o_ref[...], sel, NN, precision=lax.Precision.HIGHEST)

    assert n_head <= LANES
    row = pl.BlockSpec((t, w), lambda i: (i, 0))
    return pl.pallas_call(
        body, name=name, grid=(s // t,),
        in_specs=[pl.BlockSpec((t, w), lambda i: (i, gblk)), row, row],
        out_specs=[row, row, pl.BlockSpec((t, LANES), lambda i: (i, 0))],
        out_shape=[jax.ShapeDtypeStruct((s, w), F32), jax.ShapeDtypeStruct((s, w), BF16),
                   jax.ShapeDtypeStruct((s, LANES), F32)],
        compiler_params=_params(("parallel",)),
    )(proj, dz, o)


def _fox_bwd(proj, do, lse, delta, cum, cumt, qblk, kblk, vblk, name):
    s = proj.shape[0]
    tq, tk = T_ATT_Q, T_ATT_K
    nq = s // tq
    n_pair = 4
    qi_np, ki_np, diag_np = _causal_pairs(nq, tk // tq, kv_major=True)
    n_step = len(qi_np)
    scale = HEAD_DIM ** -0.5

    def body(qi_ref, ki_ref, diag_ref, q_ref, k_ref, v_ref, do_ref, lse_ref, dl_ref, cum_ref, cumt_ref,
             dq_ref, dk_ref, dv_ref, dct_ref, dcq_ref, dq_s, dk_s, dv_s, dc_s, dcq_s):
        hp = pl.program_id(0)
        step = pl.program_id(1)
        qi, ki, diag = qi_ref[step], ki_ref[step], diag_ref[step]

        @pl.when(step == 0)
        def _():
            dq_s[...] = jnp.zeros_like(dq_s)
            dcq_s[...] = jnp.zeros_like(dcq_s)

        @pl.when(qi == ki * (tk // tq))
        def _():
            dk_s[...] = jnp.zeros_like(dk_s)
            dv_s[...] = jnp.zeros_like(dv_s)
            dc_s[...] = jnp.zeros_like(dc_s)

        q = q_ref[...] * scale
        do = do_ref[...]
        kb = k_ref[...].astype(BF16)
        vb = v_ref[...].astype(BF16)
        sub = _iota((8, tk), 0)
        dq_new = jnp.zeros((tq, LANES), F32)
        dcq_new = jnp.zeros((tq, LANES), F32)
        lane = _iota((tq, LANES), 1)
        for hh in range(2):
            h = 2 * hp + hh
            hm = _head_mask(hh)
            qm = jnp.where(hm, q, 0.0).astype(BF16)
            dom = jnp.where(hm, do, 0.0).astype(BF16)
            sc = _fox_scores(qm, kb, cum_ref, cumt_ref, h, qi, ki, diag, tq, tk)
            p = jnp.exp(sc - _lane_pick(lse_ref[0], hh))
            dv_s[...] += _dot(p.astype(BF16), dom, TN)
            dp = _dot(dom, vb, NT)
            ds = p * (dp - _lane_pick(dl_ref[...], h))
            dc_s[...] += jnp.where(sub == h, -jnp.sum(ds, axis=0, keepdims=True), 0.0)
            dcq_new = dcq_new + jnp.where(lane == h, jnp.sum(ds, axis=1, keepdims=True), 0.0)
            dsb = ds.astype(BF16)
            dk_s[...] += _dot(dsb, qm, TN)
            dq_new = dq_new + jnp.where(hm, _dot(dsb, kb, NN), 0.0)
        row0 = pl.multiple_of(qi * tq, tq)
        dq_s[pl.ds(row0, tq), :] += dq_new * scale
        dcq_s[pl.ds(row0, tq), :] += dcq_new

        @pl.when(qi == nq - 1)
        def _():
            dk_ref[...] = dk_s[...].astype(BF16)
            dv_ref[...] = dv_s[...].astype(BF16)
            dct_ref[0] = dc_s[...]

        @pl.when(step == n_step - 1)
        def _():
            dq_ref[...] = dq_s[...].astype(BF16)
            dcq_ref[0] = dcq_s[...]

    grid_spec = pltpu.PrefetchScalarGridSpec(
        num_scalar_prefetch=3, grid=(n_pair, n_step),
        in_specs=[pl.BlockSpec((tq, LANES), lambda hp, st, qi, ki, dg: (qi[st], qblk + hp)),
                  pl.BlockSpec((tk, LANES), lambda hp, st, qi, ki, dg: (ki[st], kblk + hp)),
                  pl.BlockSpec((tk, LANES), lambda hp, st, qi, ki, dg: (ki[st], vblk + hp)),
                  pl.BlockSpec((tq, LANES), lambda hp, st, qi, ki, dg: (qi[st], hp)),
                  pl.BlockSpec((1, tq, LANES), lambda hp, st, qi, ki, dg: (hp, qi[st], 0)),
                  pl.BlockSpec((tq, LANES), lambda hp, st, qi, ki, dg: (qi[st], 0)),
                  pl.BlockSpec((tq, LANES), lambda hp, st, qi, ki, dg: (qi[st], 0)),
                  pl.BlockSpec((8, tk), lambda hp, st, qi, ki, dg: (0, ki[st]))],
        out_specs=[pl.BlockSpec((s, LANES), lambda hp, st, qi, ki, dg: (0, hp)),
                   pl.BlockSpec((tk, LANES), lambda hp, st, qi, ki, dg: (ki[st], hp)),
                   pl.BlockSpec((tk, LANES), lambda hp, st, qi, ki, dg: (ki[st], hp)),
                   pl.BlockSpec((1, 8, tk), lambda hp, st, qi, ki, dg: (hp, 0, ki[st])),
                   pl.BlockSpec((1, s, LANES), lambda hp, st, qi, ki, dg: (hp, 0, 0))],
        scratch_shapes=[pltpu.VMEM((s, LANES), F32), pltpu.VMEM((tk, LANES), F32), pltpu.VMEM((tk, LANES), F32),
                        pltpu.VMEM((8, tk), F32), pltpu.VMEM((s, LANES), F32)])
    w = n_pair * LANES
    return pl.pallas_call(
        body, name=name, grid_spec=grid_spec,
        out_shape=[jax.ShapeDtypeStruct((s, w), BF16), jax.ShapeDtypeStruct((s, w), BF16),
                   jax.ShapeDtypeStruct((s, w), BF16), jax.ShapeDtypeStruct((n_pair, 8, s), F32),
                   jax.ShapeDtypeStruct((n_pair, s, LANES), F32)],
        compiler_params=_params(("parallel", "arbitrary")),
    )(jnp.asarray(qi_np), jnp.asarray(ki_np), jnp.asarray(diag_np), proj, proj, proj, do, lse, delta, cum, cumt)


def _forget_bwd(dcumt4, dcumq4, sgt, name):
    s = sgt.shape[1]
    t = T_CUM
    n = s // t

    def body(d_ref, dq_ref, sg_ref, df_ref, dbf_ref, carry):
        i = pl.program_id(0)

        @pl.when(i == 0)
        def _():
            carry[...] = jnp.zeros_like(carry)
            dbf_ref[...] = jnp.zeros_like(dbf_ref)

        dq = dq_ref[0] + dq_ref[1] + dq_ref[2] + dq_ref[3]
        d = d_ref[0] + d_ref[1] + d_ref[2] + d_ref[3] + dq.T[0:8, :]
        upper = (_iota((t, t), 0) >= _iota((t, t), 1)).astype(F32)
        dlog = _dot(d, upper, NN, precision=lax.Precision.HIGHEST) + carry[:, 0:1]
        carry[...] += jnp.sum(d, axis=1, keepdims=True)
        dzt = dlog * sg_ref[...]
        dbf_ref[...] += jnp.sum(dzt, axis=1, keepdims=True)
        padded = jnp.concatenate([dzt, jnp.zeros((LANES - 8, t), F32)], axis=0)
        df_ref[...] = padded.T.astype(BF16)

    return pl.pallas_call(
        body, name=name, grid=(n,),
        in_specs=[pl.BlockSpec((4, 8, t), lambda i: (0, 0, n - 1 - i)),
                  pl.BlockSpec((4, t, LANES), lambda i: (0, n - 1 - i, 0)),
                  pl.BlockSpec((8, t), lambda i: (0, n - 1 - i))],
        out_specs=[pl.BlockSpec((t, LANES), lambda i: (n - 1 - i, 0)), pl.BlockSpec((8, LANES), lambda i: (0, 0))],
        out_shape=[jax.ShapeDtypeStruct((s, LANES), BF16), jax.ShapeDtypeStruct((8, LANES), F32)],
        scratch_shapes=[pltpu.VMEM((8, LANES), F32)],
        compiler_params=_params(("arbitrary",)),
    )(dcumt4, dcumq4, sgt)


def _mem_softmax(qm, kp):
    sc = _dot(qm, kp, NT) * (HEAD_DIM ** -0.5)
    e = jnp.exp(sc - jnp.max(sc, axis=1, keepdims=True))
    return e / jnp.sum(e, axis=1, keepdims=True)


def _branch_m_fwd(proj, mkv, mblk, name):
    s = proj.shape[0]
    t = T_M
    mw = mkv.shape[1] // 2
    ml = mkv.shape[0]

    def body(m_ref, kv_ref, z_ref):
        outs = []
        for pr in range(mw // LANES):
            qp = m_ref[:, pr * LANES:(pr + 1) * LANES]
            kp = kv_ref[:, pr * LANES:(pr + 1) * LANES].astype(BF16)
            vp = kv_ref[:, mw + pr * LANES:mw + (pr + 1) * LANES].astype(BF16)
            oh = []
            for hh in range(2):
                qm = jnp.where(_head_mask(hh), qp, 0.0).astype(BF16)
                oh.append(_dot(_mem_softmax(qm, kp).astype(BF16), vp, NN))
            outs.append(jnp.where(_iota((t, LANES), 1) < HEAD_DIM, oh[0], oh[1]))
        o = jnp.concatenate(outs, axis=1)
        z_ref[...] = (o * _silu(m_ref[:, mw:2 * mw])).astype(BF16)

    return pl.pallas_call(
        body, name=name, grid=(s // t,),
        in_specs=[pl.BlockSpec((t, 2 * mw), lambda i: (i, mblk)), pl.BlockSpec((ml, 2 * mw), lambda i: (0, 0))],
        out_specs=pl.BlockSpec((t, mw), lambda i: (i, 0)),
        out_shape=jax.ShapeDtypeStruct((s, mw), BF16),
        compiler_params=_params(("parallel",)),
    )(proj, mkv)


def _branch_m_bwd(proj, mkv, dz, mblk, name):
    s = proj.shape[0]
    t = T_M
    mw = mkv.shape[1] // 2
    ml = mkv.shape[0]
    scale = HEAD_DIM ** -0.5

    def body(m_ref, kv_ref, dz_ref, dm_ref, dkv_ref):
        i = pl.program_id(0)

        @pl.when(i == 0)
        def _():
            dkv_ref[...] = jnp.zeros_like(dkv_ref)

        gt = m_ref[:, mw:2 * mw]
        dz = dz_ref[...]
        do = dz * _silu(gt)
        outs = []
        for pr in range(mw // LANES):
            cols = slice(pr * LANES, (pr + 1) * LANES)
            vcols = slice(mw + pr * LANES, mw + (pr + 1) * LANES)
            qp = m_ref[:, cols]
            kp = kv_ref[:, cols].astype(BF16)
            vp = kv_ref[:, vcols].astype(BF16)
            dop = do[:, cols]
            oh = []
            dq = jnp.zeros((t, LANES), F32)
            dk = jnp.zeros((ml, LANES), F32)
            dv = jnp.zeros((ml, LANES), F32)
            for hh in range(2):
                hm = _head_mask(hh)
                qm = jnp.where(hm, qp, 0.0).astype(BF16)
                dom = jnp.where(hm, dop, 0.0).astype(BF16)
                p = _mem_softmax(qm, kp)
                pb = p.astype(BF16)
                oh.append(_dot(pb, vp, NN))
                dv = dv + _dot(pb, dom, TN)
                dp = _dot(dom, vp, NT)
                ds = p * (dp - jnp.sum(dp * p, axis=1, keepdims=True))
                dsb = (ds * scale).astype(BF16)
                dq = dq + jnp.where(hm, _dot(dsb, kp, NN), 0.0)
                dk = dk + _dot(dsb, qm, TN)
            outs.append(jnp.where(_iota((t, LANES), 1) < HEAD_DIM, oh[0], oh[1]))
            dm_ref[:, cols] = dq.astype(BF16)
            dkv_ref[:, cols] += dk
            dkv_ref[:, vcols] += dv
        o = jnp.concatenate(outs, axis=1)
        dm_ref[:, mw:2 * mw] = (dz * o * _dsilu(gt)).astype(BF16)

    return pl.pallas_call(
        body, name=name, grid=(s // t,),
        in_specs=[pl.BlockSpec((t, 2 * mw), lambda i: (i, mblk)), pl.BlockSpec((ml, 2 * mw), lambda i: (0, 0)),
                  pl.BlockSpec((t, mw), lambda i: (i, 0))],
        out_specs=[pl.BlockSpec((t, 2 * mw), lambda i: (i, 0)), pl.BlockSpec((ml, 2 * mw), lambda i: (0, 0))],
        out_shape=[jax.ShapeDtypeStruct((s, 2 * mw), BF16), jax.ShapeDtypeStruct((ml, 2 * mw), F32)],
        compiler_params=_params(("arbitrary",)),
    )(proj, mkv, dz)


def _merge_parts(z_refs, g_refs, pcat, bounds):
    ys, sgs = [], []
    merged = None
    for zr, gr, (lo, hi) in zip(z_refs, g_refs, bounds):
        y = _dot(zr[...], pcat[lo:hi, :], NN)
        sg = _sigmoid(gr[...])
        ys.append(y)
        sgs.append(sg)
        merged = sg * y if merged is None else merged + sg * y
    return ys, sgs, merged


def _branch_bounds(zs):
    bounds, lo = [], 0
    for z in zs:
        bounds.append((lo, lo + z.shape[1]))
        lo += z.shape[1]
    return bounds


def _merge_fwd(zs, proj, gblk, pcat, wout, x, lng, lnb, alpha, name):
    s, d = x.shape
    t = T_MERGE
    bounds = _branch_bounds(zs)

    def body(*refs):
        z_refs, g_refs = refs[0:4], refs[4:8]
        pcat_hbm, wout_hbm, x_ref, lng_ref, lnb_ref, y_ref, pcat_v, wout_v = refs[8:]

        @pl.when(pl.program_id(0) == 0)
        def _():
            pltpu.sync_copy(pcat_hbm, pcat_v)
            pltpu.sync_copy(wout_hbm, wout_v)

        _, _, merged = _merge_parts(z_refs, g_refs, pcat_v, bounds)
        h = alpha * x_ref[...] + _dot(merged.astype(BF16), wout_v[...], NN)
        xh, _ = _ln_stats(h)
        y_ref[...] = xh * lng_ref[...] + lnb_ref[...]

    row = pl.BlockSpec((t, d), lambda i: (i, 0))
    vec = pl.BlockSpec((1, d), lambda i: (0, 0))
    in_specs = ([pl.BlockSpec((t, z.shape[1]), lambda i: (i, 0)) for z in zs]
                + [pl.BlockSpec((t, d), lambda i, k=k: (i, gblk + k)) for k in range(4)]
                + [ANY, ANY, row, vec, vec])
    return pl.pallas_call(
        body, name=name, grid=(s // t,), in_specs=in_specs, out_specs=row,
        out_shape=jax.ShapeDtypeStruct((s, d), F32),
        scratch_shapes=[pltpu.VMEM(pcat.shape, BF16), pltpu.VMEM(wout.shape, BF16)],
        compiler_params=_params(("arbitrary",)),
    )(*zs, proj, proj, proj, proj, pcat, wout, x, lng, lnb)


def _merge_bwd(zs, proj, gblk, pcat, wout, x, lng, lnb, dy, alpha, name):
    s, d = x.shape
    t = T_MERGE
    n = s // t
    bounds = _branch_bounds(zs)

    def body(*refs):
        z_refs, g_refs = refs[0:4], refs[4:8]
        pcat_hbm, wout_hbm, x_ref, lng_ref, lnb_ref, dy_ref = refs[8:14]
        dx_ref, dg_ref = refs[14:16]
        dz_refs = refs[16:20]
        dpcat_hbm, dwout_hbm, dlng_ref, dlnb_ref = refs[20:24]
        pcat_v, wout_v, dpcat_v, dwout_v = refs[24:]
        i = pl.program_id(0)

        @pl.when(i == 0)
        def _():
            pltpu.sync_copy(pcat_hbm, pcat_v)
            pltpu.sync_copy(wout_hbm, wout_v)
            dpcat_v[...] = jnp.zeros_like(dpcat_v)
            dwout_v[...] = jnp.zeros_like(dwout_v)
            dlng_ref[...] = jnp.zeros_like(dlng_ref)
            dlnb_ref[...] = jnp.zeros_like(dlnb_ref)

        ys, sgs, merged = _merge_parts(z_refs, g_refs, pcat_v, bounds)
        mb = merged.astype(BF16)
        h = alpha * x_ref[...] + _dot(mb, wout_v[...], NN)
        xh, rstd = _ln_stats(h)
        dyv = dy_ref[...]
        dlng_ref[...] += jnp.sum(dyv * xh, axis=0, keepdims=True)
        dlnb_ref[...] += jnp.sum(dyv, axis=0, keepdims=True)
        dxh = dyv * lng_ref[...]
        dh = rstd * (dxh - jnp.mean(dxh, axis=-1, keepdims=True) - xh * jnp.mean(dxh * xh, axis=-1, keepdims=True))
        dx_ref[...] = alpha * dh
        dhb = dh.astype(BF16)
        dwout_v[...] += _dot(mb, dhb, TN)
        dmerged = _dot(dhb, wout_v[...], NT)
        for k, (zr, (lo, hi)) in enumerate(zip(z_refs, bounds)):
            sg = sgs[k]
            dg_ref[:, k * d:(k + 1) * d] = (dmerged * ys[k] * sg * (1.0 - sg)).astype(BF16)
            dyk = (dmerged * sg).astype(BF16)
            dpcat_v[lo:hi, :] += _dot(zr[...], dyk, TN)
            dz_refs[k][...] = _dot(dyk, pcat_v[lo:hi, :], NT)

        @pl.when(i == n - 1)
        def _():
            pltpu.sync_copy(dpcat_v, dpcat_hbm)
            pltpu.sync_copy(dwout_v, dwout_hbm)

    row = pl.BlockSpec((t, d), lambda i: (i, 0))
    vec = pl.BlockSpec((1, d), lambda i: (0, 0))
    z_specs = [pl.BlockSpec((t, z.shape[1]), lambda i: (i, 0)) for z in zs]
    in_specs = (z_specs + [pl.BlockSpec((t, d), lambda i, k=k: (i, gblk + k)) for k in range(4)]
                + [ANY, ANY, row, vec, vec, row])
    out_specs = [row, pl.BlockSpec((t, 4 * d), lambda i: (i, 0))] + z_specs + [ANY, ANY, vec, vec]
    vo = jax.ShapeDtypeStruct((1, d), F32)
    out_shape = ([jax.ShapeDtypeStruct((s, d), F32), jax.ShapeDtypeStruct((s, 4 * d), BF16)]
                 + [jax.ShapeDtypeStruct(z.shape, F32) for z in zs]
                 + [jax.ShapeDtypeStruct(pcat.shape, F32), jax.ShapeDtypeStruct(wout.shape, F32), vo, vo])
    return pl.pallas_call(
        body, name=name, grid=(n,), in_specs=in_specs, out_specs=out_specs, out_shape=out_shape,
        scratch_shapes=[pltpu.VMEM(pcat.shape, BF16), pltpu.VMEM(wout.shape, BF16),
                        pltpu.VMEM(pcat.shape, F32), pltpu.VMEM(wout.shape, F32)],
        compiler_params=_params(("arbitrary",)),
    )(*zs, proj, proj, proj, proj, pcat, wout, x, lng, lnb, dy)


def _loss_head(y, target, name):
    s, d = y.shape
    t = T_ELEM

    def body(y_ref, t_ref, dy_ref, loss_ref):
        @pl.when(pl.program_id(0) == 0)
        def _():
            loss_ref[...] = jnp.zeros_like(loss_ref)

        e = y_ref[...] - t_ref[...]
        dy_ref[...] = e * (1.0 / d)
        loss_ref[...] += 0.5 * jnp.sum(jnp.mean(e * e, axis=-1, keepdims=True), axis=0, keepdims=True)

    row = pl.BlockSpec((t, d), lambda i: (i, 0))
    return pl.pallas_call(
        body, name=name, grid=(s // t,), in_specs=[row, row],
        out_specs=[row, pl.BlockSpec((8, LANES), lambda i: (0, 0))],
        out_shape=[jax.ShapeDtypeStruct((s, d), F32), jax.ShapeDtypeStruct((8, LANES), F32)],
        compiler_params=_params(("arbitrary",)),
    )(y, target)


def _adamw(w, g, m, v, name):
    r, c = w.shape
    t = _pick(r, (256, 128, 64, 32, 16, 8))

    def body(w_ref, g_ref, m_ref, v_ref, d_ref, nm_ref, nv_ref):
        gv = g_ref[...]
        nm = ADAM_B1 * m_ref[...] + (1.0 - ADAM_B1) * gv
        nv = ADAM_B2 * v_ref[...] + (1.0 - ADAM_B2) * (gv * gv)
        m_hat = nm / (1.0 - ADAM_B1 ** ADAM_STEP)
        v_hat = nv / (1.0 - ADAM_B2 ** ADAM_STEP)
        d_ref[...] = -ADAM_LR * (m_hat / (jnp.sqrt(v_hat) + ADAM_EPS) + ADAM_WD * w_ref[...])
        nm_ref[...] = nm
        nv_ref[...] = nv

    blk = pl.BlockSpec((t, c), lambda i: (i, 0))
    o = jax.ShapeDtypeStruct((r, c), F32)
    return pl.pallas_call(body, name=name, grid=(r // t,), in_specs=[blk] * 4, out_specs=[blk] * 3,
                          out_shape=[o, o, o], compiler_params=_params(("parallel",)))(w, g, m, v)


def _sum_leading(x, out_dtype, name):
    k, r, c = x.shape
    t = _pick(r, (256, 128, 64, 32, 16, 8))

    def body(x_ref, o_ref):
        acc = x_ref[0].astype(F32)
        for j in range(1, k):
            acc = acc + x_ref[j].astype(F32)
        o_ref[...] = acc.astype(out_dtype)

    return pl.pallas_call(body, name=name, grid=(r // t,),
                          in_specs=[pl.BlockSpec((k, t, c), lambda i: (0, i, 0))],
                          out_specs=pl.BlockSpec((t, c), lambda i: (i, 0)),
                          out_shape=jax.ShapeDtypeStruct((r, c), out_dtype), compiler_params=_params(("parallel",)))(x)


def _position():
    return lax.axis_index("x"), lax.axis_index("y"), lax.axis_index("c")


def _chip_peers(x, y):
    return [(1 - x, y), (x, 1 - y), (1 - x, 1 - y)]


def _comm_call(body, n_in, out_shape, n_remote, n_local, name):
    return pl.pallas_call(
        body, name=name, in_specs=[ANY] * n_in, out_specs=[ANY] * len(out_shape), out_shape=out_shape,
        scratch_shapes=[pltpu.SemaphoreType.DMA((n_remote,)), pltpu.SemaphoreType.DMA((n_remote,)),
                        pltpu.SemaphoreType.DMA((max(n_local, 1),))])


def _run_copies(local, remote, send, recv, loc):
    copies = [pltpu.make_async_copy(src, dst, loc.at[k]) for k, (src, dst) in enumerate(local)]
    copies += [pltpu.make_async_remote_copy(src_ref=src, dst_ref=dst, send_sem=send.at[k], recv_sem=recv.at[k],
                                            device_id=peer, device_id_type=MESH)
               for k, (src, dst, peer) in enumerate(remote)]
    for cp in copies:
        cp.start()
    for cp in copies:
        cp.wait()


COPY_BYTES = 2 * 1024 * 1024


def _n_copies(rows, row_bytes, align):
    n = 8
    while n > 1 and (rows % (n * align) or rows // n * row_bytes < COPY_BYTES):
        n //= 2
    return n


def _allgather_chips(arrs, name):
    n = len(arrs)
    per_layer = [a.size * a.dtype.itemsize // a.shape[0] >= COPY_BYTES for a in arrs]
    n_each = [a.shape[0] if pl_ else 1 for a, pl_ in zip(arrs, per_layer)]

    def body(*refs):
        ins, outs = refs[:n], refs[n:2 * n]
        send, recv, loc = refs[2 * n:]
        x, y, c = _position()
        me = 2 * x + y
        local, remote = [], []
        for a in range(n):
            if per_layer[a]:
                parts = [(ins[a].at[l], outs[a].at[me, l]) for l in range(arrs[a].shape[0])]
            else:
                parts = [(ins[a], outs[a].at[me])]
            local += parts
            for px, py in _chip_peers(x, y):
                remote += [(src, dst, (px, py, c)) for src, dst in parts]
        _run_copies(local, remote, send, recv, loc)

    out_shape = [jax.ShapeDtypeStruct((4,) + a.shape, a.dtype) for a in arrs]
    return _comm_call(body, n, out_shape, 3 * sum(n_each), sum(n_each), name)(*arrs)


def _allgather_all(v, name):
    def body(v_ref, o_ref, send, recv, loc):
        x, y, c = _position()
        me = 4 * x + 2 * y + c
        remote = []
        for k in range(1, 8):
            fx, fy, fc = (k >> 2) & 1, (k >> 1) & 1, k & 1
            remote.append((v_ref, o_ref.at[me], (x ^ fx, y ^ fy, c ^ fc)))
        _run_copies([(v_ref, o_ref.at[me])], remote, send, recv, loc)

    return _comm_call(body, 1, [jax.ShapeDtypeStruct((8,) + v.shape, v.dtype)], 7, 1, name)(v)[0]


def _pair_exchange_sum(g, name):
    _, _, n, r, cc = g.shape

    def body(c_ref, mine_ref, send_ref, o_ref, land, send_sem, recv_sem):
        x, y, c = _position()
        slot = (pl.program_id(0) * n + pl.program_id(1)) % 2
        push = pltpu.make_async_remote_copy(
            src_ref=send_ref.at[0, 0, 0], dst_ref=land.at[slot], send_sem=send_sem.at[slot],
            recv_sem=recv_sem.at[slot], device_id=(x, y, 1 - c), device_id_type=MESH)
        push.start()
        push.wait_recv()
        o_ref[0, 0] = (mine_ref[0, 0, 0] + land[slot]).astype(BF16)
        push.wait_send()

    grid_spec = pltpu.PrefetchScalarGridSpec(
        num_scalar_prefetch=1, grid=(4, n),
        in_specs=[pl.BlockSpec((1, 1, 1, r, cc), lambda j, k, c: (j, c[0], k, 0, 0)),
                  pl.BlockSpec((1, 1, 1, r, cc), lambda j, k, c: (j, 1 - c[0], k, 0, 0))],
        out_specs=pl.BlockSpec((1, 1, r, cc), lambda j, k, c: (j, k, 0, 0)),
        scratch_shapes=[pltpu.VMEM((2, r, cc), F32), pltpu.SemaphoreType.DMA((2,)), pltpu.SemaphoreType.DMA((2,))])
    core = lax.axis_index("c").astype(jnp.int32).reshape(1)
    return pl.pallas_call(
        body, name=name, grid_spec=grid_spec, out_shape=jax.ShapeDtypeStruct((4, n, r, cc), BF16),
        compiler_params=_params(("arbitrary", "arbitrary")))(core, g, g)


def _chip_exchange(ps, name):
    n = len(ps)
    n_each = [p.shape[1] for p in ps]

    def body(*refs):
        ins, outs = refs[:n], refs[n:2 * n]
        send, recv, loc = refs[2 * n:]
        x, y, c = _position()
        me = 2 * x + y
        local, remote = [], []
        for a in range(n):
            for k in range(ps[a].shape[1]):
                local.append((ins[a].at[me, k], outs[a].at[me, k]))
                for px, py in _chip_peers(x, y):
                    remote.append((ins[a].at[2 * px + py, k], outs[a].at[me, k], (px, py, c)))
        _run_copies(local, remote, send, recv, loc)

    out_shape = [jax.ShapeDtypeStruct(p.shape, p.dtype) for p in ps]
    return _comm_call(body, n, out_shape, 3 * sum(n_each), sum(n_each), name)(*ps)


def _pair_allgather(rs, name):
    n = len(rs)
    n_each = [r.shape[0] for r in rs]

    def body(*refs):
        ins, outs = refs[:n], refs[n:2 * n]
        send, recv, loc = refs[2 * n:]
        x, y, c = _position()
        local, remote = [], []
        for a in range(n):
            for k in range(rs[a].shape[0]):
                local.append((ins[a].at[k], outs[a].at[c, k]))
                remote.append((ins[a].at[k], outs[a].at[c, k], (x, y, 1 - c)))
        _run_copies(local, remote, send, recv, loc)

    out_shape = [jax.ShapeDtypeStruct((2,) + r.shape, r.dtype) for r in rs]
    return _comm_call(body, n, out_shape, sum(n_each), sum(n_each), name)(*rs)


def _reduce_scatter(gs):
    ps = []
    for a, g in enumerate(gs):
        _, rows, c = g.shape
        k = _n_copies(rows // 2, c * 4, 16)
        ps.append(_pair_exchange_sum(g.reshape(4, 2, k, rows // (2 * k), c), f"rs_pair_exchange_sum_{a}"))
    qs = _chip_exchange(ps, "rs_chip_exchange")
    rs = [_sum_leading(q.reshape(4, -1, q.shape[-1]), F32, f"rs_chip_sum_{a}").reshape(q.shape[1:])
          for a, q in enumerate(qs)]
    outs = _pair_allgather(rs, "rs_pair_allgather")
    return [o.reshape(-1, o.shape[-1]) for o in outs]


def _pad_rows(a, rows):
    return jnp.pad(a, ((0, rows - a.shape[0]), (0, 0)))


def _shard_cols(a):
    r, c4 = a.shape
    return a.reshape(r, 4, c4 // 4).transpose(1, 0, 2)


def kernel(x, mem, w_in, b_forget, conv_a_w, conv_a_b, ln_a_g, ln_a_b, conv_b_w, w_kv_mem, mem_ln_g, mem_ln_b, p_a, p_b, p_c, p_m, w_out, ln_g, ln_b, loss_target, m_w_in, m_b_forget, m_conv_a_w, m_conv_a_b, m_ln_a_g, m_ln_a_b, m_conv_b_w, m_w_kv_mem, m_mem_ln_g, m_mem_ln_b, m_p_a, m_p_b, m_p_c, m_p_m, m_w_out, m_ln_g, m_ln_b, v_w_in, v_b_forget, v_conv_a_w, v_conv_a_b, v_ln_a_g, v_ln_a_b, v_conv_b_w, v_w_kv_mem, v_mem_ln_g, v_mem_ln_b, v_p_a, v_p_b, v_p_c, v_p_m, v_w_out, v_ln_g, v_ln_b):
    depth = w_in.shape[0]
    x0 = x[0]
    s, d = x0.shape
    aw = conv_a_w.shape[2] * 4
    mw = p_m.shape[1]
    n_head = b_forget.shape[1]
    alpha = (2.0 * depth) ** 0.25
    in_cols = w_in.shape[2] * 4
    assert aw == n_head * HEAD_DIM and mw % LANES == 0 and in_cols == 11 * aw + n_head + 2 * mw + 4 * d
    cf0 = 10 * aw
    n_main = in_cols - n_head
    fblk = n_main // LANES
    n_pad = n_main + LANES
    gblk = (11 * aw + 2 * mw) // d
    mblk = (11 * aw) // (2 * mw)
    qblk, kblk, vblk = 7 * aw // LANES, 8 * aw // LANES, 9 * aw // LANES
    assert (11 * aw + 2 * mw) % d == 0 and (11 * aw) % (2 * mw) == 0

    gathered = _allgather_chips(
        [w_in.astype(BF16), p_a.astype(BF16), p_b.astype(BF16), p_c.astype(BF16), p_m.astype(BF16),
         w_kv_mem.astype(BF16), w_out.astype(BF16), conv_a_w, conv_b_w], "gather_weights")
    w_in_g, p_a_g, p_b_g, p_c_g, p_m_g, w_kv_g, w_out_g, conv_a_g, conv_b_g = gathered

    def cols(g):
        return jnp.concatenate([g[j] for j in range(4)], axis=-1)

    def rows(g):
        return jnp.concatenate([g[j] for j in range(4)], axis=-2)

    w_full = cols(w_in_g)
    w_pad = jnp.concatenate([w_full[:, :, :cf0], w_full[:, :, cf0 + n_head:], w_full[:, :, cf0:cf0 + n_head],
                             jnp.zeros((depth, d, LANES - n_head), BF16)], axis=-1)
    pcat = jnp.concatenate([cols(p_a_g), cols(p_b_g), cols(p_c_g), cols(p_m_g)], axis=1)
    w_kv = rows(w_kv_g)
    wout = rows(w_out_g)
    conv_a = jnp.pad(cols(conv_a_g), ((0, 0), (0, HALO_A - CONV_A), (0, 0)))
    conv_b = jnp.pad(cols(conv_b_g), ((0, 0), (0, HALO_B - CONV_B), (0, 0)))
    bf_pad = jnp.pad(b_forget, ((0, 0), (0, LANES - n_head)))
    pieces = [(0, 3 * aw), (3 * aw, 7 * aw), (7 * aw, 10 * aw), (10 * aw, 11 * aw),
              (11 * aw, 11 * aw + 2 * mw), (11 * aw + 2 * mw, n_main), (n_main, n_pad)]

    mem_n = _ln_rows(mem[0], mem_ln_g[None], mem_ln_b[None], "mem_ln")

    xs, saved = [x0], []
    for l in range(depth):
        xl = xs[-1]
        proj = _mm(xl, w_pad[l], name=f"proj_{l}")
        za = _branch_a_fwd(proj, conv_a[l], conv_a_b[l][None], ln_a_g[l][None], ln_a_b[l][None], f"a_fwd_{l}")
        zb = _branch_b_fwd(proj, conv_b[l], f"b_fwd_{l}")
        cum, cumt, sgt = _forget_prep(proj, bf_pad[l][None], fblk, f"forget_prep_{l}")
        o_c, lse = _fox_fwd(proj, cum, cumt, qblk, kblk, vblk, f"fox_fwd_{l}")
        zc = _gate_mul(proj, o_c, 10, f"c_gate_{l}")
        mkv = _mm(mem_n, w_kv[l], name=f"mkv_{l}")
        zm = _branch_m_fwd(proj, mkv, mblk, f"m_fwd_{l}")
        zs = [za, zb, zc, zm]
        y = _merge_fwd(zs, proj, gblk, pcat[l], wout[l], xl, ln_g[l][None], ln_b[l][None], alpha, f"merge_fwd_{l}")
        xs.append(y)
        saved.append((proj, zs, cum, cumt, sgt, o_c, lse, mkv))

    dy, loss_part = _loss_head(xs[-1], loss_target[0], "loss_head")

    g_w_in, g_conv_a, g_conv_b, g_w_kv, g_pcat, g_wout = [], [], [], [], [], []
    small = []
    dmem_n = None
    for l in reversed(range(depth)):
        proj, zs, cum, cumt, sgt, o_c, lse, mkv = saved[l]
        xl = xs[l]
        (dx, d_g, dza, dzb, dzc, dzm, dpcat, dwout, dlng, dlnb) = _merge_bwd(
            zs, proj, gblk, pcat[l], wout[l], xl, ln_g[l][None], ln_b[l][None], dy, alpha, f"merge_bwd_{l}")
        d_m, dmkv = _branch_m_bwd(proj, mkv, dzm, mblk, f"m_bwd_{l}")
        g_w_kv.append(_mm(mem_n, dmkv, ta=True, name=f"dwkv_{l}"))
        dmem_n = _mm(dmkv, w_kv[l], tb=True, add=dmem_n, name=f"dmem_{l}")
        do, d_cg, delta = _fox_bwd_prep(proj, dzc, o_c, 10, f"fox_bwd_prep_{l}")
        dq, dk, dv, dcumt4, dcumq4 = _fox_bwd(proj, do, lse, delta, cum, cumt, qblk, kblk, vblk, f"fox_bwd_{l}")
        d_f, dbf = _forget_bwd(dcumt4, dcumq4, sgt, f"forget_bwd_{l}")
        d_b, dconv_b = _branch_b_bwd(proj, dzb, conv_b[l], f"b_bwd_{l}")
        d_a, dconv_a, dconv_ab, dlag, dlab = _branch_a_bwd(
            proj, dza, conv_a[l], conv_a_b[l][None], ln_a_g[l][None], ln_a_b[l][None], f"a_bwd_{l}")
        dparts = [d_a, d_b, jnp.concatenate([dq, dk, dv], axis=1), d_cg, d_m, d_g, d_f]
        dx = _input_grad(dparts, [w_pad[l][:, lo:hi] for lo, hi in pieces], dx, f"dx_{l}")
        dw_parts = [_mm(xl, dp, ta=True, name=f"dw_{l}_{k}") for k, dp in enumerate(dparts)]
        dw = jnp.concatenate(dw_parts[:3] + [dw_parts[6][:, :n_head]] + dw_parts[3:6], axis=1)
        g_w_in.append(dw)
        g_conv_a.append(dconv_a)
        g_conv_b.append(dconv_b)
        g_pcat.append(dpcat)
        g_wout.append(dwout)
        small.append([dbf[:, 0], dconv_ab[0], dlag[0], dlab[0], dlng[0], dlnb[0]])
        dy = dx
    grad_x = dy
    dmlg, dmlb = _ln_rows_param_grads(mem[0], dmem_n, "mem_ln_grads")
    for lst in (g_w_in, g_conv_a, g_conv_b, g_w_kv, g_pcat, g_wout, small):
        lst.reverse()

    pa_end, pb_end, pc_end = aw, 2 * aw, 3 * aw
    rs_in = [
        _shard_cols(jnp.concatenate(g_w_in, axis=0)),
        _shard_cols(jnp.concatenate(g_conv_a, axis=0)),
        _shard_cols(jnp.concatenate([_pad_rows(g, 2 * HALO_B) for g in g_conv_b], axis=0)),
        jnp.concatenate([g.reshape(4, g.shape[0] // 4, g.shape[1]) for g in g_w_kv], axis=1),
        _shard_cols(jnp.concatenate([g[:pa_end] for g in g_pcat], axis=0)),
        _shard_cols(jnp.concatenate([g[pa_end:pb_end] for g in g_pcat], axis=0)),
        _shard_cols(jnp.concatenate([g[pb_end:pc_end] for g in g_pcat], axis=0)),
        _shard_cols(jnp.concatenate([g[pc_end:] for g in g_pcat], axis=0)),
        jnp.concatenate([g.reshape(4, g.shape[0] // 4, g.shape[1]) for g in g_wout], axis=1),
    ]
    rs_out = _reduce_scatter(rs_in)
    gw_in = rs_out[0].reshape(depth, d, -1)
    g_ca = rs_out[1].reshape(depth, HALO_A, -1)[:, :CONV_A]
    g_cb = rs_out[2].reshape(depth, 2 * HALO_B, -1)[:, :CONV_B]
    gw_kv = rs_out[3].reshape(depth, -1, 2 * mw)
    gp_a = rs_out[4].reshape(depth, aw, -1)
    gp_b = rs_out[5].reshape(depth, aw, -1)
    gp_c = rs_out[6].reshape(depth, aw, -1)
    gp_m = rs_out[7].reshape(depth, mw, -1)
    gw_out = rs_out[8].reshape(depth, -1, d)

    flat = jnp.concatenate([jnp.concatenate(p) for p in small] + [dmlg[0], dmlb[0], loss_part[0, 0:1]])
    n_small = flat.shape[0]
    n_rows = -(-n_small // (8 * LANES)) * 8
    vec = jnp.pad(flat, (0, n_rows * LANES - n_small)).reshape(n_rows, LANES)
    tot = _sum_leading(_allgather_all(vec, "gather_small"), F32, "sum_small").reshape(-1)
    per_layer = n_head + 3 * aw + 2 * d
    tl = tot[:depth * per_layer].reshape(depth, per_layer)
    offs = np.cumsum([0, n_head, aw, aw, aw, d, d])
    g_bf, g_cab, g_lag, g_lab, g_lg, g_lb = [tl[:, offs[k]:offs[k + 1]] for k in range(6)]
    base = depth * per_layer
    g_mlg, g_mlb = tot[base:base + d], tot[base + d:base + 2 * d]
    loss = tot[base + 2 * d]

    grads = [gw_in, g_bf, g_ca, g_cab, g_lag, g_lab, g_cb, gw_kv, g_mlg, g_mlb, gp_a, gp_b, gp_c, gp_m, gw_out, g_lg, g_lb]
    ws = [w_in, b_forget, conv_a_w, conv_a_b, ln_a_g, ln_a_b, conv_b_w, w_kv_mem, mem_ln_g, mem_ln_b, p_a, p_b, p_c, p_m, w_out, ln_g, ln_b]
    ms = [m_w_in, m_b_forget, m_conv_a_w, m_conv_a_b, m_ln_a_g, m_ln_a_b, m_conv_b_w, m_w_kv_mem, m_mem_ln_g, m_mem_ln_b, m_p_a, m_p_b, m_p_c, m_p_m, m_w_out, m_ln_g, m_ln_b]
    vs = [v_w_in, v_b_forget, v_conv_a_w, v_conv_a_b, v_ln_a_g, v_ln_a_b, v_conv_b_w, v_w_kv_mem, v_mem_ln_g, v_mem_ln_b, v_p_a, v_p_b, v_p_c, v_p_m, v_w_out, v_ln_g, v_ln_b]
    deltas, new_ms, new_vs = [], [], []
    for k, (wk, gk, mk, vk) in enumerate(zip(ws, grads, ms, vs)):
        shape = wk.shape
        two_d = (1, shape[0]) if wk.ndim == 1 else (int(np.prod(shape[:-1])), shape[-1])
        dk_, nm_, nv_ = _adamw(wk.reshape(two_d), gk.reshape(two_d), mk.reshape(two_d), vk.reshape(two_d), f"adamw_{k}")
        deltas.append(dk_.reshape(shape))
        new_ms.append(nm_.reshape(shape))
        new_vs.append(nv_.reshape(shape))
        grads[k] = gk.reshape(shape)
    return (loss, grad_x[None], *grads, *deltas, *new_ms, *new_vs)


def _gate_mul(proj, o, gblk, name):
    s, w = o.shape
    t = T_ELEM

    def body(g_ref, o_ref, z_ref):
        z_ref[...] = (o_ref[...] * _silu(g_ref[...])).astype(BF16)

    row = pl.BlockSpec((t, w), lambda i: (i, 0))
    return pl.pallas_call(body, name=name, grid=(s // t,), in_specs=[pl.BlockSpec((t, w), lambda i: (i, gblk)), row],
                          out_specs=row, out_shape=jax.ShapeDtypeStruct((s, w), BF16),
                          compiler_params=_params(("parallel",)))(proj, o)
```

```python
import functools
import math

import numpy as np
import jax
import jax.numpy as jnp
from jax import lax
from jax.experimental import pallas as pl
from jax.experimental.pallas import tpu as pltpu

F32 = jnp.float32
BF16 = jnp.bfloat16
MESH = pl.DeviceIdType.MESH
ANY = pl.BlockSpec(memory_space=pl.ANY)

LN_EPS = 1e-5
NEG_BIG = -1e30
HEAD_DIM = 64
LANES = 128
CONV_A = 31
CONV_B = 3
HALO_A = 32
HALO_B = 8
CHUNK = 32
VMEM_LIMIT = 60 * 1024 * 1024

ADAM_LR, ADAM_B1, ADAM_B2, ADAM_EPS, ADAM_WD, ADAM_STEP = 0.001, 0.9, 0.999, 1e-08, 0.01, 10

T_MM = 512
T_A = 128
T_B = 256
T_ATT_Q = 512
T_ATT_K = 1024
T_CUM = 512
T_M = 512
T_MERGE = 256
T_DX = 256
T_ELEM = 512


def _pick(n, prefs):
    for p in prefs:
        if n % p == 0:
            return p
    return n


def _params(sem=None):
    return pltpu.CompilerParams(dimension_semantics=sem, vmem_limit_bytes=VMEM_LIMIT)


def _sigmoid(x):
    return jax.nn.sigmoid(x)


def _silu(x):
    return x * _sigmoid(x)


def _dsilu(x):
    s = _sigmoid(x)
    return s * (1.0 + x * (1.0 - s))


def _dot(a, b, dims, precision=None):
    return lax.dot_general(a, b, (dims, ((), ())), preferred_element_type=F32, precision=precision)


NN = ((1,), (0,))
NT = ((1,), (1,))
TN = ((0,), (0,))


def _iota(shape, dim):
    return lax.broadcasted_iota(jnp.int32, shape, dim)


def _mm(a, b, *, ta=False, tb=False, add=None, out_dtype=F32, name):
    m = a.shape[1] if ta else a.shape[0]
    k = a.shape[0] if ta else a.shape[1]
    n = b.shape[0] if tb else b.shape[1]
    tm = _pick(m, (1024, 512, 256)) if ta else _pick(m, (T_MM, 256))
    tn = _pick(n, (1152, 1024, 768, 512, 384, 256, 128))
    tk = _pick(k, (1024, 512, 256))
    nk = k // tk
    dims = ((0,) if ta else (1,), (1,) if tb else (0,))

    def body(*refs):
        if add is None:
            a_ref, b_ref, o_ref, acc_ref = refs
        else:
            a_ref, b_ref, add_ref, o_ref, acc_ref = refs
        kk = pl.program_id(2)
        p = _dot(a_ref[...].astype(BF16), b_ref[...].astype(BF16), dims)

        @pl.when(kk == 0)
        def _():
            acc_ref[...] = p

        @pl.when(kk > 0)
        def _():
            acc_ref[...] += p

        @pl.when(kk == nk - 1)
        def _():
            r = acc_ref[...]
            if add is not None:
                r = r + add_ref[...]
            o_ref[...] = r.astype(out_dtype)

    a_spec = (pl.BlockSpec((tk, tm), lambda j, i, kk: (kk, i)) if ta
              else pl.BlockSpec((tm, tk), lambda j, i, kk: (i, kk)))
    b_spec = (pl.BlockSpec((tn, tk), lambda j, i, kk: (j, kk)) if tb
              else pl.BlockSpec((tk, tn), lambda j, i, kk: (kk, j)))
    o_spec = pl.BlockSpec((tm, tn), lambda j, i, kk: (i, j))
    in_specs = [a_spec, b_spec] + ([o_spec] if add is not None else [])
    args = (a, b) + ((add,) if add is not None else ())
    return pl.pallas_call(
        body, name=name, grid=(n // tn, m // tm, nk), in_specs=in_specs, out_specs=o_spec,
        out_shape=jax.ShapeDtypeStruct((m, n), out_dtype),
        scratch_shapes=[pltpu.VMEM((tm, tn), F32)],
        compiler_params=_params(("parallel", "parallel", "arbitrary")),
    )(*args)


def _input_grad(dparts, wparts, add, name):
    s, d = add.shape
    t = T_DX
    n_p = len(dparts)

    def body(*refs):
        d_refs, w_hbm = refs[:n_p], refs[n_p:2 * n_p]
        add_ref, o_ref = refs[2 * n_p], refs[2 * n_p + 1]
        w_v = refs[2 * n_p + 2:]

        @pl.when(pl.program_id(0) == 0)
        def _():
            for p in range(n_p):
                pltpu.sync_copy(w_hbm[p], w_v[p])

        acc = add_ref[...]
        for p in range(n_p):
            acc = acc + _dot(d_refs[p][...], w_v[p][...], NT)
        o_ref[...] = acc

    row = pl.BlockSpec((t, d), lambda i: (i, 0))
    in_specs = ([pl.BlockSpec((t, dp.shape[1]), lambda i: (i, 0)) for dp in dparts] + [ANY] * n_p + [row])
    return pl.pallas_call(
        body, name=name, grid=(s // t,), in_specs=in_specs, out_specs=row,
        out_shape=jax.ShapeDtypeStruct((s, d), F32),
        scratch_shapes=[pltpu.VMEM(w.shape, BF16) for w in wparts],
        compiler_params=_params(("arbitrary",)),
    )(*dparts, *wparts, add)


def _ln_rows(x, g, b, name):
    r, d = x.shape
    t = _pick(r, (256,))

    def body(x_ref, g_ref, b_ref, o_ref):
        xv = x_ref[...]
        mu = jnp.mean(xv, axis=-1, keepdims=True)
        dv = xv - mu
        var = jnp.mean(dv * dv, axis=-1, keepdims=True)
        o_ref[...] = dv * lax.rsqrt(var + LN_EPS) * g_ref[...] + b_ref[...]

    row = pl.BlockSpec((t, d), lambda i: (i, 0))
    vec = pl.BlockSpec((1, d), lambda i: (0, 0))
    return pl.pallas_call(body, name=name, grid=(r // t,), in_specs=[row, vec, vec], out_specs=row,
                          out_shape=jax.ShapeDtypeStruct((r, d), F32), compiler_params=_params(("parallel",)))(x, g, b)


def _ln_rows_param_grads(x, dy, name):
    r, d = x.shape

    def body(x_ref, dy_ref, dg_ref, db_ref):
        xv = x_ref[...]
        mu = jnp.mean(xv, axis=-1, keepdims=True)
        dv = xv - mu
        var = jnp.mean(dv * dv, axis=-1, keepdims=True)
        xh = dv * lax.rsqrt(var + LN_EPS)
        dg_ref[...] = jnp.sum(dy_ref[...] * xh, axis=0, keepdims=True)
        db_ref[...] = jnp.sum(dy_ref[...], axis=0, keepdims=True)

    full = pl.BlockSpec((r, d), lambda i: (0, 0))
    vec = pl.BlockSpec((1, d), lambda i: (0, 0))
    o = jax.ShapeDtypeStruct((1, d), F32)
    return pl.pallas_call(body, name=name, grid=(1,), in_specs=[full, full], out_specs=[vec, vec],
                          out_shape=[o, o], compiler_params=_params(("arbitrary",)))(x, dy)


def _conv_a_chunk(glu_ref, cw_ref, r0):
    acc = cw_ref[0:1, :] * glu_ref[pl.ds(r0 + 2, CHUNK), :]
    for k in range(1, CONV_A):
        acc = acc + cw_ref[k:k + 1, :] * glu_ref[pl.ds(r0 + 2 + k, CHUNK), :]
    return acc


def _ln_stats(c):
    mu = jnp.mean(c, axis=-1, keepdims=True)
    d = c - mu
    var = jnp.mean(d * d, axis=-1, keepdims=True)
    rstd = lax.rsqrt(var + LN_EPS)
    return d * rstd, rstd


def _branch_a_fwd(proj, cw, cb, lg, lb, name):
    s = proj.shape[0]
    t = T_A
    w = cw.shape[1]
    r = t // HALO_A

    def body(u_ref, v_ref, gt_ref, hu_ref, hv_ref, cw_ref, cb_ref, lg_ref, lb_ref, z_ref, conv_ref, glu):
        i = pl.program_id(0)
        hglu = hu_ref[...] * _sigmoid(hv_ref[...])
        glu[0:HALO_A, :] = jnp.where(i > 0, hglu, 0.0)
        glu[HALO_A:HALO_A + t, :] = u_ref[...] * _sigmoid(v_ref[...])
        for c in range(t // CHUNK):
            r0 = c * CHUNK
            conv = _conv_a_chunk(glu, cw_ref, r0) + cb_ref[...]
            conv_ref[r0:r0 + CHUNK, :] = conv
            xh, _ = _ln_stats(conv)
            a3 = _silu(xh * lg_ref[...] + lb_ref[...])
            z_ref[r0:r0 + CHUNK, :] = (a3 * _silu(gt_ref[r0:r0 + CHUNK, :])).astype(BF16)

    def cur(col):
        return pl.BlockSpec((t, w), lambda i, col=col: (i, col))

    def prev(col):
        return pl.BlockSpec((HALO_A, w), lambda i, col=col: (jnp.maximum(i * r - 1, 0), col))

    vec = pl.BlockSpec((1, w), lambda i: (0, 0))
    return pl.pallas_call(
        body, name=name, grid=(s // t,),
        in_specs=[cur(0), cur(1), cur(2), prev(0), prev(1), pl.BlockSpec((HALO_A, w), lambda i: (0, 0)), vec, vec, vec],
        out_specs=[pl.BlockSpec((t, w), lambda i: (i, 0)), pl.BlockSpec((t, w), lambda i: (i, 0))],
        out_shape=[jax.ShapeDtypeStruct((s, w), BF16), jax.ShapeDtypeStruct((s, w), F32)],
        scratch_shapes=[pltpu.VMEM((HALO_A + t, w), F32)],
        compiler_params=_params(("parallel",)),
    )(proj, proj, proj, proj, proj, cw, cb, lg, lb)


def _branch_a_bwd(proj, conv, dz, cw, lg, lb, name):
    s = proj.shape[0]
    t = T_A
    w = cw.shape[1]
    r = t // HALO_A
    n = s // t
    nblk = s // HALO_A
    ext = t + HALO_A

    def body(u_ref, v_ref, gt_ref, dz_ref, conv_ref, pu_ref, pv_ref, ngt_ref, ndz_ref, nconv_ref,
             cw_ref, lg_ref, lb_ref, da_ref, dw_ref, dcb_ref, dlg_ref, dlb_ref, glu, dc, dw8):
        i = pl.program_id(0)

        @pl.when(i == 0)
        def _():
            dw8[...] = jnp.zeros_like(dw8)
            dcb_ref[...] = jnp.zeros_like(dcb_ref)
            dlg_ref[...] = jnp.zeros_like(dlg_ref)
            dlb_ref[...] = jnp.zeros_like(dlb_ref)

        glu[0:HALO_A, :] = jnp.where(i > 0, pu_ref[...] * _sigmoid(pv_ref[...]), 0.0)
        glu[HALO_A:HALO_A + t, :] = u_ref[...] * _sigmoid(v_ref[...])
        has_next = i < n - 1
        dcb = jnp.zeros((1, w), F32)
        dlg = jnp.zeros((1, w), F32)
        dlb = jnp.zeros((1, w), F32)
        for c in range(ext // CHUNK):
            r0 = c * CHUNK
            own = r0 < t
            xh, rstd = _ln_stats(conv_ref[r0:r0 + CHUNK, :] if own else nconv_ref[...])
            a2 = xh * lg_ref[...] + lb_ref[...]
            if own:
                gt = gt_ref[r0:r0 + CHUNK, :]
                dzc = dz_ref[r0:r0 + CHUNK, :]
            else:
                gt = ngt_ref[...]
                dzc = ndz_ref[...]
            da2 = dzc * _silu(gt) * _dsilu(a2)
            dxh = da2 * lg_ref[...]
            dconv = rstd * (dxh - jnp.mean(dxh, axis=-1, keepdims=True)
                            - xh * jnp.mean(dxh * xh, axis=-1, keepdims=True))
            if own:
                dc[r0:r0 + CHUNK, :] = dconv
                da_ref[r0:r0 + CHUNK, 2 * w:3 * w] = (dzc * _silu(a2) * _dsilu(gt)).astype(BF16)
                dcb = dcb + jnp.sum(dconv, axis=0, keepdims=True)
                dlg = dlg + jnp.sum(da2 * xh, axis=0, keepdims=True)
                dlb = dlb + jnp.sum(da2, axis=0, keepdims=True)
            else:
                dc[r0:r0 + CHUNK, :] = jnp.where(has_next, dconv, 0.0)
        dcb_ref[...] += dcb
        dlg_ref[...] += dlg
        dlb_ref[...] += dlb
        for c in range(t // CHUNK):
            r0 = c * CHUNK
            dcc = dc[r0:r0 + CHUNK, :]
            dglu = cw_ref[0:1, :] * dc[pl.ds(r0 + CONV_A - 1, CHUNK), :]
            for k in range(1, CONV_A):
                dglu = dglu + cw_ref[k:k + 1, :] * dc[pl.ds(r0 + CONV_A - 1 - k, CHUNK), :]
            for k in range(CONV_A):
                prod = dcc * glu[pl.ds(r0 + 2 + k, CHUNK), :]
                dw8[k] += jnp.sum(prod.reshape(CHUNK // 8, 8, w), axis=0)
            sv = _sigmoid(v_ref[r0:r0 + CHUNK, :])
            da_ref[r0:r0 + CHUNK, 0:w] = (dglu * sv).astype(BF16)
            da_ref[r0:r0 + CHUNK, w:2 * w] = (dglu * u_ref[r0:r0 + CHUNK, :] * sv * (1.0 - sv)).astype(BF16)

        @pl.when(i == n - 1)
        def _():
            dw_ref[...] = jnp.sum(dw8[...], axis=1)

    def cur(col):
        return pl.BlockSpec((t, w), lambda i, col=col: (i, col))

    def prev(col):
        return pl.BlockSpec((HALO_A, w), lambda i, col=col: (jnp.maximum(i * r - 1, 0), col))

    def nxt(col):
        return pl.BlockSpec((HALO_A, w), lambda i, col=col: (jnp.minimum((i + 1) * r, nblk - 1), col))

    own_rows = pl.BlockSpec((t, w), lambda i: (i, 0))
    vec = pl.BlockSpec((1, w), lambda i: (0, 0))
    vo = jax.ShapeDtypeStruct((1, w), F32)
    return pl.pallas_call(
        body, name=name, grid=(n,),
        in_specs=[cur(0), cur(1), cur(2), own_rows, own_rows, prev(0), prev(1), nxt(2), nxt(0), nxt(0),
                  pl.BlockSpec((HALO_A, w), lambda i: (0, 0)), vec, vec],
        out_specs=[pl.BlockSpec((t, 3 * w), lambda i: (i, 0)), pl.BlockSpec((HALO_A, w), lambda i: (0, 0)), vec, vec, vec],
        out_shape=[jax.ShapeDtypeStruct((s, 3 * w), BF16), jax.ShapeDtypeStruct((HALO_A, w), F32), vo, vo, vo],
        scratch_shapes=[pltpu.VMEM((HALO_A + t, w), F32), pltpu.VMEM((ext, w), F32),
                        pltpu.VMEM((HALO_A, 8, w), F32)],
        compiler_params=_params(("arbitrary",)),
    )(proj, proj, proj, dz, conv, proj, proj, proj, dz, conv, cw, lg, lb)


def _conv_b(u_ext, cw_ref, t):
    acc = cw_ref[0:1, :] * u_ext[pl.ds(HALO_B - 2, t), :]
    for k in range(1, CONV_B):
        acc = acc + cw_ref[k:k + 1, :] * u_ext[pl.ds(HALO_B - 2 + k, t), :]
    return acc


def _branch_b_fwd(proj, cw, name):
    s = proj.shape[0]
    t = T_B
    w = cw.shape[1]
    r = t // HALO_B

    def body(h_ref, b_ref, c_ref, gt_ref, ph_ref, pc_ref, cw_ref, z_ref, u_ext):
        i = pl.program_id(0)
        u_ext[0:HALO_B, :] = jnp.where(i > 0, pc_ref[...] * ph_ref[...], 0.0)
        u_ext[HALO_B:HALO_B + t, :] = c_ref[...] * h_ref[...]
        cv = _conv_b(u_ext, cw_ref, t)
        z_ref[...] = (b_ref[...] * cv * _silu(gt_ref[...])).astype(BF16)

    def cur(col):
        return pl.BlockSpec((t, w), lambda i, col=col: (i, col))

    def prev(col):
        return pl.BlockSpec((HALO_B, w), lambda i, col=col: (jnp.maximum(i * r - 1, 0), col))

    return pl.pallas_call(
        body, name=name, grid=(s // t,),
        in_specs=[cur(3), cur(4), cur(5), cur(6), prev(3), prev(5), pl.BlockSpec((HALO_B, w), lambda i: (0, 0))],
        out_specs=pl.BlockSpec((t, w), lambda i: (i, 0)),
        out_shape=jax.ShapeDtypeStruct((s, w), BF16),
        scratch_shapes=[pltpu.VMEM((HALO_B + t, w), F32)],
        compiler_params=_params(("parallel",)),
    )(proj, proj, proj, proj, proj, proj, cw)


def _branch_b_bwd(proj, dz, cw, name):
    s = proj.shape[0]
    t = T_B
    w = cw.shape[1]
    r = t // HALO_B
    n = s // t
    nblk = s // HALO_B

    def body(h_ref, b_ref, c_ref, gt_ref, dz_ref, ph_ref, pc_ref, nb_ref, ngt_ref, ndz_ref, cw_ref,
             db_ref, dw_ref, u_ext, dcv_ext, dw8):
        i = pl.program_id(0)

        @pl.when(i == 0)
        def _():
            dw8[...] = jnp.zeros_like(dw8)

        u_ext[0:HALO_B, :] = jnp.where(i > 0, pc_ref[...] * ph_ref[...], 0.0)
        u_ext[HALO_B:HALO_B + t, :] = c_ref[...] * h_ref[...]
        cv = _conv_b(u_ext, cw_ref, t)
        gt = gt_ref[...]
        dhb = dz_ref[...] * _silu(gt)
        db_ref[:, 3 * w:4 * w] = (dz_ref[...] * b_ref[...] * cv * _dsilu(gt)).astype(BF16)
        db_ref[:, w:2 * w] = (dhb * cv).astype(BF16)
        dcv = dhb * b_ref[...]
        dcv_ext[0:t, :] = dcv
        ndcv = ndz_ref[...] * _silu(ngt_ref[...]) * nb_ref[...]
        dcv_ext[t:t + HALO_B, :] = jnp.where(i < n - 1, ndcv, 0.0)
        du = cw_ref[0:1, :] * dcv_ext[pl.ds(2, t), :]
        for k in range(1, CONV_B):
            du = du + cw_ref[k:k + 1, :] * dcv_ext[pl.ds(2 - k, t), :]
        db_ref[:, 2 * w:3 * w] = (du * h_ref[...]).astype(BF16)
        db_ref[:, 0:w] = (du * c_ref[...]).astype(BF16)
        for k in range(CONV_B):
            prod = dcv * u_ext[pl.ds(HALO_B - 2 + k, t), :]
            dw8[k] += jnp.sum(prod.reshape(t // 8, 8, w), axis=0)

        @pl.when(i == n - 1)
        def _():
            dw_ref[...] = jnp.sum(dw8[...], axis=1)

    def cur(col):
        return pl.BlockSpec((t, w), lambda i, col=col: (i, col))

    def prev(col):
        return pl.BlockSpec((HALO_B, w), lambda i, col=col: (jnp.maximum(i * r - 1, 0), col))

    def nxt(col):
        return pl.BlockSpec((HALO_B, w), lambda i, col=col: (jnp.minimum((i + 1) * r, nblk - 1), col))

    return pl.pallas_call(
        body, name=name, grid=(n,),
        in_specs=[cur(3), cur(4), cur(5), cur(6), pl.BlockSpec((t, w), lambda i: (i, 0)),
                  prev(3), prev(5), nxt(4), nxt(6),
                  pl.BlockSpec((HALO_B, w), lambda i: (jnp.minimum((i + 1) * r, nblk - 1), 0)),
                  pl.BlockSpec((HALO_B, w), lambda i: (0, 0))],
        out_specs=[pl.BlockSpec((t, 4 * w), lambda i: (i, 0)), pl.BlockSpec((HALO_B, w), lambda i: (0, 0))],
        out_shape=[jax.ShapeDtypeStruct((s, 4 * w), BF16), jax.ShapeDtypeStruct((HALO_B, w), F32)],
        scratch_shapes=[pltpu.VMEM((HALO_B + t, w), F32), pltpu.VMEM((t + HALO_B, w), F32),
                        pltpu.VMEM((HALO_B, 8, w), F32)],
        compiler_params=_params(("arbitrary",)),
    )(proj, proj, proj, proj, dz, proj, proj, proj, proj, dz, cw)


def _forget_prep(proj, bf, fblk, name):
    s = proj.shape[0]
    t = T_CUM

    def body(f_ref, bf_ref, cum_ref, cumt_ref, sgt_ref, carry):
        i = pl.program_id(0)

        @pl.when(i == 0)
        def _():
            carry[...] = jnp.zeros_like(carry)

        z = f_ref[...] + bf_ref[...]
        logf = jnp.minimum(z, 0.0) - jnp.log1p(jnp.exp(-jnp.abs(z)))
        tri = (_iota((t, t), 0) >= _iota((t, t), 1)).astype(F32)
        cum = _dot(tri, logf, NN, precision=lax.Precision.HIGHEST) + carry[0:1, :]
        carry[0:1, :] = cum[t - 1:t, :]
        cum_ref[...] = cum
        cumt_ref[...] = cum.T[0:8, :]
        sgt_ref[...] = _sigmoid(-z).T[0:8, :]

    return pl.pallas_call(
        body, name=name, grid=(s // t,),
        in_specs=[pl.BlockSpec((t, LANES), lambda i: (i, fblk)), pl.BlockSpec((1, LANES), lambda i: (0, 0))],
        out_specs=[pl.BlockSpec((t, LANES), lambda i: (i, 0)), pl.BlockSpec((8, t), lambda i: (0, i)),
                   pl.BlockSpec((8, t), lambda i: (0, i))],
        out_shape=[jax.ShapeDtypeStruct((s, LANES), F32), jax.ShapeDtypeStruct((8, s), F32),
                   jax.ShapeDtypeStruct((8, s), F32)],
        scratch_shapes=[pltpu.VMEM((8, LANES), F32)],
        compiler_params=_params(("arbitrary",)),
    )(proj, bf)


def _lane_pick(x, lane):
    return jnp.sum(jnp.where(_iota(x.shape, 1) == lane, x, 0.0), axis=1, keepdims=True)


def _sublane_pick(x, row):
    return jnp.sum(jnp.where(_iota(x.shape, 0) == row, x, 0.0), axis=0, keepdims=True)


def _head_mask(hh):
    lane = _iota((1, LANES), 1)
    return (lane >= HEAD_DIM * hh) & (lane < HEAD_DIM * (hh + 1))


def _causal_pairs(nq, ratio, kv_major):
    if kv_major:
        pairs = [(q, k) for k in range(nq // ratio) for q in range(k * ratio, nq)]
    else:
        pairs = [(q, k) for q in range(nq) for k in range(q // ratio + 1)]
    qs = np.asarray([p[0] for p in pairs], np.int32)
    ks = np.asarray([p[1] for p in pairs], np.int32)
    return qs, ks, (ks == qs // ratio).astype(np.int32)


def _fox_scores(qm, kb, cum_ref, cumt_ref, h, qi, ki, diag, tq, tk):
    cq0 = _lane_pick(cum_ref[0:1, :], h)
    sc = _dot(qm, kb, NT) + (cq0 - _sublane_pick(cumt_ref[...], h))
    causal = (qi * tq + _iota((tq, tk), 0)) >= (ki * tk + _iota((tq, tk), 1))
    return jnp.where(causal | (diag == 0), sc, NEG_BIG)


def _fox_fwd(proj, cum, cumt, qblk, kblk, vblk, name):
    s = proj.shape[0]
    tq, tk = T_ATT_Q, T_ATT_K
    n_pair = 4
    qi_np, ki_np, diag_np = _causal_pairs(s // tq, tk // tq, kv_major=False)
    scale = HEAD_DIM ** -0.5

    def body(qi_ref, ki_ref, diag_ref, q_ref, k_ref, v_ref, cum_ref, cumt_ref, o_ref, lse_ref, m_s, l_s, acc_s):
        hp = pl.program_id(0)
        step = pl.program_id(1)
        qi, ki, diag = qi_ref[step], ki_ref[step], diag_ref[step]

        @pl.when(ki == 0)
        def _():
            m_s[...] = jnp.full_like(m_s, NEG_BIG)
            l_s[...] = jnp.zeros_like(l_s)
            acc_s[...] = jnp.zeros_like(acc_s)

        q = q_ref[...] * scale
        kb = k_ref[...].astype(BF16)
        vb = v_ref[...].astype(BF16)
        for hh in range(2):
            qm = jnp.where(_head_mask(hh), q, 0.0).astype(BF16)
            sc = _fox_scores(qm, kb, cum_ref, cumt_ref, 2 * hp + hh, qi, ki, diag, tq, tk)
            m_old = m_s[hh]
            m_new = jnp.maximum(m_old, jnp.max(sc, axis=1, keepdims=True))
            p = jnp.exp(sc - m_new)
            alpha = jnp.exp(m_old - m_new)
            l_s[hh] = alpha * l_s[hh] + jnp.sum(p, axis=1, keepdims=True)
            acc_s[hh] = alpha * acc_s[hh] + _dot(p.astype(BF16), vb, NN)
            m_s[hh] = m_new

        @pl.when(diag == 1)
        def _():
            lane = _iota((tq, LANES), 1)
            o_ref[...] = jnp.where(lane < HEAD_DIM, acc_s[0] / l_s[0], acc_s[1] / l_s[1])
            lse0 = m_s[0] + jnp.log(l_s[0])
            lse1 = m_s[1] + jnp.log(l_s[1])
            lse_ref[0] = jnp.where(lane == 0, lse0, jnp.where(lane == 1, lse1, 0.0))

    grid_spec = pltpu.PrefetchScalarGridSpec(
        num_scalar_prefetch=3, grid=(n_pair, len(qi_np)),
        in_specs=[pl.BlockSpec((tq, LANES), lambda hp, st, qi, ki, dg: (qi[st], qblk + hp)),
                  pl.BlockSpec((tk, LANES), lambda hp, st, qi, ki, dg: (ki[st], kblk + hp)),
                  pl.BlockSpec((tk, LANES), lambda hp, st, qi, ki, dg: (ki[st], vblk + hp)),
                  pl.BlockSpec((tq, LANES), lambda hp, st, qi, ki, dg: (qi[st], 0)),
                  pl.BlockSpec((8, tk), lambda hp, st, qi, ki, dg: (0, ki[st]))],
        out_specs=[pl.BlockSpec((tq, LANES), lambda hp, st, qi, ki, dg: (qi[st], hp)),
                   pl.BlockSpec((1, tq, LANES), lambda hp, st, qi, ki, dg: (hp, qi[st], 0))],
        scratch_shapes=[pltpu.VMEM((2, tq, 1), F32), pltpu.VMEM((2, tq, 1), F32), pltpu.VMEM((2, tq, LANES), F32)])
    return pl.pallas_call(
        body, name=name, grid_spec=grid_spec,
        out_shape=[jax.ShapeDtypeStruct((s, n_pair * LANES), F32), jax.ShapeDtypeStruct((n_pair, s, LANES), F32)],
        compiler_params=_params(("parallel", "arbitrary")),
    )(jnp.asarray(qi_np), jnp.asarray(ki_np), jnp.asarray(diag_np), proj, proj, proj, cum, cumt)


def _fox_bwd_prep(proj, dz, o, gblk, name):
    s, w = o.shape
    t = T_ELEM
    n_head = w // HEAD_DIM

    def body(gt_ref, dz_ref, o_ref, do_ref, dg_ref, dl_ref):
        gt = gt_ref[...]
        do = dz_ref[...] * _silu(gt)
        do_ref[...] = do
        dg_ref[...] = (dz_ref[...] * o_ref[...] * _dsilu(gt)).astype(BF16)
        sel = (_iota((w, LANES), 0) // HEAD_DIM == _iota((w, LANES), 1)).astype(F32)
        dl_ref[...] = _dot(do * o_ref[...], sel, NN, precision=lax.Precision.HIGHEST)

    assert n_head <= LANES
    row = pl.BlockSpec((t, w), lambda i: (i, 0))
    return pl.pallas_call(
        body, name=name, grid=(s // t,),
        in_specs=[pl.BlockSpec((t, w), lambda i: (i, gblk)), row, row],
        out_specs=[row, row, pl.BlockSpec((t, LANES), lambda i: (i, 0))],
        out_shape=[jax.ShapeDtypeStruct((s, w), F32), jax.ShapeDtypeStruct((s, w), BF16),
                   jax.ShapeDtypeStruct((s, LANES), F32)],
        compiler_params=_params(("parallel",)),
    )(proj, dz, o)


def _fox_bwd(proj, do, lse, delta, cum, cumt, qblk, kblk, vblk, name):
    s = proj.shape[0]
    tq, tk = T_ATT_Q, T_ATT_K
    nq = s // tq
    n_pair = 4
    qi_np, ki_np, diag_np = _causal_pairs(nq, tk // tq, kv_major=True)
    n_step = len(qi_np)
    scale = HEAD_DIM ** -0.5

    def body(qi_ref, ki_ref, diag_ref, q_ref, k_ref, v_ref, do_ref, lse_ref, dl_ref, cum_ref, cumt_ref,
             dq_ref, dk_ref, dv_ref, dct_ref, dcq_ref, dq_s, dk_s, dv_s, dc_s, dcq_s):
        hp = pl.program_id(0)
        step = pl.program_id(1)
        qi, ki, diag = qi_ref[step], ki_ref[step], diag_ref[step]

        @pl.when(step == 0)
        def _():
            dq_s[...] = jnp.zeros_like(dq_s)
            dcq_s[...] = jnp.zeros_like(dcq_s)

        @pl.when(qi == ki * (tk // tq))
        def _():
            dk_s[...] = jnp.zeros_like(dk_s)
            dv_s[...] = jnp.zeros_like(dv_s)
            dc_s[...] = jnp.zeros_like(dc_s)

        q = q_ref[...] * scale
        do = do_ref[...]
        kb = k_ref[...].astype(BF16)
        vb = v_ref[...].astype(BF16)
        sub = _iota((8, tk), 0)
        dq_new = jnp.zeros((tq, LANES), F32)
        dcq_new = jnp.zeros((tq, LANES), F32)
        lane = _iota((tq, LANES), 1)
        for hh in range(2):
            h = 2 * hp + hh
            hm = _head_mask(hh)
            qm = jnp.where(hm, q, 0.0).astype(BF16)
            dom = jnp.where(hm, do, 0.0).astype(BF16)
            sc = _fox_scores(qm, kb, cum_ref, cumt_ref, h, qi, ki, diag, tq, tk)
            p = jnp.exp(sc - _lane_pick(lse_ref[0], hh))
            dv_s[...] += _dot(p.astype(BF16), dom, TN)
            dp = _dot(dom, vb, NT)
            ds = p * (dp - _lane_pick(dl_ref[...], h))
            dc_s[...] += jnp.where(sub == h, -jnp.sum(ds, axis=0, keepdims=True), 0.0)
            dcq_new = dcq_new + jnp.where(lane == h, jnp.sum(ds, axis=1, keepdims=True), 0.0)
            dsb = ds.astype(BF16)
            dk_s[...] += _dot(dsb, qm, TN)
            dq_new = dq_new + jnp.where(hm, _dot(dsb, kb, NN), 0.0)
        row0 = pl.multiple_of(qi * tq, tq)
        dq_s[pl.ds(row0, tq), :] += dq_new * scale
        dcq_s[pl.ds(row0, tq), :] += dcq_new

        @pl.when(qi == nq - 1)
        def _():
            dk_ref[...] = dk_s[...].astype(BF16)
            dv_ref[...] = dv_s[...].astype(BF16)
            dct_ref[0] = dc_s[...]

        @pl.when(step == n_step - 1)
        def _():
            dq_ref[...] = dq_s[...].astype(BF16)
            dcq_ref[0] = dcq_s[...]

    grid_spec = pltpu.PrefetchScalarGridSpec(
        num_scalar_prefetch=3, grid=(n_pair, n_step),
        in_specs=[pl.BlockSpec((tq, LANES), lambda hp, st, qi, ki, dg: (qi[st], qblk + hp)),
                  pl.BlockSpec((tk, LANES), lambda hp, st, qi, ki, dg: (ki[st], kblk + hp)),
                  pl.BlockSpec((tk, LANES), lambda hp, st, qi, ki, dg: (ki[st], vblk + hp)),
                  pl.BlockSpec((tq, LANES), lambda hp, st, qi, ki, dg: (qi[st], hp)),
                  pl.BlockSpec((1, tq, LANES), lambda hp, st, qi, ki, dg: (hp, qi[st], 0)),
                  pl.BlockSpec((tq, LANES), lambda hp, st, qi, ki, dg: (qi[st], 0)),
                  pl.BlockSpec((tq, LANES), lambda hp, st, qi, ki, dg: (qi[st], 0)),
                  pl.BlockSpec((8, tk), lambda hp, st, qi, ki, dg: (0, ki[st]))],
        out_specs=[pl.BlockSpec((s, LANES), lambda hp, st, qi, ki, dg: (0, hp)),
                   pl.BlockSpec((tk, LANES), lambda hp, st, qi, ki, dg: (ki[st], hp)),
                   pl.BlockSpec((tk, LANES), lambda hp, st, qi, ki, dg: (ki[st], hp)),
                   pl.BlockSpec((1, 8, tk), lambda hp, st, qi, ki, dg: (hp, 0, ki[st])),
                   pl.BlockSpec((1, s, LANES), lambda hp, st, qi, ki, dg: (hp, 0, 0))],
        scratch_shapes=[pltpu.VMEM((s, LANES), F32), pltpu.VMEM((tk, LANES), F32), pltpu.VMEM((tk, LANES), F32),
                        pltpu.VMEM((8, tk), F32), pltpu.VMEM((s, LANES), F32)])
    w = n_pair * LANES
    return pl.pallas_call(
        body, name=name, grid_spec=grid_spec,
        out_shape=[jax.ShapeDtypeStruct((s, w), BF16), jax.ShapeDtypeStruct((s, w), BF16),
                   jax.ShapeDtypeStruct((s, w), BF16), jax.ShapeDtypeStruct((n_pair, 8, s), F32),
                   jax.ShapeDtypeStruct((n_pair, s, LANES), F32)],
        compiler_params=_params(("parallel", "arbitrary")),
    )(jnp.asarray(qi_np), jnp.asarray(ki_np), jnp.asarray(diag_np), proj, proj, proj, do, lse, delta, cum, cumt)


def _forget_bwd(dcumt4, dcumq4, sgt, name):
    s = sgt.shape[1]
    t = T_CUM
    n = s // t

    def body(d_ref, dq_ref, sg_ref, df_ref, dbf_ref, carry):
        i = pl.program_id(0)

        @pl.when(i == 0)
        def _():
            carry[...] = jnp.zeros_like(carry)
            dbf_ref[...] = jnp.zeros_like(dbf_ref)

        dq = dq_ref[0] + dq_ref[1] + dq_ref[2] + dq_ref[3]
        d = d_ref[0] + d_ref[1] + d_ref[2] + d_ref[3] + dq.T[0:8, :]
        upper = (_iota((t, t), 0) >= _iota((t, t), 1)).astype(F32)
        dlog = _dot(d, upper, NN, precision=lax.Precision.HIGHEST) + carry[:, 0:1]
        carry[...] += jnp.sum(d, axis=1, keepdims=True)
        dzt = dlog * sg_ref[...]
        dbf_ref[...] += jnp.sum(dzt, axis=1, keepdims=True)
        padded = jnp.concatenate([dzt, jnp.zeros((LANES - 8, t), F32)], axis=0)
        df_ref[...] = padded.T.astype(BF16)

    return pl.pallas_call(
        body, name=name, grid=(n,),
        in_specs=[pl.BlockSpec((4, 8, t), lambda i: (0, 0, n - 1 - i)),
                  pl.BlockSpec((4, t, LANES), lambda i: (0, n - 1 - i, 0)),
                  pl.BlockSpec((8, t), lambda i: (0, n - 1 - i))],
        out_specs=[pl.BlockSpec((t, LANES), lambda i: (n - 1 - i, 0)), pl.BlockSpec((8, LANES), lambda i: (0, 0))],
        out_shape=[jax.ShapeDtypeStruct((s, LANES), BF16), jax.ShapeDtypeStruct((8, LANES), F32)],
        scratch_shapes=[pltpu.VMEM((8, LANES), F32)],
        compiler_params=_params(("arbitrary",)),
    )(dcumt4, dcumq4, sgt)


def _mem_softmax(qm, kp):
    sc = _dot(qm, kp, NT) * (HEAD_DIM ** -0.5)
    e = jnp.exp(sc - jnp.max(sc, axis=1, keepdims=True))
    return e / jnp.sum(e, axis=1, keepdims=True)


def _branch_m_fwd(proj, mkv, mblk, name):
    s = proj.shape[0]
    t = T_M
    mw = mkv.shape[1] // 2
    ml = mkv.shape[0]

    def body(m_ref, kv_ref, z_ref):
        outs = []
        for pr in range(mw // LANES):
            qp = m_ref[:, pr * LANES:(pr + 1) * LANES]
            kp = kv_ref[:, pr * LANES:(pr + 1) * LANES].astype(BF16)
            vp = kv_ref[:, mw + pr * LANES:mw + (pr + 1) * LANES].astype(BF16)
            oh = []
            for hh in range(2):
                qm = jnp.where(_head_mask(hh), qp, 0.0).astype(BF16)
                oh.append(_dot(_mem_softmax(qm, kp).astype(BF16), vp, NN))
            outs.append(jnp.where(_iota((t, LANES), 1) < HEAD_DIM, oh[0], oh[1]))
        o = jnp.concatenate(outs, axis=1)
        z_ref[...] = (o * _silu(m_ref[:, mw:2 * mw])).astype(BF16)

    return pl.pallas_call(
        body, name=name, grid=(s // t,),
        in_specs=[pl.BlockSpec((t, 2 * mw), lambda i: (i, mblk)), pl.BlockSpec((ml, 2 * mw), lambda i: (0, 0))],
        out_specs=pl.BlockSpec((t, mw), lambda i: (i, 0)),
        out_shape=jax.ShapeDtypeStruct((s, mw), BF16),
        compiler_params=_params(("parallel",)),
    )(proj, mkv)


def _branch_m_bwd(proj, mkv, dz, mblk, name):
    s = proj.shape[0]
    t = T_M
    mw = mkv.shape[1] // 2
    ml = mkv.shape[0]
    scale = HEAD_DIM ** -0.5

    def body(m_ref, kv_ref, dz_ref, dm_ref, dkv_ref):
        i = pl.program_id(0)

        @pl.when(i == 0)
        def _():
            dkv_ref[...] = jnp.zeros_like(dkv_ref)

        gt = m_ref[:, mw:2 * mw]
        dz = dz_ref[...]
        do = dz * _silu(gt)
        outs = []
        for pr in range(mw // LANES):
            cols = slice(pr * LANES, (pr + 1) * LANES)
            vcols = slice(mw + pr * LANES, mw + (pr + 1) * LANES)
            qp = m_ref[:, cols]
            kp = kv_ref[:, cols].astype(BF16)
            vp = kv_ref[:, vcols].astype(BF16)
            dop = do[:, cols]
            oh = []
            dq = jnp.zeros((t, LANES), F32)
            dk = jnp.zeros((ml, LANES), F32)
            dv = jnp.zeros((ml, LANES), F32)
            for hh in range(2):
                hm = _head_mask(hh)
                qm = jnp.where(hm, qp, 0.0).astype(BF16)
                dom = jnp.where(hm, dop, 0.0).astype(BF16)
                p = _mem_softmax(qm, kp)
                pb = p.astype(BF16)
                oh.append(_dot(pb, vp, NN))
                dv = dv + _dot(pb, dom, TN)
                dp = _dot(dom, vp, NT)
                ds = p * (dp - jnp.sum(dp * p, axis=1, keepdims=True))
                dsb = (ds * scale).astype(BF16)
                dq = dq + jnp.where(hm, _dot(dsb, kp, NN), 0.0)
                dk = dk + _dot(dsb, qm, TN)
            outs.append(jnp.where(_iota((t, LANES), 1) < HEAD_DIM, oh[0], oh[1]))
            dm_ref[:, cols] = dq.astype(BF16)
            dkv_ref[:, cols] += dk
            dkv_ref[:, vcols] += dv
        o = jnp.concatenate(outs, axis=1)
        dm_ref[:, mw:2 * mw] = (dz * o * _dsilu(gt)).astype(BF16)

    return pl.pallas_call(
        body, name=name, grid=(s // t,),
        in_specs=[pl.BlockSpec((t, 2 * mw), lambda i: (i, mblk)), pl.BlockSpec((ml, 2 * mw), lambda i: (0, 0)),
                  pl.BlockSpec((t, mw), lambda i: (i, 0))],
        out_specs=[pl.BlockSpec((t, 2 * mw), lambda i: (i, 0)), pl.BlockSpec((ml, 2 * mw), lambda i: (0, 0))],
        out_shape=[jax.ShapeDtypeStruct((s, 2 * mw), BF16), jax.ShapeDtypeStruct((ml, 2 * mw), F32)],
        compiler_params=_params(("arbitrary",)),
    )(proj, mkv, dz)


def _merge_parts(z_refs, g_refs, pcat, bounds):
    ys, sgs = [], []
    merged = None
    for zr, gr, (lo, hi) in zip(z_refs, g_refs, bounds):
        y = _dot(zr[...], pcat[lo:hi, :], NN)
        sg = _sigmoid(gr[...])
        ys.append(y)
        sgs.append(sg)
        merged = sg * y if merged is None else merged + sg * y
    return ys, sgs, merged


def _branch_bounds(zs):
    bounds, lo = [], 0
    for z in zs:
        bounds.append((lo, lo + z.shape[1]))
        lo += z.shape[1]
    return bounds


def _merge_fwd(zs, proj, gblk, pcat, wout, x, lng, lnb, alpha, name):
    s, d = x.shape
    t = T_MERGE
    bounds = _branch_bounds(zs)

    def body(*refs):
        z_refs, g_refs = refs[0:4], refs[4:8]
        pcat_hbm, wout_hbm, x_ref, lng_ref, lnb_ref, y_ref, pcat_v, wout_v = refs[8:]

        @pl.when(pl.program_id(0) == 0)
        def _():
            pltpu.sync_copy(pcat_hbm, pcat_v)
            pltpu.sync_copy(wout_hbm, wout_v)

        _, _, merged = _merge_parts(z_refs, g_refs, pcat_v, bounds)
        h = alpha * x_ref[...] + _dot(merged.astype(BF16), wout_v[...], NN)
        xh, _ = _ln_stats(h)
        y_ref[...] = xh * lng_ref[...] + lnb_ref[...]

    row = pl.BlockSpec((t, d), lambda i: (i, 0))
    vec = pl.BlockSpec((1, d), lambda i: (0, 0))
    in_specs = ([pl.BlockSpec((t, z.shape[1]), lambda i: (i, 0)) for z in zs]
                + [pl.BlockSpec((t, d), lambda i, k=k: (i, gblk + k)) for k in range(4)]
                + [ANY, ANY, row, vec, vec])
    return pl.pallas_call(
        body, name=name, grid=(s // t,), in_specs=in_specs, out_specs=row,
        out_shape=jax.ShapeDtypeStruct((s, d), F32),
        scratch_shapes=[pltpu.VMEM(pcat.shape, BF16), pltpu.VMEM(wout.shape, BF16)],
        compiler_params=_params(("arbitrary",)),
    )(*zs, proj, proj, proj, proj, pcat, wout, x, lng, lnb)


def _merge_bwd(zs, proj, gblk, pcat, wout, x, lng, lnb, dy, alpha, name):
    s, d = x.shape
    t = T_MERGE
    n = s // t
    bounds = _branch_bounds(zs)

    def body(*refs):
        z_refs, g_refs = refs[0:4], refs[4:8]
        pcat_hbm, wout_hbm, x_ref, lng_ref, lnb_ref, dy_ref = refs[8:14]
        dx_ref, dg_ref = refs[14:16]
        dz_refs = refs[16:20]
        dpcat_hbm, dwout_hbm, dlng_ref, dlnb_ref = refs[20:24]
        pcat_v, wout_v, dpcat_v, dwout_v = refs[24:]
        i = pl.program_id(0)

        @pl.when(i == 0)
        def _():
            pltpu.sync_copy(pcat_hbm, pcat_v)
            pltpu.sync_copy(wout_hbm, wout_v)
            dpcat_v[...] = jnp.zeros_like(dpcat_v)
            dwout_v[...] = jnp.zeros_like(dwout_v)
            dlng_ref[...] = jnp.zeros_like(dlng_ref)
            dlnb_ref[...] = jnp.zeros_like(dlnb_ref)

        ys, sgs, merged = _merge_parts(z_refs, g_refs, pcat_v, bounds)
        mb = merged.astype(BF16)
        h = alpha * x_ref[...] + _dot(mb, wout_v[...], NN)
        xh, rstd = _ln_stats(h)
        dyv = dy_ref[...]
        dlng_ref[...] += jnp.sum(dyv * xh, axis=0, keepdims=True)
        dlnb_ref[...] += jnp.sum(dyv, axis=0, keepdims=True)
        dxh = dyv * lng_ref[...]
        dh = rstd * (dxh - jnp.mean(dxh, axis=-1, keepdims=True) - xh * jnp.mean(dxh * xh, axis=-1, keepdims=True))
        dx_ref[...] = alpha * dh
        dhb = dh.astype(BF16)
        dwout_v[...] += _dot(mb, dhb, TN)
        dmerged = _dot(dhb, wout_v[...], NT)
        for k, (zr, (lo, hi)) in enumerate(zip(z_refs, bounds)):
            sg = sgs[k]
            dg_ref[:, k * d:(k + 1) * d] = (dmerged * ys[k] * sg * (1.0 - sg)).astype(BF16)
            dyk = (dmerged * sg).astype(BF16)
            dpcat_v[lo:hi, :] += _dot(zr[...], dyk, TN)
            dz_refs[k][...] = _dot(dyk, pcat_v[lo:hi, :], NT)

        @pl.when(i == n - 1)
        def _():
            pltpu.sync_copy(dpcat_v, dpcat_hbm)
            pltpu.sync_copy(dwout_v, dwout_hbm)

    row = pl.BlockSpec((t, d), lambda i: (i, 0))
    vec = pl.BlockSpec((1, d), lambda i: (0, 0))
    z_specs = [pl.BlockSpec((t, z.shape[1]), lambda i: (i, 0)) for z in zs]
    in_specs = (z_specs + [pl.BlockSpec((t, d), lambda i, k=k: (i, gblk + k)) for k in range(4)]
                + [ANY, ANY, row, vec, vec, row])
    out_specs = [row, pl.BlockSpec((t, 4 * d), lambda i: (i, 0))] + z_specs + [ANY, ANY, vec, vec]
    vo = jax.ShapeDtypeStruct((1, d), F32)
    out_shape = ([jax.ShapeDtypeStruct((s, d), F32), jax.ShapeDtypeStruct((s, 4 * d), BF16)]
                 + [jax.ShapeDtypeStruct(z.shape, F32) for z in zs]
                 + [jax.ShapeDtypeStruct(pcat.shape, F32), jax.ShapeDtypeStruct(wout.shape, F32), vo, vo])
    return pl.pallas_call(
        body, name=name, grid=(n,), in_specs=in_specs, out_specs=out_specs, out_shape=out_shape,
        scratch_shapes=[pltpu.VMEM(pcat.shape, BF16), pltpu.VMEM(wout.shape, BF16),
                        pltpu.VMEM(pcat.shape, F32), pltpu.VMEM(wout.shape, F32)],
        compiler_params=_params(("arbitrary",)),
    )(*zs, proj, proj, proj, proj, pcat, wout, x, lng, lnb, dy)


def _loss_head(y, target, name):
    s, d = y.shape
    t = T_ELEM

    def body(y_ref, t_ref, dy_ref, loss_ref):
        @pl.when(pl.program_id(0) == 0)
        def _():
            loss_ref[...] = jnp.zeros_like(loss_ref)

        e = y_ref[...] - t_ref[...]
        dy_ref[...] = e * (1.0 / d)
        loss_ref[...] += 0.5 * jnp.sum(jnp.mean(e * e, axis=-1, keepdims=True), axis=0, keepdims=True)

    row = pl.BlockSpec((t, d), lambda i: (i, 0))
    return pl.pallas_call(
        body, name=name, grid=(s // t,), in_specs=[row, row],
        out_specs=[row, pl.BlockSpec((8, LANES), lambda i: (0, 0))],
        out_shape=[jax.ShapeDtypeStruct((s, d), F32), jax.ShapeDtypeStruct((8, LANES), F32)],
        compiler_params=_params(("arbitrary",)),
    )(y, target)


def _adamw(w, g, m, v, name):
    r, c = w.shape
    t = _pick(r, (256, 128, 64, 32, 16, 8))

    def body(w_ref, g_ref, m_ref, v_ref, d_ref, nm_ref, nv_ref):
        gv = g_ref[...]
        nm = ADAM_B1 * m_ref[...] + (1.0 - ADAM_B1) * gv
        nv = ADAM_B2 * v_ref[...] + (1.0 - ADAM_B2) * (gv * gv)
        m_hat = nm / (1.0 - ADAM_B1 ** ADAM_STEP)
        v_hat = nv / (1.0 - ADAM_B2 ** ADAM_STEP)
        d_ref[...] = -ADAM_LR * (m_hat / (jnp.sqrt(v_hat) + ADAM_EPS) + ADAM_WD * w_ref[...])
        nm_ref[...] = nm
        nv_ref[...] = nv

    blk = pl.BlockSpec((t, c), lambda i: (i, 0))
    o = jax.ShapeDtypeStruct((r, c), F32)
    return pl.pallas_call(body, name=name, grid=(r // t,), in_specs=[blk] * 4, out_specs=[blk] * 3,
                          out_shape=[o, o, o], compiler_params=_params(("parallel",)))(w, g, m, v)


def _sum_leading(x, out_dtype, name):
    k, r, c = x.shape
    t = _pick(r, (256, 128, 64, 32, 16, 8))

    def body(x_ref, o_ref):
        acc = x_ref[0].astype(F32)
        for j in range(1, k):
            acc = acc + x_ref[j].astype(F32)
        o_ref[...] = acc.astype(out_dtype)

    return pl.pallas_call(body, name=name, grid=(r // t,),
                          in_specs=[pl.BlockSpec((k, t, c), lambda i: (0, i, 0))],
                          out_specs=pl.BlockSpec((t, c), lambda i: (i, 0)),
                          out_shape=jax.ShapeDtypeStruct((r, c), out_dtype), compiler_params=_params(("parallel",)))(x)


def _position():
    return lax.axis_index("x"), lax.axis_index("y"), lax.axis_index("c")


def _chip_peers(x, y):
    return [(1 - x, y), (x, 1 - y), (1 - x, 1 - y)]


def _comm_call(body, n_in, out_shape, n_remote, n_local, name):
    return pl.pallas_call(
        body, name=name, in_specs=[ANY] * n_in, out_specs=[ANY] * len(out_shape), out_shape=out_shape,
        scratch_shapes=[pltpu.SemaphoreType.DMA((n_remote,)), pltpu.SemaphoreType.DMA((n_remote,)),
                        pltpu.SemaphoreType.DMA((max(n_local, 1),))])


def _run_copies(local, remote, send, recv, loc):
    copies = [pltpu.make_async_copy(src, dst, loc.at[k]) for k, (src, dst) in enumerate(local)]
    copies += [pltpu.make_async_remote_copy(src_ref=src, dst_ref=dst, send_sem=send.at[k], recv_sem=recv.at[k],
                                            device_id=peer, device_id_type=MESH)
               for k, (src, dst, peer) in enumerate(remote)]
    for cp in copies:
        cp.start()
    for cp in copies:
        cp.wait()


COPY_BYTES = 1024 * 1024


def _n_copies(rows, row_bytes, align):
    n = 8
    while n > 1 and (rows % (n * align) or rows // n * row_bytes < COPY_BYTES):
        n //= 2
    return n


def _allgather_chips(arrs, name):
    n = len(arrs)
    per_layer = [a.size * a.dtype.itemsize // a.shape[0] >= COPY_BYTES for a in arrs]
    n_each = [a.shape[0] if pl_ else 1 for a, pl_ in zip(arrs, per_layer)]

    def body(*refs):
        ins, outs = refs[:n], refs[n:2 * n]
        send, recv, loc = refs[2 * n:]
        x, y, c = _position()
        me = 2 * x + y
        local, remote = [], []
        for a in range(n):
            if per_layer[a]:
                parts = [(ins[a].at[l], outs[a].at[me, l]) for l in range(arrs[a].shape[0])]
            else:
                parts = [(ins[a], outs[a].at[me])]
            local += parts
            for px, py in _chip_peers(x, y):
                remote += [(src, dst, (px, py, c)) for src, dst in parts]
        _run_copies(local, remote, send, recv, loc)

    out_shape = [jax.ShapeDtypeStruct((4,) + a.shape, a.dtype) for a in arrs]
    return _comm_call(body, n, out_shape, 3 * sum(n_each), sum(n_each), name)(*arrs)


def _allgather_all(v, name):
    def body(v_ref, o_ref, send, recv, loc):
        x, y, c = _position()
        me = 4 * x + 2 * y + c
        remote = []
        for k in range(1, 8):
            fx, fy, fc = (k >> 2) & 1, (k >> 1) & 1, k & 1
            remote.append((v_ref, o_ref.at[me], (x ^ fx, y ^ fy, c ^ fc)))
        _run_copies([(v_ref, o_ref.at[me])], remote, send, recv, loc)

    return _comm_call(body, 1, [jax.ShapeDtypeStruct((8,) + v.shape, v.dtype)], 7, 1, name)(v)[0]


def _pair_exchange_sum(g, name):
    _, _, n, r, cc = g.shape

    def body(c_ref, mine_ref, send_ref, o_ref, land, send_sem, recv_sem):
        x, y, c = _position()
        slot = (pl.program_id(0) * n + pl.program_id(1)) % 2
        push = pltpu.make_async_remote_copy(
            src_ref=send_ref.at[0, 0, 0], dst_ref=land.at[slot], send_sem=send_sem.at[slot],
            recv_sem=recv_sem.at[slot], device_id=(x, y, 1 - c), device_id_type=MESH)
        push.start()
        push.wait_recv()
        o_ref[0, 0] = (mine_ref[0, 0, 0] + land[slot]).astype(BF16)
        push.wait_send()

    grid_spec = pltpu.PrefetchScalarGridSpec(
        num_scalar_prefetch=1, grid=(4, n),
        in_specs=[pl.BlockSpec((1, 1, 1, r, cc), lambda j, k, c: (j, c[0], k, 0, 0)),
                  pl.BlockSpec((1, 1, 1, r, cc), lambda j, k, c: (j, 1 - c[0], k, 0, 0))],
        out_specs=pl.BlockSpec((1, 1, r, cc), lambda j, k, c: (j, k, 0, 0)),
        scratch_shapes=[pltpu.VMEM((2, r, cc), F32), pltpu.SemaphoreType.DMA((2,)), pltpu.SemaphoreType.DMA((2,))])
    return pl.pallas_call(
        body, name=name, grid_spec=grid_spec, out_shape=jax.ShapeDtypeStruct((4, n, r, cc), BF16),
        compiler_params=_params(("arbitrary", "arbitrary")))(_scalar(lax.axis_index("c")), g, g)


def _scalar(v):
    return v.astype(jnp.int32).reshape(1)


def _chip_exchange_sum(p, name):
    _, n, r, cc = p.shape

    def body(i0, i1, i2, i3, own_ref, s0_ref, s1_ref, s2_ref, mine_ref, theirs_ref, land, total, land_pair,
             send_sem, recv_sem, pair_send, pair_recv):
        x, y, c = _position()
        slot = pl.program_id(0) % 2
        pushes = [pltpu.make_async_remote_copy(
            src_ref=src.at[0, 0], dst_ref=land.at[slot, j], send_sem=send_sem.at[slot, j], recv_sem=recv_sem.at[slot, j],
            device_id=(px, py, c), device_id_type=MESH)
            for j, (src, (px, py)) in enumerate(zip((s0_ref, s1_ref, s2_ref), _chip_peers(x, y)))]
        for cp in pushes:
            cp.start()
        acc = own_ref[0, 0].astype(F32)
        for j, cp in enumerate(pushes):
            cp.wait_recv()
            acc = acc + land[slot, j].astype(F32)
        mine_ref[0] = acc
        total[slot] = acc
        share = pltpu.make_async_remote_copy(
            src_ref=total.at[slot], dst_ref=land_pair.at[slot], send_sem=pair_send.at[slot], recv_sem=pair_recv.at[slot],
            device_id=(x, y, 1 - c), device_id_type=MESH)
        share.start()
        share.wait_recv()
        theirs_ref[0] = land_pair[slot]
        for cp in pushes:
            cp.wait_send()
        share.wait_send()

    def slot_spec(which):
        return pl.BlockSpec((1, 1, r, cc), lambda k, *idx, which=which: (idx[which][0], k, 0, 0))

    out_spec = pl.BlockSpec((1, r, cc), lambda k, *idx: (k, 0, 0))
    grid_spec = pltpu.PrefetchScalarGridSpec(
        num_scalar_prefetch=4, grid=(n,), in_specs=[slot_spec(0), slot_spec(1), slot_spec(2), slot_spec(3)],
        out_specs=[out_spec, out_spec],
        scratch_shapes=[pltpu.VMEM((2, 3, r, cc), BF16), pltpu.VMEM((2, r, cc), F32), pltpu.VMEM((2, r, cc), F32),
                        pltpu.SemaphoreType.DMA((2, 3)), pltpu.SemaphoreType.DMA((2, 3)),
                        pltpu.SemaphoreType.DMA((2,)), pltpu.SemaphoreType.DMA((2,))])
    o = jax.ShapeDtypeStruct((n, r, cc), F32)
    x, y, _ = _position()
    chips = [_scalar(2 * x + y)] + [_scalar(2 * px + py) for px, py in _chip_peers(x, y)]
    return pl.pallas_call(body, name=name, grid_spec=grid_spec, out_shape=[o, o],
                          compiler_params=_params(("arbitrary",)))(*chips, p, p, p, p)


def _reduce_scatter(gs):
    c = lax.axis_index("c")
    outs = []
    for a, g in enumerate(gs):
        _, rows, cc = g.shape
        k = _n_copies(rows // 2, cc * 4, 16)
        p = _pair_exchange_sum(g.reshape(4, 2, k, rows // (2 * k), cc), f"rs_pair_exchange_sum_{a}")
        mine, theirs = _chip_exchange_sum(p, f"rs_chip_exchange_sum_{a}")
        mine, theirs = mine.reshape(rows // 2, cc), theirs.reshape(rows // 2, cc)
        outs.append(jnp.where(c == 0, jnp.concatenate([mine, theirs]), jnp.concatenate([theirs, mine])))
    return outs


def _gather_shards(w, name):
    rows, cc = w.shape
    half = rows // 2
    n = _n_copies(half, cc * w.dtype.itemsize, 16)
    r = half // n

    def body(core_ref, mine_ref, other_ref, out_ref, land, land_pair, send_sem, recv_sem, pair_send, pair_recv, out_sem):
        x, y, c = _position()
        k = pl.program_id(0)
        slot = k % 2
        me = 2 * x + y
        chips = [2 * px + py for px, py in _chip_peers(x, y)]
        pushes = [pltpu.make_async_remote_copy(
            src_ref=mine_ref.at[0, 0], dst_ref=land.at[slot, j], send_sem=send_sem.at[slot, j],
            recv_sem=recv_sem.at[slot, j], device_id=(px, py, c), device_id_type=MESH)
            for j, (px, py) in enumerate(_chip_peers(x, y))]
        for cp in pushes:
            cp.start()
        writes = [pltpu.make_async_copy(mine_ref.at[0, 0], out_ref.at[me, c, k], out_sem.at[0]),
                  pltpu.make_async_copy(other_ref.at[0, 0], out_ref.at[me, 1 - c, k], out_sem.at[1])]
        for cp in writes:
            cp.start()
        passes = []
        for j, cp in enumerate(pushes):
            cp.wait_recv()
            passes.append(pltpu.make_async_remote_copy(
                src_ref=land.at[slot, j], dst_ref=land_pair.at[slot, j], send_sem=pair_send.at[slot, j],
                recv_sem=pair_recv.at[slot, j], device_id=(x, y, 1 - c), device_id_type=MESH))
            passes[j].start()
            writes.append(pltpu.make_async_copy(land.at[slot, j], out_ref.at[chips[j], c, k], out_sem.at[2 + j]))
            writes[-1].start()
        for j, cp in enumerate(passes):
            cp.wait_recv()
            writes.append(pltpu.make_async_copy(land_pair.at[slot, j], out_ref.at[chips[j], 1 - c, k], out_sem.at[5 + j]))
            writes[-1].start()
        for cp in writes:
            cp.wait()
        for cp in pushes + passes:
            cp.wait_send()

    grid_spec = pltpu.PrefetchScalarGridSpec(
        num_scalar_prefetch=1, grid=(n,),
        in_specs=[pl.BlockSpec((1, 1, r, cc), lambda k, core: (core[0], k, 0, 0)),
                  pl.BlockSpec((1, 1, r, cc), lambda k, core: (1 - core[0], k, 0, 0))],
        out_specs=ANY,
        scratch_shapes=[pltpu.VMEM((2, 3, r, cc), w.dtype), pltpu.VMEM((2, 3, r, cc), w.dtype),
                        pltpu.SemaphoreType.DMA((2, 3)), pltpu.SemaphoreType.DMA((2, 3)),
                        pltpu.SemaphoreType.DMA((2, 3)), pltpu.SemaphoreType.DMA((2, 3)),
                        pltpu.SemaphoreType.DMA((8,))])
    w4 = w.reshape(2, n, r, cc)
    out = pl.pallas_call(body, name=name, grid_spec=grid_spec,
                         out_shape=jax.ShapeDtypeStruct((4, 2, n, r, cc), w.dtype),
                         compiler_params=_params(("arbitrary",)))(_scalar(lax.axis_index("c")), w4, w4)
    return out.reshape(4, rows, cc)


def _pad_rows(a, rows):
    return jnp.pad(a, ((0, rows - a.shape[0]), (0, 0)))


def _shard_cols(a):
    r, c4 = a.shape
    return a.reshape(r, 4, c4 // 4).transpose(1, 0, 2)


def kernel(x, mem, w_in, b_forget, conv_a_w, conv_a_b, ln_a_g, ln_a_b, conv_b_w, w_kv_mem, mem_ln_g, mem_ln_b, p_a, p_b, p_c, p_m, w_out, ln_g, ln_b, loss_target, m_w_in, m_b_forget, m_conv_a_w, m_conv_a_b, m_ln_a_g, m_ln_a_b, m_conv_b_w, m_w_kv_mem, m_mem_ln_g, m_mem_ln_b, m_p_a, m_p_b, m_p_c, m_p_m, m_w_out, m_ln_g, m_ln_b, v_w_in, v_b_forget, v_conv_a_w, v_conv_a_b, v_ln_a_g, v_ln_a_b, v_conv_b_w, v_w_kv_mem, v_mem_ln_g, v_mem_ln_b, v_p_a, v_p_b, v_p_c, v_p_m, v_w_out, v_ln_g, v_ln_b):
    depth = w_in.shape[0]
    x0 = x[0]
    s, d = x0.shape
    aw = conv_a_w.shape[2] * 4
    mw = p_m.shape[1]
    n_head = b_forget.shape[1]
    alpha = (2.0 * depth) ** 0.25
    in_cols = w_in.shape[2] * 4
    assert aw == n_head * HEAD_DIM and mw % LANES == 0 and in_cols == 11 * aw + n_head + 2 * mw + 4 * d
    cf0 = 10 * aw
    n_main = in_cols - n_head
    fblk = n_main // LANES
    n_pad = n_main + LANES
    gblk = (11 * aw + 2 * mw) // d
    mblk = (11 * aw) // (2 * mw)
    qblk, kblk, vblk = 7 * aw // LANES, 8 * aw // LANES, 9 * aw // LANES
    assert (11 * aw + 2 * mw) % d == 0 and (11 * aw) % (2 * mw) == 0

    def gather(w, name):
        flat = _gather_shards(w.astype(BF16).reshape(-1, w.shape[-1]), name)
        return flat.reshape((4,) + w.shape)

    w_in_g, p_a_g, p_b_g, p_c_g = (gather(w, f"gather_{nm}") for w, nm in
                                   ((w_in, "w_in"), (p_a, "p_a"), (p_b, "p_b"), (p_c, "p_c")))
    p_m_g, w_kv_g, w_out_g = (gather(w, f"gather_{nm}") for w, nm in
                              ((p_m, "p_m"), (w_kv_mem, "w_kv"), (w_out, "w_out")))
    conv_a_g, conv_b_g = _allgather_chips([conv_a_w, conv_b_w], "gather_conv_taps")

    def cols(g):
        return jnp.concatenate([g[j] for j in range(4)], axis=-1)

    def rows(g):
        return jnp.concatenate([g[j] for j in range(4)], axis=-2)

    w_full = cols(w_in_g)
    w_pad = jnp.concatenate([w_full[:, :, :cf0], w_full[:, :, cf0 + n_head:], w_full[:, :, cf0:cf0 + n_head],
                             jnp.zeros((depth, d, LANES - n_head), BF16)], axis=-1)
    pcat = jnp.concatenate([cols(p_a_g), cols(p_b_g), cols(p_c_g), cols(p_m_g)], axis=1)
    w_kv = rows(w_kv_g)
    wout = rows(w_out_g)
    conv_a = jnp.pad(cols(conv_a_g), ((0, 0), (0, HALO_A - CONV_A), (0, 0)))
    conv_b = jnp.pad(cols(conv_b_g), ((0, 0), (0, HALO_B - CONV_B), (0, 0)))
    bf_pad = jnp.pad(b_forget, ((0, 0), (0, LANES - n_head)))
    pieces = [(0, 3 * aw), (3 * aw, 7 * aw), (7 * aw, 10 * aw), (10 * aw, 11 * aw),
              (11 * aw, 11 * aw + 2 * mw), (11 * aw + 2 * mw, n_main), (n_main, n_pad)]

    mem_n = _ln_rows(mem[0], mem_ln_g[None], mem_ln_b[None], "mem_ln")

    xs, saved = [x0], []
    for l in range(depth):
        xl = xs[-1]
        proj = _mm(xl, w_pad[l], name=f"proj_{l}")
        za, conv_out = _branch_a_fwd(proj, conv_a[l], conv_a_b[l][None], ln_a_g[l][None], ln_a_b[l][None], f"a_fwd_{l}")
        zb = _branch_b_fwd(proj, conv_b[l], f"b_fwd_{l}")
        cum, cumt, sgt = _forget_prep(proj, bf_pad[l][None], fblk, f"forget_prep_{l}")
        o_c, lse = _fox_fwd(proj, cum, cumt, qblk, kblk, vblk, f"fox_fwd_{l}")
        zc = _gate_mul(proj, o_c, 10, f"c_gate_{l}")
        mkv = _mm(mem_n, w_kv[l], name=f"mkv_{l}")
        zm = _branch_m_fwd(proj, mkv, mblk, f"m_fwd_{l}")
        zs = [za, zb, zc, zm]
        y = _merge_fwd(zs, proj, gblk, pcat[l], wout[l], xl, ln_g[l][None], ln_b[l][None], alpha, f"merge_fwd_{l}")
        xs.append(y)
        saved.append((proj, zs, conv_out, cum, cumt, sgt, o_c, lse, mkv))

    dy, loss_part = _loss_head(xs[-1], loss_target[0], "loss_head")

    g_w_in, g_conv_a, g_conv_b, g_w_kv, g_pcat, g_wout = [], [], [], [], [], []
    small = []
    dmem_n = None
    for l in reversed(range(depth)):
        proj, zs, conv_out, cum, cumt, sgt, o_c, lse, mkv = saved[l]
        xl = xs[l]
        (dx, d_g, dza, dzb, dzc, dzm, dpcat, dwout, dlng, dlnb) = _merge_bwd(
            zs, proj, gblk, pcat[l], wout[l], xl, ln_g[l][None], ln_b[l][None], dy, alpha, f"merge_bwd_{l}")
        d_m, dmkv = _branch_m_bwd(proj, mkv, dzm, mblk, f"m_bwd_{l}")
        g_w_kv.append(_mm(mem_n, dmkv, ta=True, name=f"dwkv_{l}"))
        dmem_n = _mm(dmkv, w_kv[l], tb=True, add=dmem_n, name=f"dmem_{l}")
        do, d_cg, delta = _fox_bwd_prep(proj, dzc, o_c, 10, f"fox_bwd_prep_{l}")
        dq, dk, dv, dcumt4, dcumq4 = _fox_bwd(proj, do, lse, delta, cum, cumt, qblk, kblk, vblk, f"fox_bwd_{l}")
        d_f, dbf = _forget_bwd(dcumt4, dcumq4, sgt, f"forget_bwd_{l}")
        d_b, dconv_b = _branch_b_bwd(proj, dzb, conv_b[l], f"b_bwd_{l}")
        d_a, dconv_a, dconv_ab, dlag, dlab = _branch_a_bwd(
            proj, conv_out, dza, conv_a[l], ln_a_g[l][None], ln_a_b[l][None], f"a_bwd_{l}")
        dparts = [d_a, d_b, jnp.concatenate([dq, dk, dv], axis=1), d_cg, d_m, d_g, d_f]
        dx = _input_grad(dparts, [w_pad[l][:, lo:hi] for lo, hi in pieces], dx, f"dx_{l}")
        dw_parts = [_mm(xl, dp, ta=True, name=f"dw_{l}_{k}") for k, dp in enumerate(dparts)]
        dw = jnp.concatenate(dw_parts[:3] + [dw_parts[6][:, :n_head]] + dw_parts[3:6], axis=1)
        g_w_in.append(dw)
        g_conv_a.append(dconv_a)
        g_conv_b.append(dconv_b)
        g_pcat.append(dpcat)
        g_wout.append(dwout)
        small.append([dbf[:, 0], dconv_ab[0], dlag[0], dlab[0], dlng[0], dlnb[0]])
        dy = dx
    grad_x = dy
    dmlg, dmlb = _ln_rows_param_grads(mem[0], dmem_n, "mem_ln_grads")
    for lst in (g_w_in, g_conv_a, g_conv_b, g_w_kv, g_pcat, g_wout, small):
        lst.reverse()

    pa_end, pb_end, pc_end = aw, 2 * aw, 3 * aw
    rs_in = [
        _shard_cols(jnp.concatenate(g_w_in, axis=0)),
        _shard_cols(jnp.concatenate(g_conv_a, axis=0)),
        _shard_cols(jnp.concatenate([_pad_rows(g, 2 * HALO_B) for g in g_conv_b], axis=0)),
        jnp.concatenate([g.reshape(4, g.shape[0] // 4, g.shape[1]) for g in g_w_kv], axis=1),
        _shard_cols(jnp.concatenate([g[:pa_end] for g in g_pcat], axis=0)),
        _shard_cols(jnp.concatenate([g[pa_end:pb_end] for g in g_pcat], axis=0)),
        _shard_cols(jnp.concatenate([g[pb_end:pc_end] for g in g_pcat], axis=0)),
        _shard_cols(jnp.concatenate([g[pc_end:] for g in g_pcat], axis=0)),
        jnp.concatenate([g.reshape(4, g.shape[0] // 4, g.shape[1]) for g in g_wout], axis=1),
    ]
    rs_out = _reduce_scatter(rs_in)
    gw_in = rs_out[0].reshape(depth, d, -1)
    g_ca = rs_out[1].reshape(depth, HALO_A, -1)[:, :CONV_A]
    g_cb = rs_out[2].reshape(depth, 2 * HALO_B, -1)[:, :CONV_B]
    gw_kv = rs_out[3].reshape(depth, -1, 2 * mw)
    gp_a = rs_out[4].reshape(depth, aw, -1)
    gp_b = rs_out[5].reshape(depth, aw, -1)
    gp_c = rs_out[6].reshape(depth, aw, -1)
    gp_m = rs_out[7].reshape(depth, mw, -1)
    gw_out = rs_out[8].reshape(depth, -1, d)

    flat = jnp.concatenate([jnp.concatenate(p) for p in small] + [dmlg[0], dmlb[0], loss_part[0, 0:1]])
    n_small = flat.shape[0]
    n_rows = -(-n_small // (8 * LANES)) * 8
    vec = jnp.pad(flat, (0, n_rows * LANES - n_small)).reshape(n_rows, LANES)
    tot = _sum_leading(_allgather_all(vec, "gather_small"), F32, "sum_small").reshape(-1)
    per_layer = n_head + 3 * aw + 2 * d
    tl = tot[:depth * per_layer].reshape(depth, per_layer)
    offs = np.cumsum([0, n_head, aw, aw, aw, d, d])
    g_bf, g_cab, g_lag, g_lab, g_lg, g_lb = [tl[:, offs[k]:offs[k + 1]] for k in range(6)]
    base = depth * per_layer
    g_mlg, g_mlb = tot[base:base + d], tot[base + d:base + 2 * d]
    loss = tot[base + 2 * d]

    grads = [gw_in, g_bf, g_ca, g_cab, g_lag, g_lab, g_cb, gw_kv, g_mlg, g_mlb, gp_a, gp_b, gp_c, gp_m, gw_out, g_lg, g_lb]
    ws = [w_in, b_forget, conv_a_w, conv_a_b, ln_a_g, ln_a_b, conv_b_w, w_kv_mem, mem_ln_g, mem_ln_b, p_a, p_b, p_c, p_m, w_out, ln_g, ln_b]
    ms = [m_w_in, m_b_forget, m_conv_a_w, m_conv_a_b, m_ln_a_g, m_ln_a_b, m_conv_b_w, m_w_kv_mem, m_mem_ln_g, m_mem_ln_b, m_p_a, m_p_b, m_p_c, m_p_m, m_w_out, m_ln_g, m_ln_b]
    vs = [v_w_in, v_b_forget, v_conv_a_w, v_conv_a_b, v_ln_a_g, v_ln_a_b, v_conv_b_w, v_w_kv_mem, v_mem_ln_g, v_mem_ln_b, v_p_a, v_p_b, v_p_c, v_p_m, v_w_out, v_ln_g, v_ln_b]
    deltas, new_ms, new_vs = [], [], []
    for k, (wk, gk, mk, vk) in enumerate(zip(ws, grads, ms, vs)):
        shape = wk.shape
        two_d = (1, shape[0]) if wk.ndim == 1 else (int(np.prod(shape[:-1])), shape[-1])
        dk_, nm_, nv_ = _adamw(wk.reshape(two_d), gk.reshape(two_d), mk.reshape(two_d), vk.reshape(two_d), f"adamw_{k}")
        deltas.append(dk_.reshape(shape))
        new_ms.append(nm_.reshape(shape))
        new_vs.append(nv_.reshape(shape))
        grads[k] = gk.reshape(shape)
    return (loss, grad_x[None], *grads, *deltas, *new_ms, *new_vs)


def _gate_mul(proj, o, gblk, name):
    s, w = o.shape
    t = T_ELEM

    def body(g_ref, o_ref, z_ref):
        z_ref[...] = (o_ref[...] * _silu(g_ref[...])).astype(BF16)

    row = pl.BlockSpec((t, w), lambda i: (i, 0))
    return pl.pallas_call(body, name=name, grid=(s // t,), in_specs=[pl.BlockSpec((t, w), lambda i: (i, gblk)), row],
                          out_specs=row, out_shape=jax.ShapeDtypeStruct((s, w), BF16),
                          compiler_params=_params(("parallel",)))(proj, o)
```

```python
import functools
import math

import numpy as np
import jax
import jax.numpy as jnp
from jax import lax
from jax.experimental import pallas as pl
from jax.experimental.pallas import tpu as pltpu

F32 = jnp.float32
BF16 = jnp.bfloat16
MESH = pl.DeviceIdType.MESH
ANY = pl.BlockSpec(memory_space=pl.ANY)

LN_EPS = 1e-5
NEG_BIG = -1e30
HEAD_DIM = 64
LANES = 128
CONV_A = 31
CONV_B = 3
HALO_A = 32
HALO_B = 8
CHUNK = 32
VMEM_LIMIT = 60 * 1024 * 1024

ADAM_LR, ADAM_B1, ADAM_B2, ADAM_EPS, ADAM_WD, ADAM_STEP = 0.001, 0.9, 0.999, 1e-08, 0.01, 10

T_MM = 512
T_A = 128
T_B = 256
T_ATT_Q = 512
T_ATT_K = 1024
T_CUM = 512
T_M = 512
T_MERGE = 256
T_DX = 256
T_ELEM = 512


def _pick(n, prefs):
    for p in prefs:
        if n % p == 0:
            return p
    return n


def _params(sem=None):
    return pltpu.CompilerParams(dimension_semantics=sem, vmem_limit_bytes=VMEM_LIMIT)


def _sigmoid(x):
    return jax.nn.sigmoid(x)


def _silu(x):
    return x * _sigmoid(x)


def _dsilu(x):
    s = _sigmoid(x)
    return s * (1.0 + x * (1.0 - s))


def _dot(a, b, dims, precision=None):
    return lax.dot_general(a, b, (dims, ((), ())), preferred_element_type=F32, precision=precision)


NN = ((1,), (0,))
NT = ((1,), (1,))
TN = ((0,), (0,))


def _iota(shape, dim):
    return lax.broadcasted_iota(jnp.int32, shape, dim)


def _mm(a, b, *, ta=False, tb=False, add=None, out_dtype=F32, tm=None, name):
    m = a.shape[1] if ta else a.shape[0]
    k = a.shape[0] if ta else a.shape[1]
    n = b.shape[0] if tb else b.shape[1]
    if tm is None:
        tm = _pick(m, (1024, 512, 256)) if ta else _pick(m, (T_MM, 256))
    tn = _pick(n, (1152, 1024, 768, 512, 384, 256, 128))
    tk = _pick(k, (1024, 512, 256))
    nk = k // tk
    dims = ((0,) if ta else (1,), (1,) if tb else (0,))

    def body(*refs):
        if add is None:
            a_ref, b_ref, o_ref, acc_ref = refs
        else:
            a_ref, b_ref, add_ref, o_ref, acc_ref = refs
        kk = pl.program_id(2)
        p = _dot(a_ref[...].astype(BF16), b_ref[...].astype(BF16), dims)

        @pl.when(kk == 0)
        def _():
            acc_ref[...] = p

        @pl.when(kk > 0)
        def _():
            acc_ref[...] += p

        @pl.when(kk == nk - 1)
        def _():
            r = acc_ref[...]
            if add is not None:
                r = r + add_ref[...]
            o_ref[...] = r.astype(out_dtype)

    a_spec = (pl.BlockSpec((tk, tm), lambda j, i, kk: (kk, i)) if ta
              else pl.BlockSpec((tm, tk), lambda j, i, kk: (i, kk)))
    b_spec = (pl.BlockSpec((tn, tk), lambda j, i, kk: (j, kk)) if tb
              else pl.BlockSpec((tk, tn), lambda j, i, kk: (kk, j)))
    o_spec = pl.BlockSpec((tm, tn), lambda j, i, kk: (i, j))
    in_specs = [a_spec, b_spec] + ([o_spec] if add is not None else [])
    args = (a, b) + ((add,) if add is not None else ())
    return pl.pallas_call(
        body, name=name, grid=(n // tn, m // tm, nk), in_specs=in_specs, out_specs=o_spec,
        out_shape=jax.ShapeDtypeStruct((m, n), out_dtype),
        scratch_shapes=[pltpu.VMEM((tm, tn), F32)],
        compiler_params=_params(("parallel", "parallel", "arbitrary")),
    )(*args)


def _input_grad(dparts, wparts, add, name):
    s, d = add.shape
    t = T_DX
    n_p = len(dparts)

    def body(*refs):
        d_refs, w_hbm = refs[:n_p], refs[n_p:2 * n_p]
        add_ref, o_ref = refs[2 * n_p], refs[2 * n_p + 1]
        w_v = refs[2 * n_p + 2:]

        @pl.when(pl.program_id(0) == 0)
        def _():
            for p in range(n_p):
                pltpu.sync_copy(w_hbm[p], w_v[p])

        acc = add_ref[...]
        for p in range(n_p):
            acc = acc + _dot(d_refs[p][...], w_v[p][...], NT)
        o_ref[...] = acc

    row = pl.BlockSpec((t, d), lambda i: (i, 0))
    in_specs = ([pl.BlockSpec((t, dp.shape[1]), lambda i: (i, 0)) for dp in dparts] + [ANY] * n_p + [row])
    return pl.pallas_call(
        body, name=name, grid=(s // t,), in_specs=in_specs, out_specs=row,
        out_shape=jax.ShapeDtypeStruct((s, d), F32),
        scratch_shapes=[pltpu.VMEM(w.shape, BF16) for w in wparts],
        compiler_params=_params(("arbitrary",)),
    )(*dparts, *wparts, add)


def _ln_rows(x, g, b, name):
    r, d = x.shape
    t = _pick(r, (256,))

    def body(x_ref, g_ref, b_ref, o_ref):
        xv = x_ref[...]
        mu = jnp.mean(xv, axis=-1, keepdims=True)
        dv = xv - mu
        var = jnp.mean(dv * dv, axis=-1, keepdims=True)
        o_ref[...] = dv * lax.rsqrt(var + LN_EPS) * g_ref[...] + b_ref[...]

    row = pl.BlockSpec((t, d), lambda i: (i, 0))
    vec = pl.BlockSpec((1, d), lambda i: (0, 0))
    return pl.pallas_call(body, name=name, grid=(r // t,), in_specs=[row, vec, vec], out_specs=row,
                          out_shape=jax.ShapeDtypeStruct((r, d), F32), compiler_params=_params(("parallel",)))(x, g, b)


def _ln_rows_param_grads(x, dy, name):
    r, d = x.shape

    def body(x_ref, dy_ref, dg_ref, db_ref):
        xv = x_ref[...]
        mu = jnp.mean(xv, axis=-1, keepdims=True)
        dv = xv - mu
        var = jnp.mean(dv * dv, axis=-1, keepdims=True)
        xh = dv * lax.rsqrt(var + LN_EPS)
        dg_ref[...] = jnp.sum(dy_ref[...] * xh, axis=0, keepdims=True)
        db_ref[...] = jnp.sum(dy_ref[...], axis=0, keepdims=True)

    full = pl.BlockSpec((r, d), lambda i: (0, 0))
    vec = pl.BlockSpec((1, d), lambda i: (0, 0))
    o = jax.ShapeDtypeStruct((1, d), F32)
    return pl.pallas_call(body, name=name, grid=(1,), in_specs=[full, full], out_specs=[vec, vec],
                          out_shape=[o, o], compiler_params=_params(("arbitrary",)))(x, dy)


def _conv_a_chunk(glu_ref, cw_ref, r0):
    acc = cw_ref[0:1, :] * glu_ref[pl.ds(r0 + 2, CHUNK), :]
    for k in range(1, CONV_A):
        acc = acc + cw_ref[k:k + 1, :] * glu_ref[pl.ds(r0 + 2 + k, CHUNK), :]
    return acc


def _ln_stats(c):
    mu = jnp.mean(c, axis=-1, keepdims=True)
    d = c - mu
    var = jnp.mean(d * d, axis=-1, keepdims=True)
    rstd = lax.rsqrt(var + LN_EPS)
    return d * rstd, rstd


def _branch_a_fwd(proj, cw, cb, lg, lb, name):
    s = proj.shape[0]
    t = T_A
    w = cw.shape[1]
    r = t // HALO_A

    def body(u_ref, v_ref, gt_ref, hu_ref, hv_ref, cw_ref, cb_ref, lg_ref, lb_ref, z_ref, conv_ref, glu):
        i = pl.program_id(0)
        hglu = hu_ref[...] * _sigmoid(hv_ref[...])
        glu[0:HALO_A, :] = jnp.where(i > 0, hglu, 0.0)
        glu[HALO_A:HALO_A + t, :] = u_ref[...] * _sigmoid(v_ref[...])
        for c in range(t // CHUNK):
            r0 = c * CHUNK
            conv = _conv_a_chunk(glu, cw_ref, r0) + cb_ref[...]
            conv_ref[r0:r0 + CHUNK, :] = conv
            xh, _ = _ln_stats(conv)
            a3 = _silu(xh * lg_ref[...] + lb_ref[...])
            z_ref[r0:r0 + CHUNK, :] = (a3 * _silu(gt_ref[r0:r0 + CHUNK, :])).astype(BF16)

    def cur(col):
        return pl.BlockSpec((t, w), lambda i, col=col: (i, col))

    def prev(col):
        return pl.BlockSpec((HALO_A, w), lambda i, col=col: (jnp.maximum(i * r - 1, 0), col))

    vec = pl.BlockSpec((1, w), lambda i: (0, 0))
    return pl.pallas_call(
        body, name=name, grid=(s // t,),
        in_specs=[cur(0), cur(1), cur(2), prev(0), prev(1), pl.BlockSpec((HALO_A, w), lambda i: (0, 0)), vec, vec, vec],
        out_specs=[pl.BlockSpec((t, w), lambda i: (i, 0)), pl.BlockSpec((t, w), lambda i: (i, 0))],
        out_shape=[jax.ShapeDtypeStruct((s, w), BF16), jax.ShapeDtypeStruct((s, w), F32)],
        scratch_shapes=[pltpu.VMEM((HALO_A + t, w), F32)],
        compiler_params=_params(("parallel",)),
    )(proj, proj, proj, proj, proj, cw, cb, lg, lb)


def _branch_a_bwd(proj, conv, dz, cw, lg, lb, name):
    s = proj.shape[0]
    t = T_A
    w = cw.shape[1]
    r = t // HALO_A
    n = s // t
    nblk = s // HALO_A
    ext = t + HALO_A

    def body(u_ref, v_ref, gt_ref, dz_ref, conv_ref, pu_ref, pv_ref, ngt_ref, ndz_ref, nconv_ref,
             cw_ref, lg_ref, lb_ref, da_ref, dw_ref, dcb_ref, dlg_ref, dlb_ref, glu, dc, dw8):
        i = pl.program_id(0)

        @pl.when(i == 0)
        def _():
            dw8[...] = jnp.zeros_like(dw8)
            dcb_ref[...] = jnp.zeros_like(dcb_ref)
            dlg_ref[...] = jnp.zeros_like(dlg_ref)
            dlb_ref[...] = jnp.zeros_like(dlb_ref)

        glu[0:HALO_A, :] = jnp.where(i > 0, pu_ref[...] * _sigmoid(pv_ref[...]), 0.0)
        glu[HALO_A:HALO_A + t, :] = u_ref[...] * _sigmoid(v_ref[...])
        has_next = i < n - 1
        dcb = jnp.zeros((1, w), F32)
        dlg = jnp.zeros((1, w), F32)
        dlb = jnp.zeros((1, w), F32)
        for c in range(ext // CHUNK):
            r0 = c * CHUNK
            own = r0 < t
            xh, rstd = _ln_stats(conv_ref[r0:r0 + CHUNK, :] if own else nconv_ref[...])
            a2 = xh * lg_ref[...] + lb_ref[...]
            if own:
                gt = gt_ref[r0:r0 + CHUNK, :]
                dzc = dz_ref[r0:r0 + CHUNK, :]
            else:
                gt = ngt_ref[...]
                dzc = ndz_ref[...]
            da2 = dzc * _silu(gt) * _dsilu(a2)
            dxh = da2 * lg_ref[...]
            dconv = rstd * (dxh - jnp.mean(dxh, axis=-1, keepdims=True)
                            - xh * jnp.mean(dxh * xh, axis=-1, keepdims=True))
            if own:
                dc[r0:r0 + CHUNK, :] = dconv
                da_ref[r0:r0 + CHUNK, 2 * w:3 * w] = (dzc * _silu(a2) * _dsilu(gt)).astype(BF16)
                dcb = dcb + jnp.sum(dconv, axis=0, keepdims=True)
                dlg = dlg + jnp.sum(da2 * xh, axis=0, keepdims=True)
                dlb = dlb + jnp.sum(da2, axis=0, keepdims=True)
            else:
                dc[r0:r0 + CHUNK, :] = jnp.where(has_next, dconv, 0.0)
        dcb_ref[...] += dcb
        dlg_ref[...] += dlg
        dlb_ref[...] += dlb
        for c in range(t // CHUNK):
            r0 = c * CHUNK
            dcc = dc[r0:r0 + CHUNK, :]
            dglu = cw_ref[0:1, :] * dc[pl.ds(r0 + CONV_A - 1, CHUNK), :]
            for k in range(1, CONV_A):
                dglu = dglu + cw_ref[k:k + 1, :] * dc[pl.ds(r0 + CONV_A - 1 - k, CHUNK), :]
            for k in range(CONV_A):
                prod = dcc * glu[pl.ds(r0 + 2 + k, CHUNK), :]
                dw8[k] += jnp.sum(prod.reshape(CHUNK // 8, 8, w), axis=0)
            sv = _sigmoid(v_ref[r0:r0 + CHUNK, :])
            da_ref[r0:r0 + CHUNK, 0:w] = (dglu * sv).astype(BF16)
            da_ref[r0:r0 + CHUNK, w:2 * w] = (dglu * u_ref[r0:r0 + CHUNK, :] * sv * (1.0 - sv)).astype(BF16)

        @pl.when(i == n - 1)
        def _():
            dw_ref[...] = jnp.sum(dw8[...], axis=1)

    def cur(col):
        return pl.BlockSpec((t, w), lambda i, col=col: (i, col))

    def prev(col):
        return pl.BlockSpec((HALO_A, w), lambda i, col=col: (jnp.maximum(i * r - 1, 0), col))

    def nxt(col):
        return pl.BlockSpec((HALO_A, w), lambda i, col=col: (jnp.minimum((i + 1) * r, nblk - 1), col))

    own_rows = pl.BlockSpec((t, w), lambda i: (i, 0))
    vec = pl.BlockSpec((1, w), lambda i: (0, 0))
    vo = jax.ShapeDtypeStruct((1, w), F32)
    return pl.pallas_call(
        body, name=name, grid=(n,),
        in_specs=[cur(0), cur(1), cur(2), own_rows, own_rows, prev(0), prev(1), nxt(2), nxt(0), nxt(0),
                  pl.BlockSpec((HALO_A, w), lambda i: (0, 0)), vec, vec],
        out_specs=[pl.BlockSpec((t, 3 * w), lambda i: (i, 0)), pl.BlockSpec((HALO_A, w), lambda i: (0, 0)), vec, vec, vec],
        out_shape=[jax.ShapeDtypeStruct((s, 3 * w), BF16), jax.ShapeDtypeStruct((HALO_A, w), F32), vo, vo, vo],
        scratch_shapes=[pltpu.VMEM((HALO_A + t, w), F32), pltpu.VMEM((ext, w), F32),
                        pltpu.VMEM((HALO_A, 8, w), F32)],
        compiler_params=_params(("arbitrary",)),
    )(proj, proj, proj, dz, conv, proj, proj, proj, dz, conv, cw, lg, lb)


def _conv_b(u_ext, cw_ref, t):
    acc = cw_ref[0:1, :] * u_ext[pl.ds(HALO_B - 2, t), :]
    for k in range(1, CONV_B):
        acc = acc + cw_ref[k:k + 1, :] * u_ext[pl.ds(HALO_B - 2 + k, t), :]
    return acc


def _branch_b_fwd(proj, cw, name):
    s = proj.shape[0]
    t = T_B
    w = cw.shape[1]
    r = t // HALO_B

    def body(h_ref, b_ref, c_ref, gt_ref, ph_ref, pc_ref, cw_ref, z_ref, u_ext):
        i = pl.program_id(0)
        u_ext[0:HALO_B, :] = jnp.where(i > 0, pc_ref[...] * ph_ref[...], 0.0)
        u_ext[HALO_B:HALO_B + t, :] = c_ref[...] * h_ref[...]
        cv = _conv_b(u_ext, cw_ref, t)
        z_ref[...] = (b_ref[...] * cv * _silu(gt_ref[...])).astype(BF16)

    def cur(col):
        return pl.BlockSpec((t, w), lambda i, col=col: (i, col))

    def prev(col):
        return pl.BlockSpec((HALO_B, w), lambda i, col=col: (jnp.maximum(i * r - 1, 0), col))

    return pl.pallas_call(
        body, name=name, grid=(s // t,),
        in_specs=[cur(3), cur(4), cur(5), cur(6), prev(3), prev(5), pl.BlockSpec((HALO_B, w), lambda i: (0, 0))],
        out_specs=pl.BlockSpec((t, w), lambda i: (i, 0)),
        out_shape=jax.ShapeDtypeStruct((s, w), BF16),
        scratch_shapes=[pltpu.VMEM((HALO_B + t, w), F32)],
        compiler_params=_params(("parallel",)),
    )(proj, proj, proj, proj, proj, proj, cw)


def _branch_b_bwd(proj, dz, cw, name):
    s = proj.shape[0]
    t = T_B
    w = cw.shape[1]
    r = t // HALO_B
    n = s // t
    nblk = s // HALO_B

    def body(h_ref, b_ref, c_ref, gt_ref, dz_ref, ph_ref, pc_ref, nb_ref, ngt_ref, ndz_ref, cw_ref,
             db_ref, dw_ref, u_ext, dcv_ext, dw8):
        i = pl.program_id(0)

        @pl.when(i == 0)
        def _():
            dw8[...] = jnp.zeros_like(dw8)

        u_ext[0:HALO_B, :] = jnp.where(i > 0, pc_ref[...] * ph_ref[...], 0.0)
        u_ext[HALO_B:HALO_B + t, :] = c_ref[...] * h_ref[...]
        cv = _conv_b(u_ext, cw_ref, t)
        gt = gt_ref[...]
        dhb = dz_ref[...] * _silu(gt)
        db_ref[:, 3 * w:4 * w] = (dz_ref[...] * b_ref[...] * cv * _dsilu(gt)).astype(BF16)
        db_ref[:, w:2 * w] = (dhb * cv).astype(BF16)
        dcv = dhb * b_ref[...]
        dcv_ext[0:t, :] = dcv
        ndcv = ndz_ref[...] * _silu(ngt_ref[...]) * nb_ref[...]
        dcv_ext[t:t + HALO_B, :] = jnp.where(i < n - 1, ndcv, 0.0)
        du = cw_ref[0:1, :] * dcv_ext[pl.ds(2, t), :]
        for k in range(1, CONV_B):
            du = du + cw_ref[k:k + 1, :] * dcv_ext[pl.ds(2 - k, t), :]
        db_ref[:, 2 * w:3 * w] = (du * h_ref[...]).astype(BF16)
        db_ref[:, 0:w] = (du * c_ref[...]).astype(BF16)
        for k in range(CONV_B):
            prod = dcv * u_ext[pl.ds(HALO_B - 2 + k, t), :]
            dw8[k] += jnp.sum(prod.reshape(t // 8, 8, w), axis=0)

        @pl.when(i == n - 1)
        def _():
            dw_ref[...] = jnp.sum(dw8[...], axis=1)

    def cur(col):
        return pl.BlockSpec((t, w), lambda i, col=col: (i, col))

    def prev(col):
        return pl.BlockSpec((HALO_B, w), lambda i, col=col: (jnp.maximum(i * r - 1, 0), col))

    def nxt(col):
        return pl.BlockSpec((HALO_B, w), lambda i, col=col: (jnp.minimum((i + 1) * r, nblk - 1), col))

    return pl.pallas_call(
        body, name=name, grid=(n,),
        in_specs=[cur(3), cur(4), cur(5), cur(6), pl.BlockSpec((t, w), lambda i: (i, 0)),
                  prev(3), prev(5), nxt(4), nxt(6),
                  pl.BlockSpec((HALO_B, w), lambda i: (jnp.minimum((i + 1) * r, nblk - 1), 0)),
                  pl.BlockSpec((HALO_B, w), lambda i: (0, 0))],
        out_specs=[pl.BlockSpec((t, 4 * w), lambda i: (i, 0)), pl.BlockSpec((HALO_B, w), lambda i: (0, 0))],
        out_shape=[jax.ShapeDtypeStruct((s, 4 * w), BF16), jax.ShapeDtypeStruct((HALO_B, w), F32)],
        scratch_shapes=[pltpu.VMEM((HALO_B + t, w), F32), pltpu.VMEM((t + HALO_B, w), F32),
                        pltpu.VMEM((HALO_B, 8, w), F32)],
        compiler_params=_params(("arbitrary",)),
    )(proj, proj, proj, proj, dz, proj, proj, proj, proj, dz, cw)


def _forget_prep(proj, bf, fblk, name):
    s = proj.shape[0]
    t = T_CUM

    def body(f_ref, bf_ref, cum_ref, cumt_ref, sgt_ref, carry):
        i = pl.program_id(0)

        @pl.when(i == 0)
        def _():
            carry[...] = jnp.zeros_like(carry)

        z = f_ref[...] + bf_ref[...]
        logf = jnp.minimum(z, 0.0) - jnp.log1p(jnp.exp(-jnp.abs(z)))
        tri = (_iota((t, t), 0) >= _iota((t, t), 1)).astype(F32)
        cum = _dot(tri, logf, NN, precision=lax.Precision.HIGHEST) + carry[0:1, :]
        carry[0:1, :] = cum[t - 1:t, :]
        cum_ref[...] = cum
        cumt_ref[...] = cum.T[0:8, :]
        sgt_ref[...] = _sigmoid(-z).T[0:8, :]

    return pl.pallas_call(
        body, name=name, grid=(s // t,),
        in_specs=[pl.BlockSpec((t, LANES), lambda i: (i, fblk)), pl.BlockSpec((1, LANES), lambda i: (0, 0))],
        out_specs=[pl.BlockSpec((t, LANES), lambda i: (i, 0)), pl.BlockSpec((8, t), lambda i: (0, i)),
                   pl.BlockSpec((8, t), lambda i: (0, i))],
        out_shape=[jax.ShapeDtypeStruct((s, LANES), F32), jax.ShapeDtypeStruct((8, s), F32),
                   jax.ShapeDtypeStruct((8, s), F32)],
        scratch_shapes=[pltpu.VMEM((8, LANES), F32)],
        compiler_params=_params(("arbitrary",)),
    )(proj, bf)


def _lane_pick(x, lane):
    return jnp.sum(jnp.where(_iota(x.shape, 1) == lane, x, 0.0), axis=1, keepdims=True)


def _sublane_pick(x, row):
    return jnp.sum(jnp.where(_iota(x.shape, 0) == row, x, 0.0), axis=0, keepdims=True)


def _head_mask(hh):
    lane = _iota((1, LANES), 1)
    return (lane >= HEAD_DIM * hh) & (lane < HEAD_DIM * (hh + 1))


def _causal_pairs(nq, ratio, kv_major):
    if kv_major:
        pairs = [(q, k) for k in range(nq // ratio) for q in range(k * ratio, nq)]
    else:
        pairs = [(q, k) for q in range(nq) for k in range(q // ratio + 1)]
    qs = np.asarray([p[0] for p in pairs], np.int32)
    ks = np.asarray([p[1] for p in pairs], np.int32)
    return qs, ks, (ks == qs // ratio).astype(np.int32)


def _fox_scores(qm, kb, cum_ref, cumt_ref, h, qi, ki, diag, tq, tk):
    cq0 = _lane_pick(cum_ref[0:1, :], h)
    sc = _dot(qm, kb, NT) + (cq0 - _sublane_pick(cumt_ref[...], h))
    causal = (qi * tq + _iota((tq, tk), 0)) >= (ki * tk + _iota((tq, tk), 1))
    return jnp.where(causal | (diag == 0), sc, NEG_BIG)


def _fox_fwd(proj, cum, cumt, qblk, kblk, vblk, name):
    s = proj.shape[0]
    tq, tk = T_ATT_Q, T_ATT_K
    n_pair = 4
    qi_np, ki_np, diag_np = _causal_pairs(s // tq, tk // tq, kv_major=False)
    scale = HEAD_DIM ** -0.5

    def body(qi_ref, ki_ref, diag_ref, q_ref, k_ref, v_ref, cum_ref, cumt_ref, o_ref, lse_ref, m_s, acc_s):
        hp = pl.program_id(0)
        step = pl.program_id(1)
        qi, ki, diag = qi_ref[step], ki_ref[step], diag_ref[step]

        @pl.when(ki == 0)
        def _():
            m_s[...] = jnp.full_like(m_s, NEG_BIG)
            acc_s[...] = jnp.zeros_like(acc_s)

        q = q_ref[...] * scale
        kb = k_ref[...].astype(BF16)
        v = v_ref[...]
        m_old = [m_s[0], m_s[1]]
        acc_old = [acc_s[0], acc_s[1]]
        scores, values = [], []
        for hh in range(2):
            hm = _head_mask(hh)
            qm = jnp.where(hm, q, 0.0).astype(BF16)
            values.append(jnp.where(hm, v, 1.0).astype(BF16))
            scores.append(_fox_scores(qm, kb, cum_ref, cumt_ref, 2 * hp + hh, qi, ki, diag, tq, tk))
        m_new = [jnp.maximum(m_old[hh], jnp.max(scores[hh], axis=1, keepdims=True)) for hh in range(2)]
        probs = [jnp.exp(scores[hh] - m_new[hh]).astype(BF16) for hh in range(2)]
        acc_new = [jnp.exp(m_old[hh] - m_new[hh]) * acc_old[hh] + _dot(probs[hh], values[hh], NN) for hh in range(2)]
        for hh in range(2):
            acc_s[hh] = acc_new[hh]
            m_s[hh] = m_new[hh]

        @pl.when(diag == 1)
        def _():
            lane = _iota((tq, LANES), 1)
            a0, a1 = acc_s[0], acc_s[1]
            o_ref[...] = jnp.where(lane < HEAD_DIM, a0 / pltpu.roll(a0, HEAD_DIM, axis=1),
                                   a1 / pltpu.roll(a1, HEAD_DIM, axis=1))
            lse0 = m_s[0] + jnp.log(a0[:, HEAD_DIM:HEAD_DIM + 1])
            lse1 = m_s[1] + jnp.log(a1[:, 0:1])
            lse_ref[0] = jnp.where(lane == 0, lse0, jnp.where(lane == 1, lse1, 0.0))

    grid_spec = pltpu.PrefetchScalarGridSpec(
        num_scalar_prefetch=3, grid=(n_pair, len(qi_np)),
        in_specs=[pl.BlockSpec((tq, LANES), lambda hp, st, qi, ki, dg: (qi[st], qblk + hp)),
                  pl.BlockSpec((tk, LANES), lambda hp, st, qi, ki, dg: (ki[st], kblk + hp)),
                  pl.BlockSpec((tk, LANES), lambda hp, st, qi, ki, dg: (ki[st], vblk + hp)),
                  pl.BlockSpec((tq, LANES), lambda hp, st, qi, ki, dg: (qi[st], 0)),
                  pl.BlockSpec((8, tk), lambda hp, st, qi, ki, dg: (0, ki[st]))],
        out_specs=[pl.BlockSpec((tq, LANES), lambda hp, st, qi, ki, dg: (qi[st], hp)),
                   pl.BlockSpec((1, tq, LANES), lambda hp, st, qi, ki, dg: (hp, qi[st], 0))],
        scratch_shapes=[pltpu.VMEM((2, tq, 1), F32), pltpu.VMEM((2, tq, LANES), F32)])
    return pl.pallas_call(
        body, name=name, grid_spec=grid_spec,
        out_shape=[jax.ShapeDtypeStruct((s, n_pair * LANES), F32), jax.ShapeDtypeStruct((n_pair, s, LANES), F32)],
        compiler_params=_params(("parallel", "arbitrary")),
    )(jnp.asarray(qi_np), jnp.asarray(ki_np), jnp.asarray(diag_np), proj, proj, proj, cum, cumt)


def _fox_bwd_prep(proj, dz, o, gblk, name):
    s, w = o.shape
    t = T_ELEM
    n_head = w // HEAD_DIM

    def body(gt_ref, dz_ref, o_ref, do_ref, dg_ref, dl_ref):
        gt = gt_ref[...]
        do = dz_ref[...] * _silu(gt)
        do_ref[...] = do
        dg_ref[...] = (dz_ref[...] * o_ref[...] * _dsilu(gt)).astype(BF16)
        sel = (_iota((w, LANES), 0) // HEAD_DIM == _iota((w, LANES), 1)).astype(F32)
        dl_ref[...] = _dot(do * o_ref[...], sel, NN, precision=lax.Precision.HIGHEST)

    assert n_head <= LANES
    row = pl.BlockSpec((t, w), lambda i: (i, 0))
    return pl.pallas_call(
        body, name=name, grid=(s // t,),
        in_specs=[pl.BlockSpec((t, w), lambda i: (i, gblk)), row, row],
        out_specs=[row, row, pl.BlockSpec((t, LANES), lambda i: (i, 0))],
        out_shape=[jax.ShapeDtypeStruct((s, w), F32), jax.ShapeDtypeStruct((s, w), BF16),
                   jax.ShapeDtypeStruct((s, LANES), F32)],
        compiler_params=_params(("parallel",)),
    )(proj, dz, o)


def _fox_bwd(proj, do, lse, delta, cum, cumt, qblk, kblk, vblk, name):
    s = proj.shape[0]
    tq, tk = T_ATT_Q, T_ATT_K
    nq = s // tq
    n_pair = 4
    qi_np, ki_np, diag_np = _causal_pairs(nq, tk // tq, kv_major=True)
    n_step = len(qi_np)
    scale = HEAD_DIM ** -0.5

    def body(qi_ref, ki_ref, diag_ref, q_ref, k_ref, v_ref, do_ref, lse_ref, dl_ref, cum_ref, cumt_ref,
             dq_ref, dk_ref, dv_ref, dct_ref, dcq_ref, dq_s, dk_s, dv_s, dc_s, dcq_s):
        hp = pl.program_id(0)
        step = pl.program_id(1)
        qi, ki, diag = qi_ref[step], ki_ref[step], diag_ref[step]

        @pl.when(step == 0)
        def _():
            dq_s[...] = jnp.zeros_like(dq_s)
            dcq_s[...] = jnp.zeros_like(dcq_s)

        @pl.when(qi == ki * (tk // tq))
        def _():
            dk_s[...] = jnp.zeros_like(dk_s)
            dv_s[...] = jnp.zeros_like(dv_s)
            dc_s[...] = jnp.zeros_like(dc_s)

        q = q_ref[...] * scale
        do = do_ref[...]
        kb = k_ref[...].astype(BF16)
        vb = v_ref[...].astype(BF16)
        sub = _iota((8, tk), 0)
        dq_new = jnp.zeros((tq, LANES), F32)
        dcq_new = jnp.zeros((tq, LANES), F32)
        lane = _iota((tq, LANES), 1)
        for hh in range(2):
            h = 2 * hp + hh
            hm = _head_mask(hh)
            qm = jnp.where(hm, q, 0.0).astype(BF16)
            dom = jnp.where(hm, do, 0.0).astype(BF16)
            sc = _fox_scores(qm, kb, cum_ref, cumt_ref, h, qi, ki, diag, tq, tk)
            p = jnp.exp(sc - _lane_pick(lse_ref[0], hh))
            dv_s[...] += _dot(p.astype(BF16), dom, TN)
            dp = _dot(dom, vb, NT)
            ds = p * (dp - _lane_pick(dl_ref[...], h))
            dc_s[...] += jnp.where(sub == h, -jnp.sum(ds, axis=0, keepdims=True), 0.0)
            dcq_new = dcq_new + jnp.where(lane == h, jnp.sum(ds, axis=1, keepdims=True), 0.0)
            dsb = ds.astype(BF16)
            dk_s[...] += _dot(dsb, qm, TN)
            dq_new = dq_new + jnp.where(hm, _dot(dsb, kb, NN), 0.0)
        row0 = pl.multiple_of(qi * tq, tq)
        dq_s[pl.ds(row0, tq), :] += dq_new * scale
        dcq_s[pl.ds(row0, tq), :] += dcq_new

        @pl.when(qi == nq - 1)
        def _():
            dk_ref[...] = dk_s[...].astype(BF16)
            dv_ref[...] = dv_s[...].astype(BF16)
            dct_ref[0] = dc_s[...]

        @pl.when(step == n_step - 1)
        def _():
            dq_ref[...] = dq_s[...].astype(BF16)
            dcq_ref[0] = dcq_s[...]

    grid_spec = pltpu.PrefetchScalarGridSpec(
        num_scalar_prefetch=3, grid=(n_pair, n_step),
        in_specs=[pl.BlockSpec((tq, LANES), lambda hp, st, qi, ki, dg: (qi[st], qblk + hp)),
                  pl.BlockSpec((tk, LANES), lambda hp, st, qi, ki, dg: (ki[st], kblk + hp)),
                  pl.BlockSpec((tk, LANES), lambda hp, st, qi, ki, dg: (ki[st], vblk + hp)),
                  pl.BlockSpec((tq, LANES), lambda hp, st, qi, ki, dg: (qi[st], hp)),
                  pl.BlockSpec((1, tq, LANES), lambda hp, st, qi, ki, dg: (hp, qi[st], 0)),
                  pl.BlockSpec((tq, LANES), lambda hp, st, qi, ki, dg: (qi[st], 0)),
                  pl.BlockSpec((tq, LANES), lambda hp, st, qi, ki, dg: (qi[st], 0)),
                  pl.BlockSpec((8, tk), lambda hp, st, qi, ki, dg: (0, ki[st]))],
        out_specs=[pl.BlockSpec((s, LANES), lambda hp, st, qi, ki, dg: (0, hp)),
                   pl.BlockSpec((tk, LANES), lambda hp, st, qi, ki, dg: (ki[st], hp)),
                   pl.BlockSpec((tk, LANES), lambda hp, st, qi, ki, dg: (ki[st], hp)),
                   pl.BlockSpec((1, 8, tk), lambda hp, st, qi, ki, dg: (hp, 0, ki[st])),
                   pl.BlockSpec((1, s, LANES), lambda hp, st, qi, ki, dg: (hp, 0, 0))],
        scratch_shapes=[pltpu.VMEM((s, LANES), F32), pltpu.VMEM((tk, LANES), F32), pltpu.VMEM((tk, LANES), F32),
                        pltpu.VMEM((8, tk), F32), pltpu.VMEM((s, LANES), F32)])
    w = n_pair * LANES
    return pl.pallas_call(
        body, name=name, grid_spec=grid_spec,
        out_shape=[jax.ShapeDtypeStruct((s, w), BF16), jax.ShapeDtypeStruct((s, w), BF16),
                   jax.ShapeDtypeStruct((s, w), BF16), jax.ShapeDtypeStruct((n_pair, 8, s), F32),
                   jax.ShapeDtypeStruct((n_pair, s, LANES), F32)],
        compiler_params=_params(("parallel", "arbitrary")),
    )(jnp.asarray(qi_np), jnp.asarray(ki_np), jnp.asarray(diag_np), proj, proj, proj, do, lse, delta, cum, cumt)


def _forget_bwd(dcumt4, dcumq4, sgt, name):
    s = sgt.shape[1]
    t = T_CUM
    n = s // t

    def body(d_ref, dq_ref, sg_ref, df_ref, dbf_ref, carry):
        i = pl.program_id(0)

        @pl.when(i == 0)
        def _():
            carry[...] = jnp.zeros_like(carry)
            dbf_ref[...] = jnp.zeros_like(dbf_ref)

        dq = dq_ref[0] + dq_ref[1] + dq_ref[2] + dq_ref[3]
        d = d_ref[0] + d_ref[1] + d_ref[2] + d_ref[3] + dq.T[0:8, :]
        upper = (_iota((t, t), 0) >= _iota((t, t), 1)).astype(F32)
        dlog = _dot(d, upper, NN, precision=lax.Precision.HIGHEST) + carry[:, 0:1]
        carry[...] += jnp.sum(d, axis=1, keepdims=True)
        dzt = dlog * sg_ref[...]
        dbf_ref[...] += jnp.sum(dzt, axis=1, keepdims=True)
        padded = jnp.concatenate([dzt, jnp.zeros((LANES - 8, t), F32)], axis=0)
        df_ref[...] = padded.T.astype(BF16)

    return pl.pallas_call(
        body, name=name, grid=(n,),
        in_specs=[pl.BlockSpec((4, 8, t), lambda i: (0, 0, n - 1 - i)),
                  pl.BlockSpec((4, t, LANES), lambda i: (0, n - 1 - i, 0)),
                  pl.BlockSpec((8, t), lambda i: (0, n - 1 - i))],
        out_specs=[pl.BlockSpec((t, LANES), lambda i: (n - 1 - i, 0)), pl.BlockSpec((8, LANES), lambda i: (0, 0))],
        out_shape=[jax.ShapeDtypeStruct((s, LANES), BF16), jax.ShapeDtypeStruct((8, LANES), F32)],
        scratch_shapes=[pltpu.VMEM((8, LANES), F32)],
        compiler_params=_params(("arbitrary",)),
    )(dcumt4, dcumq4, sgt)


def _mem_softmax(qm, kp):
    sc = _dot(qm, kp, NT) * (HEAD_DIM ** -0.5)
    e = jnp.exp(sc - jnp.max(sc, axis=1, keepdims=True))
    return e / jnp.sum(e, axis=1, keepdims=True)


def _branch_m_fwd(proj, mkv, mblk, name):
    s = proj.shape[0]
    t = T_M
    mw = mkv.shape[1] // 2
    ml = mkv.shape[0]

    def body(m_ref, kv_ref, z_ref):
        outs = []
        for pr in range(mw // LANES):
            qp = m_ref[:, pr * LANES:(pr + 1) * LANES]
            kp = kv_ref[:, pr * LANES:(pr + 1) * LANES].astype(BF16)
            vp = kv_ref[:, mw + pr * LANES:mw + (pr + 1) * LANES].astype(BF16)
            oh = []
            for hh in range(2):
                qm = jnp.where(_head_mask(hh), qp, 0.0).astype(BF16)
                oh.append(_dot(_mem_softmax(qm, kp).astype(BF16), vp, NN))
            outs.append(jnp.where(_iota((t, LANES), 1) < HEAD_DIM, oh[0], oh[1]))
        o = jnp.concatenate(outs, axis=1)
        z_ref[...] = (o * _silu(m_ref[:, mw:2 * mw])).astype(BF16)

    return pl.pallas_call(
        body, name=name, grid=(s // t,),
        in_specs=[pl.BlockSpec((t, 2 * mw), lambda i: (i, mblk)), pl.BlockSpec((ml, 2 * mw), lambda i: (0, 0))],
        out_specs=pl.BlockSpec((t, mw), lambda i: (i, 0)),
        out_shape=jax.ShapeDtypeStruct((s, mw), BF16),
        compiler_params=_params(("parallel",)),
    )(proj, mkv)


def _branch_m_bwd(proj, mkv, dz, mblk, name):
    s = proj.shape[0]
    t = T_M
    mw = mkv.shape[1] // 2
    ml = mkv.shape[0]
    scale = HEAD_DIM ** -0.5

    def body(m_ref, kv_ref, dz_ref, dm_ref, dkv_ref):
        i = pl.program_id(0)

        @pl.when(i == 0)
        def _():
            dkv_ref[...] = jnp.zeros_like(dkv_ref)

        gt = m_ref[:, mw:2 * mw]
        dz = dz_ref[...]
        do = dz * _silu(gt)
        outs = []
        for pr in range(mw // LANES):
            cols = slice(pr * LANES, (pr + 1) * LANES)
            vcols = slice(mw + pr * LANES, mw + (pr + 1) * LANES)
            qp = m_ref[:, cols]
            kp = kv_ref[:, cols].astype(BF16)
            vp = kv_ref[:, vcols].astype(BF16)
            dop = do[:, cols]
            oh = []
            dq = jnp.zeros((t, LANES), F32)
            dk = jnp.zeros((ml, LANES), F32)
            dv = jnp.zeros((ml, LANES), F32)
            for hh in range(2):
                hm = _head_mask(hh)
                qm = jnp.where(hm, qp, 0.0).astype(BF16)
                dom = jnp.where(hm, dop, 0.0).astype(BF16)
                p = _mem_softmax(qm, kp)
                pb = p.astype(BF16)
                oh.append(_dot(pb, vp, NN))
                dv = dv + _dot(pb, dom, TN)
                dp = _dot(dom, vp, NT)
                ds = p * (dp - jnp.sum(dp * p, axis=1, keepdims=True))
                dsb = (ds * scale).astype(BF16)
                dq = dq + jnp.where(hm, _dot(dsb, kp, NN), 0.0)
                dk = dk + _dot(dsb, qm, TN)
            outs.append(jnp.where(_iota((t, LANES), 1) < HEAD_DIM, oh[0], oh[1]))
            dm_ref[:, cols] = dq.astype(BF16)
            dkv_ref[:, cols] += dk
            dkv_ref[:, vcols] += dv
        o = jnp.concatenate(outs, axis=1)
        dm_ref[:, mw:2 * mw] = (dz * o * _dsilu(gt)).astype(BF16)

    return pl.pallas_call(
        body, name=name, grid=(s // t,),
        in_specs=[pl.BlockSpec((t, 2 * mw), lambda i: (i, mblk)), pl.BlockSpec((ml, 2 * mw), lambda i: (0, 0)),
                  pl.BlockSpec((t, mw), lambda i: (i, 0))],
        out_specs=[pl.BlockSpec((t, 2 * mw), lambda i: (i, 0)), pl.BlockSpec((ml, 2 * mw), lambda i: (0, 0))],
        out_shape=[jax.ShapeDtypeStruct((s, 2 * mw), BF16), jax.ShapeDtypeStruct((ml, 2 * mw), F32)],
        compiler_params=_params(("arbitrary",)),
    )(proj, mkv, dz)


def _merge_parts(z_refs, g_refs, pcat, bounds):
    ys, sgs = [], []
    merged = None
    for zr, gr, (lo, hi) in zip(z_refs, g_refs, bounds):
        y = _dot(zr[...], pcat[lo:hi, :], NN)
        sg = _sigmoid(gr[...])
        ys.append(y)
        sgs.append(sg)
        merged = sg * y if merged is None else merged + sg * y
    return ys, sgs, merged


def _branch_bounds(zs):
    bounds, lo = [], 0
    for z in zs:
        bounds.append((lo, lo + z.shape[1]))
        lo += z.shape[1]
    return bounds


def _matmul_copies(x, name):
    s, d = x.shape
    t = T_MERGE

    def body(x_ref, b_ref, bt_ref):
        b_ref[...] = x_ref[...].astype(BF16)
        bt_ref[...] = x_ref[...].T.astype(BF16)

    return pl.pallas_call(
        body, name=name, grid=(s // t,), in_specs=[pl.BlockSpec((t, d), lambda i: (i, 0))],
        out_specs=[pl.BlockSpec((t, d), lambda i: (i, 0)), pl.BlockSpec((d, t), lambda i: (0, i))],
        out_shape=[jax.ShapeDtypeStruct((s, d), BF16), jax.ShapeDtypeStruct((d, s), BF16)],
        compiler_params=_params(("parallel",)))(x)


def _merge_fwd(zs, proj, gblk, pcat, wout, x, lng, lnb, alpha, name):
    s, d = x.shape
    t = T_MERGE
    bounds = _branch_bounds(zs)

    def body(*refs):
        z_refs, g_refs = refs[0:4], refs[4:8]
        pcat_hbm, wout_hbm, x_ref, lng_ref, lnb_ref, y_ref, yb_ref, ybt_ref, pcat_v, wout_v = refs[8:]

        @pl.when(pl.program_id(0) == 0)
        def _():
            pltpu.sync_copy(pcat_hbm, pcat_v)
            pltpu.sync_copy(wout_hbm, wout_v)

        _, _, merged = _merge_parts(z_refs, g_refs, pcat_v, bounds)
        h = alpha * x_ref[...] + _dot(merged.astype(BF16), wout_v[...], NN)
        xh, _ = _ln_stats(h)
        y = xh * lng_ref[...] + lnb_ref[...]
        y_ref[...] = y
        yb_ref[...] = y.astype(BF16)
        ybt_ref[...] = y.T.astype(BF16)

    row = pl.BlockSpec((t, d), lambda i: (i, 0))
    vec = pl.BlockSpec((1, d), lambda i: (0, 0))
    in_specs = ([pl.BlockSpec((t, z.shape[1]), lambda i: (i, 0)) for z in zs]
                + [pl.BlockSpec((t, d), lambda i, k=k: (i, gblk + k)) for k in range(4)]
                + [ANY, ANY, row, vec, vec])
    return pl.pallas_call(
        body, name=name, grid=(s // t,), in_specs=in_specs,
        out_specs=[row, row, pl.BlockSpec((d, t), lambda i: (0, i))],
        out_shape=[jax.ShapeDtypeStruct((s, d), F32), jax.ShapeDtypeStruct((s, d), BF16),
                   jax.ShapeDtypeStruct((d, s), BF16)],
        scratch_shapes=[pltpu.VMEM(pcat.shape, BF16), pltpu.VMEM(wout.shape, BF16)],
        compiler_params=_params(("arbitrary",)),
    )(*zs, proj, proj, proj, proj, pcat, wout, x, lng, lnb)


def _merge_bwd(zs, proj, gblk, pcat, wout, x, lng, lnb, dy, alpha, name):
    s, d = x.shape
    t = T_MERGE
    n = s // t
    bounds = _branch_bounds(zs)

    def body(*refs):
        z_refs, g_refs = refs[0:4], refs[4:8]
        pcat_hbm, wout_hbm, x_ref, lng_ref, lnb_ref, dy_ref = refs[8:14]
        dx_ref, dg_ref = refs[14:16]
        dz_refs = refs[16:20]
        dpcat_hbm, dwout_hbm, dlng_ref, dlnb_ref = refs[20:24]
        pcat_v, wout_v, dpcat_v, dwout_v = refs[24:]
        i = pl.program_id(0)

        @pl.when(i == 0)
        def _():
            pltpu.sync_copy(pcat_hbm, pcat_v)
            pltpu.sync_copy(wout_hbm, wout_v)
            dpcat_v[...] = jnp.zeros_like(dpcat_v)
            dwout_v[...] = jnp.zeros_like(dwout_v)
            dlng_ref[...] = jnp.zeros_like(dlng_ref)
            dlnb_ref[...] = jnp.zeros_like(dlnb_ref)

        ys, sgs, merged = _merge_parts(z_refs, g_refs, pcat_v, bounds)
        mb = merged.astype(BF16)
        h = alpha * x_ref[...] + _dot(mb, wout_v[...], NN)
        xh, rstd = _ln_stats(h)
        dyv = dy_ref[...]
        dlng_ref[...] += jnp.sum(dyv * xh, axis=0, keepdims=True)
        dlnb_ref[...] += jnp.sum(dyv, axis=0, keepdims=True)
        dxh = dyv * lng_ref[...]
        dh = rstd * (dxh - jnp.mean(dxh, axis=-1, keepdims=True) - xh * jnp.mean(dxh * xh, axis=-1, keepdims=True))
        dx_ref[...] = alpha * dh
        dhb = dh.astype(BF16)
        dwout_v[...] += _dot(mb, dhb, TN)
        dmerged = _dot(dhb, wout_v[...], NT)
        for k, (zr, (lo, hi)) in enumerate(zip(z_refs, bounds)):
            sg = sgs[k]
            dg_ref[:, k * d:(k + 1) * d] = (dmerged * ys[k] * sg * (1.0 - sg)).astype(BF16)
            dyk = (dmerged * sg).astype(BF16)
            dpcat_v[lo:hi, :] += _dot(zr[...], dyk, TN)
            dz_refs[k][...] = _dot(dyk, pcat_v[lo:hi, :], NT)

        @pl.when(i == n - 1)
        def _():
            pltpu.sync_copy(dpcat_v, dpcat_hbm)
            pltpu.sync_copy(dwout_v, dwout_hbm)

    row = pl.BlockSpec((t, d), lambda i: (i, 0))
    vec = pl.BlockSpec((1, d), lambda i: (0, 0))
    z_specs = [pl.BlockSpec((t, z.shape[1]), lambda i: (i, 0)) for z in zs]
    in_specs = (z_specs + [pl.BlockSpec((t, d), lambda i, k=k: (i, gblk + k)) for k in range(4)]
                + [ANY, ANY, row, vec, vec, row])
    out_specs = [row, pl.BlockSpec((t, 4 * d), lambda i: (i, 0))] + z_specs + [ANY, ANY, vec, vec]
    vo = jax.ShapeDtypeStruct((1, d), F32)
    out_shape = ([jax.ShapeDtypeStruct((s, d), F32), jax.ShapeDtypeStruct((s, 4 * d), BF16)]
                 + [jax.ShapeDtypeStruct(z.shape, F32) for z in zs]
                 + [jax.ShapeDtypeStruct(pcat.shape, F32), jax.ShapeDtypeStruct(wout.shape, F32), vo, vo])
    return pl.pallas_call(
        body, name=name, grid=(n,), in_specs=in_specs, out_specs=out_specs, out_shape=out_shape,
        scratch_shapes=[pltpu.VMEM(pcat.shape, BF16), pltpu.VMEM(wout.shape, BF16),
                        pltpu.VMEM(pcat.shape, F32), pltpu.VMEM(wout.shape, F32)],
        compiler_params=_params(("arbitrary",)),
    )(*zs, proj, proj, proj, proj, pcat, wout, x, lng, lnb, dy)


def _loss_head(y, target, name):
    s, d = y.shape
    t = T_ELEM

    def body(y_ref, t_ref, dy_ref, loss_ref):
        @pl.when(pl.program_id(0) == 0)
        def _():
            loss_ref[...] = jnp.zeros_like(loss_ref)

        e = y_ref[...] - t_ref[...]
        dy_ref[...] = e * (1.0 / d)
        loss_ref[...] += 0.5 * jnp.sum(jnp.mean(e * e, axis=-1, keepdims=True), axis=0, keepdims=True)

    row = pl.BlockSpec((t, d), lambda i: (i, 0))
    return pl.pallas_call(
        body, name=name, grid=(s // t,), in_specs=[row, row],
        out_specs=[row, pl.BlockSpec((8, LANES), lambda i: (0, 0))],
        out_shape=[jax.ShapeDtypeStruct((s, d), F32), jax.ShapeDtypeStruct((8, LANES), F32)],
        compiler_params=_params(("arbitrary",)),
    )(y, target)


def _adamw(w, g, m, v, name):
    r, c = w.shape
    t = _pick(r, (256, 128, 64, 32, 16, 8))

    def body(w_ref, g_ref, m_ref, v_ref, d_ref, nm_ref, nv_ref):
        gv = g_ref[...]
        nm = ADAM_B1 * m_ref[...] + (1.0 - ADAM_B1) * gv
        nv = ADAM_B2 * v_ref[...] + (1.0 - ADAM_B2) * (gv * gv)
        m_hat = nm / (1.0 - ADAM_B1 ** ADAM_STEP)
        v_hat = nv / (1.0 - ADAM_B2 ** ADAM_STEP)
        d_ref[...] = -ADAM_LR * (m_hat / (jnp.sqrt(v_hat) + ADAM_EPS) + ADAM_WD * w_ref[...])
        nm_ref[...] = nm
        nv_ref[...] = nv

    blk = pl.BlockSpec((t, c), lambda i: (i, 0))
    o = jax.ShapeDtypeStruct((r, c), F32)
    return pl.pallas_call(body, name=name, grid=(r // t,), in_specs=[blk] * 4, out_specs=[blk] * 3,
                          out_shape=[o, o, o], compiler_params=_params(("parallel",)))(w, g, m, v)


def _sum_leading(x, out_dtype, name):
    k, r, c = x.shape
    t = _pick(r, (256, 128, 64, 32, 16, 8))

    def body(x_ref, o_ref):
        acc = x_ref[0].astype(F32)
        for j in range(1, k):
            acc = acc + x_ref[j].astype(F32)
        o_ref[...] = acc.astype(out_dtype)

    return pl.pallas_call(body, name=name, grid=(r // t,),
                          in_specs=[pl.BlockSpec((k, t, c), lambda i: (0, i, 0))],
                          out_specs=pl.BlockSpec((t, c), lambda i: (i, 0)),
                          out_shape=jax.ShapeDtypeStruct((r, c), out_dtype), compiler_params=_params(("parallel",)))(x)


def _position():
    return lax.axis_index("x"), lax.axis_index("y"), lax.axis_index("c")


def _chip_peers(x, y):
    return [(1 - x, y), (x, 1 - y), (1 - x, 1 - y)]


def _comm_call(body, n_in, out_shape, n_remote, n_local, name):
    return pl.pallas_call(
        body, name=name, in_specs=[ANY] * n_in, out_specs=[ANY] * len(out_shape), out_shape=out_shape,
        scratch_shapes=[pltpu.SemaphoreType.DMA((n_remote,)), pltpu.SemaphoreType.DMA((n_remote,)),
                        pltpu.SemaphoreType.DMA((max(n_local, 1),))])


def _run_copies(local, remote, send, recv, loc):
    copies = [pltpu.make_async_copy(src, dst, loc.at[k]) for k, (src, dst) in enumerate(local)]
    copies += [pltpu.make_async_remote_copy(src_ref=src, dst_ref=dst, send_sem=send.at[k], recv_sem=recv.at[k],
                                            device_id=peer, device_id_type=MESH)
               for k, (src, dst, peer) in enumerate(remote)]
    for cp in copies:
        cp.start()
    for cp in copies:
        cp.wait()


COPY_BYTES = 1024 * 1024


def _n_copies(rows, row_bytes, align):
    n = 8
    while n > 1 and (rows % (n * align) or rows // n * row_bytes < COPY_BYTES):
        n //= 2
    return n


def _allgather_chips(arrs, name):
    n = len(arrs)
    per_layer = [a.size * a.dtype.itemsize // a.shape[0] >= COPY_BYTES for a in arrs]
    n_each = [a.shape[0] if pl_ else 1 for a, pl_ in zip(arrs, per_layer)]

    def body(*refs):
        ins, outs = refs[:n], refs[n:2 * n]
        send, recv, loc = refs[2 * n:]
        x, y, c = _position()
        me = 2 * x + y
        local, remote = [], []
        for a in range(n):
            if per_layer[a]:
                parts = [(ins[a].at[l], outs[a].at[me, l]) for l in range(arrs[a].shape[0])]
            else:
                parts = [(ins[a], outs[a].at[me])]
            local += parts
            for px, py in _chip_peers(x, y):
                remote += [(src, dst, (px, py, c)) for src, dst in parts]
        _run_copies(local, remote, send, recv, loc)

    out_shape = [jax.ShapeDtypeStruct((4,) + a.shape, a.dtype) for a in arrs]
    return _comm_call(body, n, out_shape, 3 * sum(n_each), sum(n_each), name)(*arrs)


def _allgather_all(v, name):
    def body(v_ref, o_ref, send, recv, loc):
        x, y, c = _position()
        me = 4 * x + 2 * y + c
        remote = []
        for k in range(1, 8):
            fx, fy, fc = (k >> 2) & 1, (k >> 1) & 1, k & 1
            remote.append((v_ref, o_ref.at[me], (x ^ fx, y ^ fy, c ^ fc)))
        _run_copies([(v_ref, o_ref.at[me])], remote, send, recv, loc)

    return _comm_call(body, 1, [jax.ShapeDtypeStruct((8,) + v.shape, v.dtype)], 7, 1, name)(v)[0]


def _pair_exchange_sum(g, name):
    _, _, n, r, cc = g.shape

    def body(c_ref, mine_ref, send_ref, o_ref, land, send_sem, recv_sem):
        x, y, c = _position()
        slot = (pl.program_id(0) * n + pl.program_id(1)) % 2
        push = pltpu.make_async_remote_copy(
            src_ref=send_ref.at[0, 0, 0], dst_ref=land.at[slot], send_sem=send_sem.at[slot],
            recv_sem=recv_sem.at[slot], device_id=(x, y, 1 - c), device_id_type=MESH)
        push.start()
        push.wait_recv()
        o_ref[0, 0] = (mine_ref[0, 0, 0] + land[slot]).astype(BF16)
        push.wait_send()

    grid_spec = pltpu.PrefetchScalarGridSpec(
        num_scalar_prefetch=1, grid=(4, n),
        in_specs=[pl.BlockSpec((1, 1, 1, r, cc), lambda j, k, c: (j, c[0], k, 0, 0)),
                  pl.BlockSpec((1, 1, 1, r, cc), lambda j, k, c: (j, 1 - c[0], k, 0, 0))],
        out_specs=pl.BlockSpec((1, 1, r, cc), lambda j, k, c: (j, k, 0, 0)),
        scratch_shapes=[pltpu.VMEM((2, r, cc), F32), pltpu.SemaphoreType.DMA((2,)), pltpu.SemaphoreType.DMA((2,))])
    return pl.pallas_call(
        body, name=name, grid_spec=grid_spec, out_shape=jax.ShapeDtypeStruct((4, n, r, cc), BF16),
        compiler_params=_params(("arbitrary", "arbitrary")))(_scalar(lax.axis_index("c")), g, g)


def _scalar(v):
    return v.astype(jnp.int32).reshape(1)


def _chip_exchange_sum(p, name):
    _, n, r, cc = p.shape

    def body(i0, i1, i2, i3, own_ref, s0_ref, s1_ref, s2_ref, mine_ref, theirs_ref, land, total, land_pair,
             send_sem, recv_sem, pair_send, pair_recv):
        x, y, c = _position()
        slot = pl.program_id(0) % 2
        pushes = [pltpu.make_async_remote_copy(
            src_ref=src.at[0, 0], dst_ref=land.at[slot, j], send_sem=send_sem.at[slot, j], recv_sem=recv_sem.at[slot, j],
            device_id=(px, py, c), device_id_type=MESH)
            for j, (src, (px, py)) in enumerate(zip((s0_ref, s1_ref, s2_ref), _chip_peers(x, y)))]
        for cp in pushes:
            cp.start()
        acc = own_ref[0, 0].astype(F32)
        for j, cp in enumerate(pushes):
            cp.wait_recv()
            acc = acc + land[slot, j].astype(F32)
        mine_ref[0] = acc
        total[slot] = acc
        share = pltpu.make_async_remote_copy(
            src_ref=total.at[slot], dst_ref=land_pair.at[slot], send_sem=pair_send.at[slot], recv_sem=pair_recv.at[slot],
            device_id=(x, y, 1 - c), device_id_type=MESH)
        share.start()
        share.wait_recv()
        theirs_ref[0] = land_pair[slot]
        for cp in pushes:
            cp.wait_send()
        share.wait_send()

    def slot_spec(which):
        return pl.BlockSpec((1, 1, r, cc), lambda k, *idx, which=which: (idx[which][0], k, 0, 0))

    out_spec = pl.BlockSpec((1, r, cc), lambda k, *idx: (k, 0, 0))
    grid_spec = pltpu.PrefetchScalarGridSpec(
        num_scalar_prefetch=4, grid=(n,), in_specs=[slot_spec(0), slot_spec(1), slot_spec(2), slot_spec(3)],
        out_specs=[out_spec, out_spec],
        scratch_shapes=[pltpu.VMEM((2, 3, r, cc), BF16), pltpu.VMEM((2, r, cc), F32), pltpu.VMEM((2, r, cc), F32),
                        pltpu.SemaphoreType.DMA((2, 3)), pltpu.SemaphoreType.DMA((2, 3)),
                        pltpu.SemaphoreType.DMA((2,)), pltpu.SemaphoreType.DMA((2,))])
    o = jax.ShapeDtypeStruct((n, r, cc), F32)
    x, y, _ = _position()
    chips = [_scalar(2 * x + y)] + [_scalar(2 * px + py) for px, py in _chip_peers(x, y)]
    return pl.pallas_call(body, name=name, grid_spec=grid_spec, out_shape=[o, o],
                          compiler_params=_params(("arbitrary",)))(*chips, p, p, p, p)


def _reduce_scatter(gs):
    c = lax.axis_index("c")
    outs = []
    for a, g in enumerate(gs):
        _, rows, cc = g.shape
        k = _n_copies(rows // 2, cc * 4, 16)
        p = _pair_exchange_sum(g.reshape(4, 2, k, rows // (2 * k), cc), f"rs_pair_exchange_sum_{a}")
        mine, theirs = _chip_exchange_sum(p, f"rs_chip_exchange_sum_{a}")
        mine, theirs = mine.reshape(rows // 2, cc), theirs.reshape(rows // 2, cc)
        outs.append(jnp.where(c == 0, jnp.concatenate([mine, theirs]), jnp.concatenate([theirs, mine])))
    return outs


def _gather_shards(w, name):
    rows, cc = w.shape
    half = rows // 2
    n = _n_copies(half, cc * w.dtype.itemsize, 16)
    r = half // n

    def body(core_ref, mine_ref, other_ref, out_ref, land, land_pair, send_sem, recv_sem, pair_send, pair_recv, out_sem):
        x, y, c = _position()
        k = pl.program_id(0)
        slot = k % 2
        me = 2 * x + y
        chips = [2 * px + py for px, py in _chip_peers(x, y)]
        pushes = [pltpu.make_async_remote_copy(
            src_ref=mine_ref.at[0, 0], dst_ref=land.at[slot, j], send_sem=send_sem.at[slot, j],
            recv_sem=recv_sem.at[slot, j], device_id=(px, py, c), device_id_type=MESH)
            for j, (px, py) in enumerate(_chip_peers(x, y))]
        for cp in pushes:
            cp.start()
        writes = [pltpu.make_async_copy(mine_ref.at[0, 0], out_ref.at[me, c, k], out_sem.at[0]),
                  pltpu.make_async_copy(other_ref.at[0, 0], out_ref.at[me, 1 - c, k], out_sem.at[1])]
        for cp in writes:
            cp.start()
        passes = []
        for j, cp in enumerate(pushes):
            cp.wait_recv()
            passes.append(pltpu.make_async_remote_copy(
                src_ref=land.at[slot, j], dst_ref=land_pair.at[slot, j], send_sem=pair_send.at[slot, j],
                recv_sem=pair_recv.at[slot, j], device_id=(x, y, 1 - c), device_id_type=MESH))
            passes[j].start()
            writes.append(pltpu.make_async_copy(land.at[slot, j], out_ref.at[chips[j], c, k], out_sem.at[2 + j]))
            writes[-1].start()
        for j, cp in enumerate(passes):
            cp.wait_recv()
            writes.append(pltpu.make_async_copy(land_pair.at[slot, j], out_ref.at[chips[j], 1 - c, k], out_sem.at[5 + j]))
            writes[-1].start()
        for cp in writes:
            cp.wait()
        for cp in pushes + passes:
            cp.wait_send()

    grid_spec = pltpu.PrefetchScalarGridSpec(
        num_scalar_prefetch=1, grid=(n,),
        in_specs=[pl.BlockSpec((1, 1, r, cc), lambda k, core: (core[0], k, 0, 0)),
                  pl.BlockSpec((1, 1, r, cc), lambda k, core: (1 - core[0], k, 0, 0))],
        out_specs=ANY,
        scratch_shapes=[pltpu.VMEM((2, 3, r, cc), w.dtype), pltpu.VMEM((2, 3, r, cc), w.dtype),
                        pltpu.SemaphoreType.DMA((2, 3)), pltpu.SemaphoreType.DMA((2, 3)),
                        pltpu.SemaphoreType.DMA((2, 3)), pltpu.SemaphoreType.DMA((2, 3)),
                        pltpu.SemaphoreType.DMA((8,))])
    w4 = w.reshape(2, n, r, cc)
    out = pl.pallas_call(body, name=name, grid_spec=grid_spec,
                         out_shape=jax.ShapeDtypeStruct((4, 2, n, r, cc), w.dtype),
                         compiler_params=_params(("arbitrary",)))(_scalar(lax.axis_index("c")), w4, w4)
    return out.reshape(4, rows, cc)


def _pad_rows(a, rows):
    return jnp.pad(a, ((0, rows - a.shape[0]), (0, 0)))


def _shard_cols(a):
    r, c4 = a.shape
    return a.reshape(r, 4, c4 // 4).transpose(1, 0, 2)


def kernel(x, mem, w_in, b_forget, conv_a_w, conv_a_b, ln_a_g, ln_a_b, conv_b_w, w_kv_mem, mem_ln_g, mem_ln_b, p_a, p_b, p_c, p_m, w_out, ln_g, ln_b, loss_target, m_w_in, m_b_forget, m_conv_a_w, m_conv_a_b, m_ln_a_g, m_ln_a_b, m_conv_b_w, m_w_kv_mem, m_mem_ln_g, m_mem_ln_b, m_p_a, m_p_b, m_p_c, m_p_m, m_w_out, m_ln_g, m_ln_b, v_w_in, v_b_forget, v_conv_a_w, v_conv_a_b, v_ln_a_g, v_ln_a_b, v_conv_b_w, v_w_kv_mem, v_mem_ln_g, v_mem_ln_b, v_p_a, v_p_b, v_p_c, v_p_m, v_w_out, v_ln_g, v_ln_b):
    depth = w_in.shape[0]
    x0 = x[0]
    s, d = x0.shape
    aw = conv_a_w.shape[2] * 4
    mw = p_m.shape[1]
    n_head = b_forget.shape[1]
    alpha = (2.0 * depth) ** 0.25
    in_cols = w_in.shape[2] * 4
    assert aw == n_head * HEAD_DIM and mw % LANES == 0 and in_cols == 11 * aw + n_head + 2 * mw + 4 * d
    cf0 = 10 * aw
    n_main = in_cols - n_head
    fblk = n_main // LANES
    n_pad = n_main + LANES
    gblk = (11 * aw + 2 * mw) // d
    mblk = (11 * aw) // (2 * mw)
    qblk, kblk, vblk = 7 * aw // LANES, 8 * aw // LANES, 9 * aw // LANES
    assert (11 * aw + 2 * mw) % d == 0 and (11 * aw) % (2 * mw) == 0

    def gather(w, name):
        flat = _gather_shards(w.astype(BF16).reshape(-1, w.shape[-1]), name)
        return flat.reshape((4,) + w.shape)

    w_in_g, p_a_g, p_b_g, p_c_g = (gather(w, f"gather_{nm}") for w, nm in
                                   ((w_in, "w_in"), (p_a, "p_a"), (p_b, "p_b"), (p_c, "p_c")))
    p_m_g, w_kv_g, w_out_g = (gather(w, f"gather_{nm}") for w, nm in
                              ((p_m, "p_m"), (w_kv_mem, "w_kv"), (w_out, "w_out")))
    conv_a_g, conv_b_g = _allgather_chips([conv_a_w, conv_b_w], "gather_conv_taps")

    def cols(g):
        return jnp.concatenate([g[j] for j in range(4)], axis=-1)

    def rows(g):
        return jnp.concatenate([g[j] for j in range(4)], axis=-2)

    w_full = cols(w_in_g)
    w_pad = jnp.concatenate([w_full[:, :, :cf0], w_full[:, :, cf0 + n_head:], w_full[:, :, cf0:cf0 + n_head],
                             jnp.zeros((depth, d, LANES - n_head), BF16)], axis=-1)
    pcat = jnp.concatenate([cols(p_a_g), cols(p_b_g), cols(p_c_g), cols(p_m_g)], axis=1)
    w_kv = rows(w_kv_g)
    wout = rows(w_out_g)
    conv_a = jnp.pad(cols(conv_a_g), ((0, 0), (0, HALO_A - CONV_A), (0, 0)))
    conv_b = jnp.pad(cols(conv_b_g), ((0, 0), (0, HALO_B - CONV_B), (0, 0)))
    bf_pad = jnp.pad(b_forget, ((0, 0), (0, LANES - n_head)))
    pieces = [(0, 3 * aw), (3 * aw, 7 * aw), (7 * aw, 10 * aw), (10 * aw, 11 * aw),
              (11 * aw, 11 * aw + 2 * mw), (11 * aw + 2 * mw, n_main), (n_main, n_pad)]

    mem_n = _ln_rows(mem[0], mem_ln_g[None], mem_ln_b[None], "mem_ln")

    xs, saved = [x0], []
    x_ops = [_matmul_copies(x0, "matmul_copies")]
    for l in range(depth):
        xl = xs[-1]
        proj = _mm(x_ops[l][0], w_pad[l], name=f"proj_{l}")
        za, conv_out = _branch_a_fwd(proj, conv_a[l], conv_a_b[l][None], ln_a_g[l][None], ln_a_b[l][None], f"a_fwd_{l}")
        zb = _branch_b_fwd(proj, conv_b[l], f"b_fwd_{l}")
        cum, cumt, sgt = _forget_prep(proj, bf_pad[l][None], fblk, f"forget_prep_{l}")
        o_c, lse = _fox_fwd(proj, cum, cumt, qblk, kblk, vblk, f"fox_fwd_{l}")
        zc = _gate_mul(proj, o_c, 10, f"c_gate_{l}")
        mkv = _mm(mem_n, w_kv[l], name=f"mkv_{l}")
        zm = _branch_m_fwd(proj, mkv, mblk, f"m_fwd_{l}")
        zs = [za, zb, zc, zm]
        y, y_b, y_bt = _merge_fwd(zs, proj, gblk, pcat[l], wout[l], xl, ln_g[l][None], ln_b[l][None], alpha,
                                  f"merge_fwd_{l}")
        xs.append(y)
        x_ops.append((y_b, y_bt))
        saved.append((proj, zs, conv_out, cum, cumt, sgt, o_c, lse, mkv))

    dy, loss_part = _loss_head(xs[-1], loss_target[0], "loss_head")

    g_w_in, g_conv_a, g_conv_b, g_w_kv, g_pcat, g_wout = [], [], [], [], [], []
    small = []
    dmem_n = None
    for l in reversed(range(depth)):
        proj, zs, conv_out, cum, cumt, sgt, o_c, lse, mkv = saved[l]
        xl = xs[l]
        (dx, d_g, dza, dzb, dzc, dzm, dpcat, dwout, dlng, dlnb) = _merge_bwd(
            zs, proj, gblk, pcat[l], wout[l], xl, ln_g[l][None], ln_b[l][None], dy, alpha, f"merge_bwd_{l}")
        d_m, dmkv = _branch_m_bwd(proj, mkv, dzm, mblk, f"m_bwd_{l}")
        g_w_kv.append(_mm(mem_n, dmkv, ta=True, name=f"dwkv_{l}"))
        dmem_n = _mm(dmkv, w_kv[l], tb=True, add=dmem_n, name=f"dmem_{l}")
        do, d_cg, delta = _fox_bwd_prep(proj, dzc, o_c, 10, f"fox_bwd_prep_{l}")
        dq, dk, dv, dcumt4, dcumq4 = _fox_bwd(proj, do, lse, delta, cum, cumt, qblk, kblk, vblk, f"fox_bwd_{l}")
        d_f, dbf = _forget_bwd(dcumt4, dcumq4, sgt, f"forget_bwd_{l}")
        d_b, dconv_b = _branch_b_bwd(proj, dzb, conv_b[l], f"b_bwd_{l}")
        d_a, dconv_a, dconv_ab, dlag, dlab = _branch_a_bwd(
            proj, conv_out, dza, conv_a[l], ln_a_g[l][None], ln_a_b[l][None], f"a_bwd_{l}")
        dparts = [d_a, d_b, jnp.concatenate([dq, dk, dv], axis=1), d_cg, d_m, d_g, d_f]
        dx = _input_grad(dparts, [w_pad[l][:, lo:hi] for lo, hi in pieces], dx, f"dx_{l}")
        dw_parts = [_mm(x_ops[l][1], dp, tm=d, name=f"dw_{l}_{k}") for k, dp in enumerate(dparts)]
        dw = jnp.concatenate(dw_parts[:3] + [dw_parts[6][:, :n_head]] + dw_parts[3:6], axis=1)
        g_w_in.append(dw)
        g_conv_a.append(dconv_a)
        g_conv_b.append(dconv_b)
        g_pcat.append(dpcat)
        g_wout.append(dwout)
        small.append([dbf[:, 0], dconv_ab[0], dlag[0], dlab[0], dlng[0], dlnb[0]])
        dy = dx
    grad_x = dy
    dmlg, dmlb = _ln_rows_param_grads(mem[0], dmem_n, "mem_ln_grads")
    for lst in (g_w_in, g_conv_a, g_conv_b, g_w_kv, g_pcat, g_wout, small):
        lst.reverse()

    pa_end, pb_end, pc_end = aw, 2 * aw, 3 * aw
    rs_in = [
        _shard_cols(jnp.concatenate(g_w_in, axis=0)),
        _shard_cols(jnp.concatenate(g_conv_a, axis=0)),
        _shard_cols(jnp.concatenate([_pad_rows(g, 2 * HALO_B) for g in g_conv_b], axis=0)),
        jnp.concatenate([g.reshape(4, g.shape[0] // 4, g.shape[1]) for g in g_w_kv], axis=1),
        _shard_cols(jnp.concatenate([g[:pa_end] for g in g_pcat], axis=0)),
        _shard_cols(jnp.concatenate([g[pa_end:pb_end] for g in g_pcat], axis=0)),
        _shard_cols(jnp.concatenate([g[pb_end:pc_end] for g in g_pcat], axis=0)),
        _shard_cols(jnp.concatenate([g[pc_end:] for g in g_pcat], axis=0)),
        jnp.concatenate([g.reshape(4, g.shape[0] // 4, g.shape[1]) for g in g_wout], axis=1),
    ]
    rs_out = _reduce_scatter(rs_in)
    gw_in = rs_out[0].reshape(depth, d, -1)
    g_ca = rs_out[1].reshape(depth, HALO_A, -1)[:, :CONV_A]
    g_cb = rs_out[2].reshape(depth, 2 * HALO_B, -1)[:, :CONV_B]
    gw_kv = rs_out[3].reshape(depth, -1, 2 * mw)
    gp_a = rs_out[4].reshape(depth, aw, -1)
    gp_b = rs_out[5].reshape(depth, aw, -1)
    gp_c = rs_out[6].reshape(depth, aw, -1)
    gp_m = rs_out[7].reshape(depth, mw, -1)
    gw_out = rs_out[8].reshape(depth, -1, d)

    flat = jnp.concatenate([jnp.concatenate(p) for p in small] + [dmlg[0], dmlb[0], loss_part[0, 0:1]])
    n_small = flat.shape[0]
    n_rows = -(-n_small // (8 * LANES)) * 8
    vec = jnp.pad(flat, (0, n_rows * LANES - n_small)).reshape(n_rows, LANES)
    tot = _sum_leading(_allgather_all(vec, "gather_small"), F32, "sum_small").reshape(-1)
    per_layer = n_head + 3 * aw + 2 * d
    tl = tot[:depth * per_layer].reshape(depth, per_layer)
    offs = np.cumsum([0, n_head, aw, aw, aw, d, d])
    g_bf, g_cab, g_lag, g_lab, g_lg, g_lb = [tl[:, offs[k]:offs[k + 1]] for k in range(6)]
    base = depth * per_layer
    g_mlg, g_mlb = tot[base:base + d], tot[base + d:base + 2 * d]
    loss = tot[base + 2 * d]

    grads = [gw_in, g_bf, g_ca, g_cab, g_lag, g_lab, g_cb, gw_kv, g_mlg, g_mlb, gp_a, gp_b, gp_c, gp_m, gw_out, g_lg, g_lb]
    ws = [w_in, b_forget, conv_a_w, conv_a_b, ln_a_g, ln_a_b, conv_b_w, w_kv_mem, mem_ln_g, mem_ln_b, p_a, p_b, p_c, p_m, w_out, ln_g, ln_b]
    ms = [m_w_in, m_b_forget, m_conv_a_w, m_conv_a_b, m_ln_a_g, m_ln_a_b, m_conv_b_w, m_w_kv_mem, m_mem_ln_g, m_mem_ln_b, m_p_a, m_p_b, m_p_c, m_p_m, m_w_out, m_ln_g, m_ln_b]
    vs = [v_w_in, v_b_forget, v_conv_a_w, v_conv_a_b, v_ln_a_g, v_ln_a_b, v_conv_b_w, v_w_kv_mem, v_mem_ln_g, v_mem_ln_b, v_p_a, v_p_b, v_p_c, v_p_m, v_w_out, v_ln_g, v_ln_b]
    deltas, new_ms, new_vs = [], [], []
    for k, (wk, gk, mk, vk) in enumerate(zip(ws, grads, ms, vs)):
        shape = wk.shape
        two_d = (1, shape[0]) if wk.ndim == 1 else (int(np.prod(shape[:-1])), shape[-1])
        dk_, nm_, nv_ = _adamw(wk.reshape(two_d), gk.reshape(two_d), mk.reshape(two_d), vk.reshape(two_d), f"adamw_{k}")
        deltas.append(dk_.reshape(shape))
        new_ms.append(nm_.reshape(shape))
        new_vs.append(nv_.reshape(shape))
        grads[k] = gk.reshape(shape)
    return (loss, grad_x[None], *grads, *deltas, *new_ms, *new_vs)


def _gate_mul(proj, o, gblk, name):
    s, w = o.shape
    t = T_ELEM

    def body(g_ref, o_ref, z_ref):
        z_ref[...] = (o_ref[...] * _silu(g_ref[...])).astype(BF16)

    row = pl.BlockSpec((t, w), lambda i: (i, 0))
    return pl.pallas_call(body, name=name, grid=(s // t,), in_specs=[pl.BlockSpec((t, w), lambda i: (i, gblk)), row],
                          out_specs=row, out_shape=jax.ShapeDtypeStruct((s, w), BF16),
                          compiler_params=_params(("parallel",)))(proj, o)
```

```python
import functools
import math

import numpy as np
import jax
import jax.numpy as jnp
from jax import lax
from jax.experimental import pallas as pl
from jax.experimental.pallas import tpu as pltpu

F32 = jnp.float32
BF16 = jnp.bfloat16
MESH = pl.DeviceIdType.MESH
ANY = pl.BlockSpec(memory_space=pl.ANY)

LN_EPS = 1e-5
NEG_BIG = -1e30
HEAD_DIM = 64
LANES = 128
CONV_A = 31
CONV_B = 3
HALO_A = 32
HALO_B = 8
CHUNK = 32
VMEM_LIMIT = 60 * 1024 * 1024

ADAM_LR, ADAM_B1, ADAM_B2, ADAM_EPS, ADAM_WD, ADAM_STEP = 0.001, 0.9, 0.999, 1e-08, 0.01, 10

T_MM = 512
T_A = 128
T_B = 256
T_ATT_Q = 512
T_ATT_K = 1024
T_CUM = 512
T_M = 512
T_MERGE = 256
T_DX = 256
T_ELEM = 512


def _pick(n, prefs):
    for p in prefs:
        if n % p == 0:
            return p
    return n


def _params(sem=None):
    return pltpu.CompilerParams(dimension_semantics=sem, vmem_limit_bytes=VMEM_LIMIT)


def _sigmoid(x):
    return jax.nn.sigmoid(x)


def _silu(x):
    return x * _sigmoid(x)


def _dsilu(x):
    s = _sigmoid(x)
    return s * (1.0 + x * (1.0 - s))


def _dot(a, b, dims, precision=None):
    return lax.dot_general(a, b, (dims, ((), ())), preferred_element_type=F32, precision=precision)


NN = ((1,), (0,))
NT = ((1,), (1,))
TN = ((0,), (0,))


def _iota(shape, dim):
    return lax.broadcasted_iota(jnp.int32, shape, dim)


def _mm(a, b, *, ta=False, tb=False, add=None, out_dtype=F32, tm=None, name):
    m = a.shape[1] if ta else a.shape[0]
    k = a.shape[0] if ta else a.shape[1]
    n = b.shape[0] if tb else b.shape[1]
    if tm is None:
        tm = _pick(m, (1024, 512, 256)) if ta else _pick(m, (T_MM, 256))
    tn = _pick(n, (1152, 1024, 768, 512, 384, 256, 128))
    tk = _pick(k, (1024, 512, 256))
    nk = k // tk
    dims = ((0,) if ta else (1,), (1,) if tb else (0,))

    def body(*refs):
        if add is None:
            a_ref, b_ref, o_ref, acc_ref = refs
        else:
            a_ref, b_ref, add_ref, o_ref, acc_ref = refs
        kk = pl.program_id(2)
        p = _dot(a_ref[...].astype(BF16), b_ref[...].astype(BF16), dims)

        @pl.when(kk == 0)
        def _():
            acc_ref[...] = p

        @pl.when(kk > 0)
        def _():
            acc_ref[...] += p

        @pl.when(kk == nk - 1)
        def _():
            r = acc_ref[...]
            if add is not None:
                r = r + add_ref[...]
            o_ref[...] = r.astype(out_dtype)

    a_spec = (pl.BlockSpec((tk, tm), lambda j, i, kk: (kk, i)) if ta
              else pl.BlockSpec((tm, tk), lambda j, i, kk: (i, kk)))
    b_spec = (pl.BlockSpec((tn, tk), lambda j, i, kk: (j, kk)) if tb
              else pl.BlockSpec((tk, tn), lambda j, i, kk: (kk, j)))
    o_spec = pl.BlockSpec((tm, tn), lambda j, i, kk: (i, j))
    in_specs = [a_spec, b_spec] + ([o_spec] if add is not None else [])
    args = (a, b) + ((add,) if add is not None else ())
    return pl.pallas_call(
        body, name=name, grid=(n // tn, m // tm, nk), in_specs=in_specs, out_specs=o_spec,
        out_shape=jax.ShapeDtypeStruct((m, n), out_dtype),
        scratch_shapes=[pltpu.VMEM((tm, tn), F32)],
        compiler_params=_params(("parallel", "parallel", "arbitrary")),
    )(*args)


def _input_grad(dparts, wparts, add, name):
    s, d = add.shape
    t = T_DX
    n_p = len(dparts)

    def body(*refs):
        d_refs, w_hbm = refs[:n_p], refs[n_p:2 * n_p]
        add_ref, o_ref = refs[2 * n_p], refs[2 * n_p + 1]
        w_v = refs[2 * n_p + 2:]

        @pl.when(pl.program_id(0) == 0)
        def _():
            for p in range(n_p):
                pltpu.sync_copy(w_hbm[p], w_v[p])

        acc = add_ref[...]
        for p in range(n_p):
            acc = acc + _dot(d_refs[p][...], w_v[p][...], NT)
        o_ref[...] = acc

    row = pl.BlockSpec((t, d), lambda i: (i, 0))
    in_specs = ([pl.BlockSpec((t, dp.shape[1]), lambda i: (i, 0)) for dp in dparts] + [ANY] * n_p + [row])
    return pl.pallas_call(
        body, name=name, grid=(s // t,), in_specs=in_specs, out_specs=row,
        out_shape=jax.ShapeDtypeStruct((s, d), F32),
        scratch_shapes=[pltpu.VMEM(w.shape, BF16) for w in wparts],
        compiler_params=_params(("arbitrary",)),
    )(*dparts, *wparts, add)


def _ln_rows(x, g, b, name):
    r, d = x.shape
    t = _pick(r, (256,))

    def body(x_ref, g_ref, b_ref, o_ref):
        xv = x_ref[...]
        mu = jnp.mean(xv, axis=-1, keepdims=True)
        dv = xv - mu
        var = jnp.mean(dv * dv, axis=-1, keepdims=True)
        o_ref[...] = dv * lax.rsqrt(var + LN_EPS) * g_ref[...] + b_ref[...]

    row = pl.BlockSpec((t, d), lambda i: (i, 0))
    vec = pl.BlockSpec((1, d), lambda i: (0, 0))
    return pl.pallas_call(body, name=name, grid=(r // t,), in_specs=[row, vec, vec], out_specs=row,
                          out_shape=jax.ShapeDtypeStruct((r, d), F32), compiler_params=_params(("parallel",)))(x, g, b)


def _ln_rows_param_grads(x, dy, name):
    r, d = x.shape

    def body(x_ref, dy_ref, dg_ref, db_ref):
        xv = x_ref[...]
        mu = jnp.mean(xv, axis=-1, keepdims=True)
        dv = xv - mu
        var = jnp.mean(dv * dv, axis=-1, keepdims=True)
        xh = dv * lax.rsqrt(var + LN_EPS)
        dg_ref[...] = jnp.sum(dy_ref[...] * xh, axis=0, keepdims=True)
        db_ref[...] = jnp.sum(dy_ref[...], axis=0, keepdims=True)

    full = pl.BlockSpec((r, d), lambda i: (0, 0))
    vec = pl.BlockSpec((1, d), lambda i: (0, 0))
    o = jax.ShapeDtypeStruct((1, d), F32)
    return pl.pallas_call(body, name=name, grid=(1,), in_specs=[full, full], out_specs=[vec, vec],
                          out_shape=[o, o], compiler_params=_params(("arbitrary",)))(x, dy)


def _conv_a_chunk(glu_ref, cw_ref, r0):
    acc = cw_ref[0:1, :] * glu_ref[pl.ds(r0 + 2, CHUNK), :]
    for k in range(1, CONV_A):
        acc = acc + cw_ref[k:k + 1, :] * glu_ref[pl.ds(r0 + 2 + k, CHUNK), :]
    return acc


def _ln_stats(c):
    mu = jnp.mean(c, axis=-1, keepdims=True)
    d = c - mu
    var = jnp.mean(d * d, axis=-1, keepdims=True)
    rstd = lax.rsqrt(var + LN_EPS)
    return d * rstd, rstd


def _branch_a_fwd(proj, cw, cb, lg, lb, name):
    s = proj.shape[0]
    t = T_A
    w = cw.shape[1]
    r = t // HALO_A

    def body(u_ref, v_ref, gt_ref, hu_ref, hv_ref, cw_ref, cb_ref, lg_ref, lb_ref, z_ref, conv_ref, glu):
        i = pl.program_id(0)
        hglu = hu_ref[...] * _sigmoid(hv_ref[...])
        glu[0:HALO_A, :] = jnp.where(i > 0, hglu, 0.0)
        glu[HALO_A:HALO_A + t, :] = u_ref[...] * _sigmoid(v_ref[...])
        for c in range(t // CHUNK):
            r0 = c * CHUNK
            conv = _conv_a_chunk(glu, cw_ref, r0) + cb_ref[...]
            conv_ref[r0:r0 + CHUNK, :] = conv
            xh, _ = _ln_stats(conv)
            a3 = _silu(xh * lg_ref[...] + lb_ref[...])
            z_ref[r0:r0 + CHUNK, :] = (a3 * _silu(gt_ref[r0:r0 + CHUNK, :])).astype(BF16)

    def cur(col):
        return pl.BlockSpec((t, w), lambda i, col=col: (i, col))

    def prev(col):
        return pl.BlockSpec((HALO_A, w), lambda i, col=col: (jnp.maximum(i * r - 1, 0), col))

    vec = pl.BlockSpec((1, w), lambda i: (0, 0))
    return pl.pallas_call(
        body, name=name, grid=(s // t,),
        in_specs=[cur(0), cur(1), cur(2), prev(0), prev(1), pl.BlockSpec((HALO_A, w), lambda i: (0, 0)), vec, vec, vec],
        out_specs=[pl.BlockSpec((t, w), lambda i: (i, 0)), pl.BlockSpec((t, w), lambda i: (i, 0))],
        out_shape=[jax.ShapeDtypeStruct((s, w), BF16), jax.ShapeDtypeStruct((s, w), F32)],
        scratch_shapes=[pltpu.VMEM((HALO_A + t, w), F32)],
        compiler_params=_params(("parallel",)),
    )(proj, proj, proj, proj, proj, cw, cb, lg, lb)


def _branch_a_bwd(proj, conv, dz, cw, lg, lb, name):
    s = proj.shape[0]
    t = T_A
    w = cw.shape[1]
    r = t // HALO_A
    n = s // t
    nblk = s // HALO_A
    ext = t + HALO_A

    def body(u_ref, v_ref, gt_ref, dz_ref, conv_ref, pu_ref, pv_ref, ngt_ref, ndz_ref, nconv_ref,
             cw_ref, lg_ref, lb_ref, da_ref, dw_ref, dcb_ref, dlg_ref, dlb_ref, glu, dc, dw8):
        i = pl.program_id(0)

        @pl.when(i == 0)
        def _():
            dw8[...] = jnp.zeros_like(dw8)
            dcb_ref[...] = jnp.zeros_like(dcb_ref)
            dlg_ref[...] = jnp.zeros_like(dlg_ref)
            dlb_ref[...] = jnp.zeros_like(dlb_ref)

        glu[0:HALO_A, :] = jnp.where(i > 0, pu_ref[...] * _sigmoid(pv_ref[...]), 0.0)
        glu[HALO_A:HALO_A + t, :] = u_ref[...] * _sigmoid(v_ref[...])
        has_next = i < n - 1
        dcb = jnp.zeros((1, w), F32)
        dlg = jnp.zeros((1, w), F32)
        dlb = jnp.zeros((1, w), F32)
        for c in range(ext // CHUNK):
            r0 = c * CHUNK
            own = r0 < t
            xh, rstd = _ln_stats(conv_ref[r0:r0 + CHUNK, :] if own else nconv_ref[...])
            a2 = xh * lg_ref[...] + lb_ref[...]
            if own:
                gt = gt_ref[r0:r0 + CHUNK, :]
                dzc = dz_ref[r0:r0 + CHUNK, :]
            else:
                gt = ngt_ref[...]
                dzc = ndz_ref[...]
            da2 = dzc * _silu(gt) * _dsilu(a2)
            dxh = da2 * lg_ref[...]
            dconv = rstd * (dxh - jnp.mean(dxh, axis=-1, keepdims=True)
                            - xh * jnp.mean(dxh * xh, axis=-1, keepdims=True))
            if own:
                dc[r0:r0 + CHUNK, :] = dconv
                da_ref[r0:r0 + CHUNK, 2 * w:3 * w] = (dzc * _silu(a2) * _dsilu(gt)).astype(BF16)
                dcb = dcb + jnp.sum(dconv, axis=0, keepdims=True)
                dlg = dlg + jnp.sum(da2 * xh, axis=0, keepdims=True)
                dlb = dlb + jnp.sum(da2, axis=0, keepdims=True)
            else:
                dc[r0:r0 + CHUNK, :] = jnp.where(has_next, dconv, 0.0)
        dcb_ref[...] += dcb
        dlg_ref[...] += dlg
        dlb_ref[...] += dlb
        for c in range(t // CHUNK):
            r0 = c * CHUNK
            dcc = dc[r0:r0 + CHUNK, :]
            dglu = cw_ref[0:1, :] * dc[pl.ds(r0 + CONV_A - 1, CHUNK), :]
            for k in range(1, CONV_A):
                dglu = dglu + cw_ref[k:k + 1, :] * dc[pl.ds(r0 + CONV_A - 1 - k, CHUNK), :]
            for k in range(CONV_A):
                prod = dcc * glu[pl.ds(r0 + 2 + k, CHUNK), :]
                dw8[k] += jnp.sum(prod.reshape(CHUNK // 8, 8, w), axis=0)
            sv = _sigmoid(v_ref[r0:r0 + CHUNK, :])
            da_ref[r0:r0 + CHUNK, 0:w] = (dglu * sv).astype(BF16)
            da_ref[r0:r0 + CHUNK, w:2 * w] = (dglu * u_ref[r0:r0 + CHUNK, :] * sv * (1.0 - sv)).astype(BF16)

        @pl.when(i == n - 1)
        def _():
            dw_ref[...] = jnp.sum(dw8[...], axis=1)

    def cur(col):
        return pl.BlockSpec((t, w), lambda i, col=col: (i, col))

    def prev(col):
        return pl.BlockSpec((HALO_A, w), lambda i, col=col: (jnp.maximum(i * r - 1, 0), col))

    def nxt(col):
        return pl.BlockSpec((HALO_A, w), lambda i, col=col: (jnp.minimum((i + 1) * r, nblk - 1), col))

    own_rows = pl.BlockSpec((t, w), lambda i: (i, 0))
    vec = pl.BlockSpec((1, w), lambda i: (0, 0))
    vo = jax.ShapeDtypeStruct((1, w), F32)
    return pl.pallas_call(
        body, name=name, grid=(n,),
        in_specs=[cur(0), cur(1), cur(2), own_rows, own_rows, prev(0), prev(1), nxt(2), nxt(0), nxt(0),
                  pl.BlockSpec((HALO_A, w), lambda i: (0, 0)), vec, vec],
        out_specs=[pl.BlockSpec((t, 3 * w), lambda i: (i, 0)), pl.BlockSpec((HALO_A, w), lambda i: (0, 0)), vec, vec, vec],
        out_shape=[jax.ShapeDtypeStruct((s, 3 * w), BF16), jax.ShapeDtypeStruct((HALO_A, w), F32), vo, vo, vo],
        scratch_shapes=[pltpu.VMEM((HALO_A + t, w), F32), pltpu.VMEM((ext, w), F32),
                        pltpu.VMEM((HALO_A, 8, w), F32)],
        compiler_params=_params(("arbitrary",)),
    )(proj, proj, proj, dz, conv, proj, proj, proj, dz, conv, cw, lg, lb)


def _conv_b(u_ext, cw_ref, t):
    acc = cw_ref[0:1, :] * u_ext[pl.ds(HALO_B - 2, t), :]
    for k in range(1, CONV_B):
        acc = acc + cw_ref[k:k + 1, :] * u_ext[pl.ds(HALO_B - 2 + k, t), :]
    return acc


def _branch_b_fwd(proj, cw, name):
    s = proj.shape[0]
    t = T_B
    w = cw.shape[1]
    r = t // HALO_B

    def body(h_ref, b_ref, c_ref, gt_ref, ph_ref, pc_ref, cw_ref, z_ref, u_ext):
        i = pl.program_id(0)
        u_ext[0:HALO_B, :] = jnp.where(i > 0, pc_ref[...] * ph_ref[...], 0.0)
        u_ext[HALO_B:HALO_B + t, :] = c_ref[...] * h_ref[...]
        cv = _conv_b(u_ext, cw_ref, t)
        z_ref[...] = (b_ref[...] * cv * _silu(gt_ref[...])).astype(BF16)

    def cur(col):
        return pl.BlockSpec((t, w), lambda i, col=col: (i, col))

    def prev(col):
        return pl.BlockSpec((HALO_B, w), lambda i, col=col: (jnp.maximum(i * r - 1, 0), col))

    return pl.pallas_call(
        body, name=name, grid=(s // t,),
        in_specs=[cur(3), cur(4), cur(5), cur(6), prev(3), prev(5), pl.BlockSpec((HALO_B, w), lambda i: (0, 0))],
        out_specs=pl.BlockSpec((t, w), lambda i: (i, 0)),
        out_shape=jax.ShapeDtypeStruct((s, w), BF16),
        scratch_shapes=[pltpu.VMEM((HALO_B + t, w), F32)],
        compiler_params=_params(("parallel",)),
    )(proj, proj, proj, proj, proj, proj, cw)


def _branch_b_bwd(proj, dz, cw, name):
    s = proj.shape[0]
    t = T_B
    w = cw.shape[1]
    r = t // HALO_B
    n = s // t
    nblk = s // HALO_B

    def body(h_ref, b_ref, c_ref, gt_ref, dz_ref, ph_ref, pc_ref, nb_ref, ngt_ref, ndz_ref, cw_ref,
             db_ref, dw_ref, u_ext, dcv_ext, dw8):
        i = pl.program_id(0)

        @pl.when(i == 0)
        def _():
            dw8[...] = jnp.zeros_like(dw8)

        u_ext[0:HALO_B, :] = jnp.where(i > 0, pc_ref[...] * ph_ref[...], 0.0)
        u_ext[HALO_B:HALO_B + t, :] = c_ref[...] * h_ref[...]
        cv = _conv_b(u_ext, cw_ref, t)
        gt = gt_ref[...]
        dhb = dz_ref[...] * _silu(gt)
        db_ref[:, 3 * w:4 * w] = (dz_ref[...] * b_ref[...] * cv * _dsilu(gt)).astype(BF16)
        db_ref[:, w:2 * w] = (dhb * cv).astype(BF16)
        dcv = dhb * b_ref[...]
        dcv_ext[0:t, :] = dcv
        ndcv = ndz_ref[...] * _silu(ngt_ref[...]) * nb_ref[...]
        dcv_ext[t:t + HALO_B, :] = jnp.where(i < n - 1, ndcv, 0.0)
        du = cw_ref[0:1, :] * dcv_ext[pl.ds(2, t), :]
        for k in range(1, CONV_B):
            du = du + cw_ref[k:k + 1, :] * dcv_ext[pl.ds(2 - k, t), :]
        db_ref[:, 2 * w:3 * w] = (du * h_ref[...]).astype(BF16)
        db_ref[:, 0:w] = (du * c_ref[...]).astype(BF16)
        for k in range(CONV_B):
            prod = dcv * u_ext[pl.ds(HALO_B - 2 + k, t), :]
            dw8[k] += jnp.sum(prod.reshape(t // 8, 8, w), axis=0)

        @pl.when(i == n - 1)
        def _():
            dw_ref[...] = jnp.sum(dw8[...], axis=1)

    def cur(col):
        return pl.BlockSpec((t, w), lambda i, col=col: (i, col))

    def prev(col):
        return pl.BlockSpec((HALO_B, w), lambda i, col=col: (jnp.maximum(i * r - 1, 0), col))

    def nxt(col):
        return pl.BlockSpec((HALO_B, w), lambda i, col=col: (jnp.minimum((i + 1) * r, nblk - 1), col))

    return pl.pallas_call(
        body, name=name, grid=(n,),
        in_specs=[cur(3), cur(4), cur(5), cur(6), pl.BlockSpec((t, w), lambda i: (i, 0)),
                  prev(3), prev(5), nxt(4), nxt(6),
                  pl.BlockSpec((HALO_B, w), lambda i: (jnp.minimum((i + 1) * r, nblk - 1), 0)),
                  pl.BlockSpec((HALO_B, w), lambda i: (0, 0))],
        out_specs=[pl.BlockSpec((t, 4 * w), lambda i: (i, 0)), pl.BlockSpec((HALO_B, w), lambda i: (0, 0))],
        out_shape=[jax.ShapeDtypeStruct((s, 4 * w), BF16), jax.ShapeDtypeStruct((HALO_B, w), F32)],
        scratch_shapes=[pltpu.VMEM((HALO_B + t, w), F32), pltpu.VMEM((t + HALO_B, w), F32),
                        pltpu.VMEM((HALO_B, 8, w), F32)],
        compiler_params=_params(("arbitrary",)),
    )(proj, proj, proj, proj, dz, proj, proj, proj, proj, dz, cw)


def _forget_prep(proj, bf, fblk, name):
    s = proj.shape[0]
    t = T_CUM

    def body(f_ref, bf_ref, cum_ref, cumt_ref, sgt_ref, carry):
        i = pl.program_id(0)

        @pl.when(i == 0)
        def _():
            carry[...] = jnp.zeros_like(carry)

        z = f_ref[...] + bf_ref[...]
        logf = jnp.minimum(z, 0.0) - jnp.log1p(jnp.exp(-jnp.abs(z)))
        tri = (_iota((t, t), 0) >= _iota((t, t), 1)).astype(F32)
        cum = _dot(tri, logf, NN, precision=lax.Precision.HIGHEST) + carry[0:1, :]
        carry[0:1, :] = cum[t - 1:t, :]
        cum_ref[...] = cum
        cumt_ref[...] = cum.T[0:8, :]
        sgt_ref[...] = _sigmoid(-z).T[0:8, :]

    return pl.pallas_call(
        body, name=name, grid=(s // t,),
        in_specs=[pl.BlockSpec((t, LANES), lambda i: (i, fblk)), pl.BlockSpec((1, LANES), lambda i: (0, 0))],
        out_specs=[pl.BlockSpec((t, LANES), lambda i: (i, 0)), pl.BlockSpec((8, t), lambda i: (0, i)),
                   pl.BlockSpec((8, t), lambda i: (0, i))],
        out_shape=[jax.ShapeDtypeStruct((s, LANES), F32), jax.ShapeDtypeStruct((8, s), F32),
                   jax.ShapeDtypeStruct((8, s), F32)],
        scratch_shapes=[pltpu.VMEM((8, LANES), F32)],
        compiler_params=_params(("arbitrary",)),
    )(proj, bf)


def _lane_pick(x, lane):
    return jnp.sum(jnp.where(_iota(x.shape, 1) == lane, x, 0.0), axis=1, keepdims=True)


def _sublane_pick(x, row):
    return jnp.sum(jnp.where(_iota(x.shape, 0) == row, x, 0.0), axis=0, keepdims=True)


def _head_mask(hh):
    lane = _iota((1, LANES), 1)
    return (lane >= HEAD_DIM * hh) & (lane < HEAD_DIM * (hh + 1))


def _causal_pairs(nq, ratio, kv_major):
    if kv_major:
        pairs = [(q, k) for k in range(nq // ratio) for q in range(k * ratio, nq)]
    else:
        pairs = [(q, k) for q in range(nq) for k in range(q // ratio + 1)]
    qs = np.asarray([p[0] for p in pairs], np.int32)
    ks = np.asarray([p[1] for p in pairs], np.int32)
    return qs, ks, (ks == qs // ratio).astype(np.int32)


def _fox_scores(qm, kb, cum_ref, cumt_ref, h, qi, ki, diag, tq, tk):
    cq0 = _lane_pick(cum_ref[0:1, :], h)
    sc = _dot(qm, kb, NT) + (cq0 - _sublane_pick(cumt_ref[...], h))
    if not diag:
        return sc
    causal = (_iota((tq, tk), 0) + (qi * tq - ki * tk)) >= _iota((tq, tk), 1)
    return jnp.where(causal, sc, NEG_BIG)


def _fox_fwd(proj, cum, cumt, qblk, kblk, vblk, name):
    s = proj.shape[0]
    tq, tk = T_ATT_Q, T_ATT_K
    n_pair = 4
    qi_np, ki_np, diag_np = _causal_pairs(s // tq, tk // tq, kv_major=False)
    scale = HEAD_DIM ** -0.5

    def body(qi_ref, ki_ref, diag_ref, q_ref, k_ref, v_ref, cum_ref, cumt_ref, o_ref, lse_ref, m_s, acc_s):
        hp = pl.program_id(0)
        step = pl.program_id(1)
        qi, ki, diag = qi_ref[step], ki_ref[step], diag_ref[step]

        @pl.when(ki == 0)
        def _():
            m_s[...] = jnp.full_like(m_s, NEG_BIG)
            acc_s[...] = jnp.zeros_like(acc_s)

        def update(on_diagonal):
            q = q_ref[...] * scale
            kb = k_ref[...].astype(BF16)
            v = v_ref[...]
            m_old = [m_s[0], m_s[1]]
            acc_old = [acc_s[0], acc_s[1]]
            scores, values = [], []
            for hh in range(2):
                hm = _head_mask(hh)
                qm = jnp.where(hm, q, 0.0).astype(BF16)
                values.append(jnp.where(hm, v, 1.0).astype(BF16))
                scores.append(_fox_scores(qm, kb, cum_ref, cumt_ref, 2 * hp + hh, qi, ki, on_diagonal, tq, tk))
            m_new = [jnp.maximum(m_old[hh], jnp.max(scores[hh], axis=1, keepdims=True)) for hh in range(2)]
            probs = [jnp.exp(scores[hh] - m_new[hh]).astype(BF16) for hh in range(2)]
            acc_new = [jnp.exp(m_old[hh] - m_new[hh]) * acc_old[hh] + _dot(probs[hh], values[hh], NN)
                       for hh in range(2)]
            for hh in range(2):
                acc_s[hh] = acc_new[hh]
                m_s[hh] = m_new[hh]

        @pl.when(diag == 0)
        def _():
            update(False)

        @pl.when(diag == 1)
        def _():
            update(True)
            lane = _iota((tq, LANES), 1)
            a0, a1 = acc_s[0], acc_s[1]
            o_ref[...] = jnp.where(lane < HEAD_DIM, a0 / pltpu.roll(a0, HEAD_DIM, axis=1),
                                   a1 / pltpu.roll(a1, HEAD_DIM, axis=1))
            lse0 = m_s[0] + jnp.log(a0[:, HEAD_DIM:HEAD_DIM + 1])
            lse1 = m_s[1] + jnp.log(a1[:, 0:1])
            lse_ref[0] = jnp.where(lane == 0, lse0, jnp.where(lane == 1, lse1, 0.0))

    grid_spec = pltpu.PrefetchScalarGridSpec(
        num_scalar_prefetch=3, grid=(n_pair, len(qi_np)),
        in_specs=[pl.BlockSpec((tq, LANES), lambda hp, st, qi, ki, dg: (qi[st], qblk + hp)),
                  pl.BlockSpec((tk, LANES), lambda hp, st, qi, ki, dg: (ki[st], kblk + hp)),
                  pl.BlockSpec((tk, LANES), lambda hp, st, qi, ki, dg: (ki[st], vblk + hp)),
                  pl.BlockSpec((tq, LANES), lambda hp, st, qi, ki, dg: (qi[st], 0)),
                  pl.BlockSpec((8, tk), lambda hp, st, qi, ki, dg: (0, ki[st]))],
        out_specs=[pl.BlockSpec((tq, LANES), lambda hp, st, qi, ki, dg: (qi[st], hp)),
                   pl.BlockSpec((1, tq, LANES), lambda hp, st, qi, ki, dg: (hp, qi[st], 0))],
        scratch_shapes=[pltpu.VMEM((2, tq, 1), F32), pltpu.VMEM((2, tq, LANES), F32)])
    return pl.pallas_call(
        body, name=name, grid_spec=grid_spec,
        out_shape=[jax.ShapeDtypeStruct((s, n_pair * LANES), F32), jax.ShapeDtypeStruct((n_pair, s, LANES), F32)],
        compiler_params=_params(("parallel", "arbitrary")),
    )(jnp.asarray(qi_np), jnp.asarray(ki_np), jnp.asarray(diag_np), proj, proj, proj, cum, cumt)


def _fox_bwd_prep(proj, dz, o, gblk, name):
    s, w = o.shape
    t = T_ELEM
    n_head = w // HEAD_DIM

    def body(gt_ref, dz_ref, o_ref, do_ref, dg_ref, dl_ref):
        gt = gt_ref[...]
        do = dz_ref[...] * _silu(gt)
        do_ref[...] = do
        dg_ref[...] = (dz_ref[...] * o_ref[...] * _dsilu(gt)).astype(BF16)
        sel = (_iota((w, LANES), 0) // HEAD_DIM == _iota((w, LANES), 1)).astype(F32)
        dl_ref[...] = _dot(do * o_ref[...], sel, NN, precision=lax.Precision.HIGHEST)

    assert n_head <= LANES
    row = pl.BlockSpec((t, w), lambda i: (i, 0))
    return pl.pallas_call(
        body, name=name, grid=(s // t,),
        in_specs=[pl.BlockSpec((t, w), lambda i: (i, gblk)), row, row],
        out_specs=[row, row, pl.BlockSpec((t, LANES), lambda i: (i, 0))],
        out_shape=[jax.ShapeDtypeStruct((s, w), F32), jax.ShapeDtypeStruct((s, w), BF16),
                   jax.ShapeDtypeStruct((s, LANES), F32)],
        compiler_params=_params(("parallel",)),
    )(proj, dz, o)


def _fox_bwd(proj, do, lse, delta, cum, cumt, qblk, kblk, vblk, name):
    s = proj.shape[0]
    tq, tk = T_ATT_Q, T_ATT_K
    nq = s // tq
    n_pair = 4
    qi_np, ki_np, diag_np = _causal_pairs(nq, tk // tq, kv_major=True)
    n_step = len(qi_np)
    scale = HEAD_DIM ** -0.5

    def body(qi_ref, ki_ref, diag_ref, q_ref, k_ref, v_ref, do_ref, lse_ref, dl_ref, cum_ref, cumt_ref,
             dq_ref, dk_ref, dv_ref, dct_ref, dcq_ref, dq_s, dk_s, dv_s, dc_s, dcq_s):
        hp = pl.program_id(0)
        step = pl.program_id(1)
        qi, ki, diag = qi_ref[step], ki_ref[step], diag_ref[step]

        @pl.when(step == 0)
        def _():
            dq_s[...] = jnp.zeros_like(dq_s)
            dcq_s[...] = jnp.zeros_like(dcq_s)

        @pl.when(qi == ki * (tk // tq))
        def _():
            dk_s[...] = jnp.zeros_like(dk_s)
            dv_s[...] = jnp.zeros_like(dv_s)
            dc_s[...] = jnp.zeros_like(dc_s)

        def update(on_diagonal):
            q = q_ref[...] * scale
            do = do_ref[...]
            kb = k_ref[...].astype(BF16)
            vb = v_ref[...].astype(BF16)
            sub = _iota((8, tk), 0)
            dq_new = jnp.zeros((tq, LANES), F32)
            dcq_new = jnp.zeros((tq, LANES), F32)
            lane = _iota((tq, LANES), 1)
            for hh in range(2):
                h = 2 * hp + hh
                hm = _head_mask(hh)
                qm = jnp.where(hm, q, 0.0).astype(BF16)
                dom = jnp.where(hm, do, 0.0).astype(BF16)
                sc = _fox_scores(qm, kb, cum_ref, cumt_ref, h, qi, ki, on_diagonal, tq, tk)
                p = jnp.exp(sc - _lane_pick(lse_ref[0], hh))
                dv_s[...] += _dot(p.astype(BF16), dom, TN)
                dp = _dot(dom, vb, NT)
                ds = p * (dp - _lane_pick(dl_ref[...], h))
                dc_s[...] += jnp.where(sub == h, -jnp.sum(ds, axis=0, keepdims=True), 0.0)
                dcq_new = dcq_new + jnp.where(lane == h, jnp.sum(ds, axis=1, keepdims=True), 0.0)
                dsb = ds.astype(BF16)
                dk_s[...] += _dot(dsb, qm, TN)
                dq_new = dq_new + jnp.where(hm, _dot(dsb, kb, NN), 0.0)
            row0 = pl.multiple_of(qi * tq, tq)
            dq_s[pl.ds(row0, tq), :] += dq_new * scale
            dcq_s[pl.ds(row0, tq), :] += dcq_new

        @pl.when(diag == 0)
        def _():
            update(False)

        @pl.when(diag == 1)
        def _():
            update(True)

        @pl.when(qi == nq - 1)
        def _():
            dk_ref[...] = dk_s[...].astype(BF16)
            dv_ref[...] = dv_s[...].astype(BF16)
            dct_ref[0] = dc_s[...]

        @pl.when(step == n_step - 1)
        def _():
            dq_ref[...] = dq_s[...].astype(BF16)
            dcq_ref[0] = dcq_s[...]

    grid_spec = pltpu.PrefetchScalarGridSpec(
        num_scalar_prefetch=3, grid=(n_pair, n_step),
        in_specs=[pl.BlockSpec((tq, LANES), lambda hp, st, qi, ki, dg: (qi[st], qblk + hp)),
                  pl.BlockSpec((tk, LANES), lambda hp, st, qi, ki, dg: (ki[st], kblk + hp)),
                  pl.BlockSpec((tk, LANES), lambda hp, st, qi, ki, dg: (ki[st], vblk + hp)),
                  pl.BlockSpec((tq, LANES), lambda hp, st, qi, ki, dg: (qi[st], hp)),
                  pl.BlockSpec((1, tq, LANES), lambda hp, st, qi, ki, dg: (hp, qi[st], 0)),
                  pl.BlockSpec((tq, LANES), lambda hp, st, qi, ki, dg: (qi[st], 0)),
                  pl.BlockSpec((tq, LANES), lambda hp, st, qi, ki, dg: (qi[st], 0)),
                  pl.BlockSpec((8, tk), lambda hp, st, qi, ki, dg: (0, ki[st]))],
        out_specs=[pl.BlockSpec((s, LANES), lambda hp, st, qi, ki, dg: (0, hp)),
                   pl.BlockSpec((tk, LANES), lambda hp, st, qi, ki, dg: (ki[st], hp)),
                   pl.BlockSpec((tk, LANES), lambda hp, st, qi, ki, dg: (ki[st], hp)),
                   pl.BlockSpec((1, 8, tk), lambda hp, st, qi, ki, dg: (hp, 0, ki[st])),
                   pl.BlockSpec((1, s, LANES), lambda hp, st, qi, ki, dg: (hp, 0, 0))],
        scratch_shapes=[pltpu.VMEM((s, LANES), F32), pltpu.VMEM((tk, LANES), F32), pltpu.VMEM((tk, LANES), F32),
                        pltpu.VMEM((8, tk), F32), pltpu.VMEM((s, LANES), F32)])
    w = n_pair * LANES
    return pl.pallas_call(
        body, name=name, grid_spec=grid_spec,
        out_shape=[jax.ShapeDtypeStruct((s, w), BF16), jax.ShapeDtypeStruct((s, w), BF16),
                   jax.ShapeDtypeStruct((s, w), BF16), jax.ShapeDtypeStruct((n_pair, 8, s), F32),
                   jax.ShapeDtypeStruct((n_pair, s, LANES), F32)],
        compiler_params=_params(("parallel", "arbitrary")),
    )(jnp.asarray(qi_np), jnp.asarray(ki_np), jnp.asarray(diag_np), proj, proj, proj, do, lse, delta, cum, cumt)


def _forget_bwd(dcumt4, dcumq4, sgt, name):
    s = sgt.shape[1]
    t = T_CUM
    n = s // t

    def body(d_ref, dq_ref, sg_ref, df_ref, dbf_ref, carry):
        i = pl.program_id(0)

        @pl.when(i == 0)
        def _():
            carry[...] = jnp.zeros_like(carry)
            dbf_ref[...] = jnp.zeros_like(dbf_ref)

        dq = dq_ref[0] + dq_ref[1] + dq_ref[2] + dq_ref[3]
        d = d_ref[0] + d_ref[1] + d_ref[2] + d_ref[3] + dq.T[0:8, :]
        upper = (_iota((t, t), 0) >= _iota((t, t), 1)).astype(F32)
        dlog = _dot(d, upper, NN, precision=lax.Precision.HIGHEST) + carry[:, 0:1]
        carry[...] += jnp.sum(d, axis=1, keepdims=True)
        dzt = dlog * sg_ref[...]
        dbf_ref[...] += jnp.sum(dzt, axis=1, keepdims=True)
        padded = jnp.concatenate([dzt, jnp.zeros((LANES - 8, t), F32)], axis=0)
        df_ref[...] = padded.T.astype(BF16)

    return pl.pallas_call(
        body, name=name, grid=(n,),
        in_specs=[pl.BlockSpec((4, 8, t), lambda i: (0, 0, n - 1 - i)),
                  pl.BlockSpec((4, t, LANES), lambda i: (0, n - 1 - i, 0)),
                  pl.BlockSpec((8, t), lambda i: (0, n - 1 - i))],
        out_specs=[pl.BlockSpec((t, LANES), lambda i: (n - 1 - i, 0)), pl.BlockSpec((8, LANES), lambda i: (0, 0))],
        out_shape=[jax.ShapeDtypeStruct((s, LANES), BF16), jax.ShapeDtypeStruct((8, LANES), F32)],
        scratch_shapes=[pltpu.VMEM((8, LANES), F32)],
        compiler_params=_params(("arbitrary",)),
    )(dcumt4, dcumq4, sgt)


def _mem_softmax(qm, kp):
    sc = _dot(qm, kp, NT) * (HEAD_DIM ** -0.5)
    e = jnp.exp(sc - jnp.max(sc, axis=1, keepdims=True))
    return e / jnp.sum(e, axis=1, keepdims=True)


def _branch_m_fwd(proj, mkv, mblk, name):
    s = proj.shape[0]
    t = T_M
    mw = mkv.shape[1] // 2
    ml = mkv.shape[0]

    def body(m_ref, kv_ref, z_ref):
        outs = []
        for pr in range(mw // LANES):
            qp = m_ref[:, pr * LANES:(pr + 1) * LANES]
            kp = kv_ref[:, pr * LANES:(pr + 1) * LANES].astype(BF16)
            vp = kv_ref[:, mw + pr * LANES:mw + (pr + 1) * LANES].astype(BF16)
            oh = []
            for hh in range(2):
                qm = jnp.where(_head_mask(hh), qp, 0.0).astype(BF16)
                oh.append(_dot(_mem_softmax(qm, kp).astype(BF16), vp, NN))
            outs.append(jnp.where(_iota((t, LANES), 1) < HEAD_DIM, oh[0], oh[1]))
        o = jnp.concatenate(outs, axis=1)
        z_ref[...] = (o * _silu(m_ref[:, mw:2 * mw])).astype(BF16)

    return pl.pallas_call(
        body, name=name, grid=(s // t,),
        in_specs=[pl.BlockSpec((t, 2 * mw), lambda i: (i, mblk)), pl.BlockSpec((ml, 2 * mw), lambda i: (0, 0))],
        out_specs=pl.BlockSpec((t, mw), lambda i: (i, 0)),
        out_shape=jax.ShapeDtypeStruct((s, mw), BF16),
        compiler_params=_params(("parallel",)),
    )(proj, mkv)


def _branch_m_bwd(proj, mkv, dz, mblk, name):
    s = proj.shape[0]
    t = T_M
    mw = mkv.shape[1] // 2
    ml = mkv.shape[0]
    scale = HEAD_DIM ** -0.5

    def body(m_ref, kv_ref, dz_ref, dm_ref, dkv_ref):
        i = pl.program_id(0)

        @pl.when(i == 0)
        def _():
            dkv_ref[...] = jnp.zeros_like(dkv_ref)

        gt = m_ref[:, mw:2 * mw]
        dz = dz_ref[...]
        do = dz * _silu(gt)
        outs = []
        for pr in range(mw // LANES):
            cols = slice(pr * LANES, (pr + 1) * LANES)
            vcols = slice(mw + pr * LANES, mw + (pr + 1) * LANES)
            qp = m_ref[:, cols]
            kp = kv_ref[:, cols].astype(BF16)
            vp = kv_ref[:, vcols].astype(BF16)
            dop = do[:, cols]
            oh = []
            dq = jnp.zeros((t, LANES), F32)
            dk = jnp.zeros((ml, LANES), F32)
            dv = jnp.zeros((ml, LANES), F32)
            for hh in range(2):
                hm = _head_mask(hh)
                qm = jnp.where(hm, qp, 0.0).astype(BF16)
                dom = jnp.where(hm, dop, 0.0).astype(BF16)
                p = _mem_softmax(qm, kp)
                pb = p.astype(BF16)
                oh.append(_dot(pb, vp, NN))
                dv = dv + _dot(pb, dom, TN)
                dp = _dot(dom, vp, NT)
                ds = p * (dp - jnp.sum(dp * p, axis=1, keepdims=True))
                dsb = (ds * scale).astype(BF16)
                dq = dq + jnp.where(hm, _dot(dsb, kp, NN), 0.0)
                dk = dk + _dot(dsb, qm, TN)
            outs.append(jnp.where(_iota((t, LANES), 1) < HEAD_DIM, oh[0], oh[1]))
            dm_ref[:, cols] = dq.astype(BF16)
            dkv_ref[:, cols] += dk
            dkv_ref[:, vcols] += dv
        o = jnp.concatenate(outs, axis=1)
        dm_ref[:, mw:2 * mw] = (dz * o * _dsilu(gt)).astype(BF16)

    return pl.pallas_call(
        body, name=name, grid=(s // t,),
        in_specs=[pl.BlockSpec((t, 2 * mw), lambda i: (i, mblk)), pl.BlockSpec((ml, 2 * mw), lambda i: (0, 0)),
                  pl.BlockSpec((t, mw), lambda i: (i, 0))],
        out_specs=[pl.BlockSpec((t, 2 * mw), lambda i: (i, 0)), pl.BlockSpec((ml, 2 * mw), lambda i: (0, 0))],
        out_shape=[jax.ShapeDtypeStruct((s, 2 * mw), BF16), jax.ShapeDtypeStruct((ml, 2 * mw), F32)],
        compiler_params=_params(("arbitrary",)),
    )(proj, mkv, dz)


def _merge_parts(z_refs, g_refs, pcat, bounds):
    ys, sgs = [], []
    merged = None
    for zr, gr, (lo, hi) in zip(z_refs, g_refs, bounds):
        y = _dot(zr[...], pcat[lo:hi, :], NN)
        sg = _sigmoid(gr[...])
        ys.append(y)
        sgs.append(sg)
        merged = sg * y if merged is None else merged + sg * y
    return ys, sgs, merged


def _branch_bounds(zs):
    bounds, lo = [], 0
    for z in zs:
        bounds.append((lo, lo + z.shape[1]))
        lo += z.shape[1]
    return bounds


def _matmul_copies(x, name):
    s, d = x.shape
    t = T_MERGE

    def body(x_ref, b_ref, bt_ref):
        b_ref[...] = x_ref[...].astype(BF16)
        bt_ref[...] = x_ref[...].T.astype(BF16)

    return pl.pallas_call(
        body, name=name, grid=(s // t,), in_specs=[pl.BlockSpec((t, d), lambda i: (i, 0))],
        out_specs=[pl.BlockSpec((t, d), lambda i: (i, 0)), pl.BlockSpec((d, t), lambda i: (0, i))],
        out_shape=[jax.ShapeDtypeStruct((s, d), BF16), jax.ShapeDtypeStruct((d, s), BF16)],
        compiler_params=_params(("parallel",)))(x)


def _merge_fwd(zs, proj, gblk, pcat, wout, x, lng, lnb, alpha, name):
    s, d = x.shape
    t = T_MERGE
    bounds = _branch_bounds(zs)

    def body(*refs):
        z_refs, g_refs = refs[0:4], refs[4:8]
        pcat_hbm, wout_hbm, x_ref, lng_ref, lnb_ref, y_ref, yb_ref, ybt_ref, pcat_v, wout_v = refs[8:]

        @pl.when(pl.program_id(0) == 0)
        def _():
            pltpu.sync_copy(pcat_hbm, pcat_v)
            pltpu.sync_copy(wout_hbm, wout_v)

        _, _, merged = _merge_parts(z_refs, g_refs, pcat_v, bounds)
        h = alpha * x_ref[...] + _dot(merged.astype(BF16), wout_v[...], NN)
        xh, _ = _ln_stats(h)
        y = xh * lng_ref[...] + lnb_ref[...]
        y_ref[...] = y
        yb_ref[...] = y.astype(BF16)
        ybt_ref[...] = y.T.astype(BF16)

    row = pl.BlockSpec((t, d), lambda i: (i, 0))
    vec = pl.BlockSpec((1, d), lambda i: (0, 0))
    in_specs = ([pl.BlockSpec((t, z.shape[1]), lambda i: (i, 0)) for z in zs]
                + [pl.BlockSpec((t, d), lambda i, k=k: (i, gblk + k)) for k in range(4)]
                + [ANY, ANY, row, vec, vec])
    return pl.pallas_call(
        body, name=name, grid=(s // t,), in_specs=in_specs,
        out_specs=[row, row, pl.BlockSpec((d, t), lambda i: (0, i))],
        out_shape=[jax.ShapeDtypeStruct((s, d), F32), jax.ShapeDtypeStruct((s, d), BF16),
                   jax.ShapeDtypeStruct((d, s), BF16)],
        scratch_shapes=[pltpu.VMEM(pcat.shape, BF16), pltpu.VMEM(wout.shape, BF16)],
        compiler_params=_params(("arbitrary",)),
    )(*zs, proj, proj, proj, proj, pcat, wout, x, lng, lnb)


def _merge_bwd(zs, proj, gblk, pcat, wout, x, lng, lnb, dy, alpha, name):
    s, d = x.shape
    t = T_MERGE
    n = s // t
    bounds = _branch_bounds(zs)

    def body(*refs):
        z_refs, g_refs = refs[0:4], refs[4:8]
        pcat_hbm, wout_hbm, x_ref, lng_ref, lnb_ref, dy_ref = refs[8:14]
        dx_ref, dg_ref = refs[14:16]
        dz_refs = refs[16:20]
        dpcat_hbm, dwout_hbm, dlng_ref, dlnb_ref = refs[20:24]
        pcat_v, wout_v, dpcat_v, dwout_v = refs[24:]
        i = pl.program_id(0)

        @pl.when(i == 0)
        def _():
            pltpu.sync_copy(pcat_hbm, pcat_v)
            pltpu.sync_copy(wout_hbm, wout_v)
            dpcat_v[...] = jnp.zeros_like(dpcat_v)
            dwout_v[...] = jnp.zeros_like(dwout_v)
            dlng_ref[...] = jnp.zeros_like(dlng_ref)
            dlnb_ref[...] = jnp.zeros_like(dlnb_ref)

        ys, sgs, merged = _merge_parts(z_refs, g_refs, pcat_v, bounds)
        mb = merged.astype(BF16)
        h = alpha * x_ref[...] + _dot(mb, wout_v[...], NN)
        xh, rstd = _ln_stats(h)
        dyv = dy_ref[...]
        dlng_ref[...] += jnp.sum(dyv * xh, axis=0, keepdims=True)
        dlnb_ref[...] += jnp.sum(dyv, axis=0, keepdims=True)
        dxh = dyv * lng_ref[...]
        dh = rstd * (dxh - jnp.mean(dxh, axis=-1, keepdims=True) - xh * jnp.mean(dxh * xh, axis=-1, keepdims=True))
        dx_ref[...] = alpha * dh
        dhb = dh.astype(BF16)
        dwout_v[...] += _dot(mb, dhb, TN)
        dmerged = _dot(dhb, wout_v[...], NT)
        for k, (zr, (lo, hi)) in enumerate(zip(z_refs, bounds)):
            sg = sgs[k]
            dg_ref[:, k * d:(k + 1) * d] = (dmerged * ys[k] * sg * (1.0 - sg)).astype(BF16)
            dyk = (dmerged * sg).astype(BF16)
            dpcat_v[lo:hi, :] += _dot(zr[...], dyk, TN)
            dz_refs[k][...] = _dot(dyk, pcat_v[lo:hi, :], NT)

        @pl.when(i == n - 1)
        def _():
            pltpu.sync_copy(dpcat_v, dpcat_hbm)
            pltpu.sync_copy(dwout_v, dwout_hbm)

    row = pl.BlockSpec((t, d), lambda i: (i, 0))
    vec = pl.BlockSpec((1, d), lambda i: (0, 0))
    z_specs = [pl.BlockSpec((t, z.shape[1]), lambda i: (i, 0)) for z in zs]
    in_specs = (z_specs + [pl.BlockSpec((t, d), lambda i, k=k: (i, gblk + k)) for k in range(4)]
                + [ANY, ANY, row, vec, vec, row])
    out_specs = [row, pl.BlockSpec((t, 4 * d), lambda i: (i, 0))] + z_specs + [ANY, ANY, vec, vec]
    vo = jax.ShapeDtypeStruct((1, d), F32)
    out_shape = ([jax.ShapeDtypeStruct((s, d), F32), jax.ShapeDtypeStruct((s, 4 * d), BF16)]
                 + [jax.ShapeDtypeStruct(z.shape, F32) for z in zs]
                 + [jax.ShapeDtypeStruct(pcat.shape, F32), jax.ShapeDtypeStruct(wout.shape, F32), vo, vo])
    return pl.pallas_call(
        body, name=name, grid=(n,), in_specs=in_specs, out_specs=out_specs, out_shape=out_shape,
        scratch_shapes=[pltpu.VMEM(pcat.shape, BF16), pltpu.VMEM(wout.shape, BF16),
                        pltpu.VMEM(pcat.shape, F32), pltpu.VMEM(wout.shape, F32)],
        compiler_params=_params(("arbitrary",)),
    )(*zs, proj, proj, proj, proj, pcat, wout, x, lng, lnb, dy)


def _loss_head(y, target, name):
    s, d = y.shape
    t = T_ELEM

    def body(y_ref, t_ref, dy_ref, loss_ref):
        @pl.when(pl.program_id(0) == 0)
        def _():
            loss_ref[...] = jnp.zeros_like(loss_ref)

        e = y_ref[...] - t_ref[...]
        dy_ref[...] = e * (1.0 / d)
        loss_ref[...] += 0.5 * jnp.sum(jnp.mean(e * e, axis=-1, keepdims=True), axis=0, keepdims=True)

    row = pl.BlockSpec((t, d), lambda i: (i, 0))
    return pl.pallas_call(
        body, name=name, grid=(s // t,), in_specs=[row, row],
        out_specs=[row, pl.BlockSpec((8, LANES), lambda i: (0, 0))],
        out_shape=[jax.ShapeDtypeStruct((s, d), F32), jax.ShapeDtypeStruct((8, LANES), F32)],
        compiler_params=_params(("arbitrary",)),
    )(y, target)


def _adamw(w, g, m, v, name):
    r, c = w.shape
    t = _pick(r, (256, 128, 64, 32, 16, 8))

    def body(w_ref, g_ref, m_ref, v_ref, d_ref, nm_ref, nv_ref):
        gv = g_ref[...]
        nm = ADAM_B1 * m_ref[...] + (1.0 - ADAM_B1) * gv
        nv = ADAM_B2 * v_ref[...] + (1.0 - ADAM_B2) * (gv * gv)
        m_hat = nm / (1.0 - ADAM_B1 ** ADAM_STEP)
        v_hat = nv / (1.0 - ADAM_B2 ** ADAM_STEP)
        d_ref[...] = -ADAM_LR * (m_hat / (jnp.sqrt(v_hat) + ADAM_EPS) + ADAM_WD * w_ref[...])
        nm_ref[...] = nm
        nv_ref[...] = nv

    blk = pl.BlockSpec((t, c), lambda i: (i, 0))
    o = jax.ShapeDtypeStruct((r, c), F32)
    return pl.pallas_call(body, name=name, grid=(r // t,), in_specs=[blk] * 4, out_specs=[blk] * 3,
                          out_shape=[o, o, o], compiler_params=_params(("parallel",)))(w, g, m, v)


def _sum_leading(x, out_dtype, name):
    k, r, c = x.shape
    t = _pick(r, (256, 128, 64, 32, 16, 8))

    def body(x_ref, o_ref):
        acc = x_ref[0].astype(F32)
        for j in range(1, k):
            acc = acc + x_ref[j].astype(F32)
        o_ref[...] = acc.astype(out_dtype)

    return pl.pallas_call(body, name=name, grid=(r // t,),
                          in_specs=[pl.BlockSpec((k, t, c), lambda i: (0, i, 0))],
                          out_specs=pl.BlockSpec((t, c), lambda i: (i, 0)),
                          out_shape=jax.ShapeDtypeStruct((r, c), out_dtype), compiler_params=_params(("parallel",)))(x)


def _position():
    return lax.axis_index("x"), lax.axis_index("y"), lax.axis_index("c")


def _chip_peers(x, y):
    return [(1 - x, y), (x, 1 - y), (1 - x, 1 - y)]


def _comm_call(body, n_in, out_shape, n_remote, n_local, name):
    return pl.pallas_call(
        body, name=name, in_specs=[ANY] * n_in, out_specs=[ANY] * len(out_shape), out_shape=out_shape,
        scratch_shapes=[pltpu.SemaphoreType.DMA((n_remote,)), pltpu.SemaphoreType.DMA((n_remote,)),
                        pltpu.SemaphoreType.DMA((max(n_local, 1),))])


def _run_copies(local, remote, send, recv, loc):
    copies = [pltpu.make_async_copy(src, dst, loc.at[k]) for k, (src, dst) in enumerate(local)]
    copies += [pltpu.make_async_remote_copy(src_ref=src, dst_ref=dst, send_sem=send.at[k], recv_sem=recv.at[k],
                                            device_id=peer, device_id_type=MESH)
               for k, (src, dst, peer) in enumerate(remote)]
    for cp in copies:
        cp.start()
    for cp in copies:
        cp.wait()


COPY_BYTES = 1024 * 1024


def _n_copies(rows, row_bytes, align):
    n = 8
    while n > 1 and (rows % (n * align) or rows // n * row_bytes < COPY_BYTES):
        n //= 2
    return n


def _allgather_chips(arrs, name):
    n = len(arrs)
    per_layer = [a.size * a.dtype.itemsize // a.shape[0] >= COPY_BYTES for a in arrs]
    n_each = [a.shape[0] if pl_ else 1 for a, pl_ in zip(arrs, per_layer)]

    def body(*refs):
        ins, outs = refs[:n], refs[n:2 * n]
        send, recv, loc = refs[2 * n:]
        x, y, c = _position()
        me = 2 * x + y
        local, remote = [], []
        for a in range(n):
            if per_layer[a]:
                parts = [(ins[a].at[l], outs[a].at[me, l]) for l in range(arrs[a].shape[0])]
            else:
                parts = [(ins[a], outs[a].at[me])]
            local += parts
            for px, py in _chip_peers(x, y):
                remote += [(src, dst, (px, py, c)) for src, dst in parts]
        _run_copies(local, remote, send, recv, loc)

    out_shape = [jax.ShapeDtypeStruct((4,) + a.shape, a.dtype) for a in arrs]
    return _comm_call(body, n, out_shape, 3 * sum(n_each), sum(n_each), name)(*arrs)


def _allgather_all(v, name):
    def body(v_ref, o_ref, send, recv, loc):
        x, y, c = _position()
        me = 4 * x + 2 * y + c
        remote = []
        for k in range(1, 8):
            fx, fy, fc = (k >> 2) & 1, (k >> 1) & 1, k & 1
            remote.append((v_ref, o_ref.at[me], (x ^ fx, y ^ fy, c ^ fc)))
        _run_copies([(v_ref, o_ref.at[me])], remote, send, recv, loc)

    return _comm_call(body, 1, [jax.ShapeDtypeStruct((8,) + v.shape, v.dtype)], 7, 1, name)(v)[0]


def _pair_exchange_sum(g, name):
    _, _, n, r, cc = g.shape

    def body(c_ref, mine_ref, send_ref, o_ref, land, send_sem, recv_sem):
        x, y, c = _position()
        slot = (pl.program_id(0) * n + pl.program_id(1)) % 2
        push = pltpu.make_async_remote_copy(
            src_ref=send_ref.at[0, 0, 0], dst_ref=land.at[slot], send_sem=send_sem.at[slot],
            recv_sem=recv_sem.at[slot], device_id=(x, y, 1 - c), device_id_type=MESH)
        push.start()
        push.wait_recv()
        o_ref[0, 0] = (mine_ref[0, 0, 0] + land[slot]).astype(BF16)
        push.wait_send()

    grid_spec = pltpu.PrefetchScalarGridSpec(
        num_scalar_prefetch=1, grid=(4, n),
        in_specs=[pl.BlockSpec((1, 1, 1, r, cc), lambda j, k, c: (j, c[0], k, 0, 0)),
                  pl.BlockSpec((1, 1, 1, r, cc), lambda j, k, c: (j, 1 - c[0], k, 0, 0))],
        out_specs=pl.BlockSpec((1, 1, r, cc), lambda j, k, c: (j, k, 0, 0)),
        scratch_shapes=[pltpu.VMEM((2, r, cc), F32), pltpu.SemaphoreType.DMA((2,)), pltpu.SemaphoreType.DMA((2,))])
    return pl.pallas_call(
        body, name=name, grid_spec=grid_spec, out_shape=jax.ShapeDtypeStruct((4, n, r, cc), BF16),
        compiler_params=_params(("arbitrary", "arbitrary")))(_scalar(lax.axis_index("c")), g, g)


def _scalar(v):
    return v.astype(jnp.int32).reshape(1)


def _chip_exchange_sum(p, name):
    _, n, r, cc = p.shape

    def body(i0, i1, i2, i3, own_ref, s0_ref, s1_ref, s2_ref, mine_ref, theirs_ref, land, total, land_pair,
             send_sem, recv_sem, pair_send, pair_recv):
        x, y, c = _position()
        slot = pl.program_id(0) % 2
        pushes = [pltpu.make_async_remote_copy(
            src_ref=src.at[0, 0], dst_ref=land.at[slot, j], send_sem=send_sem.at[slot, j], recv_sem=recv_sem.at[slot, j],
            device_id=(px, py, c), device_id_type=MESH)
            for j, (src, (px, py)) in enumerate(zip((s0_ref, s1_ref, s2_ref), _chip_peers(x, y)))]
        for cp in pushes:
            cp.start()
        acc = own_ref[0, 0].astype(F32)
        for j, cp in enumerate(pushes):
            cp.wait_recv()
            acc = acc + land[slot, j].astype(F32)
        mine_ref[0] = acc
        total[slot] = acc
        share = pltpu.make_async_remote_copy(
            src_ref=total.at[slot], dst_ref=land_pair.at[slot], send_sem=pair_send.at[slot], recv_sem=pair_recv.at[slot],
            device_id=(x, y, 1 - c), device_id_type=MESH)
        share.start()
        share.wait_recv()
        theirs_ref[0] = land_pair[slot]
        for cp in pushes:
            cp.wait_send()
        share.wait_send()

    def slot_spec(which):
        return pl.BlockSpec((1, 1, r, cc), lambda k, *idx, which=which: (idx[which][0], k, 0, 0))

    out_spec = pl.BlockSpec((1, r, cc), lambda k, *idx: (k, 0, 0))
    grid_spec = pltpu.PrefetchScalarGridSpec(
        num_scalar_prefetch=4, grid=(n,), in_specs=[slot_spec(0), slot_spec(1), slot_spec(2), slot_spec(3)],
        out_specs=[out_spec, out_spec],
        scratch_shapes=[pltpu.VMEM((2, 3, r, cc), BF16), pltpu.VMEM((2, r, cc), F32), pltpu.VMEM((2, r, cc), F32),
                        pltpu.SemaphoreType.DMA((2, 3)), pltpu.SemaphoreType.DMA((2, 3)),
                        pltpu.SemaphoreType.DMA((2,)), pltpu.SemaphoreType.DMA((2,))])
    o = jax.ShapeDtypeStruct((n, r, cc), F32)
    x, y, _ = _position()
    chips = [_scalar(2 * x + y)] + [_scalar(2 * px + py) for px, py in _chip_peers(x, y)]
    return pl.pallas_call(body, name=name, grid_spec=grid_spec, out_shape=[o, o],
                          compiler_params=_params(("arbitrary",)))(*chips, p, p, p, p)


def _reduce_scatter(gs):
    c = lax.axis_index("c")
    outs = []
    for a, g in enumerate(gs):
        _, rows, cc = g.shape
        k = _n_copies(rows // 2, cc * 4, 16)
        p = _pair_exchange_sum(g.reshape(4, 2, k, rows // (2 * k), cc), f"rs_pair_exchange_sum_{a}")
        mine, theirs = _chip_exchange_sum(p, f"rs_chip_exchange_sum_{a}")
        mine, theirs = mine.reshape(rows // 2, cc), theirs.reshape(rows // 2, cc)
        outs.append(jnp.where(c == 0, jnp.concatenate([mine, theirs]), jnp.concatenate([theirs, mine])))
    return outs


def _gather_shards(w, name):
    rows, cc = w.shape
    half = rows // 2
    n = _n_copies(half, cc * w.dtype.itemsize, 16)
    r = half // n

    def body(core_ref, mine_ref, other_ref, out_ref, land, land_pair, send_sem, recv_sem, pair_send, pair_recv, out_sem):
        x, y, c = _position()
        k = pl.program_id(0)
        slot = k % 2
        me = 2 * x + y
        chips = [2 * px + py for px, py in _chip_peers(x, y)]
        pushes = [pltpu.make_async_remote_copy(
            src_ref=mine_ref.at[0, 0], dst_ref=land.at[slot, j], send_sem=send_sem.at[slot, j],
            recv_sem=recv_sem.at[slot, j], device_id=(px, py, c), device_id_type=MESH)
            for j, (px, py) in enumerate(_chip_peers(x, y))]
        for cp in pushes:
            cp.start()
        writes = [pltpu.make_async_copy(mine_ref.at[0, 0], out_ref.at[me, c, k], out_sem.at[0]),
                  pltpu.make_async_copy(other_ref.at[0, 0], out_ref.at[me, 1 - c, k], out_sem.at[1])]
        for cp in writes:
            cp.start()
        passes = []
        for j, cp in enumerate(pushes):
            cp.wait_recv()
            passes.append(pltpu.make_async_remote_copy(
                src_ref=land.at[slot, j], dst_ref=land_pair.at[slot, j], send_sem=pair_send.at[slot, j],
                recv_sem=pair_recv.at[slot, j], device_id=(x, y, 1 - c), device_id_type=MESH))
            passes[j].start()
            writes.append(pltpu.make_async_copy(land.at[slot, j], out_ref.at[chips[j], c, k], out_sem.at[2 + j]))
            writes[-1].start()
        for j, cp in enumerate(passes):
            cp.wait_recv()
            writes.append(pltpu.make_async_copy(land_pair.at[slot, j], out_ref.at[chips[j], 1 - c, k], out_sem.at[5 + j]))
            writes[-1].start()
        for cp in writes:
            cp.wait()
        for cp in pushes + passes:
            cp.wait_send()

    grid_spec = pltpu.PrefetchScalarGridSpec(
        num_scalar_prefetch=1, grid=(n,),
        in_specs=[pl.BlockSpec((1, 1, r, cc), lambda k, core: (core[0], k, 0, 0)),
                  pl.BlockSpec((1, 1, r, cc), lambda k, core: (1 - core[0], k, 0, 0))],
        out_specs=ANY,
        scratch_shapes=[pltpu.VMEM((2, 3, r, cc), w.dtype), pltpu.VMEM((2, 3, r, cc), w.dtype),
                        pltpu.SemaphoreType.DMA((2, 3)), pltpu.SemaphoreType.DMA((2, 3)),
                        pltpu.SemaphoreType.DMA((2, 3)), pltpu.SemaphoreType.DMA((2, 3)),
                        pltpu.SemaphoreType.DMA((8,))])
    w4 = w.reshape(2, n, r, cc)
    out = pl.pallas_call(body, name=name, grid_spec=grid_spec,
                         out_shape=jax.ShapeDtypeStruct((4, 2, n, r, cc), w.dtype),
                         compiler_params=_params(("arbitrary",)))(_scalar(lax.axis_index("c")), w4, w4)
    return out.reshape(4, rows, cc)


def _pad_rows(a, rows):
    return jnp.pad(a, ((0, rows - a.shape[0]), (0, 0)))


def _shard_cols(a):
    r, c4 = a.shape
    return a.reshape(r, 4, c4 // 4).transpose(1, 0, 2)


def kernel(x, mem, w_in, b_forget, conv_a_w, conv_a_b, ln_a_g, ln_a_b, conv_b_w, w_kv_mem, mem_ln_g, mem_ln_b, p_a, p_b, p_c, p_m, w_out, ln_g, ln_b, loss_target, m_w_in, m_b_forget, m_conv_a_w, m_conv_a_b, m_ln_a_g, m_ln_a_b, m_conv_b_w, m_w_kv_mem, m_mem_ln_g, m_mem_ln_b, m_p_a, m_p_b, m_p_c, m_p_m, m_w_out, m_ln_g, m_ln_b, v_w_in, v_b_forget, v_conv_a_w, v_conv_a_b, v_ln_a_g, v_ln_a_b, v_conv_b_w, v_w_kv_mem, v_mem_ln_g, v_mem_ln_b, v_p_a, v_p_b, v_p_c, v_p_m, v_w_out, v_ln_g, v_ln_b):
    depth = w_in.shape[0]
    x0 = x[0]
    s, d = x0.shape
    aw = conv_a_w.shape[2] * 4
    mw = p_m.shape[1]
    n_head = b_forget.shape[1]
    alpha = (2.0 * depth) ** 0.25
    in_cols = w_in.shape[2] * 4
    assert aw == n_head * HEAD_DIM and mw % LANES == 0 and in_cols == 11 * aw + n_head + 2 * mw + 4 * d
    cf0 = 10 * aw
    n_main = in_cols - n_head
    fblk = n_main // LANES
    n_pad = n_main + LANES
    gblk = (11 * aw + 2 * mw) // d
    mblk = (11 * aw) // (2 * mw)
    qblk, kblk, vblk = 7 * aw // LANES, 8 * aw // LANES, 9 * aw // LANES
    assert (11 * aw + 2 * mw) % d == 0 and (11 * aw) % (2 * mw) == 0

    def gather(w, name):
        flat = _gather_shards(w.astype(BF16).reshape(-1, w.shape[-1]), name)
        return flat.reshape((4,) + w.shape)

    w_in_g, p_a_g, p_b_g, p_c_g = (gather(w, f"gather_{nm}") for w, nm in
                                   ((w_in, "w_in"), (p_a, "p_a"), (p_b, "p_b"), (p_c, "p_c")))
    p_m_g, w_kv_g, w_out_g = (gather(w, f"gather_{nm}") for w, nm in
                              ((p_m, "p_m"), (w_kv_mem, "w_kv"), (w_out, "w_out")))
    conv_a_g, conv_b_g = _allgather_chips([conv_a_w, conv_b_w], "gather_conv_taps")

    def cols(g):
        return jnp.concatenate([g[j] for j in range(4)], axis=-1)

    def rows(g):
        return jnp.concatenate([g[j] for j in range(4)], axis=-2)

    w_full = cols(w_in_g)
    w_pad = jnp.concatenate([w_full[:, :, :cf0], w_full[:, :, cf0 + n_head:], w_full[:, :, cf0:cf0 + n_head],
                             jnp.zeros((depth, d, LANES - n_head), BF16)], axis=-1)
    pcat = jnp.concatenate([cols(p_a_g), cols(p_b_g), cols(p_c_g), cols(p_m_g)], axis=1)
    w_kv = rows(w_kv_g)
    wout = rows(w_out_g)
    conv_a = jnp.pad(cols(conv_a_g), ((0, 0), (0, HALO_A - CONV_A), (0, 0)))
    conv_b = jnp.pad(cols(conv_b_g), ((0, 0), (0, HALO_B - CONV_B), (0, 0)))
    bf_pad = jnp.pad(b_forget, ((0, 0), (0, LANES - n_head)))
    pieces = [(0, 3 * aw), (3 * aw, 7 * aw), (7 * aw, 10 * aw), (10 * aw, 11 * aw),
              (11 * aw, 11 * aw + 2 * mw), (11 * aw + 2 * mw, n_main), (n_main, n_pad)]

    mem_n = _ln_rows(mem[0], mem_ln_g[None], mem_ln_b[None], "mem_ln")

    xs, saved = [x0], []
    x_ops = [_matmul_copies(x0, "matmul_copies")]
    for l in range(depth):
        xl = xs[-1]
        proj = _mm(x_ops[l][0], w_pad[l], name=f"proj_{l}")
        za, conv_out = _branch_a_fwd(proj, conv_a[l], conv_a_b[l][None], ln_a_g[l][None], ln_a_b[l][None], f"a_fwd_{l}")
        zb = _branch_b_fwd(proj, conv_b[l], f"b_fwd_{l}")
        cum, cumt, sgt = _forget_prep(proj, bf_pad[l][None], fblk, f"forget_prep_{l}")
        o_c, lse = _fox_fwd(proj, cum, cumt, qblk, kblk, vblk, f"fox_fwd_{l}")
        zc = _gate_mul(proj, o_c, 10, f"c_gate_{l}")
        mkv = _mm(mem_n, w_kv[l], name=f"mkv_{l}")
        zm = _branch_m_fwd(proj, mkv, mblk, f"m_fwd_{l}")
        zs = [za, zb, zc, zm]
        y, y_b, y_bt = _merge_fwd(zs, proj, gblk, pcat[l], wout[l], xl, ln_g[l][None], ln_b[l][None], alpha,
                                  f"merge_fwd_{l}")
        xs.append(y)
        x_ops.append((y_b, y_bt))
        saved.append((proj, zs, conv_out, cum, cumt, sgt, o_c, lse, mkv))

    dy, loss_part = _loss_head(xs[-1], loss_target[0], "loss_head")

    g_w_in, g_conv_a, g_conv_b, g_w_kv, g_pcat, g_wout = [], [], [], [], [], []
    small = []
    dmem_n = None
    for l in reversed(range(depth)):
        proj, zs, conv_out, cum, cumt, sgt, o_c, lse, mkv = saved[l]
        xl = xs[l]
        (dx, d_g, dza, dzb, dzc, dzm, dpcat, dwout, dlng, dlnb) = _merge_bwd(
            zs, proj, gblk, pcat[l], wout[l], xl, ln_g[l][None], ln_b[l][None], dy, alpha, f"merge_bwd_{l}")
        d_m, dmkv = _branch_m_bwd(proj, mkv, dzm, mblk, f"m_bwd_{l}")
        g_w_kv.append(_mm(mem_n, dmkv, ta=True, name=f"dwkv_{l}"))
        dmem_n = _mm(dmkv, w_kv[l], tb=True, add=dmem_n, name=f"dmem_{l}")
        do, d_cg, delta = _fox_bwd_prep(proj, dzc, o_c, 10, f"fox_bwd_prep_{l}")
        dq, dk, dv, dcumt4, dcumq4 = _fox_bwd(proj, do, lse, delta, cum, cumt, qblk, kblk, vblk, f"fox_bwd_{l}")
        d_f, dbf = _forget_bwd(dcumt4, dcumq4, sgt, f"forget_bwd_{l}")
        d_b, dconv_b = _branch_b_bwd(proj, dzb, conv_b[l], f"b_bwd_{l}")
        d_a, dconv_a, dconv_ab, dlag, dlab = _branch_a_bwd(
            proj, conv_out, dza, conv_a[l], ln_a_g[l][None], ln_a_b[l][None], f"a_bwd_{l}")
        dparts = [d_a, d_b, jnp.concatenate([dq, dk, dv], axis=1), d_cg, d_m, d_g, d_f]
        dx = _input_grad(dparts, [w_pad[l][:, lo:hi] for lo, hi in pieces], dx, f"dx_{l}")
        dw_parts = [_mm(x_ops[l][1], dp, tm=d, name=f"dw_{l}_{k}") for k, dp in enumerate(dparts)]
        dw = jnp.concatenate(dw_parts[:3] + [dw_parts[6][:, :n_head]] + dw_parts[3:6], axis=1)
        g_w_in.append(dw)
        g_conv_a.append(dconv_a)
        g_conv_b.append(dconv_b)
        g_pcat.append(dpcat)
        g_wout.append(dwout)
        small.append([dbf[:, 0], dconv_ab[0], dlag[0], dlab[0], dlng[0], dlnb[0]])
        dy = dx
    grad_x = dy
    dmlg, dmlb = _ln_rows_param_grads(mem[0], dmem_n, "mem_ln_grads")
    for lst in (g_w_in, g_conv_a, g_conv_b, g_w_kv, g_pcat, g_wout, small):
        lst.reverse()

    pa_end, pb_end, pc_end = aw, 2 * aw, 3 * aw
    rs_in = [
        _shard_cols(jnp.concatenate(g_w_in, axis=0)),
        _shard_cols(jnp.concatenate(g_conv_a, axis=0)),
        _shard_cols(jnp.concatenate([_pad_rows(g, 2 * HALO_B) for g in g_conv_b], axis=0)),
        jnp.concatenate([g.reshape(4, g.shape[0] // 4, g.shape[1]) for g in g_w_kv], axis=1),
        _shard_cols(jnp.concatenate([g[:pa_end] for g in g_pcat], axis=0)),
        _shard_cols(jnp.concatenate([g[pa_end:pb_end] for g in g_pcat], axis=0)),
        _shard_cols(jnp.concatenate([g[pb_end:pc_end] for g in g_pcat], axis=0)),
        _shard_cols(jnp.concatenate([g[pc_end:] for g in g_pcat], axis=0)),
        jnp.concatenate([g.reshape(4, g.shape[0] // 4, g.shape[1]) for g in g_wout], axis=1),
    ]
    rs_out = _reduce_scatter(rs_in)
    gw_in = rs_out[0].reshape(depth, d, -1)
    g_ca = rs_out[1].reshape(depth, HALO_A, -1)[:, :CONV_A]
    g_cb = rs_out[2].reshape(depth, 2 * HALO_B, -1)[:, :CONV_B]
    gw_kv = rs_out[3].reshape(depth, -1, 2 * mw)
    gp_a = rs_out[4].reshape(depth, aw, -1)
    gp_b = rs_out[5].reshape(depth, aw, -1)
    gp_c = rs_out[6].reshape(depth, aw, -1)
    gp_m = rs_out[7].reshape(depth, mw, -1)
    gw_out = rs_out[8].reshape(depth, -1, d)

    flat = jnp.concatenate([jnp.concatenate(p) for p in small] + [dmlg[0], dmlb[0], loss_part[0, 0:1]])
    n_small = flat.shape[0]
    n_rows = -(-n_small // (8 * LANES)) * 8
    vec = jnp.pad(flat, (0, n_rows * LANES - n_small)).reshape(n_rows, LANES)
    tot = _sum_leading(_allgather_all(vec, "gather_small"), F32, "sum_small").reshape(-1)
    per_layer = n_head + 3 * aw + 2 * d
    tl = tot[:depth * per_layer].reshape(depth, per_layer)
    offs = np.cumsum([0, n_head, aw, aw, aw, d, d])
    g_bf, g_cab, g_lag, g_lab, g_lg, g_lb = [tl[:, offs[k]:offs[k + 1]] for k in range(6)]
    base = depth * per_layer
    g_mlg, g_mlb = tot[base:base + d], tot[base + d:base + 2 * d]
    loss = tot[base + 2 * d]

    grads = [gw_in, g_bf, g_ca, g_cab, g_lag, g_lab, g_cb, gw_kv, g_mlg, g_mlb, gp_a, gp_b, gp_c, gp_m, gw_out, g_lg, g_lb]
    ws = [w_in, b_forget, conv_a_w, conv_a_b, ln_a_g, ln_a_b, conv_b_w, w_kv_mem, mem_ln_g, mem_ln_b, p_a, p_b, p_c, p_m, w_out, ln_g, ln_b]
    ms = [m_w_in, m_b_forget, m_conv_a_w, m_conv_a_b, m_ln_a_g, m_ln_a_b, m_conv_b_w, m_w_kv_mem, m_mem_ln_g, m_mem_ln_b, m_p_a, m_p_b, m_p_c, m_p_m, m_w_out, m_ln_g, m_ln_b]
    vs = [v_w_in, v_b_forget, v_conv_a_w, v_conv_a_b, v_ln_a_g, v_ln_a_b, v_conv_b_w, v_w_kv_mem, v_mem_ln_g, v_mem_ln_b, v_p_a, v_p_b, v_p_c, v_p_m, v_w_out, v_ln_g, v_ln_b]
    deltas, new_ms, new_vs = [], [], []
    for k, (wk, gk, mk, vk) in enumerate(zip(ws, grads, ms, vs)):
        shape = wk.shape
        two_d = (1, shape[0]) if wk.ndim == 1 else (int(np.prod(shape[:-1])), shape[-1])
        dk_, nm_, nv_ = _adamw(wk.reshape(two_d), gk.reshape(two_d), mk.reshape(two_d), vk.reshape(two_d), f"adamw_{k}")
        deltas.append(dk_.reshape(shape))
        new_ms.append(nm_.reshape(shape))
        new_vs.append(nv_.reshape(shape))
        grads[k] = gk.reshape(shape)
    return (loss, grad_x[None], *grads, *deltas, *new_ms, *new_vs)


def _gate_mul(proj, o, gblk, name):
    s, w = o.shape
    t = T_ELEM

    def body(g_ref, o_ref, z_ref):
        z_ref[...] = (o_ref[...] * _silu(g_ref[...])).astype(BF16)

    row = pl.BlockSpec((t, w), lambda i: (i, 0))
    return pl.pallas_call(body, name=name, grid=(s // t,), in_specs=[pl.BlockSpec((t, w), lambda i: (i, gblk)), row],
                          out_specs=row, out_shape=jax.ShapeDtypeStruct((s, w), BF16),
                          compiler_params=_params(("parallel",)))(proj, o)
```

```python
import functools
import math

import numpy as np
import jax
import jax.numpy as jnp
from jax import lax
from jax.experimental import pallas as pl
from jax.experimental.pallas import tpu as pltpu

F32 = jnp.float32
BF16 = jnp.bfloat16
MESH = pl.DeviceIdType.MESH
ANY = pl.BlockSpec(memory_space=pl.ANY)

LN_EPS = 1e-5
NEG_BIG = -1e30
HEAD_DIM = 64
LANES = 128
CONV_A = 31
CONV_B = 3
HALO_A = 32
HALO_B = 8
CHUNK = 32
VMEM_LIMIT = 60 * 1024 * 1024

ADAM_LR, ADAM_B1, ADAM_B2, ADAM_EPS, ADAM_WD, ADAM_STEP = 0.001, 0.9, 0.999, 1e-08, 0.01, 10

T_MM = 512
T_A = 128
T_B = 256
T_ATT_Q = 512
T_ATT_K = 1024
T_CUM = 512
T_M = 512
T_MERGE = 256
T_DX = 256
T_ELEM = 512


def _pick(n, prefs):
    for p in prefs:
        if n % p == 0:
            return p
    return n


def _params(sem=None):
    return pltpu.CompilerParams(dimension_semantics=sem, vmem_limit_bytes=VMEM_LIMIT)


def _sigmoid(x):
    return jax.nn.sigmoid(x)


def _silu(x):
    return x * _sigmoid(x)


def _dsilu(x):
    s = _sigmoid(x)
    return s * (1.0 + x * (1.0 - s))


def _dot(a, b, dims, precision=None):
    return lax.dot_general(a, b, (dims, ((), ())), preferred_element_type=F32, precision=precision)


NN = ((1,), (0,))
NT = ((1,), (1,))
TN = ((0,), (0,))


def _iota(shape, dim):
    return lax.broadcasted_iota(jnp.int32, shape, dim)


def _mm(a, b, *, ta=False, tb=False, add=None, out_dtype=F32, tm=None, name):
    m = a.shape[1] if ta else a.shape[0]
    k = a.shape[0] if ta else a.shape[1]
    n = b.shape[0] if tb else b.shape[1]
    if tm is None:
        tm = _pick(m, (1024, 512, 256)) if ta else _pick(m, (T_MM, 256))
    tn = _pick(n, (1152, 1024, 768, 512, 384, 256, 128))
    tk = _pick(k, (1024, 512, 256))
    nk = k // tk
    dims = ((0,) if ta else (1,), (1,) if tb else (0,))

    def body(*refs):
        if add is None:
            a_ref, b_ref, o_ref, acc_ref = refs
        else:
            a_ref, b_ref, add_ref, o_ref, acc_ref = refs
        kk = pl.program_id(2)
        p = _dot(a_ref[...].astype(BF16), b_ref[...].astype(BF16), dims)

        @pl.when(kk == 0)
        def _():
            acc_ref[...] = p

        @pl.when(kk > 0)
        def _():
            acc_ref[...] += p

        @pl.when(kk == nk - 1)
        def _():
            r = acc_ref[...]
            if add is not None:
                r = r + add_ref[...]
            o_ref[...] = r.astype(out_dtype)

    a_spec = (pl.BlockSpec((tk, tm), lambda j, i, kk: (kk, i)) if ta
              else pl.BlockSpec((tm, tk), lambda j, i, kk: (i, kk)))
    b_spec = (pl.BlockSpec((tn, tk), lambda j, i, kk: (j, kk)) if tb
              else pl.BlockSpec((tk, tn), lambda j, i, kk: (kk, j)))
    o_spec = pl.BlockSpec((tm, tn), lambda j, i, kk: (i, j))
    in_specs = [a_spec, b_spec] + ([o_spec] if add is not None else [])
    args = (a, b) + ((add,) if add is not None else ())
    return pl.pallas_call(
        body, name=name, grid=(n // tn, m // tm, nk), in_specs=in_specs, out_specs=o_spec,
        out_shape=jax.ShapeDtypeStruct((m, n), out_dtype),
        scratch_shapes=[pltpu.VMEM((tm, tn), F32)],
        compiler_params=_params(("parallel", "parallel", "arbitrary")),
    )(*args)


def _input_grad(dparts, wparts, add, name):
    s, d = add.shape
    t = T_DX
    n_p = len(dparts)

    def body(*refs):
        d_refs, w_hbm = refs[:n_p], refs[n_p:2 * n_p]
        add_ref, o_ref = refs[2 * n_p], refs[2 * n_p + 1]
        w_v = refs[2 * n_p + 2:]

        @pl.when(pl.program_id(0) == 0)
        def _():
            for p in range(n_p):
                pltpu.sync_copy(w_hbm[p], w_v[p])

        acc = add_ref[...]
        for p in range(n_p):
            acc = acc + _dot(d_refs[p][...], w_v[p][...], NT)
        o_ref[...] = acc

    row = pl.BlockSpec((t, d), lambda i: (i, 0))
    in_specs = ([pl.BlockSpec((t, dp.shape[1]), lambda i: (i, 0)) for dp in dparts] + [ANY] * n_p + [row])
    return pl.pallas_call(
        body, name=name, grid=(s // t,), in_specs=in_specs, out_specs=row,
        out_shape=jax.ShapeDtypeStruct((s, d), F32),
        scratch_shapes=[pltpu.VMEM(w.shape, BF16) for w in wparts],
        compiler_params=_params(("arbitrary",)),
    )(*dparts, *wparts, add)


def _ln_rows(x, g, b, name):
    r, d = x.shape
    t = _pick(r, (256,))

    def body(x_ref, g_ref, b_ref, o_ref):
        xv = x_ref[...]
        mu = jnp.mean(xv, axis=-1, keepdims=True)
        dv = xv - mu
        var = jnp.mean(dv * dv, axis=-1, keepdims=True)
        o_ref[...] = dv * lax.rsqrt(var + LN_EPS) * g_ref[...] + b_ref[...]

    row = pl.BlockSpec((t, d), lambda i: (i, 0))
    vec = pl.BlockSpec((1, d), lambda i: (0, 0))
    return pl.pallas_call(body, name=name, grid=(r // t,), in_specs=[row, vec, vec], out_specs=row,
                          out_shape=jax.ShapeDtypeStruct((r, d), F32), compiler_params=_params(("parallel",)))(x, g, b)


def _ln_rows_param_grads(x, dy, name):
    r, d = x.shape

    def body(x_ref, dy_ref, dg_ref, db_ref):
        xv = x_ref[...]
        mu = jnp.mean(xv, axis=-1, keepdims=True)
        dv = xv - mu
        var = jnp.mean(dv * dv, axis=-1, keepdims=True)
        xh = dv * lax.rsqrt(var + LN_EPS)
        dg_ref[...] = jnp.sum(dy_ref[...] * xh, axis=0, keepdims=True)
        db_ref[...] = jnp.sum(dy_ref[...], axis=0, keepdims=True)

    full = pl.BlockSpec((r, d), lambda i: (0, 0))
    vec = pl.BlockSpec((1, d), lambda i: (0, 0))
    o = jax.ShapeDtypeStruct((1, d), F32)
    return pl.pallas_call(body, name=name, grid=(1,), in_specs=[full, full], out_specs=[vec, vec],
                          out_shape=[o, o], compiler_params=_params(("arbitrary",)))(x, dy)


def _conv_a_chunk(glu_ref, cw_ref, r0):
    acc = cw_ref[0:1, :] * glu_ref[pl.ds(r0 + 2, CHUNK), :]
    for k in range(1, CONV_A):
        acc = acc + cw_ref[k:k + 1, :] * glu_ref[pl.ds(r0 + 2 + k, CHUNK), :]
    return acc


def _ln_stats(c):
    mu = jnp.mean(c, axis=-1, keepdims=True)
    d = c - mu
    var = jnp.mean(d * d, axis=-1, keepdims=True)
    rstd = lax.rsqrt(var + LN_EPS)
    return d * rstd, rstd


def _branch_a_fwd(proj, cw, cb, lg, lb, name):
    s = proj.shape[0]
    t = T_A
    w = cw.shape[1]
    r = t // HALO_A

    def body(u_ref, v_ref, gt_ref, hu_ref, hv_ref, cw_ref, cb_ref, lg_ref, lb_ref, z_ref, conv_ref, glu):
        i = pl.program_id(0)
        hglu = hu_ref[...] * _sigmoid(hv_ref[...])
        glu[0:HALO_A, :] = jnp.where(i > 0, hglu, 0.0)
        glu[HALO_A:HALO_A + t, :] = u_ref[...] * _sigmoid(v_ref[...])
        for c in range(t // CHUNK):
            r0 = c * CHUNK
            conv = _conv_a_chunk(glu, cw_ref, r0) + cb_ref[...]
            conv_ref[r0:r0 + CHUNK, :] = conv
            xh, _ = _ln_stats(conv)
            a3 = _silu(xh * lg_ref[...] + lb_ref[...])
            z_ref[r0:r0 + CHUNK, :] = (a3 * _silu(gt_ref[r0:r0 + CHUNK, :])).astype(BF16)

    def cur(col):
        return pl.BlockSpec((t, w), lambda i, col=col: (i, col))

    def prev(col):
        return pl.BlockSpec((HALO_A, w), lambda i, col=col: (jnp.maximum(i * r - 1, 0), col))

    vec = pl.BlockSpec((1, w), lambda i: (0, 0))
    return pl.pallas_call(
        body, name=name, grid=(s // t,),
        in_specs=[cur(0), cur(1), cur(2), prev(0), prev(1), pl.BlockSpec((HALO_A, w), lambda i: (0, 0)), vec, vec, vec],
        out_specs=[pl.BlockSpec((t, w), lambda i: (i, 0)), pl.BlockSpec((t, w), lambda i: (i, 0))],
        out_shape=[jax.ShapeDtypeStruct((s, w), BF16), jax.ShapeDtypeStruct((s, w), F32)],
        scratch_shapes=[pltpu.VMEM((HALO_A + t, w), F32)],
        compiler_params=_params(("parallel",)),
    )(proj, proj, proj, proj, proj, cw, cb, lg, lb)


def _branch_a_bwd(proj, conv, dz, cw, lg, lb, name):
    s = proj.shape[0]
    t = T_A
    w = cw.shape[1]
    r = t // HALO_A
    n = s // t
    nblk = s // HALO_A
    ext = t + HALO_A

    def body(u_ref, v_ref, gt_ref, dz_ref, conv_ref, pu_ref, pv_ref, ngt_ref, ndz_ref, nconv_ref,
             cw_ref, lg_ref, lb_ref, da_ref, dw_ref, dcb_ref, dlg_ref, dlb_ref, glu, dc, dw8):
        i = pl.program_id(0)

        @pl.when(i == 0)
        def _():
            dw8[...] = jnp.zeros_like(dw8)
            dcb_ref[...] = jnp.zeros_like(dcb_ref)
            dlg_ref[...] = jnp.zeros_like(dlg_ref)
            dlb_ref[...] = jnp.zeros_like(dlb_ref)

        glu[0:HALO_A, :] = jnp.where(i > 0, pu_ref[...] * _sigmoid(pv_ref[...]), 0.0)
        glu[HALO_A:HALO_A + t, :] = u_ref[...] * _sigmoid(v_ref[...])
        has_next = i < n - 1
        dcb = jnp.zeros((1, w), F32)
        dlg = jnp.zeros((1, w), F32)
        dlb = jnp.zeros((1, w), F32)
        for c in range(ext // CHUNK):
            r0 = c * CHUNK
            own = r0 < t
            xh, rstd = _ln_stats(conv_ref[r0:r0 + CHUNK, :] if own else nconv_ref[...])
            a2 = xh * lg_ref[...] + lb_ref[...]
            if own:
                gt = gt_ref[r0:r0 + CHUNK, :]
                dzc = dz_ref[r0:r0 + CHUNK, :]
            else:
                gt = ngt_ref[...]
                dzc = ndz_ref[...]
            da2 = dzc * _silu(gt) * _dsilu(a2)
            dxh = da2 * lg_ref[...]
            dconv = rstd * (dxh - jnp.mean(dxh, axis=-1, keepdims=True)
                            - xh * jnp.mean(dxh * xh, axis=-1, keepdims=True))
            if own:
                dc[r0:r0 + CHUNK, :] = dconv
                da_ref[r0:r0 + CHUNK, 2 * w:3 * w] = (dzc * _silu(a2) * _dsilu(gt)).astype(BF16)
                dcb = dcb + jnp.sum(dconv, axis=0, keepdims=True)
                dlg = dlg + jnp.sum(da2 * xh, axis=0, keepdims=True)
                dlb = dlb + jnp.sum(da2, axis=0, keepdims=True)
            else:
                dc[r0:r0 + CHUNK, :] = jnp.where(has_next, dconv, 0.0)
        dcb_ref[...] += dcb
        dlg_ref[...] += dlg
        dlb_ref[...] += dlb
        for c in range(t // CHUNK):
            r0 = c * CHUNK
            dcc = dc[r0:r0 + CHUNK, :]
            dglu = cw_ref[0:1, :] * dc[pl.ds(r0 + CONV_A - 1, CHUNK), :]
            for k in range(1, CONV_A):
                dglu = dglu + cw_ref[k:k + 1, :] * dc[pl.ds(r0 + CONV_A - 1 - k, CHUNK), :]
            for k in range(CONV_A):
                prod = dcc * glu[pl.ds(r0 + 2 + k, CHUNK), :]
                dw8[k] += jnp.sum(prod.reshape(CHUNK // 8, 8, w), axis=0)
            sv = _sigmoid(v_ref[r0:r0 + CHUNK, :])
            da_ref[r0:r0 + CHUNK, 0:w] = (dglu * sv).astype(BF16)
            da_ref[r0:r0 + CHUNK, w:2 * w] = (dglu * u_ref[r0:r0 + CHUNK, :] * sv * (1.0 - sv)).astype(BF16)

        @pl.when(i == n - 1)
        def _():
            dw_ref[...] = jnp.sum(dw8[...], axis=1)

    def cur(col):
        return pl.BlockSpec((t, w), lambda i, col=col: (i, col))

    def prev(col):
        return pl.BlockSpec((HALO_A, w), lambda i, col=col: (jnp.maximum(i * r - 1, 0), col))

    def nxt(col):
        return pl.BlockSpec((HALO_A, w), lambda i, col=col: (jnp.minimum((i + 1) * r, nblk - 1), col))

    own_rows = pl.BlockSpec((t, w), lambda i: (i, 0))
    vec = pl.BlockSpec((1, w), lambda i: (0, 0))
    vo = jax.ShapeDtypeStruct((1, w), F32)
    return pl.pallas_call(
        body, name=name, grid=(n,),
        in_specs=[cur(0), cur(1), cur(2), own_rows, own_rows, prev(0), prev(1), nxt(2), nxt(0), nxt(0),
                  pl.BlockSpec((HALO_A, w), lambda i: (0, 0)), vec, vec],
        out_specs=[pl.BlockSpec((t, 3 * w), lambda i: (i, 0)), pl.BlockSpec((HALO_A, w), lambda i: (0, 0)), vec, vec, vec],
        out_shape=[jax.ShapeDtypeStruct((s, 3 * w), BF16), jax.ShapeDtypeStruct((HALO_A, w), F32), vo, vo, vo],
        scratch_shapes=[pltpu.VMEM((HALO_A + t, w), F32), pltpu.VMEM((ext, w), F32),
                        pltpu.VMEM((HALO_A, 8, w), F32)],
        compiler_params=_params(("arbitrary",)),
    )(proj, proj, proj, dz, conv, proj, proj, proj, dz, conv, cw, lg, lb)


def _conv_b(u_ext, cw_ref, t):
    acc = cw_ref[0:1, :] * u_ext[pl.ds(HALO_B - 2, t), :]
    for k in range(1, CONV_B):
        acc = acc + cw_ref[k:k + 1, :] * u_ext[pl.ds(HALO_B - 2 + k, t), :]
    return acc


def _branch_b_fwd(proj, cw, name):
    s = proj.shape[0]
    t = T_B
    w = cw.shape[1]
    r = t // HALO_B

    def body(h_ref, b_ref, c_ref, gt_ref, ph_ref, pc_ref, cw_ref, z_ref, u_ext):
        i = pl.program_id(0)
        u_ext[0:HALO_B, :] = jnp.where(i > 0, pc_ref[...] * ph_ref[...], 0.0)
        u_ext[HALO_B:HALO_B + t, :] = c_ref[...] * h_ref[...]
        cv = _conv_b(u_ext, cw_ref, t)
        z_ref[...] = (b_ref[...] * cv * _silu(gt_ref[...])).astype(BF16)

    def cur(col):
        return pl.BlockSpec((t, w), lambda i, col=col: (i, col))

    def prev(col):
        return pl.BlockSpec((HALO_B, w), lambda i, col=col: (jnp.maximum(i * r - 1, 0), col))

    return pl.pallas_call(
        body, name=name, grid=(s // t,),
        in_specs=[cur(3), cur(4), cur(5), cur(6), prev(3), prev(5), pl.BlockSpec((HALO_B, w), lambda i: (0, 0))],
        out_specs=pl.BlockSpec((t, w), lambda i: (i, 0)),
        out_shape=jax.ShapeDtypeStruct((s, w), BF16),
        scratch_shapes=[pltpu.VMEM((HALO_B + t, w), F32)],
        compiler_params=_params(("parallel",)),
    )(proj, proj, proj, proj, proj, proj, cw)


def _branch_b_bwd(proj, dz, cw, name):
    s = proj.shape[0]
    t = T_B
    w = cw.shape[1]
    r = t // HALO_B
    n = s // t
    nblk = s // HALO_B

    def body(h_ref, b_ref, c_ref, gt_ref, dz_ref, ph_ref, pc_ref, nb_ref, ngt_ref, ndz_ref, cw_ref,
             db_ref, dw_ref, u_ext, dcv_ext, dw8):
        i = pl.program_id(0)

        @pl.when(i == 0)
        def _():
            dw8[...] = jnp.zeros_like(dw8)

        u_ext[0:HALO_B, :] = jnp.where(i > 0, pc_ref[...] * ph_ref[...], 0.0)
        u_ext[HALO_B:HALO_B + t, :] = c_ref[...] * h_ref[...]
        cv = _conv_b(u_ext, cw_ref, t)
        gt = gt_ref[...]
        dhb = dz_ref[...] * _silu(gt)
        db_ref[:, 3 * w:4 * w] = (dz_ref[...] * b_ref[...] * cv * _dsilu(gt)).astype(BF16)
        db_ref[:, w:2 * w] = (dhb * cv).astype(BF16)
        dcv = dhb * b_ref[...]
        dcv_ext[0:t, :] = dcv
        ndcv = ndz_ref[...] * _silu(ngt_ref[...]) * nb_ref[...]
        dcv_ext[t:t + HALO_B, :] = jnp.where(i < n - 1, ndcv, 0.0)
        du = cw_ref[0:1, :] * dcv_ext[pl.ds(2, t), :]
        for k in range(1, CONV_B):
            du = du + cw_ref[k:k + 1, :] * dcv_ext[pl.ds(2 - k, t), :]
        db_ref[:, 2 * w:3 * w] = (du * h_ref[...]).astype(BF16)
        db_ref[:, 0:w] = (du * c_ref[...]).astype(BF16)
        for k in range(CONV_B):
            prod = dcv * u_ext[pl.ds(HALO_B - 2 + k, t), :]
            dw8[k] += jnp.sum(prod.reshape(t // 8, 8, w), axis=0)

        @pl.when(i == n - 1)
        def _():
            dw_ref[...] = jnp.sum(dw8[...], axis=1)

    def cur(col):
        return pl.BlockSpec((t, w), lambda i, col=col: (i, col))

    def prev(col):
        return pl.BlockSpec((HALO_B, w), lambda i, col=col: (jnp.maximum(i * r - 1, 0), col))

    def nxt(col):
        return pl.BlockSpec((HALO_B, w), lambda i, col=col: (jnp.minimum((i + 1) * r, nblk - 1), col))

    return pl.pallas_call(
        body, name=name, grid=(n,),
        in_specs=[cur(3), cur(4), cur(5), cur(6), pl.BlockSpec((t, w), lambda i: (i, 0)),
                  prev(3), prev(5), nxt(4), nxt(6),
                  pl.BlockSpec((HALO_B, w), lambda i: (jnp.minimum((i + 1) * r, nblk - 1), 0)),
                  pl.BlockSpec((HALO_B, w), lambda i: (0, 0))],
        out_specs=[pl.BlockSpec((t, 4 * w), lambda i: (i, 0)), pl.BlockSpec((HALO_B, w), lambda i: (0, 0))],
        out_shape=[jax.ShapeDtypeStruct((s, 4 * w), BF16), jax.ShapeDtypeStruct((HALO_B, w), F32)],
        scratch_shapes=[pltpu.VMEM((HALO_B + t, w), F32), pltpu.VMEM((t + HALO_B, w), F32),
                        pltpu.VMEM((HALO_B, 8, w), F32)],
        compiler_params=_params(("arbitrary",)),
    )(proj, proj, proj, proj, dz, proj, proj, proj, proj, dz, cw)


def _forget_prep(proj, bf, fblk, name):
    s = proj.shape[0]
    t = T_CUM

    def body(f_ref, bf_ref, cum_ref, cumt_ref, sgt_ref, carry):
        i = pl.program_id(0)

        @pl.when(i == 0)
        def _():
            carry[...] = jnp.zeros_like(carry)

        z = f_ref[...] + bf_ref[...]
        logf = jnp.minimum(z, 0.0) - jnp.log1p(jnp.exp(-jnp.abs(z)))
        tri = (_iota((t, t), 0) >= _iota((t, t), 1)).astype(F32)
        cum = _dot(tri, logf, NN, precision=lax.Precision.HIGHEST) + carry[0:1, :]
        carry[0:1, :] = cum[t - 1:t, :]
        cum_ref[...] = cum
        cumt_ref[...] = cum.T[0:8, :]
        sgt_ref[...] = _sigmoid(-z).T[0:8, :]

    return pl.pallas_call(
        body, name=name, grid=(s // t,),
        in_specs=[pl.BlockSpec((t, LANES), lambda i: (i, fblk)), pl.BlockSpec((1, LANES), lambda i: (0, 0))],
        out_specs=[pl.BlockSpec((t, LANES), lambda i: (i, 0)), pl.BlockSpec((8, t), lambda i: (0, i)),
                   pl.BlockSpec((8, t), lambda i: (0, i))],
        out_shape=[jax.ShapeDtypeStruct((s, LANES), F32), jax.ShapeDtypeStruct((8, s), F32),
                   jax.ShapeDtypeStruct((8, s), F32)],
        scratch_shapes=[pltpu.VMEM((8, LANES), F32)],
        compiler_params=_params(("arbitrary",)),
    )(proj, bf)


def _lane_pick(x, lane):
    return jnp.sum(jnp.where(_iota(x.shape, 1) == lane, x, 0.0), axis=1, keepdims=True)


def _sublane_pick(x, row):
    return jnp.sum(jnp.where(_iota(x.shape, 0) == row, x, 0.0), axis=0, keepdims=True)


def _head_mask(hh):
    lane = _iota((1, LANES), 1)
    return (lane >= HEAD_DIM * hh) & (lane < HEAD_DIM * (hh + 1))


def _causal_pairs(nq, ratio, kv_major):
    if kv_major:
        pairs = [(q, k) for k in range(nq // ratio) for q in range(k * ratio, nq)]
    else:
        pairs = [(q, k) for q in range(nq) for k in range(q // ratio + 1)]
    qs = np.asarray([p[0] for p in pairs], np.int32)
    ks = np.asarray([p[1] for p in pairs], np.int32)
    return qs, ks, (ks == qs // ratio).astype(np.int32)


def _fox_scores(qm, kb, cum_ref, cumt_ref, h, qi, ki, diag, tq, tk):
    cq0 = _lane_pick(cum_ref[0:1, :], h)
    sc = _dot(qm, kb, NT) + (cq0 - _sublane_pick(cumt_ref[...], h))
    if not diag:
        return sc
    causal = (_iota((tq, tk), 0) + (qi * tq - ki * tk)) >= _iota((tq, tk), 1)
    return jnp.where(causal, sc, NEG_BIG)


def _fox_fwd(proj, cum, cumt, qblk, kblk, vblk, name):
    s = proj.shape[0]
    tq, tk = T_ATT_Q, T_ATT_K
    n_pair = 4
    qi_np, ki_np, diag_np = _causal_pairs(s // tq, tk // tq, kv_major=False)
    scale = HEAD_DIM ** -0.5

    def body(qi_ref, ki_ref, diag_ref, q_ref, k_ref, v_ref, cum_ref, cumt_ref, o_ref, lse_ref, m_s, acc_s):
        hp = pl.program_id(0)
        step = pl.program_id(1)
        qi, ki, diag = qi_ref[step], ki_ref[step], diag_ref[step]

        @pl.when(ki == 0)
        def _():
            m_s[...] = jnp.full_like(m_s, NEG_BIG)
            acc_s[...] = jnp.zeros_like(acc_s)

        def update(on_diagonal):
            q = q_ref[...] * scale
            kb = k_ref[...].astype(BF16)
            v = v_ref[...]
            m_old = [m_s[0], m_s[1]]
            acc_old = [acc_s[0], acc_s[1]]
            scores, values = [], []
            for hh in range(2):
                hm = _head_mask(hh)
                qm = jnp.where(hm, q, 0.0).astype(BF16)
                values.append(jnp.where(hm, v, 1.0).astype(BF16))
                scores.append(_fox_scores(qm, kb, cum_ref, cumt_ref, 2 * hp + hh, qi, ki, on_diagonal, tq, tk))
            m_new = [jnp.maximum(m_old[hh], jnp.max(scores[hh], axis=1, keepdims=True)) for hh in range(2)]
            probs = [jnp.exp(scores[hh] - m_new[hh]).astype(BF16) for hh in range(2)]
            acc_new = [jnp.exp(m_old[hh] - m_new[hh]) * acc_old[hh] + _dot(probs[hh], values[hh], NN)
                       for hh in range(2)]
            for hh in range(2):
                acc_s[hh] = acc_new[hh]
                m_s[hh] = m_new[hh]

        @pl.when(diag == 0)
        def _():
            update(False)

        @pl.when(diag == 1)
        def _():
            update(True)
            lane = _iota((tq, LANES), 1)
            a0, a1 = acc_s[0], acc_s[1]
            o_ref[...] = jnp.where(lane < HEAD_DIM, a0 / pltpu.roll(a0, HEAD_DIM, axis=1),
                                   a1 / pltpu.roll(a1, HEAD_DIM, axis=1))
            lse0 = m_s[0] + jnp.log(a0[:, HEAD_DIM:HEAD_DIM + 1])
            lse1 = m_s[1] + jnp.log(a1[:, 0:1])
            lse_ref[0] = jnp.where(lane == 0, lse0, jnp.where(lane == 1, lse1, 0.0))

    grid_spec = pltpu.PrefetchScalarGridSpec(
        num_scalar_prefetch=3, grid=(n_pair, len(qi_np)),
        in_specs=[pl.BlockSpec((tq, LANES), lambda hp, st, qi, ki, dg: (qi[st], qblk + hp)),
                  pl.BlockSpec((tk, LANES), lambda hp, st, qi, ki, dg: (ki[st], kblk + hp)),
                  pl.BlockSpec((tk, LANES), lambda hp, st, qi, ki, dg: (ki[st], vblk + hp)),
                  pl.BlockSpec((tq, LANES), lambda hp, st, qi, ki, dg: (qi[st], 0)),
                  pl.BlockSpec((8, tk), lambda hp, st, qi, ki, dg: (0, ki[st]))],
        out_specs=[pl.BlockSpec((tq, LANES), lambda hp, st, qi, ki, dg: (qi[st], hp)),
                   pl.BlockSpec((1, tq, LANES), lambda hp, st, qi, ki, dg: (hp, qi[st], 0))],
        scratch_shapes=[pltpu.VMEM((2, tq, 1), F32), pltpu.VMEM((2, tq, LANES), F32)])
    return pl.pallas_call(
        body, name=name, grid_spec=grid_spec,
        out_shape=[jax.ShapeDtypeStruct((s, n_pair * LANES), F32), jax.ShapeDtypeStruct((n_pair, s, LANES), F32)],
        compiler_params=_params(("parallel", "arbitrary")),
    )(jnp.asarray(qi_np), jnp.asarray(ki_np), jnp.asarray(diag_np), proj, proj, proj, cum, cumt)


def _fox_bwd_prep(proj, dz, o, gblk, name):
    s, w = o.shape
    t = T_ELEM
    n_head = w // HEAD_DIM

    def body(gt_ref, dz_ref, o_ref, do_ref, dg_ref, dl_ref):
        gt = gt_ref[...]
        do = dz_ref[...] * _silu(gt)
        do_ref[...] = do
        dg_ref[...] = (dz_ref[...] * o_ref[...] * _dsilu(gt)).astype(BF16)
        sel = (_iota((w, LANES), 0) // HEAD_DIM == _iota((w, LANES), 1)).astype(F32)
        dl_ref[...] = _dot(do * o_ref[...], sel, NN, precision=lax.Precision.HIGHEST)

    assert n_head <= LANES
    row = pl.BlockSpec((t, w), lambda i: (i, 0))
    return pl.pallas_call(
        body, name=name, grid=(s // t,),
        in_specs=[pl.BlockSpec((t, w), lambda i: (i, gblk)), row, row],
        out_specs=[row, row, pl.BlockSpec((t, LANES), lambda i: (i, 0))],
        out_shape=[jax.ShapeDtypeStruct((s, w), F32), jax.ShapeDtypeStruct((s, w), BF16),
                   jax.ShapeDtypeStruct((s, LANES), F32)],
        compiler_params=_params(("parallel",)),
    )(proj, dz, o)


def _fox_bwd(proj, do, lse, delta, cum, cumt, qblk, kblk, vblk, name):
    s = proj.shape[0]
    tq, tk = T_ATT_Q, T_ATT_K
    nq = s // tq
    n_pair = 4
    qi_np, ki_np, diag_np = _causal_pairs(nq, tk // tq, kv_major=True)
    n_step = len(qi_np)
    scale = HEAD_DIM ** -0.5

    def body(qi_ref, ki_ref, diag_ref, q_ref, k_ref, v_ref, do_ref, lse_ref, dl_ref, cum_ref, cumt_ref,
             dq_ref, dk_ref, dv_ref, dct_ref, dcq_ref, dq_s, dk_s, dv_s, dc_s, dcq_s):
        hp = pl.program_id(0)
        step = pl.program_id(1)
        qi, ki, diag = qi_ref[step], ki_ref[step], diag_ref[step]

        @pl.when(step == 0)
        def _():
            dq_s[...] = jnp.zeros_like(dq_s)
            dcq_s[...] = jnp.zeros_like(dcq_s)

        @pl.when(qi == ki * (tk // tq))
        def _():
            dk_s[...] = jnp.zeros_like(dk_s)
            dv_s[...] = jnp.zeros_like(dv_s)
            dc_s[...] = jnp.zeros_like(dc_s)

        def update(on_diagonal):
            q = q_ref[...] * scale
            do = do_ref[...]
            kb = k_ref[...].astype(BF16)
            vb = v_ref[...].astype(BF16)
            sub = _iota((8, tk), 0)
            dq_new = jnp.zeros((tq, LANES), F32)
            dcq_new = jnp.zeros((tq, LANES), F32)
            lane = _iota((tq, LANES), 1)
            for hh in range(2):
                h = 2 * hp + hh
                hm = _head_mask(hh)
                qm = jnp.where(hm, q, 0.0).astype(BF16)
                dom = jnp.where(hm, do, 0.0).astype(BF16)
                sc = _fox_scores(qm, kb, cum_ref, cumt_ref, h, qi, ki, on_diagonal, tq, tk)
                p = jnp.exp(sc - _lane_pick(lse_ref[0], hh))
                dv_s[...] += _dot(p.astype(BF16), dom, TN)
                dp = _dot(dom, vb, NT)
                ds = p * (dp - _lane_pick(dl_ref[...], h))
                dc_s[...] += jnp.where(sub == h, -jnp.sum(ds, axis=0, keepdims=True), 0.0)
                dcq_new = dcq_new + jnp.where(lane == h, jnp.sum(ds, axis=1, keepdims=True), 0.0)
                dsb = ds.astype(BF16)
                dk_s[...] += _dot(dsb, qm, TN)
                dq_new = dq_new + jnp.where(hm, _dot(dsb, kb, NN), 0.0)
            row0 = pl.multiple_of(qi * tq, tq)
            dq_s[pl.ds(row0, tq), :] += dq_new * scale
            dcq_s[pl.ds(row0, tq), :] += dcq_new

        @pl.when(diag == 0)
        def _():
            update(False)

        @pl.when(diag == 1)
        def _():
            update(True)

        @pl.when(qi == nq - 1)
        def _():
            dk_ref[...] = dk_s[...].astype(BF16)
            dv_ref[...] = dv_s[...].astype(BF16)
            dct_ref[0] = dc_s[...]

        @pl.when(step == n_step - 1)
        def _():
            dq_ref[...] = dq_s[...].astype(BF16)
            dcq_ref[0] = dcq_s[...]

    grid_spec = pltpu.PrefetchScalarGridSpec(
        num_scalar_prefetch=3, grid=(n_pair, n_step),
        in_specs=[pl.BlockSpec((tq, LANES), lambda hp, st, qi, ki, dg: (qi[st], qblk + hp)),
                  pl.BlockSpec((tk, LANES), lambda hp, st, qi, ki, dg: (ki[st], kblk + hp)),
                  pl.BlockSpec((tk, LANES), lambda hp, st, qi, ki, dg: (ki[st], vblk + hp)),
                  pl.BlockSpec((tq, LANES), lambda hp, st, qi, ki, dg: (qi[st], hp)),
                  pl.BlockSpec((1, tq, LANES), lambda hp, st, qi, ki, dg: (hp, qi[st], 0)),
                  pl.BlockSpec((tq, LANES), lambda hp, st, qi, ki, dg: (qi[st], 0)),
                  pl.BlockSpec((tq, LANES), lambda hp, st, qi, ki, dg: (qi[st], 0)),
                  pl.BlockSpec((8, tk), lambda hp, st, qi, ki, dg: (0, ki[st]))],
        out_specs=[pl.BlockSpec((s, LANES), lambda hp, st, qi, ki, dg: (0, hp)),
                   pl.BlockSpec((tk, LANES), lambda hp, st, qi, ki, dg: (ki[st], hp)),
                   pl.BlockSpec((tk, LANES), lambda hp, st, qi, ki, dg: (ki[st], hp)),
                   pl.BlockSpec((1, 8, tk), lambda hp, st, qi, ki, dg: (hp, 0, ki[st])),
                   pl.BlockSpec((1, s, LANES), lambda hp, st, qi, ki, dg: (hp, 0, 0))],
        scratch_shapes=[pltpu.VMEM((s, LANES), F32), pltpu.VMEM((tk, LANES), F32), pltpu.VMEM((tk, LANES), F32),
                        pltpu.VMEM((8, tk), F32), pltpu.VMEM((s, LANES), F32)])
    w = n_pair * LANES
    return pl.pallas_call(
        body, name=name, grid_spec=grid_spec,
        out_shape=[jax.ShapeDtypeStruct((s, w), BF16), jax.ShapeDtypeStruct((s, w), BF16),
                   jax.ShapeDtypeStruct((s, w), BF16), jax.ShapeDtypeStruct((n_pair, 8, s), F32),
                   jax.ShapeDtypeStruct((n_pair, s, LANES), F32)],
        compiler_params=_params(("parallel", "arbitrary")),
    )(jnp.asarray(qi_np), jnp.asarray(ki_np), jnp.asarray(diag_np), proj, proj, proj, do, lse, delta, cum, cumt)


def _forget_bwd(dcumt4, dcumq4, sgt, name):
    s = sgt.shape[1]
    t = T_CUM
    n = s // t

    def body(d_ref, dq_ref, sg_ref, df_ref, dbf_ref, carry):
        i = pl.program_id(0)

        @pl.when(i == 0)
        def _():
            carry[...] = jnp.zeros_like(carry)
            dbf_ref[...] = jnp.zeros_like(dbf_ref)

        dq = dq_ref[0] + dq_ref[1] + dq_ref[2] + dq_ref[3]
        d = d_ref[0] + d_ref[1] + d_ref[2] + d_ref[3] + dq.T[0:8, :]
        upper = (_iota((t, t), 0) >= _iota((t, t), 1)).astype(F32)
        dlog = _dot(d, upper, NN, precision=lax.Precision.HIGHEST) + carry[:, 0:1]
        carry[...] += jnp.sum(d, axis=1, keepdims=True)
        dzt = dlog * sg_ref[...]
        dbf_ref[...] += jnp.sum(dzt, axis=1, keepdims=True)
        padded = jnp.concatenate([dzt, jnp.zeros((LANES - 8, t), F32)], axis=0)
        df_ref[...] = padded.T.astype(BF16)

    return pl.pallas_call(
        body, name=name, grid=(n,),
        in_specs=[pl.BlockSpec((4, 8, t), lambda i: (0, 0, n - 1 - i)),
                  pl.BlockSpec((4, t, LANES), lambda i: (0, n - 1 - i, 0)),
                  pl.BlockSpec((8, t), lambda i: (0, n - 1 - i))],
        out_specs=[pl.BlockSpec((t, LANES), lambda i: (n - 1 - i, 0)), pl.BlockSpec((8, LANES), lambda i: (0, 0))],
        out_shape=[jax.ShapeDtypeStruct((s, LANES), BF16), jax.ShapeDtypeStruct((8, LANES), F32)],
        scratch_shapes=[pltpu.VMEM((8, LANES), F32)],
        compiler_params=_params(("arbitrary",)),
    )(dcumt4, dcumq4, sgt)


def _mem_softmax(qm, kp):
    sc = _dot(qm, kp, NT) * (HEAD_DIM ** -0.5)
    e = jnp.exp(sc - jnp.max(sc, axis=1, keepdims=True))
    return e / jnp.sum(e, axis=1, keepdims=True)


def _branch_m_fwd(proj, mkv, mblk, name):
    s = proj.shape[0]
    t = T_M
    mw = mkv.shape[1] // 2
    ml = mkv.shape[0]

    def body(m_ref, kv_ref, z_ref):
        outs = []
        for pr in range(mw // LANES):
            qp = m_ref[:, pr * LANES:(pr + 1) * LANES]
            kp = kv_ref[:, pr * LANES:(pr + 1) * LANES].astype(BF16)
            vp = kv_ref[:, mw + pr * LANES:mw + (pr + 1) * LANES].astype(BF16)
            oh = []
            for hh in range(2):
                qm = jnp.where(_head_mask(hh), qp, 0.0).astype(BF16)
                oh.append(_dot(_mem_softmax(qm, kp).astype(BF16), vp, NN))
            outs.append(jnp.where(_iota((t, LANES), 1) < HEAD_DIM, oh[0], oh[1]))
        o = jnp.concatenate(outs, axis=1)
        z_ref[...] = (o * _silu(m_ref[:, mw:2 * mw])).astype(BF16)

    return pl.pallas_call(
        body, name=name, grid=(s // t,),
        in_specs=[pl.BlockSpec((t, 2 * mw), lambda i: (i, mblk)), pl.BlockSpec((ml, 2 * mw), lambda i: (0, 0))],
        out_specs=pl.BlockSpec((t, mw), lambda i: (i, 0)),
        out_shape=jax.ShapeDtypeStruct((s, mw), BF16),
        compiler_params=_params(("parallel",)),
    )(proj, mkv)


def _branch_m_bwd(proj, mkv, dz, mblk, name):
    s = proj.shape[0]
    t = T_M
    mw = mkv.shape[1] // 2
    ml = mkv.shape[0]
    scale = HEAD_DIM ** -0.5

    def body(m_ref, kv_ref, dz_ref, dm_ref, dkv_ref):
        i = pl.program_id(0)

        @pl.when(i == 0)
        def _():
            dkv_ref[...] = jnp.zeros_like(dkv_ref)

        gt = m_ref[:, mw:2 * mw]
        dz = dz_ref[...]
        do = dz * _silu(gt)
        outs = []
        for pr in range(mw // LANES):
            cols = slice(pr * LANES, (pr + 1) * LANES)
            vcols = slice(mw + pr * LANES, mw + (pr + 1) * LANES)
            qp = m_ref[:, cols]
            kp = kv_ref[:, cols].astype(BF16)
            vp = kv_ref[:, vcols].astype(BF16)
            dop = do[:, cols]
            oh = []
            dq = jnp.zeros((t, LANES), F32)
            dk = jnp.zeros((ml, LANES), F32)
            dv = jnp.zeros((ml, LANES), F32)
            for hh in range(2):
                hm = _head_mask(hh)
                qm = jnp.where(hm, qp, 0.0).astype(BF16)
                dom = jnp.where(hm, dop, 0.0).astype(BF16)
                p = _mem_softmax(qm, kp)
                pb = p.astype(BF16)
                oh.append(_dot(pb, vp, NN))
                dv = dv + _dot(pb, dom, TN)
                dp = _dot(dom, vp, NT)
                ds = p * (dp - jnp.sum(dp * p, axis=1, keepdims=True))
                dsb = (ds * scale).astype(BF16)
                dq = dq + jnp.where(hm, _dot(dsb, kp, NN), 0.0)
                dk = dk + _dot(dsb, qm, TN)
            outs.append(jnp.where(_iota((t, LANES), 1) < HEAD_DIM, oh[0], oh[1]))
            dm_ref[:, cols] = dq.astype(BF16)
            dkv_ref[:, cols] += dk
            dkv_ref[:, vcols] += dv
        o = jnp.concatenate(outs, axis=1)
        dm_ref[:, mw:2 * mw] = (dz * o * _dsilu(gt)).astype(BF16)

    return pl.pallas_call(
        body, name=name, grid=(s // t,),
        in_specs=[pl.BlockSpec((t, 2 * mw), lambda i: (i, mblk)), pl.BlockSpec((ml, 2 * mw), lambda i: (0, 0)),
                  pl.BlockSpec((t, mw), lambda i: (i, 0))],
        out_specs=[pl.BlockSpec((t, 2 * mw), lambda i: (i, 0)), pl.BlockSpec((ml, 2 * mw), lambda i: (0, 0))],
        out_shape=[jax.ShapeDtypeStruct((s, 2 * mw), BF16), jax.ShapeDtypeStruct((ml, 2 * mw), F32)],
        compiler_params=_params(("arbitrary",)),
    )(proj, mkv, dz)


def _merge_parts(z_refs, g_refs, pcat, bounds):
    ys, sgs = [], []
    merged = None
    for zr, gr, (lo, hi) in zip(z_refs, g_refs, bounds):
        y = _dot(zr[...], pcat[lo:hi, :], NN)
        sg = _sigmoid(gr[...])
        ys.append(y)
        sgs.append(sg)
        merged = sg * y if merged is None else merged + sg * y
    return ys, sgs, merged


def _branch_bounds(zs):
    bounds, lo = [], 0
    for z in zs:
        bounds.append((lo, lo + z.shape[1]))
        lo += z.shape[1]
    return bounds


def _matmul_copies(x, name):
    s, d = x.shape
    t = T_MERGE

    def body(x_ref, b_ref, bt_ref):
        b_ref[...] = x_ref[...].astype(BF16)
        bt_ref[...] = x_ref[...].T.astype(BF16)

    return pl.pallas_call(
        body, name=name, grid=(s // t,), in_specs=[pl.BlockSpec((t, d), lambda i: (i, 0))],
        out_specs=[pl.BlockSpec((t, d), lambda i: (i, 0)), pl.BlockSpec((d, t), lambda i: (0, i))],
        out_shape=[jax.ShapeDtypeStruct((s, d), BF16), jax.ShapeDtypeStruct((d, s), BF16)],
        compiler_params=_params(("parallel",)))(x)


def _merge_fwd(zs, proj, gblk, pcat, wout, x, lng, lnb, alpha, name):
    s, d = x.shape
    t = T_MERGE
    bounds = _branch_bounds(zs)

    def body(*refs):
        z_refs, g_refs = refs[0:4], refs[4:8]
        pcat_hbm, wout_hbm, x_ref, lng_ref, lnb_ref, y_ref, yb_ref, ybt_ref, pcat_v, wout_v = refs[8:]

        @pl.when(pl.program_id(0) == 0)
        def _():
            pltpu.sync_copy(pcat_hbm, pcat_v)
            pltpu.sync_copy(wout_hbm, wout_v)

        _, _, merged = _merge_parts(z_refs, g_refs, pcat_v, bounds)
        h = alpha * x_ref[...] + _dot(merged.astype(BF16), wout_v[...], NN)
        xh, _ = _ln_stats(h)
        y = xh * lng_ref[...] + lnb_ref[...]
        y_ref[...] = y
        yb_ref[...] = y.astype(BF16)
        ybt_ref[...] = y.T.astype(BF16)

    row = pl.BlockSpec((t, d), lambda i: (i, 0))
    vec = pl.BlockSpec((1, d), lambda i: (0, 0))
    in_specs = ([pl.BlockSpec((t, z.shape[1]), lambda i: (i, 0)) for z in zs]
                + [pl.BlockSpec((t, d), lambda i, k=k: (i, gblk + k)) for k in range(4)]
                + [ANY, ANY, row, vec, vec])
    return pl.pallas_call(
        body, name=name, grid=(s // t,), in_specs=in_specs,
        out_specs=[row, row, pl.BlockSpec((d, t), lambda i: (0, i))],
        out_shape=[jax.ShapeDtypeStruct((s, d), F32), jax.ShapeDtypeStruct((s, d), BF16),
                   jax.ShapeDtypeStruct((d, s), BF16)],
        scratch_shapes=[pltpu.VMEM(pcat.shape, BF16), pltpu.VMEM(wout.shape, BF16)],
        compiler_params=_params(("arbitrary",)),
    )(*zs, proj, proj, proj, proj, pcat, wout, x, lng, lnb)


def _merge_bwd(zs, proj, gblk, pcat, wout, x, lng, lnb, dy, alpha, name):
    s, d = x.shape
    t = T_MERGE
    n = s // t
    bounds = _branch_bounds(zs)

    def body(*refs):
        z_refs, g_refs = refs[0:4], refs[4:8]
        pcat_hbm, wout_hbm, x_ref, lng_ref, lnb_ref, dy_ref = refs[8:14]
        dx_ref, dg_ref = refs[14:16]
        dz_refs = refs[16:20]
        dpcat_hbm, dwout_hbm, dlng_ref, dlnb_ref = refs[20:24]
        pcat_v, wout_v, dpcat_v, dwout_v = refs[24:]
        i = pl.program_id(0)

        @pl.when(i == 0)
        def _():
            pltpu.sync_copy(pcat_hbm, pcat_v)
            pltpu.sync_copy(wout_hbm, wout_v)
            dpcat_v[...] = jnp.zeros_like(dpcat_v)
            dwout_v[...] = jnp.zeros_like(dwout_v)
            dlng_ref[...] = jnp.zeros_like(dlng_ref)
            dlnb_ref[...] = jnp.zeros_like(dlnb_ref)

        ys, sgs, merged = _merge_parts(z_refs, g_refs, pcat_v, bounds)
        mb = merged.astype(BF16)
        h = alpha * x_ref[...] + _dot(mb, wout_v[...], NN)
        xh, rstd = _ln_stats(h)
        dyv = dy_ref[...]
        dlng_ref[...] += jnp.sum(dyv * xh, axis=0, keepdims=True)
        dlnb_ref[...] += jnp.sum(dyv, axis=0, keepdims=True)
        dxh = dyv * lng_ref[...]
        dh = rstd * (dxh - jnp.mean(dxh, axis=-1, keepdims=True) - xh * jnp.mean(dxh * xh, axis=-1, keepdims=True))
        dx_ref[...] = alpha * dh
        dhb = dh.astype(BF16)
        dwout_v[...] += _dot(mb, dhb, TN)
        dmerged = _dot(dhb, wout_v[...], NT)
        for k, (zr, (lo, hi)) in enumerate(zip(z_refs, bounds)):
            sg = sgs[k]
            dg_ref[:, k * d:(k + 1) * d] = (dmerged * ys[k] * sg * (1.0 - sg)).astype(BF16)
            dyk = (dmerged * sg).astype(BF16)
            dpcat_v[lo:hi, :] += _dot(zr[...], dyk, TN)
            dz_refs[k][...] = _dot(dyk, pcat_v[lo:hi, :], NT)

        @pl.when(i == n - 1)
        def _():
            pltpu.sync_copy(dpcat_v, dpcat_hbm)
            pltpu.sync_copy(dwout_v, dwout_hbm)

    row = pl.BlockSpec((t, d), lambda i: (i, 0))
    vec = pl.BlockSpec((1, d), lambda i: (0, 0))
    z_specs = [pl.BlockSpec((t, z.shape[1]), lambda i: (i, 0)) for z in zs]
    in_specs = (z_specs + [pl.BlockSpec((t, d), lambda i, k=k: (i, gblk + k)) for k in range(4)]
                + [ANY, ANY, row, vec, vec, row])
    out_specs = [row, pl.BlockSpec((t, 4 * d), lambda i: (i, 0))] + z_specs + [ANY, ANY, vec, vec]
    vo = jax.ShapeDtypeStruct((1, d), F32)
    out_shape = ([jax.ShapeDtypeStruct((s, d), F32), jax.ShapeDtypeStruct((s, 4 * d), BF16)]
                 + [jax.ShapeDtypeStruct(z.shape, F32) for z in zs]
                 + [jax.ShapeDtypeStruct(pcat.shape, F32), jax.ShapeDtypeStruct(wout.shape, F32), vo, vo])
    return pl.pallas_call(
        body, name=name, grid=(n,), in_specs=in_specs, out_specs=out_specs, out_shape=out_shape,
        scratch_shapes=[pltpu.VMEM(pcat.shape, BF16), pltpu.VMEM(wout.shape, BF16),
                        pltpu.VMEM(pcat.shape, F32), pltpu.VMEM(wout.shape, F32)],
        compiler_params=_params(("arbitrary",)),
    )(*zs, proj, proj, proj, proj, pcat, wout, x, lng, lnb, dy)


def _loss_head(y, target, name):
    s, d = y.shape
    t = T_ELEM

    def body(y_ref, t_ref, dy_ref, loss_ref):
        @pl.when(pl.program_id(0) == 0)
        def _():
            loss_ref[...] = jnp.zeros_like(loss_ref)

        e = y_ref[...] - t_ref[...]
        dy_ref[...] = e * (1.0 / d)
        loss_ref[...] += 0.5 * jnp.sum(jnp.mean(e * e, axis=-1, keepdims=True), axis=0, keepdims=True)

    row = pl.BlockSpec((t, d), lambda i: (i, 0))
    return pl.pallas_call(
        body, name=name, grid=(s // t,), in_specs=[row, row],
        out_specs=[row, pl.BlockSpec((8, LANES), lambda i: (0, 0))],
        out_shape=[jax.ShapeDtypeStruct((s, d), F32), jax.ShapeDtypeStruct((8, LANES), F32)],
        compiler_params=_params(("arbitrary",)),
    )(y, target)


def _adamw(w, g, m, v, name):
    n_l, r, c = w.shape
    t = _pick(r, (256, 128, 64, 32, 16, 8))

    def body(w_ref, g_ref, m_ref, v_ref, d_ref, nm_ref, nv_ref):
        gv = g_ref[...]
        nm = ADAM_B1 * m_ref[...] + (1.0 - ADAM_B1) * gv
        nv = ADAM_B2 * v_ref[...] + (1.0 - ADAM_B2) * (gv * gv)
        m_hat = nm / (1.0 - ADAM_B1 ** ADAM_STEP)
        v_hat = nv / (1.0 - ADAM_B2 ** ADAM_STEP)
        d_ref[...] = -ADAM_LR * (m_hat / (jnp.sqrt(v_hat) + ADAM_EPS) + ADAM_WD * w_ref[...])
        nm_ref[...] = nm
        nv_ref[...] = nv

    blk = pl.BlockSpec((1, t, c), lambda l, i: (l, i, 0))
    o = jax.ShapeDtypeStruct((n_l, r, c), F32)
    return pl.pallas_call(body, name=name, grid=(n_l, r // t), in_specs=[blk] * 4, out_specs=[blk] * 3,
                          out_shape=[o, o, o], compiler_params=_params(("parallel", "parallel")))(w, g, m, v)


def _sum_leading(x, out_dtype, name):
    k, r, c = x.shape
    t = _pick(r, (256, 128, 64, 32, 16, 8))

    def body(x_ref, o_ref):
        acc = x_ref[0].astype(F32)
        for j in range(1, k):
            acc = acc + x_ref[j].astype(F32)
        o_ref[...] = acc.astype(out_dtype)

    return pl.pallas_call(body, name=name, grid=(r // t,),
                          in_specs=[pl.BlockSpec((k, t, c), lambda i: (0, i, 0))],
                          out_specs=pl.BlockSpec((t, c), lambda i: (i, 0)),
                          out_shape=jax.ShapeDtypeStruct((r, c), out_dtype), compiler_params=_params(("parallel",)))(x)


def _position():
    return lax.axis_index("x"), lax.axis_index("y"), lax.axis_index("c")


def _chip_peers(x, y):
    return [(1 - x, y), (x, 1 - y), (1 - x, 1 - y)]


def _comm_call(body, n_in, out_shape, n_remote, n_local, name):
    return pl.pallas_call(
        body, name=name, in_specs=[ANY] * n_in, out_specs=[ANY] * len(out_shape), out_shape=out_shape,
        scratch_shapes=[pltpu.SemaphoreType.DMA((n_remote,)), pltpu.SemaphoreType.DMA((n_remote,)),
                        pltpu.SemaphoreType.DMA((max(n_local, 1),))])


def _run_copies(local, remote, send, recv, loc):
    copies = [pltpu.make_async_copy(src, dst, loc.at[k]) for k, (src, dst) in enumerate(local)]
    copies += [pltpu.make_async_remote_copy(src_ref=src, dst_ref=dst, send_sem=send.at[k], recv_sem=recv.at[k],
                                            device_id=peer, device_id_type=MESH)
               for k, (src, dst, peer) in enumerate(remote)]
    for cp in copies:
        cp.start()
    for cp in copies:
        cp.wait()


COPY_BYTES = 1024 * 1024


def _n_copies(rows, row_bytes, align):
    n = 8
    while n > 1 and (rows % (n * align) or rows // n * row_bytes < COPY_BYTES):
        n //= 2
    return n


def _allgather_chips(arrs, name):
    n = len(arrs)
    per_layer = [a.size * a.dtype.itemsize // a.shape[0] >= COPY_BYTES for a in arrs]
    n_each = [a.shape[0] if pl_ else 1 for a, pl_ in zip(arrs, per_layer)]

    def body(*refs):
        ins, outs = refs[:n], refs[n:2 * n]
        send, recv, loc = refs[2 * n:]
        x, y, c = _position()
        me = 2 * x + y
        local, remote = [], []
        for a in range(n):
            if per_layer[a]:
                parts = [(ins[a].at[l], outs[a].at[me, l]) for l in range(arrs[a].shape[0])]
            else:
                parts = [(ins[a], outs[a].at[me])]
            local += parts
            for px, py in _chip_peers(x, y):
                remote += [(src, dst, (px, py, c)) for src, dst in parts]
        _run_copies(local, remote, send, recv, loc)

    out_shape = [jax.ShapeDtypeStruct((4,) + a.shape, a.dtype) for a in arrs]
    return _comm_call(body, n, out_shape, 3 * sum(n_each), sum(n_each), name)(*arrs)


def _allgather_all(v, name):
    def body(v_ref, o_ref, send, recv, loc):
        x, y, c = _position()
        me = 4 * x + 2 * y + c
        remote = []
        for k in range(1, 8):
            fx, fy, fc = (k >> 2) & 1, (k >> 1) & 1, k & 1
            remote.append((v_ref, o_ref.at[me], (x ^ fx, y ^ fy, c ^ fc)))
        _run_copies([(v_ref, o_ref.at[me])], remote, send, recv, loc)

    return _comm_call(body, 1, [jax.ShapeDtypeStruct((8,) + v.shape, v.dtype)], 7, 1, name)(v)[0]


def _pair_exchange_sum(g, name):
    _, _, n, r, cc = g.shape

    def body(c_ref, mine_ref, send_ref, o_ref, land, send_sem, recv_sem):
        x, y, c = _position()
        slot = (pl.program_id(0) * n + pl.program_id(1)) % 2
        push = pltpu.make_async_remote_copy(
            src_ref=send_ref.at[0, 0, 0], dst_ref=land.at[slot], send_sem=send_sem.at[slot],
            recv_sem=recv_sem.at[slot], device_id=(x, y, 1 - c), device_id_type=MESH)
        push.start()
        push.wait_recv()
        o_ref[0, 0] = (mine_ref[0, 0, 0] + land[slot]).astype(BF16)
        push.wait_send()

    grid_spec = pltpu.PrefetchScalarGridSpec(
        num_scalar_prefetch=1, grid=(4, n),
        in_specs=[pl.BlockSpec((1, 1, 1, r, cc), lambda j, k, c: (j, c[0], k, 0, 0)),
                  pl.BlockSpec((1, 1, 1, r, cc), lambda j, k, c: (j, 1 - c[0], k, 0, 0))],
        out_specs=pl.BlockSpec((1, 1, r, cc), lambda j, k, c: (j, k, 0, 0)),
        scratch_shapes=[pltpu.VMEM((2, r, cc), F32), pltpu.SemaphoreType.DMA((2,)), pltpu.SemaphoreType.DMA((2,))])
    return pl.pallas_call(
        body, name=name, grid_spec=grid_spec, out_shape=jax.ShapeDtypeStruct((4, n, r, cc), BF16),
        compiler_params=_params(("arbitrary", "arbitrary")))(_scalar(lax.axis_index("c")), g, g)


def _scalar(v):
    return v.astype(jnp.int32).reshape(1)


def _chip_exchange_sum(p, name):
    _, n, r, cc = p.shape

    def body(i0, i1, i2, i3, own_ref, s0_ref, s1_ref, s2_ref, mine_ref, theirs_ref, land, total, land_pair,
             send_sem, recv_sem, pair_send, pair_recv):
        x, y, c = _position()
        slot = pl.program_id(0) % 2
        pushes = [pltpu.make_async_remote_copy(
            src_ref=src.at[0, 0], dst_ref=land.at[slot, j], send_sem=send_sem.at[slot, j], recv_sem=recv_sem.at[slot, j],
            device_id=(px, py, c), device_id_type=MESH)
            for j, (src, (px, py)) in enumerate(zip((s0_ref, s1_ref, s2_ref), _chip_peers(x, y)))]
        for cp in pushes:
            cp.start()
        acc = own_ref[0, 0].astype(F32)
        for j, cp in enumerate(pushes):
            cp.wait_recv()
            acc = acc + land[slot, j].astype(F32)
        mine_ref[0] = acc
        total[slot] = acc
        share = pltpu.make_async_remote_copy(
            src_ref=total.at[slot], dst_ref=land_pair.at[slot], send_sem=pair_send.at[slot], recv_sem=pair_recv.at[slot],
            device_id=(x, y, 1 - c), device_id_type=MESH)
        share.start()
        share.wait_recv()
        theirs_ref[0] = land_pair[slot]
        for cp in pushes:
            cp.wait_send()
        share.wait_send()

    def slot_spec(which):
        return pl.BlockSpec((1, 1, r, cc), lambda k, *idx, which=which: (idx[which][0], k, 0, 0))

    out_spec = pl.BlockSpec((1, r, cc), lambda k, *idx: (k, 0, 0))
    grid_spec = pltpu.PrefetchScalarGridSpec(
        num_scalar_prefetch=4, grid=(n,), in_specs=[slot_spec(0), slot_spec(1), slot_spec(2), slot_spec(3)],
        out_specs=[out_spec, out_spec],
        scratch_shapes=[pltpu.VMEM((2, 3, r, cc), BF16), pltpu.VMEM((2, r, cc), F32), pltpu.VMEM((2, r, cc), F32),
                        pltpu.SemaphoreType.DMA((2, 3)), pltpu.SemaphoreType.DMA((2, 3)),
                        pltpu.SemaphoreType.DMA((2,)), pltpu.SemaphoreType.DMA((2,))])
    o = jax.ShapeDtypeStruct((n, r, cc), F32)
    x, y, _ = _position()
    chips = [_scalar(2 * x + y)] + [_scalar(2 * px + py) for px, py in _chip_peers(x, y)]
    return pl.pallas_call(body, name=name, grid_spec=grid_spec, out_shape=[o, o],
                          compiler_params=_params(("arbitrary",)))(*chips, p, p, p, p)


def _reduce_scatter(gs):
    c = lax.axis_index("c")
    outs = []
    for a, g in enumerate(gs):
        _, rows, cc = g.shape
        k = _n_copies(rows // 2, cc * 4, 16)
        p = _pair_exchange_sum(g.reshape(4, 2, k, rows // (2 * k), cc), f"rs_pair_exchange_sum_{a}")
        mine, theirs = _chip_exchange_sum(p, f"rs_chip_exchange_sum_{a}")
        mine, theirs = mine.reshape(rows // 2, cc), theirs.reshape(rows // 2, cc)
        outs.append(jnp.where(c == 0, jnp.concatenate([mine, theirs]), jnp.concatenate([theirs, mine])))
    return outs


def _gather_shards(w, name):
    rows, cc = w.shape
    half = rows // 2
    n = _n_copies(half, cc * w.dtype.itemsize, 16)
    r = half // n

    def body(core_ref, mine_ref, other_ref, out_ref, land, land_pair, send_sem, recv_sem, pair_send, pair_recv, out_sem):
        x, y, c = _position()
        k = pl.program_id(0)
        slot = k % 2
        me = 2 * x + y
        chips = [2 * px + py for px, py in _chip_peers(x, y)]
        pushes = [pltpu.make_async_remote_copy(
            src_ref=mine_ref.at[0, 0], dst_ref=land.at[slot, j], send_sem=send_sem.at[slot, j],
            recv_sem=recv_sem.at[slot, j], device_id=(px, py, c), device_id_type=MESH)
            for j, (px, py) in enumerate(_chip_peers(x, y))]
        for cp in pushes:
            cp.start()
        writes = [pltpu.make_async_copy(mine_ref.at[0, 0], out_ref.at[me, c, k], out_sem.at[0]),
                  pltpu.make_async_copy(other_ref.at[0, 0], out_ref.at[me, 1 - c, k], out_sem.at[1])]
        for cp in writes:
            cp.start()
        passes = []
        for j, cp in enumerate(pushes):
            cp.wait_recv()
            passes.append(pltpu.make_async_remote_copy(
                src_ref=land.at[slot, j], dst_ref=land_pair.at[slot, j], send_sem=pair_send.at[slot, j],
                recv_sem=pair_recv.at[slot, j], device_id=(x, y, 1 - c), device_id_type=MESH))
            passes[j].start()
            writes.append(pltpu.make_async_copy(land.at[slot, j], out_ref.at[chips[j], c, k], out_sem.at[2 + j]))
            writes[-1].start()
        for j, cp in enumerate(passes):
            cp.wait_recv()
            writes.append(pltpu.make_async_copy(land_pair.at[slot, j], out_ref.at[chips[j], 1 - c, k], out_sem.at[5 + j]))
            writes[-1].start()
        for cp in writes:
            cp.wait()
        for cp in pushes + passes:
            cp.wait_send()

    grid_spec = pltpu.PrefetchScalarGridSpec(
        num_scalar_prefetch=1, grid=(n,),
        in_specs=[pl.BlockSpec((1, 1, r, cc), lambda k, core: (core[0], k, 0, 0)),
                  pl.BlockSpec((1, 1, r, cc), lambda k, core: (1 - core[0], k, 0, 0))],
        out_specs=ANY,
        scratch_shapes=[pltpu.VMEM((2, 3, r, cc), w.dtype), pltpu.VMEM((2, 3, r, cc), w.dtype),
                        pltpu.SemaphoreType.DMA((2, 3)), pltpu.SemaphoreType.DMA((2, 3)),
                        pltpu.SemaphoreType.DMA((2, 3)), pltpu.SemaphoreType.DMA((2, 3)),
                        pltpu.SemaphoreType.DMA((8,))])
    w4 = w.reshape(2, n, r, cc)
    out = pl.pallas_call(body, name=name, grid_spec=grid_spec,
                         out_shape=jax.ShapeDtypeStruct((4, 2, n, r, cc), w.dtype),
                         compiler_params=_params(("arbitrary",)))(_scalar(lax.axis_index("c")), w4, w4)
    return out.reshape(4, rows, cc)


def _pad_rows(a, rows):
    return jnp.pad(a, ((0, rows - a.shape[0]), (0, 0)))


def _shard_cols(a):
    r, c4 = a.shape
    return a.reshape(r, 4, c4 // 4).transpose(1, 0, 2)


def kernel(x, mem, w_in, b_forget, conv_a_w, conv_a_b, ln_a_g, ln_a_b, conv_b_w, w_kv_mem, mem_ln_g, mem_ln_b, p_a, p_b, p_c, p_m, w_out, ln_g, ln_b, loss_target, m_w_in, m_b_forget, m_conv_a_w, m_conv_a_b, m_ln_a_g, m_ln_a_b, m_conv_b_w, m_w_kv_mem, m_mem_ln_g, m_mem_ln_b, m_p_a, m_p_b, m_p_c, m_p_m, m_w_out, m_ln_g, m_ln_b, v_w_in, v_b_forget, v_conv_a_w, v_conv_a_b, v_ln_a_g, v_ln_a_b, v_conv_b_w, v_w_kv_mem, v_mem_ln_g, v_mem_ln_b, v_p_a, v_p_b, v_p_c, v_p_m, v_w_out, v_ln_g, v_ln_b):
    depth = w_in.shape[0]
    x0 = x[0]
    s, d = x0.shape
    aw = conv_a_w.shape[2] * 4
    mw = p_m.shape[1]
    n_head = b_forget.shape[1]
    alpha = (2.0 * depth) ** 0.25
    in_cols = w_in.shape[2] * 4
    assert aw == n_head * HEAD_DIM and mw % LANES == 0 and in_cols == 11 * aw + n_head + 2 * mw + 4 * d
    cf0 = 10 * aw
    n_main = in_cols - n_head
    fblk = n_main // LANES
    n_pad = n_main + LANES
    gblk = (11 * aw + 2 * mw) // d
    mblk = (11 * aw) // (2 * mw)
    qblk, kblk, vblk = 7 * aw // LANES, 8 * aw // LANES, 9 * aw // LANES
    assert (11 * aw + 2 * mw) % d == 0 and (11 * aw) % (2 * mw) == 0

    def gather(w, name):
        flat = _gather_shards(w.astype(BF16).reshape(-1, w.shape[-1]), name)
        return flat.reshape((4,) + w.shape)

    w_in_g, p_a_g, p_b_g, p_c_g = (gather(w, f"gather_{nm}") for w, nm in
                                   ((w_in, "w_in"), (p_a, "p_a"), (p_b, "p_b"), (p_c, "p_c")))
    p_m_g, w_kv_g, w_out_g = (gather(w, f"gather_{nm}") for w, nm in
                              ((p_m, "p_m"), (w_kv_mem, "w_kv"), (w_out, "w_out")))
    conv_a_g, conv_b_g = _allgather_chips([conv_a_w, conv_b_w], "gather_conv_taps")

    def cols(g):
        return jnp.concatenate([g[j] for j in range(4)], axis=-1)

    def rows(g):
        return jnp.concatenate([g[j] for j in range(4)], axis=-2)

    shard_w = in_cols // 4

    def global_columns(shards, lo, hi):
        return [shards[j][:, max(lo, j * shard_w) - j * shard_w:min(hi, (j + 1) * shard_w) - j * shard_w]
                for j in range(4) if max(lo, j * shard_w) < min(hi, (j + 1) * shard_w)]

    w_pad = []
    for l in range(depth):
        shards = [w_in_g[j, l] for j in range(4)]
        w_pad.append(jnp.concatenate(
            global_columns(shards, 0, cf0) + global_columns(shards, cf0 + n_head, in_cols)
            + global_columns(shards, cf0, cf0 + n_head) + [jnp.zeros((d, LANES - n_head), BF16)], axis=1))

    piece_order = [(0, 3 * aw, 0), (3 * aw, 7 * aw, 1), (7 * aw, cf0, 2), (cf0, cf0 + n_head, 6),
                   (cf0 + n_head, 11 * aw + n_head, 3), (11 * aw + n_head, 11 * aw + n_head + 2 * mw, 4),
                   (11 * aw + n_head + 2 * mw, in_cols, 5)]

    def own_columns(parts, j):
        lo, hi = j * shard_w, (j + 1) * shard_w
        return jnp.concatenate([parts[k][:, max(lo, a) - a:min(hi, b) - a]
                                for a, b, k in piece_order if max(lo, a) < min(hi, b)], axis=1)
    pcat = jnp.concatenate([cols(p_a_g), cols(p_b_g), cols(p_c_g), cols(p_m_g)], axis=1)
    w_kv = rows(w_kv_g)
    wout = rows(w_out_g)
    conv_a = jnp.pad(cols(conv_a_g), ((0, 0), (0, HALO_A - CONV_A), (0, 0)))
    conv_b = jnp.pad(cols(conv_b_g), ((0, 0), (0, HALO_B - CONV_B), (0, 0)))
    bf_pad = jnp.pad(b_forget, ((0, 0), (0, LANES - n_head)))
    pieces = [(0, 3 * aw), (3 * aw, 7 * aw), (7 * aw, 10 * aw), (10 * aw, 11 * aw),
              (11 * aw, 11 * aw + 2 * mw), (11 * aw + 2 * mw, n_main), (n_main, n_pad)]

    mem_n = _ln_rows(mem[0], mem_ln_g[None], mem_ln_b[None], "mem_ln")

    xs, saved = [x0], []
    x_ops = [_matmul_copies(x0, "matmul_copies")]
    for l in range(depth):
        xl = xs[-1]
        proj = _mm(x_ops[l][0], w_pad[l], name=f"proj_{l}")
        za, conv_out = _branch_a_fwd(proj, conv_a[l], conv_a_b[l][None], ln_a_g[l][None], ln_a_b[l][None], f"a_fwd_{l}")
        zb = _branch_b_fwd(proj, conv_b[l], f"b_fwd_{l}")
        cum, cumt, sgt = _forget_prep(proj, bf_pad[l][None], fblk, f"forget_prep_{l}")
        o_c, lse = _fox_fwd(proj, cum, cumt, qblk, kblk, vblk, f"fox_fwd_{l}")
        zc = _gate_mul(proj, o_c, 10, f"c_gate_{l}")
        mkv = _mm(mem_n, w_kv[l], name=f"mkv_{l}")
        zm = _branch_m_fwd(proj, mkv, mblk, f"m_fwd_{l}")
        zs = [za, zb, zc, zm]
        y, y_b, y_bt = _merge_fwd(zs, proj, gblk, pcat[l], wout[l], xl, ln_g[l][None], ln_b[l][None], alpha,
                                  f"merge_fwd_{l}")
        xs.append(y)
        x_ops.append((y_b, y_bt))
        saved.append((proj, zs, conv_out, cum, cumt, sgt, o_c, lse, mkv))

    dy, loss_part = _loss_head(xs[-1], loss_target[0], "loss_head")

    g_w_in, g_conv_a, g_conv_b, g_w_kv, g_pcat, g_wout = [], [], [], [], [], []
    small = []
    dmem_n = None
    for l in reversed(range(depth)):
        proj, zs, conv_out, cum, cumt, sgt, o_c, lse, mkv = saved[l]
        xl = xs[l]
        (dx, d_g, dza, dzb, dzc, dzm, dpcat, dwout, dlng, dlnb) = _merge_bwd(
            zs, proj, gblk, pcat[l], wout[l], xl, ln_g[l][None], ln_b[l][None], dy, alpha, f"merge_bwd_{l}")
        d_m, dmkv = _branch_m_bwd(proj, mkv, dzm, mblk, f"m_bwd_{l}")
        g_w_kv.append(_mm(mem_n, dmkv, ta=True, name=f"dwkv_{l}"))
        dmem_n = _mm(dmkv, w_kv[l], tb=True, add=dmem_n, name=f"dmem_{l}")
        do, d_cg, delta = _fox_bwd_prep(proj, dzc, o_c, 10, f"fox_bwd_prep_{l}")
        dq, dk, dv, dcumt4, dcumq4 = _fox_bwd(proj, do, lse, delta, cum, cumt, qblk, kblk, vblk, f"fox_bwd_{l}")
        d_f, dbf = _forget_bwd(dcumt4, dcumq4, sgt, f"forget_bwd_{l}")
        d_b, dconv_b = _branch_b_bwd(proj, dzb, conv_b[l], f"b_bwd_{l}")
        d_a, dconv_a, dconv_ab, dlag, dlab = _branch_a_bwd(
            proj, conv_out, dza, conv_a[l], ln_a_g[l][None], ln_a_b[l][None], f"a_bwd_{l}")
        dparts = [d_a, d_b, jnp.concatenate([dq, dk, dv], axis=1), d_cg, d_m, d_g, d_f]
        dx = _input_grad(dparts, [w_pad[l][:, lo:hi] for lo, hi in pieces], dx, f"dx_{l}")
        dw_parts = [_mm(x_ops[l][1], dp, tm=d, name=f"dw_{l}_{k}") for k, dp in enumerate(dparts)]
        g_w_in.append(dw_parts)
        g_conv_a.append(dconv_a)
        g_conv_b.append(dconv_b)
        g_pcat.append(dpcat)
        g_wout.append(dwout)
        small.append([dbf[:, 0], dconv_ab[0], dlag[0], dlab[0], dlng[0], dlnb[0]])
        dy = dx
    grad_x = dy
    dmlg, dmlb = _ln_rows_param_grads(mem[0], dmem_n, "mem_ln_grads")
    for lst in (g_w_in, g_conv_a, g_conv_b, g_w_kv, g_pcat, g_wout, small):
        lst.reverse()

    pa_end, pb_end, pc_end = aw, 2 * aw, 3 * aw
    rs_in = [
        jnp.stack([jnp.concatenate([own_columns(parts, j) for parts in g_w_in], axis=0) for j in range(4)]),
        _shard_cols(jnp.concatenate(g_conv_a, axis=0)),
        _shard_cols(jnp.concatenate([_pad_rows(g, 2 * HALO_B) for g in g_conv_b], axis=0)),
        jnp.concatenate([g.reshape(4, g.shape[0] // 4, g.shape[1]) for g in g_w_kv], axis=1),
        _shard_cols(jnp.concatenate([g[:pa_end] for g in g_pcat], axis=0)),
        _shard_cols(jnp.concatenate([g[pa_end:pb_end] for g in g_pcat], axis=0)),
        _shard_cols(jnp.concatenate([g[pb_end:pc_end] for g in g_pcat], axis=0)),
        _shard_cols(jnp.concatenate([g[pc_end:] for g in g_pcat], axis=0)),
        jnp.concatenate([g.reshape(4, g.shape[0] // 4, g.shape[1]) for g in g_wout], axis=1),
    ]
    rs_out = _reduce_scatter(rs_in)
    gw_in = rs_out[0].reshape(depth, d, -1)
    g_ca = rs_out[1].reshape(depth, HALO_A, -1)[:, :CONV_A]
    g_cb = rs_out[2].reshape(depth, 2 * HALO_B, -1)[:, :CONV_B]
    gw_kv = rs_out[3].reshape(depth, -1, 2 * mw)
    gp_a = rs_out[4].reshape(depth, aw, -1)
    gp_b = rs_out[5].reshape(depth, aw, -1)
    gp_c = rs_out[6].reshape(depth, aw, -1)
    gp_m = rs_out[7].reshape(depth, mw, -1)
    gw_out = rs_out[8].reshape(depth, -1, d)

    flat = jnp.concatenate([jnp.concatenate(p) for p in small] + [dmlg[0], dmlb[0], loss_part[0, 0:1]])
    n_small = flat.shape[0]
    n_rows = -(-n_small // (8 * LANES)) * 8
    vec = jnp.pad(flat, (0, n_rows * LANES - n_small)).reshape(n_rows, LANES)
    tot = _sum_leading(_allgather_all(vec, "gather_small"), F32, "sum_small").reshape(-1)
    per_layer = n_head + 3 * aw + 2 * d
    tl = tot[:depth * per_layer].reshape(depth, per_layer)
    offs = np.cumsum([0, n_head, aw, aw, aw, d, d])
    g_bf, g_cab, g_lag, g_lab, g_lg, g_lb = [tl[:, offs[k]:offs[k + 1]] for k in range(6)]
    base = depth * per_layer
    g_mlg, g_mlb = tot[base:base + d], tot[base + d:base + 2 * d]
    loss = tot[base + 2 * d]

    grads = [gw_in, g_bf, g_ca, g_cab, g_lag, g_lab, g_cb, gw_kv, g_mlg, g_mlb, gp_a, gp_b, gp_c, gp_m, gw_out, g_lg, g_lb]
    ws = [w_in, b_forget, conv_a_w, conv_a_b, ln_a_g, ln_a_b, conv_b_w, w_kv_mem, mem_ln_g, mem_ln_b, p_a, p_b, p_c, p_m, w_out, ln_g, ln_b]
    ms = [m_w_in, m_b_forget, m_conv_a_w, m_conv_a_b, m_ln_a_g, m_ln_a_b, m_conv_b_w, m_w_kv_mem, m_mem_ln_g, m_mem_ln_b, m_p_a, m_p_b, m_p_c, m_p_m, m_w_out, m_ln_g, m_ln_b]
    vs = [v_w_in, v_b_forget, v_conv_a_w, v_conv_a_b, v_ln_a_g, v_ln_a_b, v_conv_b_w, v_w_kv_mem, v_mem_ln_g, v_mem_ln_b, v_p_a, v_p_b, v_p_c, v_p_m, v_w_out, v_ln_g, v_ln_b]
    deltas, new_ms, new_vs = [], [], []
    for k, (wk, gk, mk, vk) in enumerate(zip(ws, grads, ms, vs)):
        shape = wk.shape
        as_3d = (1,) * (3 - wk.ndim) + shape
        dk_, nm_, nv_ = _adamw(wk.reshape(as_3d), gk.reshape(as_3d), mk.reshape(as_3d), vk.reshape(as_3d), f"adamw_{k}")
        deltas.append(dk_.reshape(shape))
        new_ms.append(nm_.reshape(shape))
        new_vs.append(nv_.reshape(shape))
        grads[k] = gk.reshape(shape)
    return (loss, grad_x[None], *grads, *deltas, *new_ms, *new_vs)


def _gate_mul(proj, o, gblk, name):
    s, w = o.shape
    t = T_ELEM

    def body(g_ref, o_ref, z_ref):
        z_ref[...] = (o_ref[...] * _silu(g_ref[...])).astype(BF16)

    row = pl.BlockSpec((t, w), lambda i: (i, 0))
    return pl.pallas_call(body, name=name, grid=(s // t,), in_specs=[pl.BlockSpec((t, w), lambda i: (i, gblk)), row],
                          out_specs=row, out_shape=jax.ShapeDtypeStruct((s, w), BF16),
                          compiler_params=_params(("parallel",)))(proj, o)
```

```python
import functools
import math

import numpy as np
import jax
import jax.numpy as jnp
from jax import lax
from jax.experimental import pallas as pl
from jax.experimental.pallas import tpu as pltpu

F32 = jnp.float32
BF16 = jnp.bfloat16
MESH = pl.DeviceIdType.MESH
ANY = pl.BlockSpec(memory_space=pl.ANY)

LN_EPS = 1e-5
NEG_BIG = -1e30
HEAD_DIM = 64
LANES = 128
CONV_A = 31
CONV_B = 3
HALO_A = 32
HALO_B = 8
CHUNK = 32
VMEM_LIMIT = 60 * 1024 * 1024

ADAM_LR, ADAM_B1, ADAM_B2, ADAM_EPS, ADAM_WD, ADAM_STEP = 0.001, 0.9, 0.999, 1e-08, 0.01, 10

T_MM = 512
T_A = 128
T_B = 256
T_ATT_Q = 512
T_ATT_K = 1024
T_CUM = 512
T_M = 512
T_MERGE = 256
T_DX = 256
T_DW_K = 2048
T_ELEM = 512


def _pick(n, prefs):
    for p in prefs:
        if n % p == 0:
            return p
    return n


def _params(sem=None):
    return pltpu.CompilerParams(dimension_semantics=sem, vmem_limit_bytes=VMEM_LIMIT)


def _sigmoid(x):
    return jax.nn.sigmoid(x)


def _silu(x):
    return x * _sigmoid(x)


def _dsilu(x):
    s = _sigmoid(x)
    return s * (1.0 + x * (1.0 - s))


def _dot(a, b, dims, precision=None):
    return lax.dot_general(a, b, (dims, ((), ())), preferred_element_type=F32, precision=precision)


NN = ((1,), (0,))
NT = ((1,), (1,))
TN = ((0,), (0,))


def _iota(shape, dim):
    return lax.broadcasted_iota(jnp.int32, shape, dim)


def _mm(a, b, *, ta=False, tb=False, add=None, out_dtype=F32, tm=None, tk=None, name):
    m = a.shape[1] if ta else a.shape[0]
    k = a.shape[0] if ta else a.shape[1]
    n = b.shape[0] if tb else b.shape[1]
    if tm is None:
        tm = _pick(m, (1024, 512, 256)) if ta else _pick(m, (T_MM, 256))
    tn = _pick(n, (1152, 1024, 768, 512, 384, 256, 128))
    if tk is None:
        tk = _pick(k, (1024, 512, 256))
    nk = k // tk
    dims = ((0,) if ta else (1,), (1,) if tb else (0,))

    def body(*refs):
        if add is None:
            a_ref, b_ref, o_ref, acc_ref = refs
        else:
            a_ref, b_ref, add_ref, o_ref, acc_ref = refs
        kk = pl.program_id(2)
        p = _dot(a_ref[...].astype(BF16), b_ref[...].astype(BF16), dims)

        @pl.when(kk == 0)
        def _():
            acc_ref[...] = p

        @pl.when(kk > 0)
        def _():
            acc_ref[...] += p

        @pl.when(kk == nk - 1)
        def _():
            r = acc_ref[...]
            if add is not None:
                r = r + add_ref[...]
            o_ref[...] = r.astype(out_dtype)

    a_spec = (pl.BlockSpec((tk, tm), lambda j, i, kk: (kk, i)) if ta
              else pl.BlockSpec((tm, tk), lambda j, i, kk: (i, kk)))
    b_spec = (pl.BlockSpec((tn, tk), lambda j, i, kk: (j, kk)) if tb
              else pl.BlockSpec((tk, tn), lambda j, i, kk: (kk, j)))
    o_spec = pl.BlockSpec((tm, tn), lambda j, i, kk: (i, j))
    in_specs = [a_spec, b_spec] + ([o_spec] if add is not None else [])
    args = (a, b) + ((add,) if add is not None else ())
    return pl.pallas_call(
        body, name=name, grid=(n // tn, m // tm, nk), in_specs=in_specs, out_specs=o_spec,
        out_shape=jax.ShapeDtypeStruct((m, n), out_dtype),
        scratch_shapes=[pltpu.VMEM((tm, tn), F32)],
        compiler_params=_params(("parallel", "parallel", "arbitrary")),
    )(*args)


def _input_grad(dparts, wparts, add, name):
    s, d = add.shape
    t = T_DX
    n_p = len(dparts)

    def body(*refs):
        d_refs, w_hbm = refs[:n_p], refs[n_p:2 * n_p]
        add_ref, o_ref = refs[2 * n_p], refs[2 * n_p + 1]
        w_v = refs[2 * n_p + 2:]

        @pl.when(pl.program_id(0) == 0)
        def _():
            for p in range(n_p):
                pltpu.sync_copy(w_hbm[p], w_v[p])

        acc = add_ref[...]
        for p in range(n_p):
            acc = acc + _dot(d_refs[p][...], w_v[p][...], NT)
        o_ref[...] = acc

    row = pl.BlockSpec((t, d), lambda i: (i, 0))
    in_specs = ([pl.BlockSpec((t, dp.shape[1]), lambda i: (i, 0)) for dp in dparts] + [ANY] * n_p + [row])
    return pl.pallas_call(
        body, name=name, grid=(s // t,), in_specs=in_specs, out_specs=row,
        out_shape=jax.ShapeDtypeStruct((s, d), F32),
        scratch_shapes=[pltpu.VMEM(w.shape, BF16) for w in wparts],
        compiler_params=_params(("arbitrary",)),
    )(*dparts, *wparts, add)


def _ln_rows(x, g, b, name):
    r, d = x.shape
    t = _pick(r, (256,))

    def body(x_ref, g_ref, b_ref, o_ref):
        xv = x_ref[...]
        mu = jnp.mean(xv, axis=-1, keepdims=True)
        dv = xv - mu
        var = jnp.mean(dv * dv, axis=-1, keepdims=True)
        o_ref[...] = dv * lax.rsqrt(var + LN_EPS) * g_ref[...] + b_ref[...]

    row = pl.BlockSpec((t, d), lambda i: (i, 0))
    vec = pl.BlockSpec((1, d), lambda i: (0, 0))
    return pl.pallas_call(body, name=name, grid=(r // t,), in_specs=[row, vec, vec], out_specs=row,
                          out_shape=jax.ShapeDtypeStruct((r, d), F32), compiler_params=_params(("parallel",)))(x, g, b)


def _ln_rows_param_grads(x, dy, name):
    r, d = x.shape

    def body(x_ref, dy_ref, dg_ref, db_ref):
        xv = x_ref[...]
        mu = jnp.mean(xv, axis=-1, keepdims=True)
        dv = xv - mu
        var = jnp.mean(dv * dv, axis=-1, keepdims=True)
        xh = dv * lax.rsqrt(var + LN_EPS)
        dg_ref[...] = jnp.sum(dy_ref[...] * xh, axis=0, keepdims=True)
        db_ref[...] = jnp.sum(dy_ref[...], axis=0, keepdims=True)

    full = pl.BlockSpec((r, d), lambda i: (0, 0))
    vec = pl.BlockSpec((1, d), lambda i: (0, 0))
    o = jax.ShapeDtypeStruct((1, d), F32)
    return pl.pallas_call(body, name=name, grid=(1,), in_specs=[full, full], out_specs=[vec, vec],
                          out_shape=[o, o], compiler_params=_params(("arbitrary",)))(x, dy)


def _conv_a_chunk(glu_ref, cw_ref, r0):
    acc = cw_ref[0:1, :] * glu_ref[pl.ds(r0 + 2, CHUNK), :]
    for k in range(1, CONV_A):
        acc = acc + cw_ref[k:k + 1, :] * glu_ref[pl.ds(r0 + 2 + k, CHUNK), :]
    return acc


def _ln_stats(c):
    mu = jnp.mean(c, axis=-1, keepdims=True)
    d = c - mu
    var = jnp.mean(d * d, axis=-1, keepdims=True)
    rstd = lax.rsqrt(var + LN_EPS)
    return d * rstd, rstd


def _branch_a_fwd(proj, cw, cb, lg, lb, name):
    s = proj.shape[0]
    t = T_A
    w = cw.shape[1]
    r = t // HALO_A

    def body(u_ref, v_ref, gt_ref, hu_ref, hv_ref, cw_ref, cb_ref, lg_ref, lb_ref, z_ref, conv_ref, glu):
        i = pl.program_id(0)
        hglu = hu_ref[...] * _sigmoid(hv_ref[...])
        glu[0:HALO_A, :] = jnp.where(i > 0, hglu, 0.0)
        glu[HALO_A:HALO_A + t, :] = u_ref[...] * _sigmoid(v_ref[...])
        for c in range(t // CHUNK):
            r0 = c * CHUNK
            conv = _conv_a_chunk(glu, cw_ref, r0) + cb_ref[...]
            conv_ref[r0:r0 + CHUNK, :] = conv
            xh, _ = _ln_stats(conv)
            a3 = _silu(xh * lg_ref[...] + lb_ref[...])
            z_ref[r0:r0 + CHUNK, :] = (a3 * _silu(gt_ref[r0:r0 + CHUNK, :])).astype(BF16)

    def cur(col):
        return pl.BlockSpec((t, w), lambda i, col=col: (i, col))

    def prev(col):
        return pl.BlockSpec((HALO_A, w), lambda i, col=col: (jnp.maximum(i * r - 1, 0), col))

    vec = pl.BlockSpec((1, w), lambda i: (0, 0))
    return pl.pallas_call(
        body, name=name, grid=(s // t,),
        in_specs=[cur(0), cur(1), cur(2), prev(0), prev(1), pl.BlockSpec((HALO_A, w), lambda i: (0, 0)), vec, vec, vec],
        out_specs=[pl.BlockSpec((t, w), lambda i: (i, 0)), pl.BlockSpec((t, w), lambda i: (i, 0))],
        out_shape=[jax.ShapeDtypeStruct((s, w), BF16), jax.ShapeDtypeStruct((s, w), F32)],
        scratch_shapes=[pltpu.VMEM((HALO_A + t, w), F32)],
        compiler_params=_params(("parallel",)),
    )(proj, proj, proj, proj, proj, cw, cb, lg, lb)


def _branch_a_bwd(proj, conv, dz, cw, lg, lb, name):
    s = proj.shape[0]
    t = T_A
    w = cw.shape[1]
    r = t // HALO_A
    n = s // t
    nblk = s // HALO_A
    ext = t + HALO_A

    def body(u_ref, v_ref, gt_ref, dz_ref, conv_ref, pu_ref, pv_ref, ngt_ref, ndz_ref, nconv_ref,
             cw_ref, lg_ref, lb_ref, da_ref, dw_ref, dcb_ref, dlg_ref, dlb_ref, glu, dc, dw8):
        i = pl.program_id(0)

        @pl.when(i == 0)
        def _():
            dw8[...] = jnp.zeros_like(dw8)
            dcb_ref[...] = jnp.zeros_like(dcb_ref)
            dlg_ref[...] = jnp.zeros_like(dlg_ref)
            dlb_ref[...] = jnp.zeros_like(dlb_ref)

        glu[0:HALO_A, :] = jnp.where(i > 0, pu_ref[...] * _sigmoid(pv_ref[...]), 0.0)
        glu[HALO_A:HALO_A + t, :] = u_ref[...] * _sigmoid(v_ref[...])
        has_next = i < n - 1
        dcb = jnp.zeros((1, w), F32)
        dlg = jnp.zeros((1, w), F32)
        dlb = jnp.zeros((1, w), F32)
        for c in range(ext // CHUNK):
            r0 = c * CHUNK
            own = r0 < t
            xh, rstd = _ln_stats(conv_ref[r0:r0 + CHUNK, :] if own else nconv_ref[...])
            a2 = xh * lg_ref[...] + lb_ref[...]
            if own:
                gt = gt_ref[r0:r0 + CHUNK, :]
                dzc = dz_ref[r0:r0 + CHUNK, :]
            else:
                gt = ngt_ref[...]
                dzc = ndz_ref[...]
            da2 = dzc * _silu(gt) * _dsilu(a2)
            dxh = da2 * lg_ref[...]
            dconv = rstd * (dxh - jnp.mean(dxh, axis=-1, keepdims=True)
                            - xh * jnp.mean(dxh * xh, axis=-1, keepdims=True))
            if own:
                dc[r0:r0 + CHUNK, :] = dconv
                da_ref[r0:r0 + CHUNK, 2 * w:3 * w] = (dzc * _silu(a2) * _dsilu(gt)).astype(BF16)
                dcb = dcb + jnp.sum(dconv, axis=0, keepdims=True)
                dlg = dlg + jnp.sum(da2 * xh, axis=0, keepdims=True)
                dlb = dlb + jnp.sum(da2, axis=0, keepdims=True)
            else:
                dc[r0:r0 + CHUNK, :] = jnp.where(has_next, dconv, 0.0)
        dcb_ref[...] += dcb
        dlg_ref[...] += dlg
        dlb_ref[...] += dlb
        for c in range(t // CHUNK):
            r0 = c * CHUNK
            dcc = dc[r0:r0 + CHUNK, :]
            dglu = cw_ref[0:1, :] * dc[pl.ds(r0 + CONV_A - 1, CHUNK), :]
            for k in range(1, CONV_A):
                dglu = dglu + cw_ref[k:k + 1, :] * dc[pl.ds(r0 + CONV_A - 1 - k, CHUNK), :]
            for k in range(CONV_A):
                prod = dcc * glu[pl.ds(r0 + 2 + k, CHUNK), :]
                dw8[k] += jnp.sum(prod.reshape(CHUNK // 8, 8, w), axis=0)
            sv = _sigmoid(v_ref[r0:r0 + CHUNK, :])
            da_ref[r0:r0 + CHUNK, 0:w] = (dglu * sv).astype(BF16)
            da_ref[r0:r0 + CHUNK, w:2 * w] = (dglu * u_ref[r0:r0 + CHUNK, :] * sv * (1.0 - sv)).astype(BF16)

        @pl.when(i == n - 1)
        def _():
            dw_ref[...] = jnp.sum(dw8[...], axis=1)

    def cur(col):
        return pl.BlockSpec((t, w), lambda i, col=col: (i, col))

    def prev(col):
        return pl.BlockSpec((HALO_A, w), lambda i, col=col: (jnp.maximum(i * r - 1, 0), col))

    def nxt(col):
        return pl.BlockSpec((HALO_A, w), lambda i, col=col: (jnp.minimum((i + 1) * r, nblk - 1), col))

    own_rows = pl.BlockSpec((t, w), lambda i: (i, 0))
    vec = pl.BlockSpec((1, w), lambda i: (0, 0))
    vo = jax.ShapeDtypeStruct((1, w), F32)
    return pl.pallas_call(
        body, name=name, grid=(n,),
        in_specs=[cur(0), cur(1), cur(2), own_rows, own_rows, prev(0), prev(1), nxt(2), nxt(0), nxt(0),
                  pl.BlockSpec((HALO_A, w), lambda i: (0, 0)), vec, vec],
        out_specs=[pl.BlockSpec((t, 3 * w), lambda i: (i, 0)), pl.BlockSpec((HALO_A, w), lambda i: (0, 0)), vec, vec, vec],
        out_shape=[jax.ShapeDtypeStruct((s, 3 * w), BF16), jax.ShapeDtypeStruct((HALO_A, w), F32), vo, vo, vo],
        scratch_shapes=[pltpu.VMEM((HALO_A + t, w), F32), pltpu.VMEM((ext, w), F32),
                        pltpu.VMEM((HALO_A, 8, w), F32)],
        compiler_params=_params(("arbitrary",)),
    )(proj, proj, proj, dz, conv, proj, proj, proj, dz, conv, cw, lg, lb)


def _conv_b(u_ext, cw_ref, t):
    acc = cw_ref[0:1, :] * u_ext[pl.ds(HALO_B - 2, t), :]
    for k in range(1, CONV_B):
        acc = acc + cw_ref[k:k + 1, :] * u_ext[pl.ds(HALO_B - 2 + k, t), :]
    return acc


def _branch_b_fwd(proj, cw, name):
    s = proj.shape[0]
    t = T_B
    w = cw.shape[1]
    r = t // HALO_B

    def body(h_ref, b_ref, c_ref, gt_ref, ph_ref, pc_ref, cw_ref, z_ref, u_ext):
        i = pl.program_id(0)
        u_ext[0:HALO_B, :] = jnp.where(i > 0, pc_ref[...] * ph_ref[...], 0.0)
        u_ext[HALO_B:HALO_B + t, :] = c_ref[...] * h_ref[...]
        cv = _conv_b(u_ext, cw_ref, t)
        z_ref[...] = (b_ref[...] * cv * _silu(gt_ref[...])).astype(BF16)

    def cur(col):
        return pl.BlockSpec((t, w), lambda i, col=col: (i, col))

    def prev(col):
        return pl.BlockSpec((HALO_B, w), lambda i, col=col: (jnp.maximum(i * r - 1, 0), col))

    return pl.pallas_call(
        body, name=name, grid=(s // t,),
        in_specs=[cur(3), cur(4), cur(5), cur(6), prev(3), prev(5), pl.BlockSpec((HALO_B, w), lambda i: (0, 0))],
        out_specs=pl.BlockSpec((t, w), lambda i: (i, 0)),
        out_shape=jax.ShapeDtypeStruct((s, w), BF16),
        scratch_shapes=[pltpu.VMEM((HALO_B + t, w), F32)],
        compiler_params=_params(("parallel",)),
    )(proj, proj, proj, proj, proj, proj, cw)


def _branch_b_bwd(proj, dz, cw, name):
    s = proj.shape[0]
    t = T_B
    w = cw.shape[1]
    r = t // HALO_B
    n = s // t
    nblk = s // HALO_B

    def body(h_ref, b_ref, c_ref, gt_ref, dz_ref, ph_ref, pc_ref, nb_ref, ngt_ref, ndz_ref, cw_ref,
             db_ref, dw_ref, u_ext, dcv_ext, dw8):
        i = pl.program_id(0)

        @pl.when(i == 0)
        def _():
            dw8[...] = jnp.zeros_like(dw8)

        u_ext[0:HALO_B, :] = jnp.where(i > 0, pc_ref[...] * ph_ref[...], 0.0)
        u_ext[HALO_B:HALO_B + t, :] = c_ref[...] * h_ref[...]
        cv = _conv_b(u_ext, cw_ref, t)
        gt = gt_ref[...]
        dhb = dz_ref[...] * _silu(gt)
        db_ref[:, 3 * w:4 * w] = (dz_ref[...] * b_ref[...] * cv * _dsilu(gt)).astype(BF16)
        db_ref[:, w:2 * w] = (dhb * cv).astype(BF16)
        dcv = dhb * b_ref[...]
        dcv_ext[0:t, :] = dcv
        ndcv = ndz_ref[...] * _silu(ngt_ref[...]) * nb_ref[...]
        dcv_ext[t:t + HALO_B, :] = jnp.where(i < n - 1, ndcv, 0.0)
        du = cw_ref[0:1, :] * dcv_ext[pl.ds(2, t), :]
        for k in range(1, CONV_B):
            du = du + cw_ref[k:k + 1, :] * dcv_ext[pl.ds(2 - k, t), :]
        db_ref[:, 2 * w:3 * w] = (du * h_ref[...]).astype(BF16)
        db_ref[:, 0:w] = (du * c_ref[...]).astype(BF16)
        for k in range(CONV_B):
            prod = dcv * u_ext[pl.ds(HALO_B - 2 + k, t), :]
            dw8[k] += jnp.sum(prod.reshape(t // 8, 8, w), axis=0)

        @pl.when(i == n - 1)
        def _():
            dw_ref[...] = jnp.sum(dw8[...], axis=1)

    def cur(col):
        return pl.BlockSpec((t, w), lambda i, col=col: (i, col))

    def prev(col):
        return pl.BlockSpec((HALO_B, w), lambda i, col=col: (jnp.maximum(i * r - 1, 0), col))

    def nxt(col):
        return pl.BlockSpec((HALO_B, w), lambda i, col=col: (jnp.minimum((i + 1) * r, nblk - 1), col))

    return pl.pallas_call(
        body, name=name, grid=(n,),
        in_specs=[cur(3), cur(4), cur(5), cur(6), pl.BlockSpec((t, w), lambda i: (i, 0)),
                  prev(3), prev(5), nxt(4), nxt(6),
                  pl.BlockSpec((HALO_B, w), lambda i: (jnp.minimum((i + 1) * r, nblk - 1), 0)),
                  pl.BlockSpec((HALO_B, w), lambda i: (0, 0))],
        out_specs=[pl.BlockSpec((t, 4 * w), lambda i: (i, 0)), pl.BlockSpec((HALO_B, w), lambda i: (0, 0))],
        out_shape=[jax.ShapeDtypeStruct((s, 4 * w), BF16), jax.ShapeDtypeStruct((HALO_B, w), F32)],
        scratch_shapes=[pltpu.VMEM((HALO_B + t, w), F32), pltpu.VMEM((t + HALO_B, w), F32),
                        pltpu.VMEM((HALO_B, 8, w), F32)],
        compiler_params=_params(("arbitrary",)),
    )(proj, proj, proj, proj, dz, proj, proj, proj, proj, dz, cw)


def _forget_prep(proj, bf, fblk, name):
    s = proj.shape[0]
    t = T_CUM

    def body(f_ref, bf_ref, cum_ref, cumt_ref, sgt_ref, carry):
        i = pl.program_id(0)

        @pl.when(i == 0)
        def _():
            carry[...] = jnp.zeros_like(carry)

        z = f_ref[...] + bf_ref[...]
        logf = jnp.minimum(z, 0.0) - jnp.log1p(jnp.exp(-jnp.abs(z)))
        tri = (_iota((t, t), 0) >= _iota((t, t), 1)).astype(F32)
        cum = _dot(tri, logf, NN, precision=lax.Precision.HIGHEST) + carry[0:1, :]
        carry[0:1, :] = cum[t - 1:t, :]
        cum_ref[...] = cum
        cumt_ref[...] = cum.T[0:8, :]
        sgt_ref[...] = _sigmoid(-z).T[0:8, :]

    return pl.pallas_call(
        body, name=name, grid=(s // t,),
        in_specs=[pl.BlockSpec((t, LANES), lambda i: (i, fblk)), pl.BlockSpec((1, LANES), lambda i: (0, 0))],
        out_specs=[pl.BlockSpec((t, LANES), lambda i: (i, 0)), pl.BlockSpec((8, t), lambda i: (0, i)),
                   pl.BlockSpec((8, t), lambda i: (0, i))],
        out_shape=[jax.ShapeDtypeStruct((s, LANES), F32), jax.ShapeDtypeStruct((8, s), F32),
                   jax.ShapeDtypeStruct((8, s), F32)],
        scratch_shapes=[pltpu.VMEM((8, LANES), F32)],
        compiler_params=_params(("arbitrary",)),
    )(proj, bf)


def _lane_pick(x, lane):
    return jnp.sum(jnp.where(_iota(x.shape, 1) == lane, x, 0.0), axis=1, keepdims=True)


def _sublane_pick(x, row):
    return jnp.sum(jnp.where(_iota(x.shape, 0) == row, x, 0.0), axis=0, keepdims=True)


def _head_mask(hh):
    lane = _iota((1, LANES), 1)
    return (lane >= HEAD_DIM * hh) & (lane < HEAD_DIM * (hh + 1))


def _causal_pairs(nq, ratio, kv_major):
    if kv_major:
        pairs = [(q, k) for k in range(nq // ratio) for q in range(k * ratio, nq)]
    else:
        pairs = [(q, k) for q in range(nq) for k in range(q // ratio + 1)]
    qs = np.asarray([p[0] for p in pairs], np.int32)
    ks = np.asarray([p[1] for p in pairs], np.int32)
    return qs, ks, np.where(ks == qs // ratio, qs % ratio + 1, 0).astype(np.int32)


def _fox_scores(qm, kb, cum_ref, cumt, h, row0, diag, tq, tk):
    cq0 = _lane_pick(cum_ref[0:1, :], h)
    sc = _dot(qm, kb, NT) + (cq0 - _sublane_pick(cumt, h))
    if not diag:
        return sc
    causal = (_iota((tq, tk), 0) + row0) >= _iota((tq, tk), 1)
    return jnp.where(causal, sc, NEG_BIG)


def _fox_fwd(proj, cum, cumt, qblk, kblk, vblk, name):
    s = proj.shape[0]
    tq, tk = T_ATT_Q, T_ATT_K
    n_pair = 4
    qi_np, ki_np, diag_np = _causal_pairs(s // tq, tk // tq, kv_major=False)
    scale = HEAD_DIM ** -0.5

    def body(qi_ref, ki_ref, diag_ref, q_ref, k_ref, v_ref, cum_ref, cumt_ref, o_ref, lse_ref, m_s, acc_s):
        hp = pl.program_id(0)
        step = pl.program_id(1)
        qi, ki, diag = qi_ref[step], ki_ref[step], diag_ref[step]

        @pl.when(ki == 0)
        def _():
            m_s[...] = jnp.full_like(m_s, NEG_BIG)
            acc_s[...] = jnp.zeros_like(acc_s)

        def update(blocks):
            width = tk if blocks is None else blocks * tq
            q = q_ref[...] * scale
            kb = k_ref[0:width, :].astype(BF16)
            v = v_ref[0:width, :]
            cumt = cumt_ref[:, 0:width]
            m_old = [m_s[0], m_s[1]]
            acc_old = [acc_s[0], acc_s[1]]
            scores, values = [], []
            for hh in range(2):
                hm = _head_mask(hh)
                qm = jnp.where(hm, q, 0.0).astype(BF16)
                values.append(jnp.where(hm, v, 1.0).astype(BF16))
                scores.append(_fox_scores(qm, kb, cum_ref, cumt, 2 * hp + hh, qi * tq - ki * tk, blocks is not None,
                                          tq, width))
            m_new = [jnp.maximum(m_old[hh], jnp.max(scores[hh], axis=1, keepdims=True)) for hh in range(2)]
            probs = [jnp.exp(scores[hh] - m_new[hh]).astype(BF16) for hh in range(2)]
            acc_new = [jnp.exp(m_old[hh] - m_new[hh]) * acc_old[hh] + _dot(probs[hh], values[hh], NN)
                       for hh in range(2)]
            for hh in range(2):
                acc_s[hh] = acc_new[hh]
                m_s[hh] = m_new[hh]

        @pl.when(diag == 0)
        def _():
            update(None)

        for blocks in range(1, tk // tq + 1):
            pl.when(diag == blocks)(functools.partial(update, blocks))

        @pl.when(diag >= 1)
        def _():
            lane = _iota((tq, LANES), 1)
            a0, a1 = acc_s[0], acc_s[1]
            o_ref[...] = jnp.where(lane < HEAD_DIM, a0 / pltpu.roll(a0, HEAD_DIM, axis=1),
                                   a1 / pltpu.roll(a1, HEAD_DIM, axis=1))
            lse0 = m_s[0] + jnp.log(a0[:, HEAD_DIM:HEAD_DIM + 1])
            lse1 = m_s[1] + jnp.log(a1[:, 0:1])
            lse_ref[0] = jnp.where(lane == 0, lse0, jnp.where(lane == 1, lse1, 0.0))

    grid_spec = pltpu.PrefetchScalarGridSpec(
        num_scalar_prefetch=3, grid=(n_pair, len(qi_np)),
        in_specs=[pl.BlockSpec((tq, LANES), lambda hp, st, qi, ki, dg: (qi[st], qblk + hp)),
                  pl.BlockSpec((tk, LANES), lambda hp, st, qi, ki, dg: (ki[st], kblk + hp)),
                  pl.BlockSpec((tk, LANES), lambda hp, st, qi, ki, dg: (ki[st], vblk + hp)),
                  pl.BlockSpec((tq, LANES), lambda hp, st, qi, ki, dg: (qi[st], 0)),
                  pl.BlockSpec((8, tk), lambda hp, st, qi, ki, dg: (0, ki[st]))],
        out_specs=[pl.BlockSpec((tq, LANES), lambda hp, st, qi, ki, dg: (qi[st], hp)),
                   pl.BlockSpec((1, tq, LANES), lambda hp, st, qi, ki, dg: (hp, qi[st], 0))],
        scratch_shapes=[pltpu.VMEM((2, tq, 1), F32), pltpu.VMEM((2, tq, LANES), F32)])
    return pl.pallas_call(
        body, name=name, grid_spec=grid_spec,
        out_shape=[jax.ShapeDtypeStruct((s, n_pair * LANES), F32), jax.ShapeDtypeStruct((n_pair, s, LANES), F32)],
        compiler_params=_params(("parallel", "arbitrary")),
    )(jnp.asarray(qi_np), jnp.asarray(ki_np), jnp.asarray(diag_np), proj, proj, proj, cum, cumt)


def _fox_bwd_prep(proj, dz, o, gblk, name):
    s, w = o.shape
    t = T_ELEM
    n_head = w // HEAD_DIM

    def body(gt_ref, dz_ref, o_ref, do_ref, dg_ref, dl_ref):
        gt = gt_ref[...]
        do = dz_ref[...] * _silu(gt)
        do_ref[...] = do
        dg_ref[...] = (dz_ref[...] * o_ref[...] * _dsilu(gt)).astype(BF16)
        sel = (_iota((w, LANES), 0) // HEAD_DIM == _iota((w, LANES), 1)).astype(F32)
        dl_ref[...] = _dot(do * o_ref[...], sel, NN, precision=lax.Precision.HIGHEST)

    assert n_head <= LANES
    row = pl.BlockSpec((t, w), lambda i: (i, 0))
    return pl.pallas_call(
        body, name=name, grid=(s // t,),
        in_specs=[pl.BlockSpec((t, w), lambda i: (i, gblk)), row, row],
        out_specs=[row, row, pl.BlockSpec((t, LANES), lambda i: (i, 0))],
        out_shape=[jax.ShapeDtypeStruct((s, w), F32), jax.ShapeDtypeStruct((s, w), BF16),
                   jax.ShapeDtypeStruct((s, LANES), F32)],
        compiler_params=_params(("parallel",)),
    )(proj, dz, o)


def _fox_bwd(proj, do, lse, delta, cum, cumt, qblk, kblk, vblk, name):
    s = proj.shape[0]
    tq, tk = T_ATT_Q, T_ATT_K
    nq = s // tq
    n_pair = 4
    qi_np, ki_np, diag_np = _causal_pairs(nq, tk // tq, kv_major=True)
    n_step = len(qi_np)
    scale = HEAD_DIM ** -0.5

    def body(qi_ref, ki_ref, diag_ref, q_ref, k_ref, v_ref, do_ref, lse_ref, dl_ref, cum_ref, cumt_ref,
             dq_ref, dk_ref, dv_ref, dct_ref, dcq_ref, dq_s, dk_s, dv_s, dc_s, dcq_s):
        hp = pl.program_id(0)
        step = pl.program_id(1)
        qi, ki, diag = qi_ref[step], ki_ref[step], diag_ref[step]

        @pl.when(step == 0)
        def _():
            dq_s[...] = jnp.zeros_like(dq_s)
            dcq_s[...] = jnp.zeros_like(dcq_s)

        @pl.when(qi == ki * (tk // tq))
        def _():
            dk_s[...] = jnp.zeros_like(dk_s)
            dv_s[...] = jnp.zeros_like(dv_s)
            dc_s[...] = jnp.zeros_like(dc_s)

        def update(blocks):
            width = tk if blocks is None else blocks * tq
            q = q_ref[...] * scale
            do = do_ref[...]
            kb = k_ref[0:width, :].astype(BF16)
            vb = v_ref[0:width, :].astype(BF16)
            cumt = cumt_ref[:, 0:width]
            sub = _iota((8, width), 0)
            dq_new = jnp.zeros((tq, LANES), F32)
            dcq_new = jnp.zeros((tq, LANES), F32)
            lane = _iota((tq, LANES), 1)
            for hh in range(2):
                h = 2 * hp + hh
                hm = _head_mask(hh)
                qm = jnp.where(hm, q, 0.0).astype(BF16)
                dom = jnp.where(hm, do, 0.0).astype(BF16)
                sc = _fox_scores(qm, kb, cum_ref, cumt, h, qi * tq - ki * tk, blocks is not None, tq, width)
                p = jnp.exp(sc - _lane_pick(lse_ref[0], hh))
                dv_s[0:width, :] += _dot(p.astype(BF16), dom, TN)
                dp = _dot(dom, vb, NT)
                ds = p * (dp - _lane_pick(dl_ref[...], h))
                dc_s[:, 0:width] += jnp.where(sub == h, -jnp.sum(ds, axis=0, keepdims=True), 0.0)
                dcq_new = dcq_new + jnp.where(lane == h, jnp.sum(ds, axis=1, keepdims=True), 0.0)
                dsb = ds.astype(BF16)
                dk_s[0:width, :] += _dot(dsb, qm, TN)
                dq_new = dq_new + jnp.where(hm, _dot(dsb, kb, NN), 0.0)
            row0 = pl.multiple_of(qi * tq, tq)
            dq_s[pl.ds(row0, tq), :] += dq_new * scale
            dcq_s[pl.ds(row0, tq), :] += dcq_new

        @pl.when(diag == 0)
        def _():
            update(None)

        for blocks in range(1, tk // tq + 1):
            pl.when(diag == blocks)(functools.partial(update, blocks))

        @pl.when(qi == nq - 1)
        def _():
            dk_ref[...] = dk_s[...].astype(BF16)
            dv_ref[...] = dv_s[...].astype(BF16)
            dct_ref[0] = dc_s[...]

        @pl.when(step == n_step - 1)
        def _():
            dq_ref[...] = dq_s[...].astype(BF16)
            dcq_ref[0] = dcq_s[...]

    grid_spec = pltpu.PrefetchScalarGridSpec(
        num_scalar_prefetch=3, grid=(n_pair, n_step),
        in_specs=[pl.BlockSpec((tq, LANES), lambda hp, st, qi, ki, dg: (qi[st], qblk + hp)),
                  pl.BlockSpec((tk, LANES), lambda hp, st, qi, ki, dg: (ki[st], kblk + hp)),
                  pl.BlockSpec((tk, LANES), lambda hp, st, qi, ki, dg: (ki[st], vblk + hp)),
                  pl.BlockSpec((tq, LANES), lambda hp, st, qi, ki, dg: (qi[st], hp)),
                  pl.BlockSpec((1, tq, LANES), lambda hp, st, qi, ki, dg: (hp, qi[st], 0)),
                  pl.BlockSpec((tq, LANES), lambda hp, st, qi, ki, dg: (qi[st], 0)),
                  pl.BlockSpec((tq, LANES), lambda hp, st, qi, ki, dg: (qi[st], 0)),
                  pl.BlockSpec((8, tk), lambda hp, st, qi, ki, dg: (0, ki[st]))],
        out_specs=[pl.BlockSpec((s, LANES), lambda hp, st, qi, ki, dg: (0, hp)),
                   pl.BlockSpec((tk, LANES), lambda hp, st, qi, ki, dg: (ki[st], hp)),
                   pl.BlockSpec((tk, LANES), lambda hp, st, qi, ki, dg: (ki[st], hp)),
                   pl.BlockSpec((1, 8, tk), lambda hp, st, qi, ki, dg: (hp, 0, ki[st])),
                   pl.BlockSpec((1, s, LANES), lambda hp, st, qi, ki, dg: (hp, 0, 0))],
        scratch_shapes=[pltpu.VMEM((s, LANES), F32), pltpu.VMEM((tk, LANES), F32), pltpu.VMEM((tk, LANES), F32),
                        pltpu.VMEM((8, tk), F32), pltpu.VMEM((s, LANES), F32)])
    w = n_pair * LANES
    return pl.pallas_call(
        body, name=name, grid_spec=grid_spec,
        out_shape=[jax.ShapeDtypeStruct((s, w), BF16), jax.ShapeDtypeStruct((s, w), BF16),
                   jax.ShapeDtypeStruct((s, w), BF16), jax.ShapeDtypeStruct((n_pair, 8, s), F32),
                   jax.ShapeDtypeStruct((n_pair, s, LANES), F32)],
        compiler_params=_params(("parallel", "arbitrary")),
    )(jnp.asarray(qi_np), jnp.asarray(ki_np), jnp.asarray(diag_np), proj, proj, proj, do, lse, delta, cum, cumt)


def _forget_bwd(dcumt4, dcumq4, sgt, name):
    s = sgt.shape[1]
    t = T_CUM
    n = s // t

    def body(d_ref, dq_ref, sg_ref, df_ref, dbf_ref, carry):
        i = pl.program_id(0)

        @pl.when(i == 0)
        def _():
            carry[...] = jnp.zeros_like(carry)
            dbf_ref[...] = jnp.zeros_like(dbf_ref)

        dq = dq_ref[0] + dq_ref[1] + dq_ref[2] + dq_ref[3]
        d = d_ref[0] + d_ref[1] + d_ref[2] + d_ref[3] + dq.T[0:8, :]
        upper = (_iota((t, t), 0) >= _iota((t, t), 1)).astype(F32)
        dlog = _dot(d, upper, NN, precision=lax.Precision.HIGHEST) + carry[:, 0:1]
        carry[...] += jnp.sum(d, axis=1, keepdims=True)
        dzt = dlog * sg_ref[...]
        dbf_ref[...] += jnp.sum(dzt, axis=1, keepdims=True)
        padded = jnp.concatenate([dzt, jnp.zeros((LANES - 8, t), F32)], axis=0)
        df_ref[...] = padded.T.astype(BF16)

    return pl.pallas_call(
        body, name=name, grid=(n,),
        in_specs=[pl.BlockSpec((4, 8, t), lambda i: (0, 0, n - 1 - i)),
                  pl.BlockSpec((4, t, LANES), lambda i: (0, n - 1 - i, 0)),
                  pl.BlockSpec((8, t), lambda i: (0, n - 1 - i))],
        out_specs=[pl.BlockSpec((t, LANES), lambda i: (n - 1 - i, 0)), pl.BlockSpec((8, LANES), lambda i: (0, 0))],
        out_shape=[jax.ShapeDtypeStruct((s, LANES), BF16), jax.ShapeDtypeStruct((8, LANES), F32)],
        scratch_shapes=[pltpu.VMEM((8, LANES), F32)],
        compiler_params=_params(("arbitrary",)),
    )(dcumt4, dcumq4, sgt)


def _mem_softmax(qm, kp):
    sc = _dot(qm, kp, NT) * (HEAD_DIM ** -0.5)
    e = jnp.exp(sc - jnp.max(sc, axis=1, keepdims=True))
    return e / jnp.sum(e, axis=1, keepdims=True)


def _branch_m_fwd(proj, mkv, mblk, name):
    s = proj.shape[0]
    t = T_M
    mw = mkv.shape[1] // 2
    ml = mkv.shape[0]

    def body(m_ref, kv_ref, z_ref):
        outs = []
        for pr in range(mw // LANES):
            qp = m_ref[:, pr * LANES:(pr + 1) * LANES]
            kp = kv_ref[:, pr * LANES:(pr + 1) * LANES].astype(BF16)
            vp = kv_ref[:, mw + pr * LANES:mw + (pr + 1) * LANES].astype(BF16)
            oh = []
            for hh in range(2):
                qm = jnp.where(_head_mask(hh), qp, 0.0).astype(BF16)
                oh.append(_dot(_mem_softmax(qm, kp).astype(BF16), vp, NN))
            outs.append(jnp.where(_iota((t, LANES), 1) < HEAD_DIM, oh[0], oh[1]))
        o = jnp.concatenate(outs, axis=1)
        z_ref[...] = (o * _silu(m_ref[:, mw:2 * mw])).astype(BF16)

    return pl.pallas_call(
        body, name=name, grid=(s // t,),
        in_specs=[pl.BlockSpec((t, 2 * mw), lambda i: (i, mblk)), pl.BlockSpec((ml, 2 * mw), lambda i: (0, 0))],
        out_specs=pl.BlockSpec((t, mw), lambda i: (i, 0)),
        out_shape=jax.ShapeDtypeStruct((s, mw), BF16),
        compiler_params=_params(("parallel",)),
    )(proj, mkv)


def _branch_m_bwd(proj, mkv, dz, mblk, name):
    s = proj.shape[0]
    t = T_M
    mw = mkv.shape[1] // 2
    ml = mkv.shape[0]
    scale = HEAD_DIM ** -0.5

    def body(m_ref, kv_ref, dz_ref, dm_ref, dkv_ref):
        i = pl.program_id(0)

        @pl.when(i == 0)
        def _():
            dkv_ref[...] = jnp.zeros_like(dkv_ref)

        gt = m_ref[:, mw:2 * mw]
        dz = dz_ref[...]
        do = dz * _silu(gt)
        outs = []
        for pr in range(mw // LANES):
            cols = slice(pr * LANES, (pr + 1) * LANES)
            vcols = slice(mw + pr * LANES, mw + (pr + 1) * LANES)
            qp = m_ref[:, cols]
            kp = kv_ref[:, cols].astype(BF16)
            vp = kv_ref[:, vcols].astype(BF16)
            dop = do[:, cols]
            oh = []
            dq = jnp.zeros((t, LANES), F32)
            dk = jnp.zeros((ml, LANES), F32)
            dv = jnp.zeros((ml, LANES), F32)
            for hh in range(2):
                hm = _head_mask(hh)
                qm = jnp.where(hm, qp, 0.0).astype(BF16)
                dom = jnp.where(hm, dop, 0.0).astype(BF16)
                p = _mem_softmax(qm, kp)
                pb = p.astype(BF16)
                oh.append(_dot(pb, vp, NN))
                dv = dv + _dot(pb, dom, TN)
                dp = _dot(dom, vp, NT)
                ds = p * (dp - jnp.sum(dp * p, axis=1, keepdims=True))
                dsb = (ds * scale).astype(BF16)
                dq = dq + jnp.where(hm, _dot(dsb, kp, NN), 0.0)
                dk = dk + _dot(dsb, qm, TN)
            outs.append(jnp.where(_iota((t, LANES), 1) < HEAD_DIM, oh[0], oh[1]))
            dm_ref[:, cols] = dq.astype(BF16)
            dkv_ref[:, cols] += dk
            dkv_ref[:, vcols] += dv
        o = jnp.concatenate(outs, axis=1)
        dm_ref[:, mw:2 * mw] = (dz * o * _dsilu(gt)).astype(BF16)

    return pl.pallas_call(
        body, name=name, grid=(s // t,),
        in_specs=[pl.BlockSpec((t, 2 * mw), lambda i: (i, mblk)), pl.BlockSpec((ml, 2 * mw), lambda i: (0, 0)),
                  pl.BlockSpec((t, mw), lambda i: (i, 0))],
        out_specs=[pl.BlockSpec((t, 2 * mw), lambda i: (i, 0)), pl.BlockSpec((ml, 2 * mw), lambda i: (0, 0))],
        out_shape=[jax.ShapeDtypeStruct((s, 2 * mw), BF16), jax.ShapeDtypeStruct((ml, 2 * mw), F32)],
        compiler_params=_params(("arbitrary",)),
    )(proj, mkv, dz)


def _merge_parts(z_refs, g_refs, pcat, bounds):
    ys, sgs = [], []
    merged = None
    for zr, gr, (lo, hi) in zip(z_refs, g_refs, bounds):
        y = _dot(zr[...], pcat[lo:hi, :], NN)
        sg = _sigmoid(gr[...])
        ys.append(y)
        sgs.append(sg)
        merged = sg * y if merged is None else merged + sg * y
    return ys, sgs, merged


def _branch_bounds(zs):
    bounds, lo = [], 0
    for z in zs:
        bounds.append((lo, lo + z.shape[1]))
        lo += z.shape[1]
    return bounds


def _matmul_copies(x, name):
    s, d = x.shape
    t = T_MERGE

    def body(x_ref, b_ref, bt_ref):
        b_ref[...] = x_ref[...].astype(BF16)
        bt_ref[...] = x_ref[...].T.astype(BF16)

    return pl.pallas_call(
        body, name=name, grid=(s // t,), in_specs=[pl.BlockSpec((t, d), lambda i: (i, 0))],
        out_specs=[pl.BlockSpec((t, d), lambda i: (i, 0)), pl.BlockSpec((d, t), lambda i: (0, i))],
        out_shape=[jax.ShapeDtypeStruct((s, d), BF16), jax.ShapeDtypeStruct((d, s), BF16)],
        compiler_params=_params(("parallel",)))(x)


def _merge_fwd(zs, proj, gblk, pcat, wout, x, lng, lnb, alpha, name):
    s, d = x.shape
    t = T_MERGE
    bounds = _branch_bounds(zs)

    def body(*refs):
        z_refs, g_refs = refs[0:4], refs[4:8]
        pcat_hbm, wout_hbm, x_ref, lng_ref, lnb_ref, y_ref, yb_ref, ybt_ref, pcat_v, wout_v = refs[8:]

        @pl.when(pl.program_id(0) == 0)
        def _():
            pltpu.sync_copy(pcat_hbm, pcat_v)
            pltpu.sync_copy(wout_hbm, wout_v)

        _, _, merged = _merge_parts(z_refs, g_refs, pcat_v, bounds)
        h = alpha * x_ref[...] + _dot(merged.astype(BF16), wout_v[...], NN)
        xh, _ = _ln_stats(h)
        y = xh * lng_ref[...] + lnb_ref[...]
        y_ref[...] = y
        yb_ref[...] = y.astype(BF16)
        ybt_ref[...] = y.T.astype(BF16)

    row = pl.BlockSpec((t, d), lambda i: (i, 0))
    vec = pl.BlockSpec((1, d), lambda i: (0, 0))
    in_specs = ([pl.BlockSpec((t, z.shape[1]), lambda i: (i, 0)) for z in zs]
                + [pl.BlockSpec((t, d), lambda i, k=k: (i, gblk + k)) for k in range(4)]
                + [ANY, ANY, row, vec, vec])
    return pl.pallas_call(
        body, name=name, grid=(s // t,), in_specs=in_specs,
        out_specs=[row, row, pl.BlockSpec((d, t), lambda i: (0, i))],
        out_shape=[jax.ShapeDtypeStruct((s, d), F32), jax.ShapeDtypeStruct((s, d), BF16),
                   jax.ShapeDtypeStruct((d, s), BF16)],
        scratch_shapes=[pltpu.VMEM(pcat.shape, BF16), pltpu.VMEM(wout.shape, BF16)],
        compiler_params=_params(("arbitrary",)),
    )(*zs, proj, proj, proj, proj, pcat, wout, x, lng, lnb)


def _merge_bwd(zs, proj, gblk, pcat, wout, x, lng, lnb, dy, alpha, name):
    s, d = x.shape
    t = T_MERGE
    n = s // t
    bounds = _branch_bounds(zs)

    def body(*refs):
        z_refs, g_refs = refs[0:4], refs[4:8]
        pcat_hbm, wout_hbm, x_ref, lng_ref, lnb_ref, dy_ref = refs[8:14]
        dx_ref, dg_ref = refs[14:16]
        dz_refs = refs[16:20]
        dpcat_hbm, dwout_hbm, dlng_ref, dlnb_ref = refs[20:24]
        pcat_v, wout_v, dpcat_v, dwout_v = refs[24:]
        i = pl.program_id(0)

        @pl.when(i == 0)
        def _():
            pltpu.sync_copy(pcat_hbm, pcat_v)
            pltpu.sync_copy(wout_hbm, wout_v)
            dpcat_v[...] = jnp.zeros_like(dpcat_v)
            dwout_v[...] = jnp.zeros_like(dwout_v)
            dlng_ref[...] = jnp.zeros_like(dlng_ref)
            dlnb_ref[...] = jnp.zeros_like(dlnb_ref)

        ys, sgs, merged = _merge_parts(z_refs, g_refs, pcat_v, bounds)
        mb = merged.astype(BF16)
        h = alpha * x_ref[...] + _dot(mb, wout_v[...], NN)
        xh, rstd = _ln_stats(h)
        dyv = dy_ref[...]
        dlng_ref[...] += jnp.sum(dyv * xh, axis=0, keepdims=True)
        dlnb_ref[...] += jnp.sum(dyv, axis=0, keepdims=True)
        dxh = dyv * lng_ref[...]
        dh = rstd * (dxh - jnp.mean(dxh, axis=-1, keepdims=True) - xh * jnp.mean(dxh * xh, axis=-1, keepdims=True))
        dx_ref[...] = alpha * dh
        dhb = dh.astype(BF16)
        dwout_v[...] += _dot(mb, dhb, TN)
        dmerged = _dot(dhb, wout_v[...], NT)
        for k, (zr, (lo, hi)) in enumerate(zip(z_refs, bounds)):
            sg = sgs[k]
            dg_ref[:, k * d:(k + 1) * d] = (dmerged * ys[k] * sg * (1.0 - sg)).astype(BF16)
            dyk = (dmerged * sg).astype(BF16)
            dpcat_v[lo:hi, :] += _dot(zr[...], dyk, TN)
            dz_refs[k][...] = _dot(dyk, pcat_v[lo:hi, :], NT)

        @pl.when(i == n - 1)
        def _():
            pltpu.sync_copy(dpcat_v, dpcat_hbm)
            pltpu.sync_copy(dwout_v, dwout_hbm)

    row = pl.BlockSpec((t, d), lambda i: (i, 0))
    vec = pl.BlockSpec((1, d), lambda i: (0, 0))
    z_specs = [pl.BlockSpec((t, z.shape[1]), lambda i: (i, 0)) for z in zs]
    in_specs = (z_specs + [pl.BlockSpec((t, d), lambda i, k=k: (i, gblk + k)) for k in range(4)]
                + [ANY, ANY, row, vec, vec, row])
    out_specs = [row, pl.BlockSpec((t, 4 * d), lambda i: (i, 0))] + z_specs + [ANY, ANY, vec, vec]
    vo = jax.ShapeDtypeStruct((1, d), F32)
    out_shape = ([jax.ShapeDtypeStruct((s, d), F32), jax.ShapeDtypeStruct((s, 4 * d), BF16)]
                 + [jax.ShapeDtypeStruct(z.shape, F32) for z in zs]
                 + [jax.ShapeDtypeStruct(pcat.shape, F32), jax.ShapeDtypeStruct(wout.shape, F32), vo, vo])
    return pl.pallas_call(
        body, name=name, grid=(n,), in_specs=in_specs, out_specs=out_specs, out_shape=out_shape,
        scratch_shapes=[pltpu.VMEM(pcat.shape, BF16), pltpu.VMEM(wout.shape, BF16),
                        pltpu.VMEM(pcat.shape, F32), pltpu.VMEM(wout.shape, F32)],
        compiler_params=_params(("arbitrary",)),
    )(*zs, proj, proj, proj, proj, pcat, wout, x, lng, lnb, dy)


def _loss_head(y, target, name):
    s, d = y.shape
    t = T_ELEM

    def body(y_ref, t_ref, dy_ref, loss_ref):
        @pl.when(pl.program_id(0) == 0)
        def _():
            loss_ref[...] = jnp.zeros_like(loss_ref)

        e = y_ref[...] - t_ref[...]
        dy_ref[...] = e * (1.0 / d)
        loss_ref[...] += 0.5 * jnp.sum(jnp.mean(e * e, axis=-1, keepdims=True), axis=0, keepdims=True)

    row = pl.BlockSpec((t, d), lambda i: (i, 0))
    return pl.pallas_call(
        body, name=name, grid=(s // t,), in_specs=[row, row],
        out_specs=[row, pl.BlockSpec((8, LANES), lambda i: (0, 0))],
        out_shape=[jax.ShapeDtypeStruct((s, d), F32), jax.ShapeDtypeStruct((8, LANES), F32)],
        compiler_params=_params(("arbitrary",)),
    )(y, target)


def _adamw(w, g, m, v, name):
    n_l, r, c = w.shape
    t = _pick(r, (256, 128, 64, 32, 16, 8))

    def body(w_ref, g_ref, m_ref, v_ref, d_ref, nm_ref, nv_ref):
        gv = g_ref[...]
        nm = ADAM_B1 * m_ref[...] + (1.0 - ADAM_B1) * gv
        nv = ADAM_B2 * v_ref[...] + (1.0 - ADAM_B2) * (gv * gv)
        m_hat = nm / (1.0 - ADAM_B1 ** ADAM_STEP)
        v_hat = nv / (1.0 - ADAM_B2 ** ADAM_STEP)
        d_ref[...] = -ADAM_LR * (m_hat / (jnp.sqrt(v_hat) + ADAM_EPS) + ADAM_WD * w_ref[...])
        nm_ref[...] = nm
        nv_ref[...] = nv

    blk = pl.BlockSpec((1, t, c), lambda l, i: (l, i, 0))
    o = jax.ShapeDtypeStruct((n_l, r, c), F32)
    return pl.pallas_call(body, name=name, grid=(n_l, r // t), in_specs=[blk] * 4, out_specs=[blk] * 3,
                          out_shape=[o, o, o], compiler_params=_params(("parallel", "parallel")))(w, g, m, v)


def _sum_leading(x, out_dtype, name):
    k, r, c = x.shape
    t = _pick(r, (256, 128, 64, 32, 16, 8))

    def body(x_ref, o_ref):
        acc = x_ref[0].astype(F32)
        for j in range(1, k):
            acc = acc + x_ref[j].astype(F32)
        o_ref[...] = acc.astype(out_dtype)

    return pl.pallas_call(body, name=name, grid=(r // t,),
                          in_specs=[pl.BlockSpec((k, t, c), lambda i: (0, i, 0))],
                          out_specs=pl.BlockSpec((t, c), lambda i: (i, 0)),
                          out_shape=jax.ShapeDtypeStruct((r, c), out_dtype), compiler_params=_params(("parallel",)))(x)


def _position():
    return lax.axis_index("x"), lax.axis_index("y"), lax.axis_index("c")


def _chip_peers(x, y):
    return [(1 - x, y), (x, 1 - y), (1 - x, 1 - y)]


def _comm_call(body, n_in, out_shape, n_remote, n_local, name):
    return pl.pallas_call(
        body, name=name, in_specs=[ANY] * n_in, out_specs=[ANY] * len(out_shape), out_shape=out_shape,
        scratch_shapes=[pltpu.SemaphoreType.DMA((n_remote,)), pltpu.SemaphoreType.DMA((n_remote,)),
                        pltpu.SemaphoreType.DMA((max(n_local, 1),))])


def _run_copies(local, remote, send, recv, loc):
    copies = [pltpu.make_async_copy(src, dst, loc.at[k]) for k, (src, dst) in enumerate(local)]
    copies += [pltpu.make_async_remote_copy(src_ref=src, dst_ref=dst, send_sem=send.at[k], recv_sem=recv.at[k],
                                            device_id=peer, device_id_type=MESH)
               for k, (src, dst, peer) in enumerate(remote)]
    for cp in copies:
        cp.start()
    for cp in copies:
        cp.wait()


COPY_BYTES = 1024 * 1024


def _n_copies(rows, row_bytes, align):
    n = 8
    while n > 1 and (rows % (n * align) or rows // n * row_bytes < COPY_BYTES):
        n //= 2
    return n


def _allgather_chips(arrs, name):
    n = len(arrs)
    per_layer = [a.size * a.dtype.itemsize // a.shape[0] >= COPY_BYTES for a in arrs]
    n_each = [a.shape[0] if pl_ else 1 for a, pl_ in zip(arrs, per_layer)]

    def body(*refs):
        ins, outs = refs[:n], refs[n:2 * n]
        send, recv, loc = refs[2 * n:]
        x, y, c = _position()
        me = 2 * x + y
        local, remote = [], []
        for a in range(n):
            if per_layer[a]:
                parts = [(ins[a].at[l], outs[a].at[me, l]) for l in range(arrs[a].shape[0])]
            else:
                parts = [(ins[a], outs[a].at[me])]
            local += parts
            for px, py in _chip_peers(x, y):
                remote += [(src, dst, (px, py, c)) for src, dst in parts]
        _run_copies(local, remote, send, recv, loc)

    out_shape = [jax.ShapeDtypeStruct((4,) + a.shape, a.dtype) for a in arrs]
    return _comm_call(body, n, out_shape, 3 * sum(n_each), sum(n_each), name)(*arrs)


def _allgather_all(v, name):
    def body(v_ref, o_ref, send, recv, loc):
        x, y, c = _position()
        me = 4 * x + 2 * y + c
        remote = []
        for k in range(1, 8):
            fx, fy, fc = (k >> 2) & 1, (k >> 1) & 1, k & 1
            remote.append((v_ref, o_ref.at[me], (x ^ fx, y ^ fy, c ^ fc)))
        _run_copies([(v_ref, o_ref.at[me])], remote, send, recv, loc)

    return _comm_call(body, 1, [jax.ShapeDtypeStruct((8,) + v.shape, v.dtype)], 7, 1, name)(v)[0]


def _pair_exchange_sum(g, name):
    _, _, n, r, cc = g.shape

    def body(c_ref, mine_ref, send_ref, o_ref, land, send_sem, recv_sem):
        x, y, c = _position()
        slot = (pl.program_id(0) * n + pl.program_id(1)) % 2
        push = pltpu.make_async_remote_copy(
            src_ref=send_ref.at[0, 0, 0], dst_ref=land.at[slot], send_sem=send_sem.at[slot],
            recv_sem=recv_sem.at[slot], device_id=(x, y, 1 - c), device_id_type=MESH)
        push.start()
        push.wait_recv()
        o_ref[0, 0] = (mine_ref[0, 0, 0] + land[slot]).astype(BF16)
        push.wait_send()

    grid_spec = pltpu.PrefetchScalarGridSpec(
        num_scalar_prefetch=1, grid=(4, n),
        in_specs=[pl.BlockSpec((1, 1, 1, r, cc), lambda j, k, c: (j, c[0], k, 0, 0)),
                  pl.BlockSpec((1, 1, 1, r, cc), lambda j, k, c: (j, 1 - c[0], k, 0, 0))],
        out_specs=pl.BlockSpec((1, 1, r, cc), lambda j, k, c: (j, k, 0, 0)),
        scratch_shapes=[pltpu.VMEM((2, r, cc), F32), pltpu.SemaphoreType.DMA((2,)), pltpu.SemaphoreType.DMA((2,))])
    return pl.pallas_call(
        body, name=name, grid_spec=grid_spec, out_shape=jax.ShapeDtypeStruct((4, n, r, cc), BF16),
        compiler_params=_params(("arbitrary", "arbitrary")))(_scalar(lax.axis_index("c")), g, g)


def _scalar(v):
    return v.astype(jnp.int32).reshape(1)


def _chip_exchange_sum(p, name):
    _, n, r, cc = p.shape

    def body(i0, i1, i2, i3, own_ref, s0_ref, s1_ref, s2_ref, mine_ref, theirs_ref, land, total, land_pair,
             send_sem, recv_sem, pair_send, pair_recv):
        x, y, c = _position()
        slot = pl.program_id(0) % 2
        pushes = [pltpu.make_async_remote_copy(
            src_ref=src.at[0, 0], dst_ref=land.at[slot, j], send_sem=send_sem.at[slot, j], recv_sem=recv_sem.at[slot, j],
            device_id=(px, py, c), device_id_type=MESH)
            for j, (src, (px, py)) in enumerate(zip((s0_ref, s1_ref, s2_ref), _chip_peers(x, y)))]
        for cp in pushes:
            cp.start()
        acc = own_ref[0, 0].astype(F32)
        for j, cp in enumerate(pushes):
            cp.wait_recv()
            acc = acc + land[slot, j].astype(F32)
        mine_ref[0] = acc
        total[slot] = acc
        share = pltpu.make_async_remote_copy(
            src_ref=total.at[slot], dst_ref=land_pair.at[slot], send_sem=pair_send.at[slot], recv_sem=pair_recv.at[slot],
            device_id=(x, y, 1 - c), device_id_type=MESH)
        share.start()
        share.wait_recv()
        theirs_ref[0] = land_pair[slot]
        for cp in pushes:
            cp.wait_send()
        share.wait_send()

    def slot_spec(which):
        return pl.BlockSpec((1, 1, r, cc), lambda k, *idx, which=which: (idx[which][0], k, 0, 0))

    out_spec = pl.BlockSpec((1, r, cc), lambda k, *idx: (k, 0, 0))
    grid_spec = pltpu.PrefetchScalarGridSpec(
        num_scalar_prefetch=4, grid=(n,), in_specs=[slot_spec(0), slot_spec(1), slot_spec(2), slot_spec(3)],
        out_specs=[out_spec, out_spec],
        scratch_shapes=[pltpu.VMEM((2, 3, r, cc), BF16), pltpu.VMEM((2, r, cc), F32), pltpu.VMEM((2, r, cc), F32),
                        pltpu.SemaphoreType.DMA((2, 3)), pltpu.SemaphoreType.DMA((2, 3)),
                        pltpu.SemaphoreType.DMA((2,)), pltpu.SemaphoreType.DMA((2,))])
    o = jax.ShapeDtypeStruct((n, r, cc), F32)
    x, y, _ = _position()
    chips = [_scalar(2 * x + y)] + [_scalar(2 * px + py) for px, py in _chip_peers(x, y)]
    return pl.pallas_call(body, name=name, grid_spec=grid_spec, out_shape=[o, o],
                          compiler_params=_params(("arbitrary",)))(*chips, p, p, p, p)


def _reduce_scatter(gs):
    c = lax.axis_index("c")
    outs = []
    for a, g in enumerate(gs):
        _, rows, cc = g.shape
        k = _n_copies(rows // 2, cc * 4, 16)
        p = _pair_exchange_sum(g.reshape(4, 2, k, rows // (2 * k), cc), f"rs_pair_exchange_sum_{a}")
        mine, theirs = _chip_exchange_sum(p, f"rs_chip_exchange_sum_{a}")
        mine, theirs = mine.reshape(rows // 2, cc), theirs.reshape(rows // 2, cc)
        outs.append(jnp.where(c == 0, jnp.concatenate([mine, theirs]), jnp.concatenate([theirs, mine])))
    return outs


def _gather_shards(w, name):
    rows, cc = w.shape
    half = rows // 2
    n = _n_copies(half, cc * w.dtype.itemsize, 16)
    r = half // n

    def body(core_ref, mine_ref, other_ref, out_ref, land, land_pair, send_sem, recv_sem, pair_send, pair_recv, out_sem):
        x, y, c = _position()
        k = pl.program_id(0)
        slot = k % 2
        me = 2 * x + y
        chips = [2 * px + py for px, py in _chip_peers(x, y)]
        pushes = [pltpu.make_async_remote_copy(
            src_ref=mine_ref.at[0, 0], dst_ref=land.at[slot, j], send_sem=send_sem.at[slot, j],
            recv_sem=recv_sem.at[slot, j], device_id=(px, py, c), device_id_type=MESH)
            for j, (px, py) in enumerate(_chip_peers(x, y))]
        for cp in pushes:
            cp.start()
        writes = [pltpu.make_async_copy(mine_ref.at[0, 0], out_ref.at[me, c, k], out_sem.at[0]),
                  pltpu.make_async_copy(other_ref.at[0, 0], out_ref.at[me, 1 - c, k], out_sem.at[1])]
        for cp in writes:
            cp.start()
        passes = []
        for j, cp in enumerate(pushes):
            cp.wait_recv()
            passes.append(pltpu.make_async_remote_copy(
                src_ref=land.at[slot, j], dst_ref=land_pair.at[slot, j], send_sem=pair_send.at[slot, j],
                recv_sem=pair_recv.at[slot, j], device_id=(x, y, 1 - c), device_id_type=MESH))
            passes[j].start()
            writes.append(pltpu.make_async_copy(land.at[slot, j], out_ref.at[chips[j], c, k], out_sem.at[2 + j]))
            writes[-1].start()
        for j, cp in enumerate(passes):
            cp.wait_recv()
            writes.append(pltpu.make_async_copy(land_pair.at[slot, j], out_ref.at[chips[j], 1 - c, k], out_sem.at[5 + j]))
            writes[-1].start()
        for cp in writes:
            cp.wait()
        for cp in pushes + passes:
            cp.wait_send()

    grid_spec = pltpu.PrefetchScalarGridSpec(
        num_scalar_prefetch=1, grid=(n,),
        in_specs=[pl.BlockSpec((1, 1, r, cc), lambda k, core: (core[0], k, 0, 0)),
                  pl.BlockSpec((1, 1, r, cc), lambda k, core: (1 - core[0], k, 0, 0))],
        out_specs=ANY,
        scratch_shapes=[pltpu.VMEM((2, 3, r, cc), w.dtype), pltpu.VMEM((2, 3, r, cc), w.dtype),
                        pltpu.SemaphoreType.DMA((2, 3)), pltpu.SemaphoreType.DMA((2, 3)),
                        pltpu.SemaphoreType.DMA((2, 3)), pltpu.SemaphoreType.DMA((2, 3)),
                        pltpu.SemaphoreType.DMA((8,))])
    w4 = w.reshape(2, n, r, cc)
    out = pl.pallas_call(body, name=name, grid_spec=grid_spec,
                         out_shape=jax.ShapeDtypeStruct((4, 2, n, r, cc), w.dtype),
                         compiler_params=_params(("arbitrary",)))(_scalar(lax.axis_index("c")), w4, w4)
    return out.reshape(4, rows, cc)


def _pad_rows(a, rows):
    return jnp.pad(a, ((0, rows - a.shape[0]), (0, 0)))


def _shard_cols(a):
    r, c4 = a.shape
    return a.reshape(r, 4, c4 // 4).transpose(1, 0, 2)


def kernel(x, mem, w_in, b_forget, conv_a_w, conv_a_b, ln_a_g, ln_a_b, conv_b_w, w_kv_mem, mem_ln_g, mem_ln_b, p_a, p_b, p_c, p_m, w_out, ln_g, ln_b, loss_target, m_w_in, m_b_forget, m_conv_a_w, m_conv_a_b, m_ln_a_g, m_ln_a_b, m_conv_b_w, m_w_kv_mem, m_mem_ln_g, m_mem_ln_b, m_p_a, m_p_b, m_p_c, m_p_m, m_w_out, m_ln_g, m_ln_b, v_w_in, v_b_forget, v_conv_a_w, v_conv_a_b, v_ln_a_g, v_ln_a_b, v_conv_b_w, v_w_kv_mem, v_mem_ln_g, v_mem_ln_b, v_p_a, v_p_b, v_p_c, v_p_m, v_w_out, v_ln_g, v_ln_b):
    depth = w_in.shape[0]
    x0 = x[0]
    s, d = x0.shape
    aw = conv_a_w.shape[2] * 4
    mw = p_m.shape[1]
    n_head = b_forget.shape[1]
    alpha = (2.0 * depth) ** 0.25
    in_cols = w_in.shape[2] * 4
    assert aw == n_head * HEAD_DIM and mw % LANES == 0 and in_cols == 11 * aw + n_head + 2 * mw + 4 * d
    cf0 = 10 * aw
    n_main = in_cols - n_head
    fblk = n_main // LANES
    n_pad = n_main + LANES
    gblk = (11 * aw + 2 * mw) // d
    mblk = (11 * aw) // (2 * mw)
    qblk, kblk, vblk = 7 * aw // LANES, 8 * aw // LANES, 9 * aw // LANES
    assert (11 * aw + 2 * mw) % d == 0 and (11 * aw) % (2 * mw) == 0

    def gather(w, name):
        flat = _gather_shards(w.astype(BF16).reshape(-1, w.shape[-1]), name)
        return flat.reshape((4,) + w.shape)

    w_in_g, p_a_g, p_b_g, p_c_g = (gather(w, f"gather_{nm}") for w, nm in
                                   ((w_in, "w_in"), (p_a, "p_a"), (p_b, "p_b"), (p_c, "p_c")))
    p_m_g, w_kv_g, w_out_g = (gather(w, f"gather_{nm}") for w, nm in
                              ((p_m, "p_m"), (w_kv_mem, "w_kv"), (w_out, "w_out")))
    conv_a_g, conv_b_g = _allgather_chips([conv_a_w, conv_b_w], "gather_conv_taps")

    def cols(g):
        return jnp.concatenate([g[j] for j in range(4)], axis=-1)

    def rows(g):
        return jnp.concatenate([g[j] for j in range(4)], axis=-2)

    shard_w = in_cols // 4

    def global_columns(shards, lo, hi):
        return [shards[j][:, max(lo, j * shard_w) - j * shard_w:min(hi, (j + 1) * shard_w) - j * shard_w]
                for j in range(4) if max(lo, j * shard_w) < min(hi, (j + 1) * shard_w)]

    w_pad = []
    for l in range(depth):
        shards = [w_in_g[j, l] for j in range(4)]
        w_pad.append(jnp.concatenate(
            global_columns(shards, 0, cf0) + global_columns(shards, cf0 + n_head, in_cols)
            + global_columns(shards, cf0, cf0 + n_head) + [jnp.zeros((d, LANES - n_head), BF16)], axis=1))

    piece_order = [(0, 3 * aw, 0), (3 * aw, 7 * aw, 1), (7 * aw, cf0, 2), (cf0, cf0 + n_head, 6),
                   (cf0 + n_head, 11 * aw + n_head, 3), (11 * aw + n_head, 11 * aw + n_head + 2 * mw, 4),
                   (11 * aw + n_head + 2 * mw, in_cols, 5)]

    def own_columns(parts, j):
        lo, hi = j * shard_w, (j + 1) * shard_w
        return jnp.concatenate([parts[k][:, max(lo, a) - a:min(hi, b) - a]
                                for a, b, k in piece_order if max(lo, a) < min(hi, b)], axis=1)
    pcat = jnp.concatenate([cols(p_a_g), cols(p_b_g), cols(p_c_g), cols(p_m_g)], axis=1)
    w_kv = rows(w_kv_g)
    wout = rows(w_out_g)
    conv_a = jnp.pad(cols(conv_a_g), ((0, 0), (0, HALO_A - CONV_A), (0, 0)))
    conv_b = jnp.pad(cols(conv_b_g), ((0, 0), (0, HALO_B - CONV_B), (0, 0)))
    bf_pad = jnp.pad(b_forget, ((0, 0), (0, LANES - n_head)))
    pieces = [(0, 3 * aw), (3 * aw, 7 * aw), (7 * aw, 10 * aw), (10 * aw, 11 * aw),
              (11 * aw, 11 * aw + 2 * mw), (11 * aw + 2 * mw, n_main), (n_main, n_pad)]

    mem_n = _ln_rows(mem[0], mem_ln_g[None], mem_ln_b[None], "mem_ln")

    xs, saved = [x0], []
    x_ops = [_matmul_copies(x0, "matmul_copies")]
    for l in range(depth):
        xl = xs[-1]
        proj = _mm(x_ops[l][0], w_pad[l], name=f"proj_{l}")
        za, conv_out = _branch_a_fwd(proj, conv_a[l], conv_a_b[l][None], ln_a_g[l][None], ln_a_b[l][None], f"a_fwd_{l}")
        zb = _branch_b_fwd(proj, conv_b[l], f"b_fwd_{l}")
        cum, cumt, sgt = _forget_prep(proj, bf_pad[l][None], fblk, f"forget_prep_{l}")
        o_c, lse = _fox_fwd(proj, cum, cumt, qblk, kblk, vblk, f"fox_fwd_{l}")
        zc = _gate_mul(proj, o_c, 10, f"c_gate_{l}")
        mkv = _mm(mem_n, w_kv[l], name=f"mkv_{l}")
        zm = _branch_m_fwd(proj, mkv, mblk, f"m_fwd_{l}")
        zs = [za, zb, zc, zm]
        y, y_b, y_bt = _merge_fwd(zs, proj, gblk, pcat[l], wout[l], xl, ln_g[l][None], ln_b[l][None], alpha,
                                  f"merge_fwd_{l}")
        xs.append(y)
        x_ops.append((y_b, y_bt))
        saved.append((proj, zs, conv_out, cum, cumt, sgt, o_c, lse, mkv))

    dy, loss_part = _loss_head(xs[-1], loss_target[0], "loss_head")

    g_w_in, g_conv_a, g_conv_b, g_w_kv, g_pcat, g_wout = [], [], [], [], [], []
    small = []
    dmem_n = None
    for l in reversed(range(depth)):
        proj, zs, conv_out, cum, cumt, sgt, o_c, lse, mkv = saved[l]
        xl = xs[l]
        (dx, d_g, dza, dzb, dzc, dzm, dpcat, dwout, dlng, dlnb) = _merge_bwd(
            zs, proj, gblk, pcat[l], wout[l], xl, ln_g[l][None], ln_b[l][None], dy, alpha, f"merge_bwd_{l}")
        d_m, dmkv = _branch_m_bwd(proj, mkv, dzm, mblk, f"m_bwd_{l}")
        g_w_kv.append(_mm(mem_n, dmkv, ta=True, name=f"dwkv_{l}"))
        dmem_n = _mm(dmkv, w_kv[l], tb=True, add=dmem_n, name=f"dmem_{l}")
        do, d_cg, delta = _fox_bwd_prep(proj, dzc, o_c, 10, f"fox_bwd_prep_{l}")
        dq, dk, dv, dcumt4, dcumq4 = _fox_bwd(proj, do, lse, delta, cum, cumt, qblk, kblk, vblk, f"fox_bwd_{l}")
        d_f, dbf = _forget_bwd(dcumt4, dcumq4, sgt, f"forget_bwd_{l}")
        d_b, dconv_b = _branch_b_bwd(proj, dzb, conv_b[l], f"b_bwd_{l}")
        d_a, dconv_a, dconv_ab, dlag, dlab = _branch_a_bwd(
            proj, conv_out, dza, conv_a[l], ln_a_g[l][None], ln_a_b[l][None], f"a_bwd_{l}")
        dparts = [d_a, d_b, jnp.concatenate([dq, dk, dv], axis=1), d_cg, d_m, d_g, d_f]
        dx = _input_grad(dparts, [w_pad[l][:, lo:hi] for lo, hi in pieces], dx, f"dx_{l}")
        dw_parts = [_mm(x_ops[l][1], dp, tm=d, tk=_pick(s, (T_DW_K,)), name=f"dw_{l}_{k}")
                    for k, dp in enumerate(dparts)]
        g_w_in.append(dw_parts)
        g_conv_a.append(dconv_a)
        g_conv_b.append(dconv_b)
        g_pcat.append(dpcat)
        g_wout.append(dwout)
        small.append([dbf[:, 0], dconv_ab[0], dlag[0], dlab[0], dlng[0], dlnb[0]])
        dy = dx
    grad_x = dy
    dmlg, dmlb = _ln_rows_param_grads(mem[0], dmem_n, "mem_ln_grads")
    for lst in (g_w_in, g_conv_a, g_conv_b, g_w_kv, g_pcat, g_wout, small):
        lst.reverse()

    pa_end, pb_end, pc_end = aw, 2 * aw, 3 * aw
    rs_in = [
        jnp.stack([jnp.concatenate([own_columns(parts, j) for parts in g_w_in], axis=0) for j in range(4)]),
        _shard_cols(jnp.concatenate(g_conv_a, axis=0)),
        _shard_cols(jnp.concatenate([_pad_rows(g, 2 * HALO_B) for g in g_conv_b], axis=0)),
        jnp.concatenate([g.reshape(4, g.shape[0] // 4, g.shape[1]) for g in g_w_kv], axis=1),
        _shard_cols(jnp.concatenate([g[:pa_end] for g in g_pcat], axis=0)),
        _shard_cols(jnp.concatenate([g[pa_end:pb_end] for g in g_pcat], axis=0)),
        _shard_cols(jnp.concatenate([g[pb_end:pc_end] for g in g_pcat], axis=0)),
        _shard_cols(jnp.concatenate([g[pc_end:] for g in g_pcat], axis=0)),
        jnp.concatenate([g.reshape(4, g.shape[0] // 4, g.shape[1]) for g in g_wout], axis=1),
    ]
    rs_out = _reduce_scatter(rs_in)
    gw_in = rs_out[0].reshape(depth, d, -1)
    g_ca = rs_out[1].reshape(depth, HALO_A, -1)[:, :CONV_A]
    g_cb = rs_out[2].reshape(depth, 2 * HALO_B, -1)[:, :CONV_B]
    gw_kv = rs_out[3].reshape(depth, -1, 2 * mw)
    gp_a = rs_out[4].reshape(depth, aw, -1)
    gp_b = rs_out[5].reshape(depth, aw, -1)
    gp_c = rs_out[6].reshape(depth, aw, -1)
    gp_m = rs_out[7].reshape(depth, mw, -1)
    gw_out = rs_out[8].reshape(depth, -1, d)

    flat = jnp.concatenate([jnp.concatenate(p) for p in small] + [dmlg[0], dmlb[0], loss_part[0, 0:1]])
    n_small = flat.shape[0]
    n_rows = -(-n_small // (8 * LANES)) * 8
    vec = jnp.pad(flat, (0, n_rows * LANES - n_small)).reshape(n_rows, LANES)
    tot = _sum_leading(_allgather_all(vec, "gather_small"), F32, "sum_small").reshape(-1)
    per_layer = n_head + 3 * aw + 2 * d
    tl = tot[:depth * per_layer].reshape(depth, per_layer)
    offs = np.cumsum([0, n_head, aw, aw, aw, d, d])
    g_bf, g_cab, g_lag, g_lab, g_lg, g_lb = [tl[:, offs[k]:offs[k + 1]] for k in range(6)]
    base = depth * per_layer
    g_mlg, g_mlb = tot[base:base + d], tot[base + d:base + 2 * d]
    loss = tot[base + 2 * d]

    grads = [gw_in, g_bf, g_ca, g_cab, g_lag, g_lab, g_cb, gw_kv, g_mlg, g_mlb, gp_a, gp_b, gp_c, gp_m, gw_out, g_lg, g_lb]
    ws = [w_in, b_forget, conv_a_w, conv_a_b, ln_a_g, ln_a_b, conv_b_w, w_kv_mem, mem_ln_g, mem_ln_b, p_a, p_b, p_c, p_m, w_out, ln_g, ln_b]
    ms = [m_w_in, m_b_forget, m_conv_a_w, m_conv_a_b, m_ln_a_g, m_ln_a_b, m_conv_b_w, m_w_kv_mem, m_mem_ln_g, m_mem_ln_b, m_p_a, m_p_b, m_p_c, m_p_m, m_w_out, m_ln_g, m_ln_b]
    vs = [v_w_in, v_b_forget, v_conv_a_w, v_conv_a_b, v_ln_a_g, v_ln_a_b, v_conv_b_w, v_w_kv_mem, v_mem_ln_g, v_mem_ln_b, v_p_a, v_p_b, v_p_c, v_p_m, v_w_out, v_ln_g, v_ln_b]
    deltas, new_ms, new_vs = [], [], []
    for k, (wk, gk, mk, vk) in enumerate(zip(ws, grads, ms, vs)):
        shape = wk.shape
        as_3d = (1,) * (3 - wk.ndim) + shape
        dk_, nm_, nv_ = _adamw(wk.reshape(as_3d), gk.reshape(as_3d), mk.reshape(as_3d), vk.reshape(as_3d), f"adamw_{k}")
        deltas.append(dk_.reshape(shape))
        new_ms.append(nm_.reshape(shape))
        new_vs.append(nv_.reshape(shape))
        grads[k] = gk.reshape(shape)
    return (loss, grad_x[None], *grads, *deltas, *new_ms, *new_vs)


def _gate_mul(proj, o, gblk, name):
    s, w = o.shape
    t = T_ELEM

    def body(g_ref, o_ref, z_ref):
        z_ref[...] = (o_ref[...] * _silu(g_ref[...])).astype(BF16)

    row = pl.BlockSpec((t, w), lambda i: (i, 0))
    return pl.pallas_call(body, name=name, grid=(s // t,), in_specs=[pl.BlockSpec((t, w), lambda i: (i, gblk)), row],
                          out_specs=row, out_shape=jax.ShapeDtypeStruct((s, w), BF16),
                          compiler_params=_params(("parallel",)))(proj, o)
```

```python
import functools
import math

import numpy as np
import jax
import jax.numpy as jnp
from jax import lax
from jax.experimental import pallas as pl
from jax.experimental.pallas import tpu as pltpu

F32 = jnp.float32
BF16 = jnp.bfloat16
MESH = pl.DeviceIdType.MESH
ANY = pl.BlockSpec(memory_space=pl.ANY)

LN_EPS = 1e-5
NEG_BIG = -1e30
HEAD_DIM = 64
LANES = 128
CONV_A = 31
CONV_B = 3
HALO_A = 32
HALO_B = 8
CHUNK = 32
VMEM_LIMIT = 60 * 1024 * 1024

ADAM_LR, ADAM_B1, ADAM_B2, ADAM_EPS, ADAM_WD, ADAM_STEP = 0.001, 0.9, 0.999, 1e-08, 0.01, 10

T_MM = 512
T_A = 128
T_B = 256
T_ATT_Q = 512
T_ATT_K = 1024
T_ATT_K_FWD = 2048
T_CUM = 512
T_M = 512
T_MERGE = 256
T_DX = 256
T_DW_K = 2048
T_ELEM = 512


def _pick(n, prefs):
    for p in prefs:
        if n % p == 0:
            return p
    return n


def _params(sem=None):
    return pltpu.CompilerParams(dimension_semantics=sem, vmem_limit_bytes=VMEM_LIMIT)


def _sigmoid(x):
    return jax.nn.sigmoid(x)


def _silu(x):
    return x * _sigmoid(x)


def _dsilu(x):
    s = _sigmoid(x)
    return s * (1.0 + x * (1.0 - s))


def _dot(a, b, dims, precision=None):
    return lax.dot_general(a, b, (dims, ((), ())), preferred_element_type=F32, precision=precision)


NN = ((1,), (0,))
NT = ((1,), (1,))
TN = ((0,), (0,))


def _iota(shape, dim):
    return lax.broadcasted_iota(jnp.int32, shape, dim)


def _mm(a, b, *, ta=False, tb=False, add=None, out_dtype=F32, tm=None, tk=None, name):
    m = a.shape[1] if ta else a.shape[0]
    k = a.shape[0] if ta else a.shape[1]
    n = b.shape[0] if tb else b.shape[1]
    if tm is None:
        tm = _pick(m, (1024, 512, 256)) if ta else _pick(m, (T_MM, 256))
    tn = _pick(n, (1152, 1024, 768, 512, 384, 256, 128))
    if tk is None:
        tk = _pick(k, (1024, 512, 256))
    nk = k // tk
    dims = ((0,) if ta else (1,), (1,) if tb else (0,))

    def body(*refs):
        if add is None:
            a_ref, b_ref, o_ref, acc_ref = refs
        else:
            a_ref, b_ref, add_ref, o_ref, acc_ref = refs
        kk = pl.program_id(2)
        p = _dot(a_ref[...].astype(BF16), b_ref[...].astype(BF16), dims)

        @pl.when(kk == 0)
        def _():
            acc_ref[...] = p

        @pl.when(kk > 0)
        def _():
            acc_ref[...] += p

        @pl.when(kk == nk - 1)
        def _():
            r = acc_ref[...]
            if add is not None:
                r = r + add_ref[...]
            o_ref[...] = r.astype(out_dtype)

    a_spec = (pl.BlockSpec((tk, tm), lambda j, i, kk: (kk, i)) if ta
              else pl.BlockSpec((tm, tk), lambda j, i, kk: (i, kk)))
    b_spec = (pl.BlockSpec((tn, tk), lambda j, i, kk: (j, kk)) if tb
              else pl.BlockSpec((tk, tn), lambda j, i, kk: (kk, j)))
    o_spec = pl.BlockSpec((tm, tn), lambda j, i, kk: (i, j))
    in_specs = [a_spec, b_spec] + ([o_spec] if add is not None else [])
    args = (a, b) + ((add,) if add is not None else ())
    return pl.pallas_call(
        body, name=name, grid=(n // tn, m // tm, nk), in_specs=in_specs, out_specs=o_spec,
        out_shape=jax.ShapeDtypeStruct((m, n), out_dtype),
        scratch_shapes=[pltpu.VMEM((tm, tn), F32)],
        compiler_params=_params(("parallel", "parallel", "arbitrary")),
    )(*args)


def _input_grad(dparts, wparts, add, name):
    s, d = add.shape
    t = T_DX
    n_p = len(dparts)

    def body(*refs):
        d_refs, w_hbm = refs[:n_p], refs[n_p:2 * n_p]
        add_ref, o_ref = refs[2 * n_p], refs[2 * n_p + 1]
        w_v = refs[2 * n_p + 2:]

        @pl.when(pl.program_id(0) == 0)
        def _():
            for p in range(n_p):
                pltpu.sync_copy(w_hbm[p], w_v[p])

        acc = add_ref[...]
        for p in range(n_p):
            acc = acc + _dot(d_refs[p][...], w_v[p][...], NT)
        o_ref[...] = acc

    row = pl.BlockSpec((t, d), lambda i: (i, 0))
    in_specs = ([pl.BlockSpec((t, dp.shape[1]), lambda i: (i, 0)) for dp in dparts] + [ANY] * n_p + [row])
    return pl.pallas_call(
        body, name=name, grid=(s // t,), in_specs=in_specs, out_specs=row,
        out_shape=jax.ShapeDtypeStruct((s, d), F32),
        scratch_shapes=[pltpu.VMEM(w.shape, BF16) for w in wparts],
        compiler_params=_params(("arbitrary",)),
    )(*dparts, *wparts, add)


def _ln_rows(x, g, b, name):
    r, d = x.shape
    t = _pick(r, (256,))

    def body(x_ref, g_ref, b_ref, o_ref):
        xv = x_ref[...]
        mu = jnp.mean(xv, axis=-1, keepdims=True)
        dv = xv - mu
        var = jnp.mean(dv * dv, axis=-1, keepdims=True)
        o_ref[...] = dv * lax.rsqrt(var + LN_EPS) * g_ref[...] + b_ref[...]

    row = pl.BlockSpec((t, d), lambda i: (i, 0))
    vec = pl.BlockSpec((1, d), lambda i: (0, 0))
    return pl.pallas_call(body, name=name, grid=(r // t,), in_specs=[row, vec, vec], out_specs=row,
                          out_shape=jax.ShapeDtypeStruct((r, d), F32), compiler_params=_params(("parallel",)))(x, g, b)


def _ln_rows_param_grads(x, dy, name):
    r, d = x.shape

    def body(x_ref, dy_ref, dg_ref, db_ref):
        xv = x_ref[...]
        mu = jnp.mean(xv, axis=-1, keepdims=True)
        dv = xv - mu
        var = jnp.mean(dv * dv, axis=-1, keepdims=True)
        xh = dv * lax.rsqrt(var + LN_EPS)
        dg_ref[...] = jnp.sum(dy_ref[...] * xh, axis=0, keepdims=True)
        db_ref[...] = jnp.sum(dy_ref[...], axis=0, keepdims=True)

    full = pl.BlockSpec((r, d), lambda i: (0, 0))
    vec = pl.BlockSpec((1, d), lambda i: (0, 0))
    o = jax.ShapeDtypeStruct((1, d), F32)
    return pl.pallas_call(body, name=name, grid=(1,), in_specs=[full, full], out_specs=[vec, vec],
                          out_shape=[o, o], compiler_params=_params(("arbitrary",)))(x, dy)


def _conv_a_chunk(glu_ref, cw_ref, r0):
    acc = cw_ref[0:1, :] * glu_ref[pl.ds(r0 + 2, CHUNK), :]
    for k in range(1, CONV_A):
        acc = acc + cw_ref[k:k + 1, :] * glu_ref[pl.ds(r0 + 2 + k, CHUNK), :]
    return acc


def _ln_stats(c):
    mu = jnp.mean(c, axis=-1, keepdims=True)
    d = c - mu
    var = jnp.mean(d * d, axis=-1, keepdims=True)
    rstd = lax.rsqrt(var + LN_EPS)
    return d * rstd, rstd


def _branch_a_fwd(proj, cw, cb, lg, lb, name):
    s = proj.shape[0]
    t = T_A
    w = cw.shape[1]
    r = t // HALO_A

    def body(u_ref, v_ref, gt_ref, hu_ref, hv_ref, cw_ref, cb_ref, lg_ref, lb_ref, z_ref, conv_ref, glu):
        i = pl.program_id(0)
        hglu = hu_ref[...] * _sigmoid(hv_ref[...])
        glu[0:HALO_A, :] = jnp.where(i > 0, hglu, 0.0)
        glu[HALO_A:HALO_A + t, :] = u_ref[...] * _sigmoid(v_ref[...])
        for c in range(t // CHUNK):
            r0 = c * CHUNK
            conv = _conv_a_chunk(glu, cw_ref, r0) + cb_ref[...]
            conv_ref[r0:r0 + CHUNK, :] = conv
            xh, _ = _ln_stats(conv)
            a3 = _silu(xh * lg_ref[...] + lb_ref[...])
            z_ref[r0:r0 + CHUNK, :] = (a3 * _silu(gt_ref[r0:r0 + CHUNK, :])).astype(BF16)

    def cur(col):
        return pl.BlockSpec((t, w), lambda i, col=col: (i, col))

    def prev(col):
        return pl.BlockSpec((HALO_A, w), lambda i, col=col: (jnp.maximum(i * r - 1, 0), col))

    vec = pl.BlockSpec((1, w), lambda i: (0, 0))
    return pl.pallas_call(
        body, name=name, grid=(s // t,),
        in_specs=[cur(0), cur(1), cur(2), prev(0), prev(1), pl.BlockSpec((HALO_A, w), lambda i: (0, 0)), vec, vec, vec],
        out_specs=[pl.BlockSpec((t, w), lambda i: (i, 0)), pl.BlockSpec((t, w), lambda i: (i, 0))],
        out_shape=[jax.ShapeDtypeStruct((s, w), BF16), jax.ShapeDtypeStruct((s, w), F32)],
        scratch_shapes=[pltpu.VMEM((HALO_A + t, w), F32)],
        compiler_params=_params(("parallel",)),
    )(proj, proj, proj, proj, proj, cw, cb, lg, lb)


def _branch_a_bwd(proj, conv, dz, cw, lg, lb, name):
    s = proj.shape[0]
    t = T_A
    w = cw.shape[1]
    r = t // HALO_A
    n = s // t
    nblk = s // HALO_A
    ext = t + HALO_A

    def body(u_ref, v_ref, gt_ref, dz_ref, conv_ref, pu_ref, pv_ref, ngt_ref, ndz_ref, nconv_ref,
             cw_ref, lg_ref, lb_ref, da_ref, dw_ref, dcb_ref, dlg_ref, dlb_ref, glu, dc, dw8):
        i = pl.program_id(0)

        @pl.when(i == 0)
        def _():
            dw8[...] = jnp.zeros_like(dw8)
            dcb_ref[...] = jnp.zeros_like(dcb_ref)
            dlg_ref[...] = jnp.zeros_like(dlg_ref)
            dlb_ref[...] = jnp.zeros_like(dlb_ref)

        glu[0:HALO_A, :] = jnp.where(i > 0, pu_ref[...] * _sigmoid(pv_ref[...]), 0.0)
        glu[HALO_A:HALO_A + t, :] = u_ref[...] * _sigmoid(v_ref[...])
        has_next = i < n - 1
        dcb = jnp.zeros((1, w), F32)
        dlg = jnp.zeros((1, w), F32)
        dlb = jnp.zeros((1, w), F32)
        for c in range(ext // CHUNK):
            r0 = c * CHUNK
            own = r0 < t
            xh, rstd = _ln_stats(conv_ref[r0:r0 + CHUNK, :] if own else nconv_ref[...])
            a2 = xh * lg_ref[...] + lb_ref[...]
            if own:
                gt = gt_ref[r0:r0 + CHUNK, :]
                dzc = dz_ref[r0:r0 + CHUNK, :]
            else:
                gt = ngt_ref[...]
                dzc = ndz_ref[...]
            da2 = dzc * _silu(gt) * _dsilu(a2)
            dxh = da2 * lg_ref[...]
            dconv = rstd * (dxh - jnp.mean(dxh, axis=-1, keepdims=True)
                            - xh * jnp.mean(dxh * xh, axis=-1, keepdims=True))
            if own:
                dc[r0:r0 + CHUNK, :] = dconv
                da_ref[r0:r0 + CHUNK, 2 * w:3 * w] = (dzc * _silu(a2) * _dsilu(gt)).astype(BF16)
                dcb = dcb + jnp.sum(dconv, axis=0, keepdims=True)
                dlg = dlg + jnp.sum(da2 * xh, axis=0, keepdims=True)
                dlb = dlb + jnp.sum(da2, axis=0, keepdims=True)
            else:
                dc[r0:r0 + CHUNK, :] = jnp.where(has_next, dconv, 0.0)
        dcb_ref[...] += dcb
        dlg_ref[...] += dlg
        dlb_ref[...] += dlb
        for c in range(t // CHUNK):
            r0 = c * CHUNK
            dcc = dc[r0:r0 + CHUNK, :]
            dglu = cw_ref[0:1, :] * dc[pl.ds(r0 + CONV_A - 1, CHUNK), :]
            for k in range(1, CONV_A):
                dglu = dglu + cw_ref[k:k + 1, :] * dc[pl.ds(r0 + CONV_A - 1 - k, CHUNK), :]
            for k in range(CONV_A):
                prod = dcc * glu[pl.ds(r0 + 2 + k, CHUNK), :]
                dw8[k] += jnp.sum(prod.reshape(CHUNK // 8, 8, w), axis=0)
            sv = _sigmoid(v_ref[r0:r0 + CHUNK, :])
            da_ref[r0:r0 + CHUNK, 0:w] = (dglu * sv).astype(BF16)
            da_ref[r0:r0 + CHUNK, w:2 * w] = (dglu * u_ref[r0:r0 + CHUNK, :] * sv * (1.0 - sv)).astype(BF16)

        @pl.when(i == n - 1)
        def _():
            dw_ref[...] = jnp.sum(dw8[...], axis=1)

    def cur(col):
        return pl.BlockSpec((t, w), lambda i, col=col: (i, col))

    def prev(col):
        return pl.BlockSpec((HALO_A, w), lambda i, col=col: (jnp.maximum(i * r - 1, 0), col))

    def nxt(col):
        return pl.BlockSpec((HALO_A, w), lambda i, col=col: (jnp.minimum((i + 1) * r, nblk - 1), col))

    own_rows = pl.BlockSpec((t, w), lambda i: (i, 0))
    vec = pl.BlockSpec((1, w), lambda i: (0, 0))
    vo = jax.ShapeDtypeStruct((1, w), F32)
    return pl.pallas_call(
        body, name=name, grid=(n,),
        in_specs=[cur(0), cur(1), cur(2), own_rows, own_rows, prev(0), prev(1), nxt(2), nxt(0), nxt(0),
                  pl.BlockSpec((HALO_A, w), lambda i: (0, 0)), vec, vec],
        out_specs=[pl.BlockSpec((t, 3 * w), lambda i: (i, 0)), pl.BlockSpec((HALO_A, w), lambda i: (0, 0)), vec, vec, vec],
        out_shape=[jax.ShapeDtypeStruct((s, 3 * w), BF16), jax.ShapeDtypeStruct((HALO_A, w), F32), vo, vo, vo],
        scratch_shapes=[pltpu.VMEM((HALO_A + t, w), F32), pltpu.VMEM((ext, w), F32),
                        pltpu.VMEM((HALO_A, 8, w), F32)],
        compiler_params=_params(("arbitrary",)),
    )(proj, proj, proj, dz, conv, proj, proj, proj, dz, conv, cw, lg, lb)


def _conv_b(u_ext, cw_ref, t):
    acc = cw_ref[0:1, :] * u_ext[pl.ds(HALO_B - 2, t), :]
    for k in range(1, CONV_B):
        acc = acc + cw_ref[k:k + 1, :] * u_ext[pl.ds(HALO_B - 2 + k, t), :]
    return acc


def _branch_b_fwd(proj, cw, name):
    s = proj.shape[0]
    t = T_B
    w = cw.shape[1]
    r = t // HALO_B

    def body(h_ref, b_ref, c_ref, gt_ref, ph_ref, pc_ref, cw_ref, z_ref, u_ext):
        i = pl.program_id(0)
        u_ext[0:HALO_B, :] = jnp.where(i > 0, pc_ref[...] * ph_ref[...], 0.0)
        u_ext[HALO_B:HALO_B + t, :] = c_ref[...] * h_ref[...]
        cv = _conv_b(u_ext, cw_ref, t)
        z_ref[...] = (b_ref[...] * cv * _silu(gt_ref[...])).astype(BF16)

    def cur(col):
        return pl.BlockSpec((t, w), lambda i, col=col: (i, col))

    def prev(col):
        return pl.BlockSpec((HALO_B, w), lambda i, col=col: (jnp.maximum(i * r - 1, 0), col))

    return pl.pallas_call(
        body, name=name, grid=(s // t,),
        in_specs=[cur(3), cur(4), cur(5), cur(6), prev(3), prev(5), pl.BlockSpec((HALO_B, w), lambda i: (0, 0))],
        out_specs=pl.BlockSpec((t, w), lambda i: (i, 0)),
        out_shape=jax.ShapeDtypeStruct((s, w), BF16),
        scratch_shapes=[pltpu.VMEM((HALO_B + t, w), F32)],
        compiler_params=_params(("parallel",)),
    )(proj, proj, proj, proj, proj, proj, cw)


def _branch_b_bwd(proj, dz, cw, name):
    s = proj.shape[0]
    t = T_B
    w = cw.shape[1]
    r = t // HALO_B
    n = s // t
    nblk = s // HALO_B

    def body(h_ref, b_ref, c_ref, gt_ref, dz_ref, ph_ref, pc_ref, nb_ref, ngt_ref, ndz_ref, cw_ref,
             db_ref, dw_ref, u_ext, dcv_ext, dw8):
        i = pl.program_id(0)

        @pl.when(i == 0)
        def _():
            dw8[...] = jnp.zeros_like(dw8)

        u_ext[0:HALO_B, :] = jnp.where(i > 0, pc_ref[...] * ph_ref[...], 0.0)
        u_ext[HALO_B:HALO_B + t, :] = c_ref[...] * h_ref[...]
        cv = _conv_b(u_ext, cw_ref, t)
        gt = gt_ref[...]
        dhb = dz_ref[...] * _silu(gt)
        db_ref[:, 3 * w:4 * w] = (dz_ref[...] * b_ref[...] * cv * _dsilu(gt)).astype(BF16)
        db_ref[:, w:2 * w] = (dhb * cv).astype(BF16)
        dcv = dhb * b_ref[...]
        dcv_ext[0:t, :] = dcv
        ndcv = ndz_ref[...] * _silu(ngt_ref[...]) * nb_ref[...]
        dcv_ext[t:t + HALO_B, :] = jnp.where(i < n - 1, ndcv, 0.0)
        du = cw_ref[0:1, :] * dcv_ext[pl.ds(2, t), :]
        for k in range(1, CONV_B):
            du = du + cw_ref[k:k + 1, :] * dcv_ext[pl.ds(2 - k, t), :]
        db_ref[:, 2 * w:3 * w] = (du * h_ref[...]).astype(BF16)
        db_ref[:, 0:w] = (du * c_ref[...]).astype(BF16)
        for k in range(CONV_B):
            prod = dcv * u_ext[pl.ds(HALO_B - 2 + k, t), :]
            dw8[k] += jnp.sum(prod.reshape(t // 8, 8, w), axis=0)

        @pl.when(i == n - 1)
        def _():
            dw_ref[...] = jnp.sum(dw8[...], axis=1)

    def cur(col):
        return pl.BlockSpec((t, w), lambda i, col=col: (i, col))

    def prev(col):
        return pl.BlockSpec((HALO_B, w), lambda i, col=col: (jnp.maximum(i * r - 1, 0), col))

    def nxt(col):
        return pl.BlockSpec((HALO_B, w), lambda i, col=col: (jnp.minimum((i + 1) * r, nblk - 1), col))

    return pl.pallas_call(
        body, name=name, grid=(n,),
        in_specs=[cur(3), cur(4), cur(5), cur(6), pl.BlockSpec((t, w), lambda i: (i, 0)),
                  prev(3), prev(5), nxt(4), nxt(6),
                  pl.BlockSpec((HALO_B, w), lambda i: (jnp.minimum((i + 1) * r, nblk - 1), 0)),
                  pl.BlockSpec((HALO_B, w), lambda i: (0, 0))],
        out_specs=[pl.BlockSpec((t, 4 * w), lambda i: (i, 0)), pl.BlockSpec((HALO_B, w), lambda i: (0, 0))],
        out_shape=[jax.ShapeDtypeStruct((s, 4 * w), BF16), jax.ShapeDtypeStruct((HALO_B, w), F32)],
        scratch_shapes=[pltpu.VMEM((HALO_B + t, w), F32), pltpu.VMEM((t + HALO_B, w), F32),
                        pltpu.VMEM((HALO_B, 8, w), F32)],
        compiler_params=_params(("arbitrary",)),
    )(proj, proj, proj, proj, dz, proj, proj, proj, proj, dz, cw)


def _forget_prep(proj, bf, fblk, name):
    s = proj.shape[0]
    t = T_CUM

    def body(f_ref, bf_ref, cum_ref, cumt_ref, sgt_ref, carry):
        i = pl.program_id(0)

        @pl.when(i == 0)
        def _():
            carry[...] = jnp.zeros_like(carry)

        z = f_ref[...] + bf_ref[...]
        logf = jnp.minimum(z, 0.0) - jnp.log1p(jnp.exp(-jnp.abs(z)))
        tri = (_iota((t, t), 0) >= _iota((t, t), 1)).astype(F32)
        cum = _dot(tri, logf, NN, precision=lax.Precision.HIGHEST) + carry[0:1, :]
        carry[0:1, :] = cum[t - 1:t, :]
        cum_ref[...] = cum
        cumt_ref[...] = cum.T[0:8, :]
        sgt_ref[...] = _sigmoid(-z).T[0:8, :]

    return pl.pallas_call(
        body, name=name, grid=(s // t,),
        in_specs=[pl.BlockSpec((t, LANES), lambda i: (i, fblk)), pl.BlockSpec((1, LANES), lambda i: (0, 0))],
        out_specs=[pl.BlockSpec((t, LANES), lambda i: (i, 0)), pl.BlockSpec((8, t), lambda i: (0, i)),
                   pl.BlockSpec((8, t), lambda i: (0, i))],
        out_shape=[jax.ShapeDtypeStruct((s, LANES), F32), jax.ShapeDtypeStruct((8, s), F32),
                   jax.ShapeDtypeStruct((8, s), F32)],
        scratch_shapes=[pltpu.VMEM((8, LANES), F32)],
        compiler_params=_params(("arbitrary",)),
    )(proj, bf)


def _lane_pick(x, lane):
    return jnp.sum(jnp.where(_iota(x.shape, 1) == lane, x, 0.0), axis=1, keepdims=True)


def _sublane_pick(x, row):
    return jnp.sum(jnp.where(_iota(x.shape, 0) == row, x, 0.0), axis=0, keepdims=True)


def _head_mask(hh):
    lane = _iota((1, LANES), 1)
    return (lane >= HEAD_DIM * hh) & (lane < HEAD_DIM * (hh + 1))


def _causal_pairs(nq, ratio, kv_major):
    if kv_major:
        pairs = [(q, k) for k in range(nq // ratio) for q in range(k * ratio, nq)]
    else:
        pairs = [(q, k) for q in range(nq) for k in range(q // ratio + 1)]
    qs = np.asarray([p[0] for p in pairs], np.int32)
    ks = np.asarray([p[1] for p in pairs], np.int32)
    return qs, ks, np.where(ks == qs // ratio, qs % ratio + 1, 0).astype(np.int32)


def _fox_scores(qm, kb, cum_ref, cumt, h, row0, diag, tq, tk):
    cq0 = _lane_pick(cum_ref[0:1, :], h)
    sc = _dot(qm, kb, NT) + (cq0 - _sublane_pick(cumt, h))
    if not diag:
        return sc
    causal = (_iota((tq, tk), 0) + row0) >= _iota((tq, tk), 1)
    return jnp.where(causal, sc, NEG_BIG)


def _fox_fwd(proj, cum, cumt, qblk, kblk, vblk, name):
    s = proj.shape[0]
    tq, tk = T_ATT_Q, T_ATT_K_FWD
    n_pair = 4
    qi_np, ki_np, diag_np = _causal_pairs(s // tq, tk // tq, kv_major=False)
    scale = HEAD_DIM ** -0.5

    def body(qi_ref, ki_ref, diag_ref, q_ref, k_ref, v_ref, cum_ref, cumt_ref, o_ref, lse_ref, m_s, acc_s):
        hp = pl.program_id(0)
        step = pl.program_id(1)
        qi, ki, diag = qi_ref[step], ki_ref[step], diag_ref[step]

        @pl.when(ki == 0)
        def _():
            m_s[...] = jnp.full_like(m_s, NEG_BIG)
            acc_s[...] = jnp.zeros_like(acc_s)

        def update(blocks):
            width = tk if blocks is None else blocks * tq
            q = q_ref[...] * scale
            kb = k_ref[0:width, :].astype(BF16)
            v = v_ref[0:width, :]
            cumt = cumt_ref[:, 0:width]
            m_old = [m_s[0], m_s[1]]
            acc_old = [acc_s[0], acc_s[1]]
            scores, values = [], []
            for hh in range(2):
                hm = _head_mask(hh)
                qm = jnp.where(hm, q, 0.0).astype(BF16)
                values.append(jnp.where(hm, v, 1.0).astype(BF16))
                scores.append(_fox_scores(qm, kb, cum_ref, cumt, 2 * hp + hh, qi * tq - ki * tk, blocks is not None,
                                          tq, width))
            m_new = [jnp.maximum(m_old[hh], jnp.max(scores[hh], axis=1, keepdims=True)) for hh in range(2)]
            probs = [jnp.exp(scores[hh] - m_new[hh]).astype(BF16) for hh in range(2)]
            acc_new = [jnp.exp(m_old[hh] - m_new[hh]) * acc_old[hh] + _dot(probs[hh], values[hh], NN)
                       for hh in range(2)]
            for hh in range(2):
                acc_s[hh] = acc_new[hh]
                m_s[hh] = m_new[hh]

        @pl.when(diag == 0)
        def _():
            update(None)

        for blocks in range(1, tk // tq + 1):
            pl.when(diag == blocks)(functools.partial(update, blocks))

        @pl.when(diag >= 1)
        def _():
            lane = _iota((tq, LANES), 1)
            a0, a1 = acc_s[0], acc_s[1]
            o_ref[...] = jnp.where(lane < HEAD_DIM, a0 / pltpu.roll(a0, HEAD_DIM, axis=1),
                                   a1 / pltpu.roll(a1, HEAD_DIM, axis=1))
            lse0 = m_s[0] + jnp.log(a0[:, HEAD_DIM:HEAD_DIM + 1])
            lse1 = m_s[1] + jnp.log(a1[:, 0:1])
            lse_ref[0] = jnp.where(lane == 0, lse0, jnp.where(lane == 1, lse1, 0.0))

    grid_spec = pltpu.PrefetchScalarGridSpec(
        num_scalar_prefetch=3, grid=(n_pair, len(qi_np)),
        in_specs=[pl.BlockSpec((tq, LANES), lambda hp, st, qi, ki, dg: (qi[st], qblk + hp)),
                  pl.BlockSpec((tk, LANES), lambda hp, st, qi, ki, dg: (ki[st], kblk + hp)),
                  pl.BlockSpec((tk, LANES), lambda hp, st, qi, ki, dg: (ki[st], vblk + hp)),
                  pl.BlockSpec((tq, LANES), lambda hp, st, qi, ki, dg: (qi[st], 0)),
                  pl.BlockSpec((8, tk), lambda hp, st, qi, ki, dg: (0, ki[st]))],
        out_specs=[pl.BlockSpec((tq, LANES), lambda hp, st, qi, ki, dg: (qi[st], hp)),
                   pl.BlockSpec((1, tq, LANES), lambda hp, st, qi, ki, dg: (hp, qi[st], 0))],
        scratch_shapes=[pltpu.VMEM((2, tq, 1), F32), pltpu.VMEM((2, tq, LANES), F32)])
    return pl.pallas_call(
        body, name=name, grid_spec=grid_spec,
        out_shape=[jax.ShapeDtypeStruct((s, n_pair * LANES), F32), jax.ShapeDtypeStruct((n_pair, s, LANES), F32)],
        compiler_params=_params(("parallel", "arbitrary")),
    )(jnp.asarray(qi_np), jnp.asarray(ki_np), jnp.asarray(diag_np), proj, proj, proj, cum, cumt)


def _fox_bwd_prep(proj, dz, o, gblk, name):
    s, w = o.shape
    t = T_ELEM
    n_head = w // HEAD_DIM

    def body(gt_ref, dz_ref, o_ref, do_ref, dg_ref, dl_ref):
        gt = gt_ref[...]
        do = dz_ref[...] * _silu(gt)
        do_ref[...] = do
        dg_ref[...] = (dz_ref[...] * o_ref[...] * _dsilu(gt)).astype(BF16)
        sel = (_iota((w, LANES), 0) // HEAD_DIM == _iota((w, LANES), 1)).astype(F32)
        dl_ref[...] = _dot(do * o_ref[...], sel, NN, precision=lax.Precision.HIGHEST)

    assert n_head <= LANES
    row = pl.BlockSpec((t, w), lambda i: (i, 0))
    return pl.pallas_call(
        body, name=name, grid=(s // t,),
        in_specs=[pl.BlockSpec((t, w), lambda i: (i, gblk)), row, row],
        out_specs=[row, row, pl.BlockSpec((t, LANES), lambda i: (i, 0))],
        out_shape=[jax.ShapeDtypeStruct((s, w), F32), jax.ShapeDtypeStruct((s, w), BF16),
                   jax.ShapeDtypeStruct((s, LANES), F32)],
        compiler_params=_params(("parallel",)),
    )(proj, dz, o)


def _fox_bwd(proj, do, lse, delta, cum, cumt, qblk, kblk, vblk, name):
    s = proj.shape[0]
    tq, tk = T_ATT_Q, T_ATT_K
    nq = s // tq
    n_pair = 4
    qi_np, ki_np, diag_np = _causal_pairs(nq, tk // tq, kv_major=True)
    n_step = len(qi_np)
    scale = HEAD_DIM ** -0.5

    def body(qi_ref, ki_ref, diag_ref, q_ref, k_ref, v_ref, do_ref, lse_ref, dl_ref, cum_ref, cumt_ref,
             dq_ref, dk_ref, dv_ref, dct_ref, dcq_ref, dq_s, dk_s, dv_s, dc_s, dcq_s):
        hp = pl.program_id(0)
        step = pl.program_id(1)
        qi, ki, diag = qi_ref[step], ki_ref[step], diag_ref[step]

        @pl.when(step == 0)
        def _():
            dq_s[...] = jnp.zeros_like(dq_s)
            dcq_s[...] = jnp.zeros_like(dcq_s)

        @pl.when(qi == ki * (tk // tq))
        def _():
            dk_s[...] = jnp.zeros_like(dk_s)
            dv_s[...] = jnp.zeros_like(dv_s)
            dc_s[...] = jnp.zeros_like(dc_s)

        def update(blocks):
            width = tk if blocks is None else blocks * tq
            q = q_ref[...] * scale
            do = do_ref[...]
            kb = k_ref[0:width, :].astype(BF16)
            vb = v_ref[0:width, :].astype(BF16)
            cumt = cumt_ref[:, 0:width]
            sub = _iota((8, width), 0)
            dq_new = jnp.zeros((tq, LANES), F32)
            dcq_new = jnp.zeros((tq, LANES), F32)
            lane = _iota((tq, LANES), 1)
            for hh in range(2):
                h = 2 * hp + hh
                hm = _head_mask(hh)
                qm = jnp.where(hm, q, 0.0).astype(BF16)
                dom = jnp.where(hm, do, 0.0).astype(BF16)
                sc = _fox_scores(qm, kb, cum_ref, cumt, h, qi * tq - ki * tk, blocks is not None, tq, width)
                p = jnp.exp(sc - _lane_pick(lse_ref[0], hh))
                dv_s[0:width, :] += _dot(p.astype(BF16), dom, TN)
                dp = _dot(dom, vb, NT)
                ds = p * (dp - _lane_pick(dl_ref[...], h))
                dc_s[:, 0:width] += jnp.where(sub == h, -jnp.sum(ds, axis=0, keepdims=True), 0.0)
                dcq_new = dcq_new + jnp.where(lane == h, jnp.sum(ds, axis=1, keepdims=True), 0.0)
                dsb = ds.astype(BF16)
                dk_s[0:width, :] += _dot(dsb, qm, TN)
                dq_new = dq_new + jnp.where(hm, _dot(dsb, kb, NN), 0.0)
            row0 = pl.multiple_of(qi * tq, tq)
            dq_s[pl.ds(row0, tq), :] += dq_new * scale
            dcq_s[pl.ds(row0, tq), :] += dcq_new

        @pl.when(diag == 0)
        def _():
            update(None)

        for blocks in range(1, tk // tq + 1):
            pl.when(diag == blocks)(functools.partial(update, blocks))

        @pl.when(qi == nq - 1)
        def _():
            dk_ref[...] = dk_s[...].astype(BF16)
            dv_ref[...] = dv_s[...].astype(BF16)
            dct_ref[0] = dc_s[...]

        @pl.when(step == n_step - 1)
        def _():
            dq_ref[...] = dq_s[...].astype(BF16)
            dcq_ref[0] = dcq_s[...]

    grid_spec = pltpu.PrefetchScalarGridSpec(
        num_scalar_prefetch=3, grid=(n_pair, n_step),
        in_specs=[pl.BlockSpec((tq, LANES), lambda hp, st, qi, ki, dg: (qi[st], qblk + hp)),
                  pl.BlockSpec((tk, LANES), lambda hp, st, qi, ki, dg: (ki[st], kblk + hp)),
                  pl.BlockSpec((tk, LANES), lambda hp, st, qi, ki, dg: (ki[st], vblk + hp)),
                  pl.BlockSpec((tq, LANES), lambda hp, st, qi, ki, dg: (qi[st], hp)),
                  pl.BlockSpec((1, tq, LANES), lambda hp, st, qi, ki, dg: (hp, qi[st], 0)),
                  pl.BlockSpec((tq, LANES), lambda hp, st, qi, ki, dg: (qi[st], 0)),
                  pl.BlockSpec((tq, LANES), lambda hp, st, qi, ki, dg: (qi[st], 0)),
                  pl.BlockSpec((8, tk), lambda hp, st, qi, ki, dg: (0, ki[st]))],
        out_specs=[pl.BlockSpec((s, LANES), lambda hp, st, qi, ki, dg: (0, hp)),
                   pl.BlockSpec((tk, LANES), lambda hp, st, qi, ki, dg: (ki[st], hp)),
                   pl.BlockSpec((tk, LANES), lambda hp, st, qi, ki, dg: (ki[st], hp)),
                   pl.BlockSpec((1, 8, tk), lambda hp, st, qi, ki, dg: (hp, 0, ki[st])),
                   pl.BlockSpec((1, s, LANES), lambda hp, st, qi, ki, dg: (hp, 0, 0))],
        scratch_shapes=[pltpu.VMEM((s, LANES), F32), pltpu.VMEM((tk, LANES), F32), pltpu.VMEM((tk, LANES), F32),
                        pltpu.VMEM((8, tk), F32), pltpu.VMEM((s, LANES), F32)])
    w = n_pair * LANES
    return pl.pallas_call(
        body, name=name, grid_spec=grid_spec,
        out_shape=[jax.ShapeDtypeStruct((s, w), BF16), jax.ShapeDtypeStruct((s, w), BF16),
                   jax.ShapeDtypeStruct((s, w), BF16), jax.ShapeDtypeStruct((n_pair, 8, s), F32),
                   jax.ShapeDtypeStruct((n_pair, s, LANES), F32)],
        compiler_params=_params(("parallel", "arbitrary")),
    )(jnp.asarray(qi_np), jnp.asarray(ki_np), jnp.asarray(diag_np), proj, proj, proj, do, lse, delta, cum, cumt)


def _forget_bwd(dcumt4, dcumq4, sgt, name):
    s = sgt.shape[1]
    t = T_CUM
    n = s // t

    def body(d_ref, dq_ref, sg_ref, df_ref, dbf_ref, carry):
        i = pl.program_id(0)

        @pl.when(i == 0)
        def _():
            carry[...] = jnp.zeros_like(carry)
            dbf_ref[...] = jnp.zeros_like(dbf_ref)

        dq = dq_ref[0] + dq_ref[1] + dq_ref[2] + dq_ref[3]
        d = d_ref[0] + d_ref[1] + d_ref[2] + d_ref[3] + dq.T[0:8, :]
        upper = (_iota((t, t), 0) >= _iota((t, t), 1)).astype(F32)
        dlog = _dot(d, upper, NN, precision=lax.Precision.HIGHEST) + carry[:, 0:1]
        carry[...] += jnp.sum(d, axis=1, keepdims=True)
        dzt = dlog * sg_ref[...]
        dbf_ref[...] += jnp.sum(dzt, axis=1, keepdims=True)
        padded = jnp.concatenate([dzt, jnp.zeros((LANES - 8, t), F32)], axis=0)
        df_ref[...] = padded.T.astype(BF16)

    return pl.pallas_call(
        body, name=name, grid=(n,),
        in_specs=[pl.BlockSpec((4, 8, t), lambda i: (0, 0, n - 1 - i)),
                  pl.BlockSpec((4, t, LANES), lambda i: (0, n - 1 - i, 0)),
                  pl.BlockSpec((8, t), lambda i: (0, n - 1 - i))],
        out_specs=[pl.BlockSpec((t, LANES), lambda i: (n - 1 - i, 0)), pl.BlockSpec((8, LANES), lambda i: (0, 0))],
        out_shape=[jax.ShapeDtypeStruct((s, LANES), BF16), jax.ShapeDtypeStruct((8, LANES), F32)],
        scratch_shapes=[pltpu.VMEM((8, LANES), F32)],
        compiler_params=_params(("arbitrary",)),
    )(dcumt4, dcumq4, sgt)


def _mem_softmax(qm, kp):
    sc = _dot(qm, kp, NT) * (HEAD_DIM ** -0.5)
    e = jnp.exp(sc - jnp.max(sc, axis=1, keepdims=True))
    return e / jnp.sum(e, axis=1, keepdims=True)


def _branch_m_fwd(proj, mkv, mblk, name):
    s = proj.shape[0]
    t = T_M
    mw = mkv.shape[1] // 2
    ml = mkv.shape[0]

    def body(m_ref, kv_ref, z_ref):
        outs = []
        for pr in range(mw // LANES):
            qp = m_ref[:, pr * LANES:(pr + 1) * LANES]
            kp = kv_ref[:, pr * LANES:(pr + 1) * LANES].astype(BF16)
            vp = kv_ref[:, mw + pr * LANES:mw + (pr + 1) * LANES].astype(BF16)
            oh = []
            for hh in range(2):
                qm = jnp.where(_head_mask(hh), qp, 0.0).astype(BF16)
                oh.append(_dot(_mem_softmax(qm, kp).astype(BF16), vp, NN))
            outs.append(jnp.where(_iota((t, LANES), 1) < HEAD_DIM, oh[0], oh[1]))
        o = jnp.concatenate(outs, axis=1)
        z_ref[...] = (o * _silu(m_ref[:, mw:2 * mw])).astype(BF16)

    return pl.pallas_call(
        body, name=name, grid=(s // t,),
        in_specs=[pl.BlockSpec((t, 2 * mw), lambda i: (i, mblk)), pl.BlockSpec((ml, 2 * mw), lambda i: (0, 0))],
        out_specs=pl.BlockSpec((t, mw), lambda i: (i, 0)),
        out_shape=jax.ShapeDtypeStruct((s, mw), BF16),
        compiler_params=_params(("parallel",)),
    )(proj, mkv)


def _branch_m_bwd(proj, mkv, dz, mblk, name):
    s = proj.shape[0]
    t = T_M
    mw = mkv.shape[1] // 2
    ml = mkv.shape[0]
    scale = HEAD_DIM ** -0.5

    def body(m_ref, kv_ref, dz_ref, dm_ref, dkv_ref):
        i = pl.program_id(0)

        @pl.when(i == 0)
        def _():
            dkv_ref[...] = jnp.zeros_like(dkv_ref)

        gt = m_ref[:, mw:2 * mw]
        dz = dz_ref[...]
        do = dz * _silu(gt)
        outs = []
        for pr in range(mw // LANES):
            cols = slice(pr * LANES, (pr + 1) * LANES)
            vcols = slice(mw + pr * LANES, mw + (pr + 1) * LANES)
            qp = m_ref[:, cols]
            kp = kv_ref[:, cols].astype(BF16)
            vp = kv_ref[:, vcols].astype(BF16)
            dop = do[:, cols]
            oh = []
            dq = jnp.zeros((t, LANES), F32)
            dk = jnp.zeros((ml, LANES), F32)
            dv = jnp.zeros((ml, LANES), F32)
            for hh in range(2):
                hm = _head_mask(hh)
                qm = jnp.where(hm, qp, 0.0).astype(BF16)
                dom = jnp.where(hm, dop, 0.0).astype(BF16)
                p = _mem_softmax(qm, kp)
                pb = p.astype(BF16)
                oh.append(_dot(pb, vp, NN))
                dv = dv + _dot(pb, dom, TN)
                dp = _dot(dom, vp, NT)
                ds = p * (dp - jnp.sum(dp * p, axis=1, keepdims=True))
                dsb = (ds * scale).astype(BF16)
                dq = dq + jnp.where(hm, _dot(dsb, kp, NN), 0.0)
                dk = dk + _dot(dsb, qm, TN)
            outs.append(jnp.where(_iota((t, LANES), 1) < HEAD_DIM, oh[0], oh[1]))
            dm_ref[:, cols] = dq.astype(BF16)
            dkv_ref[:, cols] += dk
            dkv_ref[:, vcols] += dv
        o = jnp.concatenate(outs, axis=1)
        dm_ref[:, mw:2 * mw] = (dz * o * _dsilu(gt)).astype(BF16)

    return pl.pallas_call(
        body, name=name, grid=(s // t,),
        in_specs=[pl.BlockSpec((t, 2 * mw), lambda i: (i, mblk)), pl.BlockSpec((ml, 2 * mw), lambda i: (0, 0)),
                  pl.BlockSpec((t, mw), lambda i: (i, 0))],
        out_specs=[pl.BlockSpec((t, 2 * mw), lambda i: (i, 0)), pl.BlockSpec((ml, 2 * mw), lambda i: (0, 0))],
        out_shape=[jax.ShapeDtypeStruct((s, 2 * mw), BF16), jax.ShapeDtypeStruct((ml, 2 * mw), F32)],
        compiler_params=_params(("arbitrary",)),
    )(proj, mkv, dz)


def _merge_parts(z_refs, g_refs, pcat, bounds):
    ys, sgs = [], []
    merged = None
    for zr, gr, (lo, hi) in zip(z_refs, g_refs, bounds):
        y = _dot(zr[...], pcat[lo:hi, :], NN)
        sg = _sigmoid(gr[...])
        ys.append(y)
        sgs.append(sg)
        merged = sg * y if merged is None else merged + sg * y
    return ys, sgs, merged


def _branch_bounds(zs):
    bounds, lo = [], 0
    for z in zs:
        bounds.append((lo, lo + z.shape[1]))
        lo += z.shape[1]
    return bounds


def _matmul_copies(x, name):
    s, d = x.shape
    t = T_MERGE

    def body(x_ref, b_ref, bt_ref):
        b_ref[...] = x_ref[...].astype(BF16)
        bt_ref[...] = x_ref[...].T.astype(BF16)

    return pl.pallas_call(
        body, name=name, grid=(s // t,), in_specs=[pl.BlockSpec((t, d), lambda i: (i, 0))],
        out_specs=[pl.BlockSpec((t, d), lambda i: (i, 0)), pl.BlockSpec((d, t), lambda i: (0, i))],
        out_shape=[jax.ShapeDtypeStruct((s, d), BF16), jax.ShapeDtypeStruct((d, s), BF16)],
        compiler_params=_params(("parallel",)))(x)


def _merge_fwd(zs, proj, gblk, pcat, wout, x, lng, lnb, alpha, name):
    s, d = x.shape
    t = T_MERGE
    bounds = _branch_bounds(zs)

    def body(*refs):
        z_refs, g_refs = refs[0:4], refs[4:8]
        pcat_hbm, wout_hbm, x_ref, lng_ref, lnb_ref, y_ref, yb_ref, ybt_ref, pcat_v, wout_v = refs[8:]

        @pl.when(pl.program_id(0) == 0)
        def _():
            pltpu.sync_copy(pcat_hbm, pcat_v)
            pltpu.sync_copy(wout_hbm, wout_v)

        _, _, merged = _merge_parts(z_refs, g_refs, pcat_v, bounds)
        h = alpha * x_ref[...] + _dot(merged.astype(BF16), wout_v[...], NN)
        xh, _ = _ln_stats(h)
        y = xh * lng_ref[...] + lnb_ref[...]
        y_ref[...] = y
        yb_ref[...] = y.astype(BF16)
        ybt_ref[...] = y.T.astype(BF16)

    row = pl.BlockSpec((t, d), lambda i: (i, 0))
    vec = pl.BlockSpec((1, d), lambda i: (0, 0))
    in_specs = ([pl.BlockSpec((t, z.shape[1]), lambda i: (i, 0)) for z in zs]
                + [pl.BlockSpec((t, d), lambda i, k=k: (i, gblk + k)) for k in range(4)]
                + [ANY, ANY, row, vec, vec])
    return pl.pallas_call(
        body, name=name, grid=(s // t,), in_specs=in_specs,
        out_specs=[row, row, pl.BlockSpec((d, t), lambda i: (0, i))],
        out_shape=[jax.ShapeDtypeStruct((s, d), F32), jax.ShapeDtypeStruct((s, d), BF16),
                   jax.ShapeDtypeStruct((d, s), BF16)],
        scratch_shapes=[pltpu.VMEM(pcat.shape, BF16), pltpu.VMEM(wout.shape, BF16)],
        compiler_params=_params(("arbitrary",)),
    )(*zs, proj, proj, proj, proj, pcat, wout, x, lng, lnb)


def _merge_bwd(zs, proj, gblk, pcat, wout, x, lng, lnb, dy, alpha, name):
    s, d = x.shape
    t = T_MERGE
    n = s // t
    bounds = _branch_bounds(zs)

    def body(*refs):
        z_refs, g_refs = refs[0:4], refs[4:8]
        pcat_hbm, wout_hbm, x_ref, lng_ref, lnb_ref, dy_ref = refs[8:14]
        dx_ref, dg_ref = refs[14:16]
        dz_refs = refs[16:20]
        dpcat_hbm, dwout_hbm, dlng_ref, dlnb_ref = refs[20:24]
        pcat_v, wout_v, dpcat_v, dwout_v = refs[24:]
        i = pl.program_id(0)

        @pl.when(i == 0)
        def _():
            pltpu.sync_copy(pcat_hbm, pcat_v)
            pltpu.sync_copy(wout_hbm, wout_v)
            dpcat_v[...] = jnp.zeros_like(dpcat_v)
            dwout_v[...] = jnp.zeros_like(dwout_v)
            dlng_ref[...] = jnp.zeros_like(dlng_ref)
            dlnb_ref[...] = jnp.zeros_like(dlnb_ref)

        ys, sgs, merged = _merge_parts(z_refs, g_refs, pcat_v, bounds)
        mb = merged.astype(BF16)
        h = alpha * x_ref[...] + _dot(mb, wout_v[...], NN)
        xh, rstd = _ln_stats(h)
        dyv = dy_ref[...]
        dlng_ref[...] += jnp.sum(dyv * xh, axis=0, keepdims=True)
        dlnb_ref[...] += jnp.sum(dyv, axis=0, keepdims=True)
        dxh = dyv * lng_ref[...]
        dh = rstd * (dxh - jnp.mean(dxh, axis=-1, keepdims=True) - xh * jnp.mean(dxh * xh, axis=-1, keepdims=True))
        dx_ref[...] = alpha * dh
        dhb = dh.astype(BF16)
        dwout_v[...] += _dot(mb, dhb, TN)
        dmerged = _dot(dhb, wout_v[...], NT)
        for k, (zr, (lo, hi)) in enumerate(zip(z_refs, bounds)):
            sg = sgs[k]
            dg_ref[:, k * d:(k + 1) * d] = (dmerged * ys[k] * sg * (1.0 - sg)).astype(BF16)
            dyk = (dmerged * sg).astype(BF16)
            dpcat_v[lo:hi, :] += _dot(zr[...], dyk, TN)
            dz_refs[k][...] = _dot(dyk, pcat_v[lo:hi, :], NT)

        @pl.when(i == n - 1)
        def _():
            pltpu.sync_copy(dpcat_v, dpcat_hbm)
            pltpu.sync_copy(dwout_v, dwout_hbm)

    row = pl.BlockSpec((t, d), lambda i: (i, 0))
    vec = pl.BlockSpec((1, d), lambda i: (0, 0))
    z_specs = [pl.BlockSpec((t, z.shape[1]), lambda i: (i, 0)) for z in zs]
    in_specs = (z_specs + [pl.BlockSpec((t, d), lambda i, k=k: (i, gblk + k)) for k in range(4)]
                + [ANY, ANY, row, vec, vec, row])
    out_specs = [row, pl.BlockSpec((t, 4 * d), lambda i: (i, 0))] + z_specs + [ANY, ANY, vec, vec]
    vo = jax.ShapeDtypeStruct((1, d), F32)
    out_shape = ([jax.ShapeDtypeStruct((s, d), F32), jax.ShapeDtypeStruct((s, 4 * d), BF16)]
                 + [jax.ShapeDtypeStruct(z.shape, F32) for z in zs]
                 + [jax.ShapeDtypeStruct(pcat.shape, F32), jax.ShapeDtypeStruct(wout.shape, F32), vo, vo])
    return pl.pallas_call(
        body, name=name, grid=(n,), in_specs=in_specs, out_specs=out_specs, out_shape=out_shape,
        scratch_shapes=[pltpu.VMEM(pcat.shape, BF16), pltpu.VMEM(wout.shape, BF16),
                        pltpu.VMEM(pcat.shape, F32), pltpu.VMEM(wout.shape, F32)],
        compiler_params=_params(("arbitrary",)),
    )(*zs, proj, proj, proj, proj, pcat, wout, x, lng, lnb, dy)


def _loss_head(y, target, name):
    s, d = y.shape
    t = T_ELEM

    def body(y_ref, t_ref, dy_ref, loss_ref):
        @pl.when(pl.program_id(0) == 0)
        def _():
            loss_ref[...] = jnp.zeros_like(loss_ref)

        e = y_ref[...] - t_ref[...]
        dy_ref[...] = e * (1.0 / d)
        loss_ref[...] += 0.5 * jnp.sum(jnp.mean(e * e, axis=-1, keepdims=True), axis=0, keepdims=True)

    row = pl.BlockSpec((t, d), lambda i: (i, 0))
    return pl.pallas_call(
        body, name=name, grid=(s // t,), in_specs=[row, row],
        out_specs=[row, pl.BlockSpec((8, LANES), lambda i: (0, 0))],
        out_shape=[jax.ShapeDtypeStruct((s, d), F32), jax.ShapeDtypeStruct((8, LANES), F32)],
        compiler_params=_params(("arbitrary",)),
    )(y, target)


def _adamw(w, g, m, v, name):
    n_l, r, c = w.shape
    t = _pick(r, (256, 128, 64, 32, 16, 8))

    def body(w_ref, g_ref, m_ref, v_ref, d_ref, nm_ref, nv_ref):
        gv = g_ref[...]
        nm = ADAM_B1 * m_ref[...] + (1.0 - ADAM_B1) * gv
        nv = ADAM_B2 * v_ref[...] + (1.0 - ADAM_B2) * (gv * gv)
        m_hat = nm / (1.0 - ADAM_B1 ** ADAM_STEP)
        v_hat = nv / (1.0 - ADAM_B2 ** ADAM_STEP)
        d_ref[...] = -ADAM_LR * (m_hat / (jnp.sqrt(v_hat) + ADAM_EPS) + ADAM_WD * w_ref[...])
        nm_ref[...] = nm
        nv_ref[...] = nv

    blk = pl.BlockSpec((1, t, c), lambda l, i: (l, i, 0))
    o = jax.ShapeDtypeStruct((n_l, r, c), F32)
    return pl.pallas_call(body, name=name, grid=(n_l, r // t), in_specs=[blk] * 4, out_specs=[blk] * 3,
                          out_shape=[o, o, o], compiler_params=_params(("parallel", "parallel")))(w, g, m, v)


def _sum_leading(x, out_dtype, name):
    k, r, c = x.shape
    t = _pick(r, (256, 128, 64, 32, 16, 8))

    def body(x_ref, o_ref):
        acc = x_ref[0].astype(F32)
        for j in range(1, k):
            acc = acc + x_ref[j].astype(F32)
        o_ref[...] = acc.astype(out_dtype)

    return pl.pallas_call(body, name=name, grid=(r // t,),
                          in_specs=[pl.BlockSpec((k, t, c), lambda i: (0, i, 0))],
                          out_specs=pl.BlockSpec((t, c), lambda i: (i, 0)),
                          out_shape=jax.ShapeDtypeStruct((r, c), out_dtype), compiler_params=_params(("parallel",)))(x)


def _position():
    return lax.axis_index("x"), lax.axis_index("y"), lax.axis_index("c")


def _chip_peers(x, y):
    return [(1 - x, y), (x, 1 - y), (1 - x, 1 - y)]


def _comm_call(body, n_in, out_shape, n_remote, n_local, name):
    return pl.pallas_call(
        body, name=name, in_specs=[ANY] * n_in, out_specs=[ANY] * len(out_shape), out_shape=out_shape,
        scratch_shapes=[pltpu.SemaphoreType.DMA((n_remote,)), pltpu.SemaphoreType.DMA((n_remote,)),
                        pltpu.SemaphoreType.DMA((max(n_local, 1),))])


def _run_copies(local, remote, send, recv, loc):
    copies = [pltpu.make_async_copy(src, dst, loc.at[k]) for k, (src, dst) in enumerate(local)]
    copies += [pltpu.make_async_remote_copy(src_ref=src, dst_ref=dst, send_sem=send.at[k], recv_sem=recv.at[k],
                                            device_id=peer, device_id_type=MESH)
               for k, (src, dst, peer) in enumerate(remote)]
    for cp in copies:
        cp.start()
    for cp in copies:
        cp.wait()


COPY_BYTES = 1024 * 1024


def _n_copies(rows, row_bytes, align):
    n = 8
    while n > 1 and (rows % (n * align) or rows // n * row_bytes < COPY_BYTES):
        n //= 2
    return n


def _allgather_chips(arrs, name):
    n = len(arrs)
    per_layer = [a.size * a.dtype.itemsize // a.shape[0] >= COPY_BYTES for a in arrs]
    n_each = [a.shape[0] if pl_ else 1 for a, pl_ in zip(arrs, per_layer)]

    def body(*refs):
        ins, outs = refs[:n], refs[n:2 * n]
        send, recv, loc = refs[2 * n:]
        x, y, c = _position()
        me = 2 * x + y
        local, remote = [], []
        for a in range(n):
            if per_layer[a]:
                parts = [(ins[a].at[l], outs[a].at[me, l]) for l in range(arrs[a].shape[0])]
            else:
                parts = [(ins[a], outs[a].at[me])]
            local += parts
            for px, py in _chip_peers(x, y):
                remote += [(src, dst, (px, py, c)) for src, dst in parts]
        _run_copies(local, remote, send, recv, loc)

    out_shape = [jax.ShapeDtypeStruct((4,) + a.shape, a.dtype) for a in arrs]
    return _comm_call(body, n, out_shape, 3 * sum(n_each), sum(n_each), name)(*arrs)


def _allgather_all(v, name):
    def body(v_ref, o_ref, send, recv, loc):
        x, y, c = _position()
        me = 4 * x + 2 * y + c
        remote = []
        for k in range(1, 8):
            fx, fy, fc = (k >> 2) & 1, (k >> 1) & 1, k & 1
            remote.append((v_ref, o_ref.at[me], (x ^ fx, y ^ fy, c ^ fc)))
        _run_copies([(v_ref, o_ref.at[me])], remote, send, recv, loc)

    return _comm_call(body, 1, [jax.ShapeDtypeStruct((8,) + v.shape, v.dtype)], 7, 1, name)(v)[0]


def _pair_exchange_sum(g, name):
    _, _, n, r, cc = g.shape

    def body(c_ref, mine_ref, send_ref, o_ref, land, send_sem, recv_sem):
        x, y, c = _position()
        slot = (pl.program_id(0) * n + pl.program_id(1)) % 2
        push = pltpu.make_async_remote_copy(
            src_ref=send_ref.at[0, 0, 0], dst_ref=land.at[slot], send_sem=send_sem.at[slot],
            recv_sem=recv_sem.at[slot], device_id=(x, y, 1 - c), device_id_type=MESH)
        push.start()
        push.wait_recv()
        o_ref[0, 0] = (mine_ref[0, 0, 0] + land[slot]).astype(BF16)
        push.wait_send()

    grid_spec = pltpu.PrefetchScalarGridSpec(
        num_scalar_prefetch=1, grid=(4, n),
        in_specs=[pl.BlockSpec((1, 1, 1, r, cc), lambda j, k, c: (j, c[0], k, 0, 0)),
                  pl.BlockSpec((1, 1, 1, r, cc), lambda j, k, c: (j, 1 - c[0], k, 0, 0))],
        out_specs=pl.BlockSpec((1, 1, r, cc), lambda j, k, c: (j, k, 0, 0)),
        scratch_shapes=[pltpu.VMEM((2, r, cc), F32), pltpu.SemaphoreType.DMA((2,)), pltpu.SemaphoreType.DMA((2,))])
    return pl.pallas_call(
        body, name=name, grid_spec=grid_spec, out_shape=jax.ShapeDtypeStruct((4, n, r, cc), BF16),
        compiler_params=_params(("arbitrary", "arbitrary")))(_scalar(lax.axis_index("c")), g, g)


def _scalar(v):
    return v.astype(jnp.int32).reshape(1)


def _chip_exchange_sum(p, name):
    _, n, r, cc = p.shape

    def body(i0, i1, i2, i3, own_ref, s0_ref, s1_ref, s2_ref, mine_ref, theirs_ref, land, total, land_pair,
             send_sem, recv_sem, pair_send, pair_recv):
        x, y, c = _position()
        slot = pl.program_id(0) % 2
        pushes = [pltpu.make_async_remote_copy(
            src_ref=src.at[0, 0], dst_ref=land.at[slot, j], send_sem=send_sem.at[slot, j], recv_sem=recv_sem.at[slot, j],
            device_id=(px, py, c), device_id_type=MESH)
            for j, (src, (px, py)) in enumerate(zip((s0_ref, s1_ref, s2_ref), _chip_peers(x, y)))]
        for cp in pushes:
            cp.start()
        acc = own_ref[0, 0].astype(F32)
        for j, cp in enumerate(pushes):
            cp.wait_recv()
            acc = acc + land[slot, j].astype(F32)
        mine_ref[0] = acc
        total[slot] = acc
        share = pltpu.make_async_remote_copy(
            src_ref=total.at[slot], dst_ref=land_pair.at[slot], send_sem=pair_send.at[slot], recv_sem=pair_recv.at[slot],
            device_id=(x, y, 1 - c), device_id_type=MESH)
        share.start()
        share.wait_recv()
        theirs_ref[0] = land_pair[slot]
        for cp in pushes:
            cp.wait_send()
        share.wait_send()

    def slot_spec(which):
        return pl.BlockSpec((1, 1, r, cc), lambda k, *idx, which=which: (idx[which][0], k, 0, 0))

    out_spec = pl.BlockSpec((1, r, cc), lambda k, *idx: (k, 0, 0))
    grid_spec = pltpu.PrefetchScalarGridSpec(
        num_scalar_prefetch=4, grid=(n,), in_specs=[slot_spec(0), slot_spec(1), slot_spec(2), slot_spec(3)],
        out_specs=[out_spec, out_spec],
        scratch_shapes=[pltpu.VMEM((2, 3, r, cc), BF16), pltpu.VMEM((2, r, cc), F32), pltpu.VMEM((2, r, cc), F32),
                        pltpu.SemaphoreType.DMA((2, 3)), pltpu.SemaphoreType.DMA((2, 3)),
                        pltpu.SemaphoreType.DMA((2,)), pltpu.SemaphoreType.DMA((2,))])
    o = jax.ShapeDtypeStruct((n, r, cc), F32)
    x, y, _ = _position()
    chips = [_scalar(2 * x + y)] + [_scalar(2 * px + py) for px, py in _chip_peers(x, y)]
    return pl.pallas_call(body, name=name, grid_spec=grid_spec, out_shape=[o, o],
                          compiler_params=_params(("arbitrary",)))(*chips, p, p, p, p)


def _reduce_scatter(gs):
    c = lax.axis_index("c")
    outs = []
    for a, g in enumerate(gs):
        _, rows, cc = g.shape
        k = _n_copies(rows // 2, cc * 4, 16)
        p = _pair_exchange_sum(g.reshape(4, 2, k, rows // (2 * k), cc), f"rs_pair_exchange_sum_{a}")
        mine, theirs = _chip_exchange_sum(p, f"rs_chip_exchange_sum_{a}")
        mine, theirs = mine.reshape(rows // 2, cc), theirs.reshape(rows // 2, cc)
        outs.append(jnp.where(c == 0, jnp.concatenate([mine, theirs]), jnp.concatenate([theirs, mine])))
    return outs


def _gather_shards(w, name):
    rows, cc = w.shape
    half = rows // 2
    n = _n_copies(half, cc * w.dtype.itemsize, 16)
    r = half // n

    def body(core_ref, mine_ref, other_ref, out_ref, land, land_pair, send_sem, recv_sem, pair_send, pair_recv, out_sem):
        x, y, c = _position()
        k = pl.program_id(0)
        slot = k % 2
        me = 2 * x + y
        chips = [2 * px + py for px, py in _chip_peers(x, y)]
        pushes = [pltpu.make_async_remote_copy(
            src_ref=mine_ref.at[0, 0], dst_ref=land.at[slot, j], send_sem=send_sem.at[slot, j],
            recv_sem=recv_sem.at[slot, j], device_id=(px, py, c), device_id_type=MESH)
            for j, (px, py) in enumerate(_chip_peers(x, y))]
        for cp in pushes:
            cp.start()
        writes = [pltpu.make_async_copy(mine_ref.at[0, 0], out_ref.at[me, c, k], out_sem.at[0]),
                  pltpu.make_async_copy(other_ref.at[0, 0], out_ref.at[me, 1 - c, k], out_sem.at[1])]
        for cp in writes:
            cp.start()
        passes = []
        for j, cp in enumerate(pushes):
            cp.wait_recv()
            passes.append(pltpu.make_async_remote_copy(
                src_ref=land.at[slot, j], dst_ref=land_pair.at[slot, j], send_sem=pair_send.at[slot, j],
                recv_sem=pair_recv.at[slot, j], device_id=(x, y, 1 - c), device_id_type=MESH))
            passes[j].start()
            writes.append(pltpu.make_async_copy(land.at[slot, j], out_ref.at[chips[j], c, k], out_sem.at[2 + j]))
            writes[-1].start()
        for j, cp in enumerate(passes):
            cp.wait_recv()
            writes.append(pltpu.make_async_copy(land_pair.at[slot, j], out_ref.at[chips[j], 1 - c, k], out_sem.at[5 + j]))
            writes[-1].start()
        for cp in writes:
            cp.wait()
        for cp in pushes + passes:
            cp.wait_send()

    grid_spec = pltpu.PrefetchScalarGridSpec(
        num_scalar_prefetch=1, grid=(n,),
        in_specs=[pl.BlockSpec((1, 1, r, cc), lambda k, core: (core[0], k, 0, 0)),
                  pl.BlockSpec((1, 1, r, cc), lambda k, core: (1 - core[0], k, 0, 0))],
        out_specs=ANY,
        scratch_shapes=[pltpu.VMEM((2, 3, r, cc), w.dtype), pltpu.VMEM((2, 3, r, cc), w.dtype),
                        pltpu.SemaphoreType.DMA((2, 3)), pltpu.SemaphoreType.DMA((2, 3)),
                        pltpu.SemaphoreType.DMA((2, 3)), pltpu.SemaphoreType.DMA((2, 3)),
                        pltpu.SemaphoreType.DMA((8,))])
    w4 = w.reshape(2, n, r, cc)
    out = pl.pallas_call(body, name=name, grid_spec=grid_spec,
                         out_shape=jax.ShapeDtypeStruct((4, 2, n, r, cc), w.dtype),
                         compiler_params=_params(("arbitrary",)))(_scalar(lax.axis_index("c")), w4, w4)
    return out.reshape(4, rows, cc)


def _pad_rows(a, rows):
    return jnp.pad(a, ((0, rows - a.shape[0]), (0, 0)))


def _shard_cols(a):
    r, c4 = a.shape
    return a.reshape(r, 4, c4 // 4).transpose(1, 0, 2)


def kernel(x, mem, w_in, b_forget, conv_a_w, conv_a_b, ln_a_g, ln_a_b, conv_b_w, w_kv_mem, mem_ln_g, mem_ln_b, p_a, p_b, p_c, p_m, w_out, ln_g, ln_b, loss_target, m_w_in, m_b_forget, m_conv_a_w, m_conv_a_b, m_ln_a_g, m_ln_a_b, m_conv_b_w, m_w_kv_mem, m_mem_ln_g, m_mem_ln_b, m_p_a, m_p_b, m_p_c, m_p_m, m_w_out, m_ln_g, m_ln_b, v_w_in, v_b_forget, v_conv_a_w, v_conv_a_b, v_ln_a_g, v_ln_a_b, v_conv_b_w, v_w_kv_mem, v_mem_ln_g, v_mem_ln_b, v_p_a, v_p_b, v_p_c, v_p_m, v_w_out, v_ln_g, v_ln_b):
    depth = w_in.shape[0]
    x0 = x[0]
    s, d = x0.shape
    aw = conv_a_w.shape[2] * 4
    mw = p_m.shape[1]
    n_head = b_forget.shape[1]
    alpha = (2.0 * depth) ** 0.25
    in_cols = w_in.shape[2] * 4
    assert aw == n_head * HEAD_DIM and mw % LANES == 0 and in_cols == 11 * aw + n_head + 2 * mw + 4 * d
    cf0 = 10 * aw
    n_main = in_cols - n_head
    fblk = n_main // LANES
    n_pad = n_main + LANES
    gblk = (11 * aw + 2 * mw) // d
    mblk = (11 * aw) // (2 * mw)
    qblk, kblk, vblk = 7 * aw // LANES, 8 * aw // LANES, 9 * aw // LANES
    assert (11 * aw + 2 * mw) % d == 0 and (11 * aw) % (2 * mw) == 0

    def gather(w, name):
        flat = _gather_shards(w.astype(BF16).reshape(-1, w.shape[-1]), name)
        return flat.reshape((4,) + w.shape)

    w_in_g, p_a_g, p_b_g, p_c_g = (gather(w, f"gather_{nm}") for w, nm in
                                   ((w_in, "w_in"), (p_a, "p_a"), (p_b, "p_b"), (p_c, "p_c")))
    p_m_g, w_kv_g, w_out_g = (gather(w, f"gather_{nm}") for w, nm in
                              ((p_m, "p_m"), (w_kv_mem, "w_kv"), (w_out, "w_out")))
    conv_a_g, conv_b_g = _allgather_chips([conv_a_w, conv_b_w], "gather_conv_taps")

    def cols(g):
        return jnp.concatenate([g[j] for j in range(4)], axis=-1)

    def rows(g):
        return jnp.concatenate([g[j] for j in range(4)], axis=-2)

    shard_w = in_cols // 4

    def global_columns(shards, lo, hi):
        return [shards[j][:, max(lo, j * shard_w) - j * shard_w:min(hi, (j + 1) * shard_w) - j * shard_w]
                for j in range(4) if max(lo, j * shard_w) < min(hi, (j + 1) * shard_w)]

    w_pad = []
    for l in range(depth):
        shards = [w_in_g[j, l] for j in range(4)]
        w_pad.append(jnp.concatenate(
            global_columns(shards, 0, cf0) + global_columns(shards, cf0 + n_head, in_cols)
            + global_columns(shards, cf0, cf0 + n_head) + [jnp.zeros((d, LANES - n_head), BF16)], axis=1))

    piece_order = [(0, 3 * aw, 0), (3 * aw, 7 * aw, 1), (7 * aw, cf0, 2), (cf0, cf0 + n_head, 6),
                   (cf0 + n_head, 11 * aw + n_head, 3), (11 * aw + n_head, 11 * aw + n_head + 2 * mw, 4),
                   (11 * aw + n_head + 2 * mw, in_cols, 5)]

    def own_columns(parts, j):
        lo, hi = j * shard_w, (j + 1) * shard_w
        return jnp.concatenate([parts[k][:, max(lo, a) - a:min(hi, b) - a]
                                for a, b, k in piece_order if max(lo, a) < min(hi, b)], axis=1)
    pcat = jnp.concatenate([cols(p_a_g), cols(p_b_g), cols(p_c_g), cols(p_m_g)], axis=1)
    w_kv = rows(w_kv_g)
    wout = rows(w_out_g)
    conv_a = jnp.pad(cols(conv_a_g), ((0, 0), (0, HALO_A - CONV_A), (0, 0)))
    conv_b = jnp.pad(cols(conv_b_g), ((0, 0), (0, HALO_B - CONV_B), (0, 0)))
    bf_pad = jnp.pad(b_forget, ((0, 0), (0, LANES - n_head)))
    pieces = [(0, 3 * aw), (3 * aw, 7 * aw), (7 * aw, 10 * aw), (10 * aw, 11 * aw),
              (11 * aw, 11 * aw + 2 * mw), (11 * aw + 2 * mw, n_main), (n_main, n_pad)]

    mem_n = _ln_rows(mem[0], mem_ln_g[None], mem_ln_b[None], "mem_ln")

    xs, saved = [x0], []
    x_ops = [_matmul_copies(x0, "matmul_copies")]
    for l in range(depth):
        xl = xs[-1]
        proj = _mm(x_ops[l][0], w_pad[l], name=f"proj_{l}")
        za, conv_out = _branch_a_fwd(proj, conv_a[l], conv_a_b[l][None], ln_a_g[l][None], ln_a_b[l][None], f"a_fwd_{l}")
        zb = _branch_b_fwd(proj, conv_b[l], f"b_fwd_{l}")
        cum, cumt, sgt = _forget_prep(proj, bf_pad[l][None], fblk, f"forget_prep_{l}")
        o_c, lse = _fox_fwd(proj, cum, cumt, qblk, kblk, vblk, f"fox_fwd_{l}")
        zc = _gate_mul(proj, o_c, 10, f"c_gate_{l}")
        mkv = _mm(mem_n, w_kv[l], name=f"mkv_{l}")
        zm = _branch_m_fwd(proj, mkv, mblk, f"m_fwd_{l}")
        zs = [za, zb, zc, zm]
        y, y_b, y_bt = _merge_fwd(zs, proj, gblk, pcat[l], wout[l], xl, ln_g[l][None], ln_b[l][None], alpha,
                                  f"merge_fwd_{l}")
        xs.append(y)
        x_ops.append((y_b, y_bt))
        saved.append((proj, zs, conv_out, cum, cumt, sgt, o_c, lse, mkv))

    dy, loss_part = _loss_head(xs[-1], loss_target[0], "loss_head")

    g_w_in, g_conv_a, g_conv_b, g_w_kv, g_pcat, g_wout = [], [], [], [], [], []
    small = []
    dmem_n = None
    for l in reversed(range(depth)):
        proj, zs, conv_out, cum, cumt, sgt, o_c, lse, mkv = saved[l]
        xl = xs[l]
        (dx, d_g, dza, dzb, dzc, dzm, dpcat, dwout, dlng, dlnb) = _merge_bwd(
            zs, proj, gblk, pcat[l], wout[l], xl, ln_g[l][None], ln_b[l][None], dy, alpha, f"merge_bwd_{l}")
        d_m, dmkv = _branch_m_bwd(proj, mkv, dzm, mblk, f"m_bwd_{l}")
        g_w_kv.append(_mm(mem_n, dmkv, ta=True, name=f"dwkv_{l}"))
        dmem_n = _mm(dmkv, w_kv[l], tb=True, add=dmem_n, name=f"dmem_{l}")
        do, d_cg, delta = _fox_bwd_prep(proj, dzc, o_c, 10, f"fox_bwd_prep_{l}")
        dq, dk, dv, dcumt4, dcumq4 = _fox_bwd(proj, do, lse, delta, cum, cumt, qblk, kblk, vblk, f"fox_bwd_{l}")
        d_f, dbf = _forget_bwd(dcumt4, dcumq4, sgt, f"forget_bwd_{l}")
        d_b, dconv_b = _branch_b_bwd(proj, dzb, conv_b[l], f"b_bwd_{l}")
        d_a, dconv_a, dconv_ab, dlag, dlab = _branch_a_bwd(
            proj, conv_out, dza, conv_a[l], ln_a_g[l][None], ln_a_b[l][None], f"a_bwd_{l}")
        dparts = [d_a, d_b, jnp.concatenate([dq, dk, dv], axis=1), d_cg, d_m, d_g, d_f]
        dx = _input_grad(dparts, [w_pad[l][:, lo:hi] for lo, hi in pieces], dx, f"dx_{l}")
        dw_parts = [_mm(x_ops[l][1], dp, tm=d, tk=_pick(s, (T_DW_K,)), name=f"dw_{l}_{k}")
                    for k, dp in enumerate(dparts)]
        g_w_in.append(dw_parts)
        g_conv_a.append(dconv_a)
        g_conv_b.append(dconv_b)
        g_pcat.append(dpcat)
        g_wout.append(dwout)
        small.append([dbf[:, 0], dconv_ab[0], dlag[0], dlab[0], dlng[0], dlnb[0]])
        dy = dx
    grad_x = dy
    dmlg, dmlb = _ln_rows_param_grads(mem[0], dmem_n, "mem_ln_grads")
    for lst in (g_w_in, g_conv_a, g_conv_b, g_w_kv, g_pcat, g_wout, small):
        lst.reverse()

    pa_end, pb_end, pc_end = aw, 2 * aw, 3 * aw
    rs_in = [
        jnp.stack([jnp.concatenate([own_columns(parts, j) for parts in g_w_in], axis=0) for j in range(4)]),
        _shard_cols(jnp.concatenate(g_conv_a, axis=0)),
        _shard_cols(jnp.concatenate([_pad_rows(g, 2 * HALO_B) for g in g_conv_b], axis=0)),
        jnp.concatenate([g.reshape(4, g.shape[0] // 4, g.shape[1]) for g in g_w_kv], axis=1),
        _shard_cols(jnp.concatenate([g[:pa_end] for g in g_pcat], axis=0)),
        _shard_cols(jnp.concatenate([g[pa_end:pb_end] for g in g_pcat], axis=0)),
        _shard_cols(jnp.concatenate([g[pb_end:pc_end] for g in g_pcat], axis=0)),
        _shard_cols(jnp.concatenate([g[pc_end:] for g in g_pcat], axis=0)),
        jnp.concatenate([g.reshape(4, g.shape[0] // 4, g.shape[1]) for g in g_wout], axis=1),
    ]
    rs_out = _reduce_scatter(rs_in)
    gw_in = rs_out[0].reshape(depth, d, -1)
    g_ca = rs_out[1].reshape(depth, HALO_A, -1)[:, :CONV_A]
    g_cb = rs_out[2].reshape(depth, 2 * HALO_B, -1)[:, :CONV_B]
    gw_kv = rs_out[3].reshape(depth, -1, 2 * mw)
    gp_a = rs_out[4].reshape(depth, aw, -1)
    gp_b = rs_out[5].reshape(depth, aw, -1)
    gp_c = rs_out[6].reshape(depth, aw, -1)
    gp_m = rs_out[7].reshape(depth, mw, -1)
    gw_out = rs_out[8].reshape(depth, -1, d)

    flat = jnp.concatenate([jnp.concatenate(p) for p in small] + [dmlg[0], dmlb[0], loss_part[0, 0:1]])
    n_small = flat.shape[0]
    n_rows = -(-n_small // (8 * LANES)) * 8
    vec = jnp.pad(flat, (0, n_rows * LANES - n_small)).reshape(n_rows, LANES)
    tot = _sum_leading(_allgather_all(vec, "gather_small"), F32, "sum_small").reshape(-1)
    per_layer = n_head + 3 * aw + 2 * d
    tl = tot[:depth * per_layer].reshape(depth, per_layer)
    offs = np.cumsum([0, n_head, aw, aw, aw, d, d])
    g_bf, g_cab, g_lag, g_lab, g_lg, g_lb = [tl[:, offs[k]:offs[k + 1]] for k in range(6)]
    base = depth * per_layer
    g_mlg, g_mlb = tot[base:base + d], tot[base + d:base + 2 * d]
    loss = tot[base + 2 * d]

    grads = [gw_in, g_bf, g_ca, g_cab, g_lag, g_lab, g_cb, gw_kv, g_mlg, g_mlb, gp_a, gp_b, gp_c, gp_m, gw_out, g_lg, g_lb]
    ws = [w_in, b_forget, conv_a_w, conv_a_b, ln_a_g, ln_a_b, conv_b_w, w_kv_mem, mem_ln_g, mem_ln_b, p_a, p_b, p_c, p_m, w_out, ln_g, ln_b]
    ms = [m_w_in, m_b_forget, m_conv_a_w, m_conv_a_b, m_ln_a_g, m_ln_a_b, m_conv_b_w, m_w_kv_mem, m_mem_ln_g, m_mem_ln_b, m_p_a, m_p_b, m_p_c, m_p_m, m_w_out, m_ln_g, m_ln_b]
    vs = [v_w_in, v_b_forget, v_conv_a_w, v_conv_a_b, v_ln_a_g, v_ln_a_b, v_conv_b_w, v_w_kv_mem, v_mem_ln_g, v_mem_ln_b, v_p_a, v_p_b, v_p_c, v_p_m, v_w_out, v_ln_g, v_ln_b]
    deltas, new_ms, new_vs = [], [], []
    for k, (wk, gk, mk, vk) in enumerate(zip(ws, grads, ms, vs)):
        shape = wk.shape
        as_3d = (1,) * (3 - wk.ndim) + shape
        dk_, nm_, nv_ = _adamw(wk.reshape(as_3d), gk.reshape(as_3d), mk.reshape(as_3d), vk.reshape(as_3d), f"adamw_{k}")
        deltas.append(dk_.reshape(shape))
        new_ms.append(nm_.reshape(shape))
        new_vs.append(nv_.reshape(shape))
        grads[k] = gk.reshape(shape)
    return (loss, grad_x[None], *grads, *deltas, *new_ms, *new_vs)


def _gate_mul(proj, o, gblk, name):
    s, w = o.shape
    t = T_ELEM

    def body(g_ref, o_ref, z_ref):
        z_ref[...] = (o_ref[...] * _silu(g_ref[...])).astype(BF16)

    row = pl.BlockSpec((t, w), lambda i: (i, 0))
    return pl.pallas_call(body, name=name, grid=(s // t,), in_specs=[pl.BlockSpec((t, w), lambda i: (i, gblk)), row],
                          out_specs=row, out_shape=jax.ShapeDtypeStruct((s, w), BF16),
                          compiler_params=_params(("parallel",)))(proj, o)
```

```python
import functools
import math

import numpy as np
import jax
import jax.numpy as jnp
from jax import lax
from jax.experimental import pallas as pl
from jax.experimental.pallas import tpu as pltpu

F32 = jnp.float32
BF16 = jnp.bfloat16
MESH = pl.DeviceIdType.MESH
ANY = pl.BlockSpec(memory_space=pl.ANY)

LN_EPS = 1e-5
NEG_BIG = -1e30
HEAD_DIM = 64
LANES = 128
CONV_A = 31
CONV_B = 3
HALO_A = 32
HALO_B = 8
CHUNK = 32
VMEM_LIMIT = 60 * 1024 * 1024

ADAM_LR, ADAM_B1, ADAM_B2, ADAM_EPS, ADAM_WD, ADAM_STEP = 0.001, 0.9, 0.999, 1e-08, 0.01, 10

T_MM = 512
T_A = 128
T_B = 256
T_ATT_Q = 512
T_ATT_K = 1024
T_ATT_K_FWD = 2048
T_CUM = 512
T_M = 512
T_MERGE = 256
T_DX = 256
T_DW_K = 2048
T_ELEM = 512


def _pick(n, prefs):
    for p in prefs:
        if n % p == 0:
            return p
    return n


def _params(sem=None):
    return pltpu.CompilerParams(dimension_semantics=sem, vmem_limit_bytes=VMEM_LIMIT)


def _sigmoid(x):
    return jax.nn.sigmoid(x)


def _silu(x):
    return x * _sigmoid(x)


def _dsilu(x):
    s = _sigmoid(x)
    return s * (1.0 + x * (1.0 - s))


def _dot(a, b, dims, precision=None):
    return lax.dot_general(a, b, (dims, ((), ())), preferred_element_type=F32, precision=precision)


NN = ((1,), (0,))
NT = ((1,), (1,))
TN = ((0,), (0,))


def _iota(shape, dim):
    return lax.broadcasted_iota(jnp.int32, shape, dim)


def _mm(a, b, *, ta=False, tb=False, add=None, out_dtype=F32, tm=None, tk=None, name):
    m = a.shape[1] if ta else a.shape[0]
    k = a.shape[0] if ta else a.shape[1]
    n = b.shape[0] if tb else b.shape[1]
    if tm is None:
        tm = _pick(m, (1024, 512, 256)) if ta else _pick(m, (T_MM, 256))
    tn = _pick(n, (3456, 1152, 1024, 768, 512, 384, 256, 128))
    if tk is None:
        tk = _pick(k, (1024, 512, 256))
    nk = k // tk
    dims = ((0,) if ta else (1,), (1,) if tb else (0,))

    def body(*refs):
        if add is None:
            a_ref, b_ref, o_ref, acc_ref = refs
        else:
            a_ref, b_ref, add_ref, o_ref, acc_ref = refs
        kk = pl.program_id(2)
        p = _dot(a_ref[...].astype(BF16), b_ref[...].astype(BF16), dims)

        @pl.when(kk == 0)
        def _():
            acc_ref[...] = p

        @pl.when(kk > 0)
        def _():
            acc_ref[...] += p

        @pl.when(kk == nk - 1)
        def _():
            r = acc_ref[...]
            if add is not None:
                r = r + add_ref[...]
            o_ref[...] = r.astype(out_dtype)

    a_spec = (pl.BlockSpec((tk, tm), lambda j, i, kk: (kk, i)) if ta
              else pl.BlockSpec((tm, tk), lambda j, i, kk: (i, kk)))
    b_spec = (pl.BlockSpec((tn, tk), lambda j, i, kk: (j, kk)) if tb
              else pl.BlockSpec((tk, tn), lambda j, i, kk: (kk, j)))
    o_spec = pl.BlockSpec((tm, tn), lambda j, i, kk: (i, j))
    in_specs = [a_spec, b_spec] + ([o_spec] if add is not None else [])
    args = (a, b) + ((add,) if add is not None else ())
    return pl.pallas_call(
        body, name=name, grid=(n // tn, m // tm, nk), in_specs=in_specs, out_specs=o_spec,
        out_shape=jax.ShapeDtypeStruct((m, n), out_dtype),
        scratch_shapes=[pltpu.VMEM((tm, tn), F32)],
        compiler_params=_params(("parallel", "parallel", "arbitrary")),
    )(*args)


def _input_grad(dparts, wparts, add, name):
    s, d = add.shape
    t = T_DX
    n_p = len(dparts)

    def body(*refs):
        d_refs, w_hbm = refs[:n_p], refs[n_p:2 * n_p]
        add_ref, o_ref = refs[2 * n_p], refs[2 * n_p + 1]
        w_v = refs[2 * n_p + 2:]

        @pl.when(pl.program_id(0) == 0)
        def _():
            for p in range(n_p):
                pltpu.sync_copy(w_hbm[p], w_v[p])

        acc = add_ref[...]
        for p in range(n_p):
            acc = acc + _dot(d_refs[p][...], w_v[p][...], NT)
        o_ref[...] = acc

    row = pl.BlockSpec((t, d), lambda i: (i, 0))
    in_specs = ([pl.BlockSpec((t, dp.shape[1]), lambda i: (i, 0)) for dp in dparts] + [ANY] * n_p + [row])
    return pl.pallas_call(
        body, name=name, grid=(s // t,), in_specs=in_specs, out_specs=row,
        out_shape=jax.ShapeDtypeStruct((s, d), F32),
        scratch_shapes=[pltpu.VMEM(w.shape, BF16) for w in wparts],
        compiler_params=_params(("arbitrary",)),
    )(*dparts, *wparts, add)


def _ln_rows(x, g, b, name):
    r, d = x.shape
    t = _pick(r, (256,))

    def body(x_ref, g_ref, b_ref, o_ref):
        xv = x_ref[...]
        mu = jnp.mean(xv, axis=-1, keepdims=True)
        dv = xv - mu
        var = jnp.mean(dv * dv, axis=-1, keepdims=True)
        o_ref[...] = dv * lax.rsqrt(var + LN_EPS) * g_ref[...] + b_ref[...]

    row = pl.BlockSpec((t, d), lambda i: (i, 0))
    vec = pl.BlockSpec((1, d), lambda i: (0, 0))
    return pl.pallas_call(body, name=name, grid=(r // t,), in_specs=[row, vec, vec], out_specs=row,
                          out_shape=jax.ShapeDtypeStruct((r, d), F32), compiler_params=_params(("parallel",)))(x, g, b)


def _ln_rows_param_grads(x, dy, name):
    r, d = x.shape

    def body(x_ref, dy_ref, dg_ref, db_ref):
        xv = x_ref[...]
        mu = jnp.mean(xv, axis=-1, keepdims=True)
        dv = xv - mu
        var = jnp.mean(dv * dv, axis=-1, keepdims=True)
        xh = dv * lax.rsqrt(var + LN_EPS)
        dg_ref[...] = jnp.sum(dy_ref[...] * xh, axis=0, keepdims=True)
        db_ref[...] = jnp.sum(dy_ref[...], axis=0, keepdims=True)

    full = pl.BlockSpec((r, d), lambda i: (0, 0))
    vec = pl.BlockSpec((1, d), lambda i: (0, 0))
    o = jax.ShapeDtypeStruct((1, d), F32)
    return pl.pallas_call(body, name=name, grid=(1,), in_specs=[full, full], out_specs=[vec, vec],
                          out_shape=[o, o], compiler_params=_params(("arbitrary",)))(x, dy)


def _conv_a_chunk(glu_ref, cw_ref, r0):
    acc = cw_ref[0:1, :] * glu_ref[pl.ds(r0 + 2, CHUNK), :]
    for k in range(1, CONV_A):
        acc = acc + cw_ref[k:k + 1, :] * glu_ref[pl.ds(r0 + 2 + k, CHUNK), :]
    return acc


def _ln_stats(c):
    mu = jnp.mean(c, axis=-1, keepdims=True)
    d = c - mu
    var = jnp.mean(d * d, axis=-1, keepdims=True)
    rstd = lax.rsqrt(var + LN_EPS)
    return d * rstd, rstd


def _branch_a_fwd(proj, cw, cb, lg, lb, name):
    s = proj.shape[0]
    t = T_A
    w = cw.shape[1]
    r = t // HALO_A

    def body(u_ref, v_ref, gt_ref, hu_ref, hv_ref, cw_ref, cb_ref, lg_ref, lb_ref, z_ref, conv_ref, glu):
        i = pl.program_id(0)
        hglu = hu_ref[...] * _sigmoid(hv_ref[...])
        glu[0:HALO_A, :] = jnp.where(i > 0, hglu, 0.0)
        glu[HALO_A:HALO_A + t, :] = u_ref[...] * _sigmoid(v_ref[...])
        for c in range(t // CHUNK):
            r0 = c * CHUNK
            conv = _conv_a_chunk(glu, cw_ref, r0) + cb_ref[...]
            conv_ref[r0:r0 + CHUNK, :] = conv
            xh, _ = _ln_stats(conv)
            a3 = _silu(xh * lg_ref[...] + lb_ref[...])
            z_ref[r0:r0 + CHUNK, :] = (a3 * _silu(gt_ref[r0:r0 + CHUNK, :])).astype(BF16)

    def cur(col):
        return pl.BlockSpec((t, w), lambda i, col=col: (i, col))

    def prev(col):
        return pl.BlockSpec((HALO_A, w), lambda i, col=col: (jnp.maximum(i * r - 1, 0), col))

    vec = pl.BlockSpec((1, w), lambda i: (0, 0))
    return pl.pallas_call(
        body, name=name, grid=(s // t,),
        in_specs=[cur(0), cur(1), cur(2), prev(0), prev(1), pl.BlockSpec((HALO_A, w), lambda i: (0, 0)), vec, vec, vec],
        out_specs=[pl.BlockSpec((t, w), lambda i: (i, 0)), pl.BlockSpec((t, w), lambda i: (i, 0))],
        out_shape=[jax.ShapeDtypeStruct((s, w), BF16), jax.ShapeDtypeStruct((s, w), F32)],
        scratch_shapes=[pltpu.VMEM((HALO_A + t, w), F32)],
        compiler_params=_params(("parallel",)),
    )(proj, proj, proj, proj, proj, cw, cb, lg, lb)


def _branch_a_bwd(proj, conv, dz, cw, lg, lb, name):
    s = proj.shape[0]
    t = T_A
    w = cw.shape[1]
    r = t // HALO_A
    n = s // t
    nblk = s // HALO_A
    ext = t + HALO_A

    def body(u_ref, v_ref, gt_ref, dz_ref, conv_ref, pu_ref, pv_ref, ngt_ref, ndz_ref, nconv_ref,
             cw_ref, lg_ref, lb_ref, da_ref, dw_ref, dcb_ref, dlg_ref, dlb_ref, glu, dc, dw8):
        i = pl.program_id(0)

        @pl.when(i == 0)
        def _():
            dw8[...] = jnp.zeros_like(dw8)
            dcb_ref[...] = jnp.zeros_like(dcb_ref)
            dlg_ref[...] = jnp.zeros_like(dlg_ref)
            dlb_ref[...] = jnp.zeros_like(dlb_ref)

        glu[0:HALO_A, :] = jnp.where(i > 0, pu_ref[...] * _sigmoid(pv_ref[...]), 0.0)
        glu[HALO_A:HALO_A + t, :] = u_ref[...] * _sigmoid(v_ref[...])
        has_next = i < n - 1
        dcb = jnp.zeros((1, w), F32)
        dlg = jnp.zeros((1, w), F32)
        dlb = jnp.zeros((1, w), F32)
        for c in range(ext // CHUNK):
            r0 = c * CHUNK
            own = r0 < t
            xh, rstd = _ln_stats(conv_ref[r0:r0 + CHUNK, :] if own else nconv_ref[...])
            a2 = xh * lg_ref[...] + lb_ref[...]
            if own:
                gt = gt_ref[r0:r0 + CHUNK, :]
                dzc = dz_ref[r0:r0 + CHUNK, :]
            else:
                gt = ngt_ref[...]
                dzc = ndz_ref[...]
            da2 = dzc * _silu(gt) * _dsilu(a2)
            dxh = da2 * lg_ref[...]
            dconv = rstd * (dxh - jnp.mean(dxh, axis=-1, keepdims=True)
                            - xh * jnp.mean(dxh * xh, axis=-1, keepdims=True))
            if own:
                dc[r0:r0 + CHUNK, :] = dconv
                da_ref[r0:r0 + CHUNK, 2 * w:3 * w] = (dzc * _silu(a2) * _dsilu(gt)).astype(BF16)
                dcb = dcb + jnp.sum(dconv, axis=0, keepdims=True)
                dlg = dlg + jnp.sum(da2 * xh, axis=0, keepdims=True)
                dlb = dlb + jnp.sum(da2, axis=0, keepdims=True)
            else:
                dc[r0:r0 + CHUNK, :] = jnp.where(has_next, dconv, 0.0)
        dcb_ref[...] += dcb
        dlg_ref[...] += dlg
        dlb_ref[...] += dlb
        for c in range(t // CHUNK):
            r0 = c * CHUNK
            dcc = dc[r0:r0 + CHUNK, :]
            dglu = cw_ref[0:1, :] * dc[pl.ds(r0 + CONV_A - 1, CHUNK), :]
            for k in range(1, CONV_A):
                dglu = dglu + cw_ref[k:k + 1, :] * dc[pl.ds(r0 + CONV_A - 1 - k, CHUNK), :]
            for k in range(CONV_A):
                prod = dcc * glu[pl.ds(r0 + 2 + k, CHUNK), :]
                dw8[k] += jnp.sum(prod.reshape(CHUNK // 8, 8, w), axis=0)
            sv = _sigmoid(v_ref[r0:r0 + CHUNK, :])
            da_ref[r0:r0 + CHUNK, 0:w] = (dglu * sv).astype(BF16)
            da_ref[r0:r0 + CHUNK, w:2 * w] = (dglu * u_ref[r0:r0 + CHUNK, :] * sv * (1.0 - sv)).astype(BF16)

        @pl.when(i == n - 1)
        def _():
            dw_ref[...] = jnp.sum(dw8[...], axis=1)

    def cur(col):
        return pl.BlockSpec((t, w), lambda i, col=col: (i, col))

    def prev(col):
        return pl.BlockSpec((HALO_A, w), lambda i, col=col: (jnp.maximum(i * r - 1, 0), col))

    def nxt(col):
        return pl.BlockSpec((HALO_A, w), lambda i, col=col: (jnp.minimum((i + 1) * r, nblk - 1), col))

    own_rows = pl.BlockSpec((t, w), lambda i: (i, 0))
    vec = pl.BlockSpec((1, w), lambda i: (0, 0))
    vo = jax.ShapeDtypeStruct((1, w), F32)
    return pl.pallas_call(
        body, name=name, grid=(n,),
        in_specs=[cur(0), cur(1), cur(2), own_rows, own_rows, prev(0), prev(1), nxt(2), nxt(0), nxt(0),
                  pl.BlockSpec((HALO_A, w), lambda i: (0, 0)), vec, vec],
        out_specs=[pl.BlockSpec((t, 3 * w), lambda i: (i, 0)), pl.BlockSpec((HALO_A, w), lambda i: (0, 0)), vec, vec, vec],
        out_shape=[jax.ShapeDtypeStruct((s, 3 * w), BF16), jax.ShapeDtypeStruct((HALO_A, w), F32), vo, vo, vo],
        scratch_shapes=[pltpu.VMEM((HALO_A + t, w), F32), pltpu.VMEM((ext, w), F32),
                        pltpu.VMEM((HALO_A, 8, w), F32)],
        compiler_params=_params(("arbitrary",)),
    )(proj, proj, proj, dz, conv, proj, proj, proj, dz, conv, cw, lg, lb)


def _conv_b(u_ext, cw_ref, t):
    acc = cw_ref[0:1, :] * u_ext[pl.ds(HALO_B - 2, t), :]
    for k in range(1, CONV_B):
        acc = acc + cw_ref[k:k + 1, :] * u_ext[pl.ds(HALO_B - 2 + k, t), :]
    return acc


def _branch_b_fwd(proj, cw, name):
    s = proj.shape[0]
    t = T_B
    w = cw.shape[1]
    r = t // HALO_B

    def body(h_ref, b_ref, c_ref, gt_ref, ph_ref, pc_ref, cw_ref, z_ref, u_ext):
        i = pl.program_id(0)
        u_ext[0:HALO_B, :] = jnp.where(i > 0, pc_ref[...] * ph_ref[...], 0.0)
        u_ext[HALO_B:HALO_B + t, :] = c_ref[...] * h_ref[...]
        cv = _conv_b(u_ext, cw_ref, t)
        z_ref[...] = (b_ref[...] * cv * _silu(gt_ref[...])).astype(BF16)

    def cur(col):
        return pl.BlockSpec((t, w), lambda i, col=col: (i, col))

    def prev(col):
        return pl.BlockSpec((HALO_B, w), lambda i, col=col: (jnp.maximum(i * r - 1, 0), col))

    return pl.pallas_call(
        body, name=name, grid=(s // t,),
        in_specs=[cur(3), cur(4), cur(5), cur(6), prev(3), prev(5), pl.BlockSpec((HALO_B, w), lambda i: (0, 0))],
        out_specs=pl.BlockSpec((t, w), lambda i: (i, 0)),
        out_shape=jax.ShapeDtypeStruct((s, w), BF16),
        scratch_shapes=[pltpu.VMEM((HALO_B + t, w), F32)],
        compiler_params=_params(("parallel",)),
    )(proj, proj, proj, proj, proj, proj, cw)


def _branch_b_bwd(proj, dz, cw, name):
    s = proj.shape[0]
    t = T_B
    w = cw.shape[1]
    r = t // HALO_B
    n = s // t
    nblk = s // HALO_B

    def body(h_ref, b_ref, c_ref, gt_ref, dz_ref, ph_ref, pc_ref, nb_ref, ngt_ref, ndz_ref, cw_ref,
             db_ref, dw_ref, u_ext, dcv_ext, dw8):
        i = pl.program_id(0)

        @pl.when(i == 0)
        def _():
            dw8[...] = jnp.zeros_like(dw8)

        u_ext[0:HALO_B, :] = jnp.where(i > 0, pc_ref[...] * ph_ref[...], 0.0)
        u_ext[HALO_B:HALO_B + t, :] = c_ref[...] * h_ref[...]
        cv = _conv_b(u_ext, cw_ref, t)
        gt = gt_ref[...]
        dhb = dz_ref[...] * _silu(gt)
        db_ref[:, 3 * w:4 * w] = (dz_ref[...] * b_ref[...] * cv * _dsilu(gt)).astype(BF16)
        db_ref[:, w:2 * w] = (dhb * cv).astype(BF16)
        dcv = dhb * b_ref[...]
        dcv_ext[0:t, :] = dcv
        ndcv = ndz_ref[...] * _silu(ngt_ref[...]) * nb_ref[...]
        dcv_ext[t:t + HALO_B, :] = jnp.where(i < n - 1, ndcv, 0.0)
        du = cw_ref[0:1, :] * dcv_ext[pl.ds(2, t), :]
        for k in range(1, CONV_B):
            du = du + cw_ref[k:k + 1, :] * dcv_ext[pl.ds(2 - k, t), :]
        db_ref[:, 2 * w:3 * w] = (du * h_ref[...]).astype(BF16)
        db_ref[:, 0:w] = (du * c_ref[...]).astype(BF16)
        for k in range(CONV_B):
            prod = dcv * u_ext[pl.ds(HALO_B - 2 + k, t), :]
            dw8[k] += jnp.sum(prod.reshape(t // 8, 8, w), axis=0)

        @pl.when(i == n - 1)
        def _():
            dw_ref[...] = jnp.sum(dw8[...], axis=1)

    def cur(col):
        return pl.BlockSpec((t, w), lambda i, col=col: (i, col))

    def prev(col):
        return pl.BlockSpec((HALO_B, w), lambda i, col=col: (jnp.maximum(i * r - 1, 0), col))

    def nxt(col):
        return pl.BlockSpec((HALO_B, w), lambda i, col=col: (jnp.minimum((i + 1) * r, nblk - 1), col))

    return pl.pallas_call(
        body, name=name, grid=(n,),
        in_specs=[cur(3), cur(4), cur(5), cur(6), pl.BlockSpec((t, w), lambda i: (i, 0)),
                  prev(3), prev(5), nxt(4), nxt(6),
                  pl.BlockSpec((HALO_B, w), lambda i: (jnp.minimum((i + 1) * r, nblk - 1), 0)),
                  pl.BlockSpec((HALO_B, w), lambda i: (0, 0))],
        out_specs=[pl.BlockSpec((t, 4 * w), lambda i: (i, 0)), pl.BlockSpec((HALO_B, w), lambda i: (0, 0))],
        out_shape=[jax.ShapeDtypeStruct((s, 4 * w), BF16), jax.ShapeDtypeStruct((HALO_B, w), F32)],
        scratch_shapes=[pltpu.VMEM((HALO_B + t, w), F32), pltpu.VMEM((t + HALO_B, w), F32),
                        pltpu.VMEM((HALO_B, 8, w), F32)],
        compiler_params=_params(("arbitrary",)),
    )(proj, proj, proj, proj, dz, proj, proj, proj, proj, dz, cw)


def _forget_prep(proj, bf, fblk, name):
    s = proj.shape[0]
    t = T_CUM

    def body(f_ref, bf_ref, cum_ref, cumt_ref, sgt_ref, carry):
        i = pl.program_id(0)

        @pl.when(i == 0)
        def _():
            carry[...] = jnp.zeros_like(carry)

        z = f_ref[...] + bf_ref[...]
        logf = jnp.minimum(z, 0.0) - jnp.log1p(jnp.exp(-jnp.abs(z)))
        tri = (_iota((t, t), 0) >= _iota((t, t), 1)).astype(F32)
        cum = _dot(tri, logf, NN, precision=lax.Precision.HIGHEST) + carry[0:1, :]
        carry[0:1, :] = cum[t - 1:t, :]
        cum_ref[...] = cum
        cumt_ref[...] = cum.T[0:8, :]
        sgt_ref[...] = _sigmoid(-z).T[0:8, :]

    return pl.pallas_call(
        body, name=name, grid=(s // t,),
        in_specs=[pl.BlockSpec((t, LANES), lambda i: (i, fblk)), pl.BlockSpec((1, LANES), lambda i: (0, 0))],
        out_specs=[pl.BlockSpec((t, LANES), lambda i: (i, 0)), pl.BlockSpec((8, t), lambda i: (0, i)),
                   pl.BlockSpec((8, t), lambda i: (0, i))],
        out_shape=[jax.ShapeDtypeStruct((s, LANES), F32), jax.ShapeDtypeStruct((8, s), F32),
                   jax.ShapeDtypeStruct((8, s), F32)],
        scratch_shapes=[pltpu.VMEM((8, LANES), F32)],
        compiler_params=_params(("arbitrary",)),
    )(proj, bf)


def _lane_pick(x, lane):
    return jnp.sum(jnp.where(_iota(x.shape, 1) == lane, x, 0.0), axis=1, keepdims=True)


def _sublane_pick(x, row):
    return jnp.sum(jnp.where(_iota(x.shape, 0) == row, x, 0.0), axis=0, keepdims=True)


def _head_mask(hh):
    lane = _iota((1, LANES), 1)
    return (lane >= HEAD_DIM * hh) & (lane < HEAD_DIM * (hh + 1))


def _causal_pairs(nq, ratio, kv_major):
    if kv_major:
        pairs = [(q, k) for k in range(nq // ratio) for q in range(k * ratio, nq)]
    else:
        pairs = [(q, k) for q in range(nq) for k in range(q // ratio + 1)]
    qs = np.asarray([p[0] for p in pairs], np.int32)
    ks = np.asarray([p[1] for p in pairs], np.int32)
    return qs, ks, np.where(ks == qs // ratio, qs % ratio + 1, 0).astype(np.int32)


def _fox_scores(qm, kb, cum_ref, cumt, h, row0, diag, tq, tk):
    cq0 = _lane_pick(cum_ref[0:1, :], h)
    sc = _dot(qm, kb, NT) + (cq0 - _sublane_pick(cumt, h))
    if not diag:
        return sc
    causal = (_iota((tq, tk), 0) + row0) >= _iota((tq, tk), 1)
    return jnp.where(causal, sc, NEG_BIG)


def _fox_fwd(proj, cum, cumt, qblk, kblk, vblk, name):
    s = proj.shape[0]
    tq, tk = T_ATT_Q, T_ATT_K_FWD
    n_pair = 4
    qi_np, ki_np, diag_np = _causal_pairs(s // tq, tk // tq, kv_major=False)
    scale = HEAD_DIM ** -0.5

    def body(qi_ref, ki_ref, diag_ref, q_ref, k_ref, v_ref, cum_ref, cumt_ref, o_ref, lse_ref, m_s, acc_s):
        hp = pl.program_id(0)
        step = pl.program_id(1)
        qi, ki, diag = qi_ref[step], ki_ref[step], diag_ref[step]

        @pl.when(ki == 0)
        def _():
            m_s[...] = jnp.full_like(m_s, NEG_BIG)
            acc_s[...] = jnp.zeros_like(acc_s)

        def update(blocks):
            width = tk if blocks is None else blocks * tq
            q = q_ref[...] * scale
            kb = k_ref[0:width, :].astype(BF16)
            v = v_ref[0:width, :]
            cumt = cumt_ref[:, 0:width]
            m_old = [m_s[0], m_s[1]]
            acc_old = [acc_s[0], acc_s[1]]
            scores, values = [], []
            for hh in range(2):
                hm = _head_mask(hh)
                qm = jnp.where(hm, q, 0.0).astype(BF16)
                values.append(jnp.where(hm, v, 1.0).astype(BF16))
                scores.append(_fox_scores(qm, kb, cum_ref, cumt, 2 * hp + hh, qi * tq - ki * tk, blocks is not None,
                                          tq, width))
            m_new = [jnp.maximum(m_old[hh], jnp.max(scores[hh], axis=1, keepdims=True)) for hh in range(2)]
            probs = [jnp.exp(scores[hh] - m_new[hh]).astype(BF16) for hh in range(2)]
            acc_new = [jnp.exp(m_old[hh] - m_new[hh]) * acc_old[hh] + _dot(probs[hh], values[hh], NN)
                       for hh in range(2)]
            for hh in range(2):
                acc_s[hh] = acc_new[hh]
                m_s[hh] = m_new[hh]

        @pl.when(diag == 0)
        def _():
            update(None)

        for blocks in range(1, tk // tq + 1):
            pl.when(diag == blocks)(functools.partial(update, blocks))

        @pl.when(diag >= 1)
        def _():
            lane = _iota((tq, LANES), 1)
            a0, a1 = acc_s[0], acc_s[1]
            o_ref[...] = jnp.where(lane < HEAD_DIM, a0 / pltpu.roll(a0, HEAD_DIM, axis=1),
                                   a1 / pltpu.roll(a1, HEAD_DIM, axis=1))
            lse0 = m_s[0] + jnp.log(a0[:, HEAD_DIM:HEAD_DIM + 1])
            lse1 = m_s[1] + jnp.log(a1[:, 0:1])
            lse_ref[0] = jnp.where(lane == 0, lse0, jnp.where(lane == 1, lse1, 0.0))

    grid_spec = pltpu.PrefetchScalarGridSpec(
        num_scalar_prefetch=3, grid=(n_pair, len(qi_np)),
        in_specs=[pl.BlockSpec((tq, LANES), lambda hp, st, qi, ki, dg: (qi[st], qblk + hp)),
                  pl.BlockSpec((tk, LANES), lambda hp, st, qi, ki, dg: (ki[st], kblk + hp)),
                  pl.BlockSpec((tk, LANES), lambda hp, st, qi, ki, dg: (ki[st], vblk + hp)),
                  pl.BlockSpec((tq, LANES), lambda hp, st, qi, ki, dg: (qi[st], 0)),
                  pl.BlockSpec((8, tk), lambda hp, st, qi, ki, dg: (0, ki[st]))],
        out_specs=[pl.BlockSpec((tq, LANES), lambda hp, st, qi, ki, dg: (qi[st], hp)),
                   pl.BlockSpec((1, tq, LANES), lambda hp, st, qi, ki, dg: (hp, qi[st], 0))],
        scratch_shapes=[pltpu.VMEM((2, tq, 1), F32), pltpu.VMEM((2, tq, LANES), F32)])
    return pl.pallas_call(
        body, name=name, grid_spec=grid_spec,
        out_shape=[jax.ShapeDtypeStruct((s, n_pair * LANES), F32), jax.ShapeDtypeStruct((n_pair, s, LANES), F32)],
        compiler_params=_params(("parallel", "arbitrary")),
    )(jnp.asarray(qi_np), jnp.asarray(ki_np), jnp.asarray(diag_np), proj, proj, proj, cum, cumt)


def _fox_bwd_prep(proj, dz, o, gblk, name):
    s, w = o.shape
    t = T_ELEM
    n_head = w // HEAD_DIM

    def body(gt_ref, dz_ref, o_ref, do_ref, dg_ref, dl_ref):
        gt = gt_ref[...]
        do = dz_ref[...] * _silu(gt)
        do_ref[...] = do
        dg_ref[...] = (dz_ref[...] * o_ref[...] * _dsilu(gt)).astype(BF16)
        sel = (_iota((w, LANES), 0) // HEAD_DIM == _iota((w, LANES), 1)).astype(F32)
        dl_ref[...] = _dot(do * o_ref[...], sel, NN, precision=lax.Precision.HIGHEST)

    assert n_head <= LANES
    row = pl.BlockSpec((t, w), lambda i: (i, 0))
    return pl.pallas_call(
        body, name=name, grid=(s // t,),
        in_specs=[pl.BlockSpec((t, w), lambda i: (i, gblk)), row, row],
        out_specs=[row, row, pl.BlockSpec((t, LANES), lambda i: (i, 0))],
        out_shape=[jax.ShapeDtypeStruct((s, w), F32), jax.ShapeDtypeStruct((s, w), BF16),
                   jax.ShapeDtypeStruct((s, LANES), F32)],
        compiler_params=_params(("parallel",)),
    )(proj, dz, o)


def _fox_bwd(proj, do, lse, delta, cum, cumt, qblk, kblk, vblk, name):
    s = proj.shape[0]
    tq, tk = T_ATT_Q, T_ATT_K
    nq = s // tq
    n_pair = 4
    qi_np, ki_np, diag_np = _causal_pairs(nq, tk // tq, kv_major=True)
    n_step = len(qi_np)
    scale = HEAD_DIM ** -0.5

    def body(qi_ref, ki_ref, diag_ref, q_ref, k_ref, v_ref, do_ref, lse_ref, dl_ref, cum_ref, cumt_ref,
             dq_ref, dk_ref, dv_ref, dct_ref, dcq_ref, dq_s, dk_s, dv_s, dc_s, dcq_s):
        hp = pl.program_id(0)
        step = pl.program_id(1)
        qi, ki, diag = qi_ref[step], ki_ref[step], diag_ref[step]

        @pl.when(step == 0)
        def _():
            dq_s[...] = jnp.zeros_like(dq_s)
            dcq_s[...] = jnp.zeros_like(dcq_s)

        @pl.when(qi == ki * (tk // tq))
        def _():
            dk_s[...] = jnp.zeros_like(dk_s)
            dv_s[...] = jnp.zeros_like(dv_s)
            dc_s[...] = jnp.zeros_like(dc_s)

        def update(blocks):
            width = tk if blocks is None else blocks * tq
            q = q_ref[...] * scale
            do = do_ref[...]
            kb = k_ref[0:width, :].astype(BF16)
            vb = v_ref[0:width, :].astype(BF16)
            cumt = cumt_ref[:, 0:width]
            sub = _iota((8, width), 0)
            dq_new = jnp.zeros((tq, LANES), F32)
            dcq_new = jnp.zeros((tq, LANES), F32)
            lane = _iota((tq, LANES), 1)
            for hh in range(2):
                h = 2 * hp + hh
                hm = _head_mask(hh)
                qm = jnp.where(hm, q, 0.0).astype(BF16)
                dom = jnp.where(hm, do, 0.0).astype(BF16)
                sc = _fox_scores(qm, kb, cum_ref, cumt, h, qi * tq - ki * tk, blocks is not None, tq, width)
                p = jnp.exp(sc - _lane_pick(lse_ref[0], hh))
                dv_s[0:width, :] += _dot(p.astype(BF16), dom, TN)
                dp = _dot(dom, vb, NT)
                ds = p * (dp - _lane_pick(dl_ref[...], h))
                dc_s[:, 0:width] += jnp.where(sub == h, -jnp.sum(ds, axis=0, keepdims=True), 0.0)
                dcq_new = dcq_new + jnp.where(lane == h, jnp.sum(ds, axis=1, keepdims=True), 0.0)
                dsb = ds.astype(BF16)
                dk_s[0:width, :] += _dot(dsb, qm, TN)
                dq_new = dq_new + jnp.where(hm, _dot(dsb, kb, NN), 0.0)
            row0 = pl.multiple_of(qi * tq, tq)
            dq_s[pl.ds(row0, tq), :] += dq_new * scale
            dcq_s[pl.ds(row0, tq), :] += dcq_new

        @pl.when(diag == 0)
        def _():
            update(None)

        for blocks in range(1, tk // tq + 1):
            pl.when(diag == blocks)(functools.partial(update, blocks))

        @pl.when(qi == nq - 1)
        def _():
            dk_ref[...] = dk_s[...].astype(BF16)
            dv_ref[...] = dv_s[...].astype(BF16)
            dct_ref[0] = dc_s[...]

        @pl.when(step == n_step - 1)
        def _():
            dq_ref[...] = dq_s[...].astype(BF16)
            dcq_ref[0] = dcq_s[...]

    grid_spec = pltpu.PrefetchScalarGridSpec(
        num_scalar_prefetch=3, grid=(n_pair, n_step),
        in_specs=[pl.BlockSpec((tq, LANES), lambda hp, st, qi, ki, dg: (qi[st], qblk + hp)),
                  pl.BlockSpec((tk, LANES), lambda hp, st, qi, ki, dg: (ki[st], kblk + hp)),
                  pl.BlockSpec((tk, LANES), lambda hp, st, qi, ki, dg: (ki[st], vblk + hp)),
                  pl.BlockSpec((tq, LANES), lambda hp, st, qi, ki, dg: (qi[st], hp)),
                  pl.BlockSpec((1, tq, LANES), lambda hp, st, qi, ki, dg: (hp, qi[st], 0)),
                  pl.BlockSpec((tq, LANES), lambda hp, st, qi, ki, dg: (qi[st], 0)),
                  pl.BlockSpec((tq, LANES), lambda hp, st, qi, ki, dg: (qi[st], 0)),
                  pl.BlockSpec((8, tk), lambda hp, st, qi, ki, dg: (0, ki[st]))],
        out_specs=[pl.BlockSpec((s, LANES), lambda hp, st, qi, ki, dg: (0, hp)),
                   pl.BlockSpec((tk, LANES), lambda hp, st, qi, ki, dg: (ki[st], hp)),
                   pl.BlockSpec((tk, LANES), lambda hp, st, qi, ki, dg: (ki[st], hp)),
                   pl.BlockSpec((1, 8, tk), lambda hp, st, qi, ki, dg: (hp, 0, ki[st])),
                   pl.BlockSpec((1, s, LANES), lambda hp, st, qi, ki, dg: (hp, 0, 0))],
        scratch_shapes=[pltpu.VMEM((s, LANES), F32), pltpu.VMEM((tk, LANES), F32), pltpu.VMEM((tk, LANES), F32),
                        pltpu.VMEM((8, tk), F32), pltpu.VMEM((s, LANES), F32)])
    w = n_pair * LANES
    return pl.pallas_call(
        body, name=name, grid_spec=grid_spec,
        out_shape=[jax.ShapeDtypeStruct((s, w), BF16), jax.ShapeDtypeStruct((s, w), BF16),
                   jax.ShapeDtypeStruct((s, w), BF16), jax.ShapeDtypeStruct((n_pair, 8, s), F32),
                   jax.ShapeDtypeStruct((n_pair, s, LANES), F32)],
        compiler_params=_params(("parallel", "arbitrary")),
    )(jnp.asarray(qi_np), jnp.asarray(ki_np), jnp.asarray(diag_np), proj, proj, proj, do, lse, delta, cum, cumt)


def _forget_bwd(dcumt4, dcumq4, sgt, name):
    s = sgt.shape[1]
    t = T_CUM
    n = s // t

    def body(d_ref, dq_ref, sg_ref, df_ref, dbf_ref, carry):
        i = pl.program_id(0)

        @pl.when(i == 0)
        def _():
            carry[...] = jnp.zeros_like(carry)
            dbf_ref[...] = jnp.zeros_like(dbf_ref)

        dq = dq_ref[0] + dq_ref[1] + dq_ref[2] + dq_ref[3]
        d = d_ref[0] + d_ref[1] + d_ref[2] + d_ref[3] + dq.T[0:8, :]
        upper = (_iota((t, t), 0) >= _iota((t, t), 1)).astype(F32)
        dlog = _dot(d, upper, NN, precision=lax.Precision.HIGHEST) + carry[:, 0:1]
        carry[...] += jnp.sum(d, axis=1, keepdims=True)
        dzt = dlog * sg_ref[...]
        dbf_ref[...] += jnp.sum(dzt, axis=1, keepdims=True)
        padded = jnp.concatenate([dzt, jnp.zeros((LANES - 8, t), F32)], axis=0)
        df_ref[...] = padded.T.astype(BF16)

    return pl.pallas_call(
        body, name=name, grid=(n,),
        in_specs=[pl.BlockSpec((4, 8, t), lambda i: (0, 0, n - 1 - i)),
                  pl.BlockSpec((4, t, LANES), lambda i: (0, n - 1 - i, 0)),
                  pl.BlockSpec((8, t), lambda i: (0, n - 1 - i))],
        out_specs=[pl.BlockSpec((t, LANES), lambda i: (n - 1 - i, 0)), pl.BlockSpec((8, LANES), lambda i: (0, 0))],
        out_shape=[jax.ShapeDtypeStruct((s, LANES), BF16), jax.ShapeDtypeStruct((8, LANES), F32)],
        scratch_shapes=[pltpu.VMEM((8, LANES), F32)],
        compiler_params=_params(("arbitrary",)),
    )(dcumt4, dcumq4, sgt)


def _mem_softmax(qm, kp):
    sc = _dot(qm, kp, NT) * (HEAD_DIM ** -0.5)
    e = jnp.exp(sc - jnp.max(sc, axis=1, keepdims=True))
    return e / jnp.sum(e, axis=1, keepdims=True)


def _branch_m_fwd(proj, mkv, mblk, name):
    s = proj.shape[0]
    t = T_M
    mw = mkv.shape[1] // 2
    ml = mkv.shape[0]

    def body(m_ref, kv_ref, z_ref):
        outs = []
        for pr in range(mw // LANES):
            qp = m_ref[:, pr * LANES:(pr + 1) * LANES]
            kp = kv_ref[:, pr * LANES:(pr + 1) * LANES].astype(BF16)
            vp = kv_ref[:, mw + pr * LANES:mw + (pr + 1) * LANES].astype(BF16)
            oh = []
            for hh in range(2):
                qm = jnp.where(_head_mask(hh), qp, 0.0).astype(BF16)
                oh.append(_dot(_mem_softmax(qm, kp).astype(BF16), vp, NN))
            outs.append(jnp.where(_iota((t, LANES), 1) < HEAD_DIM, oh[0], oh[1]))
        o = jnp.concatenate(outs, axis=1)
        z_ref[...] = (o * _silu(m_ref[:, mw:2 * mw])).astype(BF16)

    return pl.pallas_call(
        body, name=name, grid=(s // t,),
        in_specs=[pl.BlockSpec((t, 2 * mw), lambda i: (i, mblk)), pl.BlockSpec((ml, 2 * mw), lambda i: (0, 0))],
        out_specs=pl.BlockSpec((t, mw), lambda i: (i, 0)),
        out_shape=jax.ShapeDtypeStruct((s, mw), BF16),
        compiler_params=_params(("parallel",)),
    )(proj, mkv)


def _branch_m_bwd(proj, mkv, dz, mblk, name):
    s = proj.shape[0]
    t = T_M
    mw = mkv.shape[1] // 2
    ml = mkv.shape[0]
    scale = HEAD_DIM ** -0.5

    def body(m_ref, kv_ref, dz_ref, dm_ref, dkv_ref):
        i = pl.program_id(0)

        @pl.when(i == 0)
        def _():
            dkv_ref[...] = jnp.zeros_like(dkv_ref)

        gt = m_ref[:, mw:2 * mw]
        dz = dz_ref[...]
        do = dz * _silu(gt)
        outs = []
        for pr in range(mw // LANES):
            cols = slice(pr * LANES, (pr + 1) * LANES)
            vcols = slice(mw + pr * LANES, mw + (pr + 1) * LANES)
            qp = m_ref[:, cols]
            kp = kv_ref[:, cols].astype(BF16)
            vp = kv_ref[:, vcols].astype(BF16)
            dop = do[:, cols]
            oh = []
            dq = jnp.zeros((t, LANES), F32)
            dk = jnp.zeros((ml, LANES), F32)
            dv = jnp.zeros((ml, LANES), F32)
            for hh in range(2):
                hm = _head_mask(hh)
                qm = jnp.where(hm, qp, 0.0).astype(BF16)
                dom = jnp.where(hm, dop, 0.0).astype(BF16)
                p = _mem_softmax(qm, kp)
                pb = p.astype(BF16)
                oh.append(_dot(pb, vp, NN))
                dv = dv + _dot(pb, dom, TN)
                dp = _dot(dom, vp, NT)
                ds = p * (dp - jnp.sum(dp * p, axis=1, keepdims=True))
                dsb = (ds * scale).astype(BF16)
                dq = dq + jnp.where(hm, _dot(dsb, kp, NN), 0.0)
                dk = dk + _dot(dsb, qm, TN)
            outs.append(jnp.where(_iota((t, LANES), 1) < HEAD_DIM, oh[0], oh[1]))
            dm_ref[:, cols] = dq.astype(BF16)
            dkv_ref[:, cols] += dk
            dkv_ref[:, vcols] += dv
        o = jnp.concatenate(outs, axis=1)
        dm_ref[:, mw:2 * mw] = (dz * o * _dsilu(gt)).astype(BF16)

    return pl.pallas_call(
        body, name=name, grid=(s // t,),
        in_specs=[pl.BlockSpec((t, 2 * mw), lambda i: (i, mblk)), pl.BlockSpec((ml, 2 * mw), lambda i: (0, 0)),
                  pl.BlockSpec((t, mw), lambda i: (i, 0))],
        out_specs=[pl.BlockSpec((t, 2 * mw), lambda i: (i, 0)), pl.BlockSpec((ml, 2 * mw), lambda i: (0, 0))],
        out_shape=[jax.ShapeDtypeStruct((s, 2 * mw), BF16), jax.ShapeDtypeStruct((ml, 2 * mw), F32)],
        compiler_params=_params(("arbitrary",)),
    )(proj, mkv, dz)


def _merge_parts(z_refs, g_refs, pcat, bounds):
    ys, sgs = [], []
    merged = None
    for zr, gr, (lo, hi) in zip(z_refs, g_refs, bounds):
        y = _dot(zr[...], pcat[lo:hi, :], NN)
        sg = _sigmoid(gr[...])
        ys.append(y)
        sgs.append(sg)
        merged = sg * y if merged is None else merged + sg * y
    return ys, sgs, merged


def _branch_bounds(zs):
    bounds, lo = [], 0
    for z in zs:
        bounds.append((lo, lo + z.shape[1]))
        lo += z.shape[1]
    return bounds


def _matmul_copies(x, name):
    s, d = x.shape
    t = T_MERGE

    def body(x_ref, b_ref, bt_ref):
        b_ref[...] = x_ref[...].astype(BF16)
        bt_ref[...] = x_ref[...].T.astype(BF16)

    return pl.pallas_call(
        body, name=name, grid=(s // t,), in_specs=[pl.BlockSpec((t, d), lambda i: (i, 0))],
        out_specs=[pl.BlockSpec((t, d), lambda i: (i, 0)), pl.BlockSpec((d, t), lambda i: (0, i))],
        out_shape=[jax.ShapeDtypeStruct((s, d), BF16), jax.ShapeDtypeStruct((d, s), BF16)],
        compiler_params=_params(("parallel",)))(x)


def _merge_fwd(zs, proj, gblk, pcat, wout, x, lng, lnb, alpha, name):
    s, d = x.shape
    t = T_MERGE
    bounds = _branch_bounds(zs)

    def body(*refs):
        z_refs, g_refs = refs[0:4], refs[4:8]
        pcat_hbm, wout_hbm, x_ref, lng_ref, lnb_ref, y_ref, yb_ref, ybt_ref, pcat_v, wout_v = refs[8:]

        @pl.when(pl.program_id(0) == 0)
        def _():
            pltpu.sync_copy(pcat_hbm, pcat_v)
            pltpu.sync_copy(wout_hbm, wout_v)

        _, _, merged = _merge_parts(z_refs, g_refs, pcat_v, bounds)
        h = alpha * x_ref[...] + _dot(merged.astype(BF16), wout_v[...], NN)
        xh, _ = _ln_stats(h)
        y = xh * lng_ref[...] + lnb_ref[...]
        y_ref[...] = y
        yb_ref[...] = y.astype(BF16)
        ybt_ref[...] = y.T.astype(BF16)

    row = pl.BlockSpec((t, d), lambda i: (i, 0))
    vec = pl.BlockSpec((1, d), lambda i: (0, 0))
    in_specs = ([pl.BlockSpec((t, z.shape[1]), lambda i: (i, 0)) for z in zs]
                + [pl.BlockSpec((t, d), lambda i, k=k: (i, gblk + k)) for k in range(4)]
                + [ANY, ANY, row, vec, vec])
    return pl.pallas_call(
        body, name=name, grid=(s // t,), in_specs=in_specs,
        out_specs=[row, row, pl.BlockSpec((d, t), lambda i: (0, i))],
        out_shape=[jax.ShapeDtypeStruct((s, d), F32), jax.ShapeDtypeStruct((s, d), BF16),
                   jax.ShapeDtypeStruct((d, s), BF16)],
        scratch_shapes=[pltpu.VMEM(pcat.shape, BF16), pltpu.VMEM(wout.shape, BF16)],
        compiler_params=_params(("arbitrary",)),
    )(*zs, proj, proj, proj, proj, pcat, wout, x, lng, lnb)


def _merge_bwd(zs, proj, gblk, pcat, wout, x, lng, lnb, dy, alpha, name):
    s, d = x.shape
    t = T_MERGE
    n = s // t
    bounds = _branch_bounds(zs)

    def body(*refs):
        z_refs, g_refs = refs[0:4], refs[4:8]
        pcat_hbm, wout_hbm, x_ref, lng_ref, lnb_ref, dy_ref = refs[8:14]
        dx_ref, dg_ref = refs[14:16]
        dz_refs = refs[16:20]
        dpcat_hbm, dwout_hbm, dlng_ref, dlnb_ref = refs[20:24]
        pcat_v, wout_v, dpcat_v, dwout_v = refs[24:]
        i = pl.program_id(0)

        @pl.when(i == 0)
        def _():
            pltpu.sync_copy(pcat_hbm, pcat_v)
            pltpu.sync_copy(wout_hbm, wout_v)
            dpcat_v[...] = jnp.zeros_like(dpcat_v)
            dwout_v[...] = jnp.zeros_like(dwout_v)
            dlng_ref[...] = jnp.zeros_like(dlng_ref)
            dlnb_ref[...] = jnp.zeros_like(dlnb_ref)

        ys, sgs, merged = _merge_parts(z_refs, g_refs, pcat_v, bounds)
        mb = merged.astype(BF16)
        h = alpha * x_ref[...] + _dot(mb, wout_v[...], NN)
        xh, rstd = _ln_stats(h)
        dyv = dy_ref[...]
        dxh = dyv * lng_ref[...]
        dh = rstd * (dxh - jnp.mean(dxh, axis=-1, keepdims=True) - xh * jnp.mean(dxh * xh, axis=-1, keepdims=True))
        dx_ref[...] = alpha * dh
        dhb = dh.astype(BF16)
        dmerged = _dot(dhb, wout_v[...], NT)
        dpcat_new = []
        for k, (zr, (lo, hi)) in enumerate(zip(z_refs, bounds)):
            sg = sgs[k]
            dg_ref[:, k * d:(k + 1) * d] = (dmerged * ys[k] * sg * (1.0 - sg)).astype(BF16)
            dyk = (dmerged * sg).astype(BF16)
            dpcat_new.append(_dot(zr[...], dyk, TN))
            dz_refs[k][...] = _dot(dyk, pcat_v[lo:hi, :], NT)
        for (lo, hi), upd in zip(bounds, dpcat_new):
            dpcat_v[lo:hi, :] += upd
        dwout_v[...] += _dot(mb, dhb, TN)
        dlng_ref[...] += jnp.sum(dyv * xh, axis=0, keepdims=True)
        dlnb_ref[...] += jnp.sum(dyv, axis=0, keepdims=True)

        @pl.when(i == n - 1)
        def _():
            pltpu.sync_copy(dpcat_v, dpcat_hbm)
            pltpu.sync_copy(dwout_v, dwout_hbm)

    row = pl.BlockSpec((t, d), lambda i: (i, 0))
    vec = pl.BlockSpec((1, d), lambda i: (0, 0))
    z_specs = [pl.BlockSpec((t, z.shape[1]), lambda i: (i, 0)) for z in zs]
    in_specs = (z_specs + [pl.BlockSpec((t, d), lambda i, k=k: (i, gblk + k)) for k in range(4)]
                + [ANY, ANY, row, vec, vec, row])
    out_specs = [row, pl.BlockSpec((t, 4 * d), lambda i: (i, 0))] + z_specs + [ANY, ANY, vec, vec]
    vo = jax.ShapeDtypeStruct((1, d), F32)
    out_shape = ([jax.ShapeDtypeStruct((s, d), F32), jax.ShapeDtypeStruct((s, 4 * d), BF16)]
                 + [jax.ShapeDtypeStruct(z.shape, F32) for z in zs]
                 + [jax.ShapeDtypeStruct(pcat.shape, F32), jax.ShapeDtypeStruct(wout.shape, F32), vo, vo])
    return pl.pallas_call(
        body, name=name, grid=(n,), in_specs=in_specs, out_specs=out_specs, out_shape=out_shape,
        scratch_shapes=[pltpu.VMEM(pcat.shape, BF16), pltpu.VMEM(wout.shape, BF16),
                        pltpu.VMEM(pcat.shape, F32), pltpu.VMEM(wout.shape, F32)],
        compiler_params=_params(("arbitrary",)),
    )(*zs, proj, proj, proj, proj, pcat, wout, x, lng, lnb, dy)


def _loss_head(y, target, name):
    s, d = y.shape
    t = T_ELEM

    def body(y_ref, t_ref, dy_ref, loss_ref):
        @pl.when(pl.program_id(0) == 0)
        def _():
            loss_ref[...] = jnp.zeros_like(loss_ref)

        e = y_ref[...] - t_ref[...]
        dy_ref[...] = e * (1.0 / d)
        loss_ref[...] += 0.5 * jnp.sum(jnp.mean(e * e, axis=-1, keepdims=True), axis=0, keepdims=True)

    row = pl.BlockSpec((t, d), lambda i: (i, 0))
    return pl.pallas_call(
        body, name=name, grid=(s // t,), in_specs=[row, row],
        out_specs=[row, pl.BlockSpec((8, LANES), lambda i: (0, 0))],
        out_shape=[jax.ShapeDtypeStruct((s, d), F32), jax.ShapeDtypeStruct((8, LANES), F32)],
        compiler_params=_params(("arbitrary",)),
    )(y, target)


def _adamw(w, g, m, v, name):
    n_l, r, c = w.shape
    t = _pick(r, (256, 128, 64, 32, 16, 8))

    def body(w_ref, g_ref, m_ref, v_ref, d_ref, nm_ref, nv_ref):
        gv = g_ref[...]
        nm = ADAM_B1 * m_ref[...] + (1.0 - ADAM_B1) * gv
        nv = ADAM_B2 * v_ref[...] + (1.0 - ADAM_B2) * (gv * gv)
        m_hat = nm / (1.0 - ADAM_B1 ** ADAM_STEP)
        v_hat = nv / (1.0 - ADAM_B2 ** ADAM_STEP)
        d_ref[...] = -ADAM_LR * (m_hat / (jnp.sqrt(v_hat) + ADAM_EPS) + ADAM_WD * w_ref[...])
        nm_ref[...] = nm
        nv_ref[...] = nv

    blk = pl.BlockSpec((1, t, c), lambda l, i: (l, i, 0))
    o = jax.ShapeDtypeStruct((n_l, r, c), F32)
    return pl.pallas_call(body, name=name, grid=(n_l, r // t), in_specs=[blk] * 4, out_specs=[blk] * 3,
                          out_shape=[o, o, o], compiler_params=_params(("parallel", "parallel")))(w, g, m, v)


def _sum_leading(x, out_dtype, name):
    k, r, c = x.shape
    t = _pick(r, (256, 128, 64, 32, 16, 8))

    def body(x_ref, o_ref):
        acc = x_ref[0].astype(F32)
        for j in range(1, k):
            acc = acc + x_ref[j].astype(F32)
        o_ref[...] = acc.astype(out_dtype)

    return pl.pallas_call(body, name=name, grid=(r // t,),
                          in_specs=[pl.BlockSpec((k, t, c), lambda i: (0, i, 0))],
                          out_specs=pl.BlockSpec((t, c), lambda i: (i, 0)),
                          out_shape=jax.ShapeDtypeStruct((r, c), out_dtype), compiler_params=_params(("parallel",)))(x)


def _position():
    return lax.axis_index("x"), lax.axis_index("y"), lax.axis_index("c")


def _chip_peers(x, y):
    return [(1 - x, y), (x, 1 - y), (1 - x, 1 - y)]


def _comm_call(body, n_in, out_shape, n_remote, n_local, name):
    return pl.pallas_call(
        body, name=name, in_specs=[ANY] * n_in, out_specs=[ANY] * len(out_shape), out_shape=out_shape,
        scratch_shapes=[pltpu.SemaphoreType.DMA((n_remote,)), pltpu.SemaphoreType.DMA((n_remote,)),
                        pltpu.SemaphoreType.DMA((max(n_local, 1),))])


def _run_copies(local, remote, send, recv, loc):
    copies = [pltpu.make_async_copy(src, dst, loc.at[k]) for k, (src, dst) in enumerate(local)]
    copies += [pltpu.make_async_remote_copy(src_ref=src, dst_ref=dst, send_sem=send.at[k], recv_sem=recv.at[k],
                                            device_id=peer, device_id_type=MESH)
               for k, (src, dst, peer) in enumerate(remote)]
    for cp in copies:
        cp.start()
    for cp in copies:
        cp.wait()


COPY_BYTES = 1024 * 1024


def _n_copies(rows, row_bytes, align):
    n = 8
    while n > 1 and (rows % (n * align) or rows // n * row_bytes < COPY_BYTES):
        n //= 2
    return n


def _allgather_chips(arrs, name):
    n = len(arrs)
    per_layer = [a.size * a.dtype.itemsize // a.shape[0] >= COPY_BYTES for a in arrs]
    n_each = [a.shape[0] if pl_ else 1 for a, pl_ in zip(arrs, per_layer)]

    def body(*refs):
        ins, outs = refs[:n], refs[n:2 * n]
        send, recv, loc = refs[2 * n:]
        x, y, c = _position()
        me = 2 * x + y
        local, remote = [], []
        for a in range(n):
            if per_layer[a]:
                parts = [(ins[a].at[l], outs[a].at[me, l]) for l in range(arrs[a].shape[0])]
            else:
                parts = [(ins[a], outs[a].at[me])]
            local += parts
            for px, py in _chip_peers(x, y):
                remote += [(src, dst, (px, py, c)) for src, dst in parts]
        _run_copies(local, remote, send, recv, loc)

    out_shape = [jax.ShapeDtypeStruct((4,) + a.shape, a.dtype) for a in arrs]
    return _comm_call(body, n, out_shape, 3 * sum(n_each), sum(n_each), name)(*arrs)


def _allgather_all(v, name):
    def body(v_ref, o_ref, send, recv, loc):
        x, y, c = _position()
        me = 4 * x + 2 * y + c
        remote = []
        for k in range(1, 8):
            fx, fy, fc = (k >> 2) & 1, (k >> 1) & 1, k & 1
            remote.append((v_ref, o_ref.at[me], (x ^ fx, y ^ fy, c ^ fc)))
        _run_copies([(v_ref, o_ref.at[me])], remote, send, recv, loc)

    return _comm_call(body, 1, [jax.ShapeDtypeStruct((8,) + v.shape, v.dtype)], 7, 1, name)(v)[0]


def _pair_exchange_sum(g, name):
    _, _, n, r, cc = g.shape

    def body(c_ref, mine_ref, send_ref, o_ref, land, send_sem, recv_sem):
        x, y, c = _position()
        slot = (pl.program_id(0) * n + pl.program_id(1)) % 2
        push = pltpu.make_async_remote_copy(
            src_ref=send_ref.at[0, 0, 0], dst_ref=land.at[slot], send_sem=send_sem.at[slot],
            recv_sem=recv_sem.at[slot], device_id=(x, y, 1 - c), device_id_type=MESH)
        push.start()
        push.wait_recv()
        o_ref[0, 0] = (mine_ref[0, 0, 0] + land[slot]).astype(BF16)
        push.wait_send()

    grid_spec = pltpu.PrefetchScalarGridSpec(
        num_scalar_prefetch=1, grid=(4, n),
        in_specs=[pl.BlockSpec((1, 1, 1, r, cc), lambda j, k, c: (j, c[0], k, 0, 0)),
                  pl.BlockSpec((1, 1, 1, r, cc), lambda j, k, c: (j, 1 - c[0], k, 0, 0))],
        out_specs=pl.BlockSpec((1, 1, r, cc), lambda j, k, c: (j, k, 0, 0)),
        scratch_shapes=[pltpu.VMEM((2, r, cc), F32), pltpu.SemaphoreType.DMA((2,)), pltpu.SemaphoreType.DMA((2,))])
    return pl.pallas_call(
        body, name=name, grid_spec=grid_spec, out_shape=jax.ShapeDtypeStruct((4, n, r, cc), BF16),
        compiler_params=_params(("arbitrary", "arbitrary")))(_scalar(lax.axis_index("c")), g, g)


def _scalar(v):
    return v.astype(jnp.int32).reshape(1)


def _chip_exchange_sum(p, name):
    _, n, r, cc = p.shape

    def body(i0, i1, i2, i3, own_ref, s0_ref, s1_ref, s2_ref, mine_ref, theirs_ref, land, total, land_pair,
             send_sem, recv_sem, pair_send, pair_recv):
        x, y, c = _position()
        slot = pl.program_id(0) % 2
        pushes = [pltpu.make_async_remote_copy(
            src_ref=src.at[0, 0], dst_ref=land.at[slot, j], send_sem=send_sem.at[slot, j], recv_sem=recv_sem.at[slot, j],
            device_id=(px, py, c), device_id_type=MESH)
            for j, (src, (px, py)) in enumerate(zip((s0_ref, s1_ref, s2_ref), _chip_peers(x, y)))]
        for cp in pushes:
            cp.start()
        acc = own_ref[0, 0].astype(F32)
        for j, cp in enumerate(pushes):
            cp.wait_recv()
            acc = acc + land[slot, j].astype(F32)
        mine_ref[0] = acc
        total[slot] = acc
        share = pltpu.make_async_remote_copy(
            src_ref=total.at[slot], dst_ref=land_pair.at[slot], send_sem=pair_send.at[slot], recv_sem=pair_recv.at[slot],
            device_id=(x, y, 1 - c), device_id_type=MESH)
        share.start()
        share.wait_recv()
        theirs_ref[0] = land_pair[slot]
        for cp in pushes:
            cp.wait_send()
        share.wait_send()

    def slot_spec(which):
        return pl.BlockSpec((1, 1, r, cc), lambda k, *idx, which=which: (idx[which][0], k, 0, 0))

    out_spec = pl.BlockSpec((1, r, cc), lambda k, *idx: (k, 0, 0))
    grid_spec = pltpu.PrefetchScalarGridSpec(
        num_scalar_prefetch=4, grid=(n,), in_specs=[slot_spec(0), slot_spec(1), slot_spec(2), slot_spec(3)],
        out_specs=[out_spec, out_spec],
        scratch_shapes=[pltpu.VMEM((2, 3, r, cc), BF16), pltpu.VMEM((2, r, cc), F32), pltpu.VMEM((2, r, cc), F32),
                        pltpu.SemaphoreType.DMA((2, 3)), pltpu.SemaphoreType.DMA((2, 3)),
                        pltpu.SemaphoreType.DMA((2,)), pltpu.SemaphoreType.DMA((2,))])
    o = jax.ShapeDtypeStruct((n, r, cc), F32)
    x, y, _ = _position()
    chips = [_scalar(2 * x + y)] + [_scalar(2 * px + py) for px, py in _chip_peers(x, y)]
    return pl.pallas_call(body, name=name, grid_spec=grid_spec, out_shape=[o, o],
                          compiler_params=_params(("arbitrary",)))(*chips, p, p, p, p)


def _reduce_scatter(gs):
    c = lax.axis_index("c")
    outs = []
    for a, g in enumerate(gs):
        _, rows, cc = g.shape
        k = _n_copies(rows // 2, cc * 4, 16)
        p = _pair_exchange_sum(g.reshape(4, 2, k, rows // (2 * k), cc), f"rs_pair_exchange_sum_{a}")
        mine, theirs = _chip_exchange_sum(p, f"rs_chip_exchange_sum_{a}")
        mine, theirs = mine.reshape(rows // 2, cc), theirs.reshape(rows // 2, cc)
        outs.append(jnp.where(c == 0, jnp.concatenate([mine, theirs]), jnp.concatenate([theirs, mine])))
    return outs


def _gather_shards(w, name):
    rows, cc = w.shape
    half = rows // 2
    n = _n_copies(half, cc * w.dtype.itemsize, 16)
    r = half // n

    def body(core_ref, mine_ref, other_ref, out_ref, land, land_pair, send_sem, recv_sem, pair_send, pair_recv, out_sem):
        x, y, c = _position()
        k = pl.program_id(0)
        slot = k % 2
        me = 2 * x + y
        chips = [2 * px + py for px, py in _chip_peers(x, y)]
        pushes = [pltpu.make_async_remote_copy(
            src_ref=mine_ref.at[0, 0], dst_ref=land.at[slot, j], send_sem=send_sem.at[slot, j],
            recv_sem=recv_sem.at[slot, j], device_id=(px, py, c), device_id_type=MESH)
            for j, (px, py) in enumerate(_chip_peers(x, y))]
        for cp in pushes:
            cp.start()
        writes = [pltpu.make_async_copy(mine_ref.at[0, 0], out_ref.at[me, c, k], out_sem.at[0]),
                  pltpu.make_async_copy(other_ref.at[0, 0], out_ref.at[me, 1 - c, k], out_sem.at[1])]
        for cp in writes:
            cp.start()
        passes = []
        for j, cp in enumerate(pushes):
            cp.wait_recv()
            passes.append(pltpu.make_async_remote_copy(
                src_ref=land.at[slot, j], dst_ref=land_pair.at[slot, j], send_sem=pair_send.at[slot, j],
                recv_sem=pair_recv.at[slot, j], device_id=(x, y, 1 - c), device_id_type=MESH))
            passes[j].start()
            writes.append(pltpu.make_async_copy(land.at[slot, j], out_ref.at[chips[j], c, k], out_sem.at[2 + j]))
            writes[-1].start()
        for j, cp in enumerate(passes):
            cp.wait_recv()
            writes.append(pltpu.make_async_copy(land_pair.at[slot, j], out_ref.at[chips[j], 1 - c, k], out_sem.at[5 + j]))
            writes[-1].start()
        for cp in writes:
            cp.wait()
        for cp in pushes + passes:
            cp.wait_send()

    grid_spec = pltpu.PrefetchScalarGridSpec(
        num_scalar_prefetch=1, grid=(n,),
        in_specs=[pl.BlockSpec((1, 1, r, cc), lambda k, core: (core[0], k, 0, 0)),
                  pl.BlockSpec((1, 1, r, cc), lambda k, core: (1 - core[0], k, 0, 0))],
        out_specs=ANY,
        scratch_shapes=[pltpu.VMEM((2, 3, r, cc), w.dtype), pltpu.VMEM((2, 3, r, cc), w.dtype),
                        pltpu.SemaphoreType.DMA((2, 3)), pltpu.SemaphoreType.DMA((2, 3)),
                        pltpu.SemaphoreType.DMA((2, 3)), pltpu.SemaphoreType.DMA((2, 3)),
                        pltpu.SemaphoreType.DMA((8,))])
    w4 = w.reshape(2, n, r, cc)
    out = pl.pallas_call(body, name=name, grid_spec=grid_spec,
                         out_shape=jax.ShapeDtypeStruct((4, 2, n, r, cc), w.dtype),
                         compiler_params=_params(("arbitrary",)))(_scalar(lax.axis_index("c")), w4, w4)
    return out.reshape(4, rows, cc)


def _pad_rows(a, rows):
    return jnp.pad(a, ((0, rows - a.shape[0]), (0, 0)))


def _shard_cols(a):
    r, c4 = a.shape
    return a.reshape(r, 4, c4 // 4).transpose(1, 0, 2)


def kernel(x, mem, w_in, b_forget, conv_a_w, conv_a_b, ln_a_g, ln_a_b, conv_b_w, w_kv_mem, mem_ln_g, mem_ln_b, p_a, p_b, p_c, p_m, w_out, ln_g, ln_b, loss_target, m_w_in, m_b_forget, m_conv_a_w, m_conv_a_b, m_ln_a_g, m_ln_a_b, m_conv_b_w, m_w_kv_mem, m_mem_ln_g, m_mem_ln_b, m_p_a, m_p_b, m_p_c, m_p_m, m_w_out, m_ln_g, m_ln_b, v_w_in, v_b_forget, v_conv_a_w, v_conv_a_b, v_ln_a_g, v_ln_a_b, v_conv_b_w, v_w_kv_mem, v_mem_ln_g, v_mem_ln_b, v_p_a, v_p_b, v_p_c, v_p_m, v_w_out, v_ln_g, v_ln_b):
    depth = w_in.shape[0]
    x0 = x[0]
    s, d = x0.shape
    aw = conv_a_w.shape[2] * 4
    mw = p_m.shape[1]
    n_head = b_forget.shape[1]
    alpha = (2.0 * depth) ** 0.25
    in_cols = w_in.shape[2] * 4
    assert aw == n_head * HEAD_DIM and mw % LANES == 0 and in_cols == 11 * aw + n_head + 2 * mw + 4 * d
    cf0 = 10 * aw
    n_main = in_cols - n_head
    fblk = n_main // LANES
    n_pad = n_main + LANES
    gblk = (11 * aw + 2 * mw) // d
    mblk = (11 * aw) // (2 * mw)
    qblk, kblk, vblk = 7 * aw // LANES, 8 * aw // LANES, 9 * aw // LANES
    assert (11 * aw + 2 * mw) % d == 0 and (11 * aw) % (2 * mw) == 0

    def gather(w, name):
        flat = _gather_shards(w.astype(BF16).reshape(-1, w.shape[-1]), name)
        return flat.reshape((4,) + w.shape)

    w_in_g, p_a_g, p_b_g, p_c_g = (gather(w, f"gather_{nm}") for w, nm in
                                   ((w_in, "w_in"), (p_a, "p_a"), (p_b, "p_b"), (p_c, "p_c")))
    p_m_g, w_kv_g, w_out_g = (gather(w, f"gather_{nm}") for w, nm in
                              ((p_m, "p_m"), (w_kv_mem, "w_kv"), (w_out, "w_out")))
    conv_a_g, conv_b_g = _allgather_chips([conv_a_w, conv_b_w], "gather_conv_taps")

    def cols(g):
        return jnp.concatenate([g[j] for j in range(4)], axis=-1)

    def rows(g):
        return jnp.concatenate([g[j] for j in range(4)], axis=-2)

    shard_w = in_cols // 4

    def global_columns(shards, lo, hi):
        return [shards[j][:, max(lo, j * shard_w) - j * shard_w:min(hi, (j + 1) * shard_w) - j * shard_w]
                for j in range(4) if max(lo, j * shard_w) < min(hi, (j + 1) * shard_w)]

    w_pad = []
    for l in range(depth):
        shards = [w_in_g[j, l] for j in range(4)]
        w_pad.append(jnp.concatenate(
            global_columns(shards, 0, cf0) + global_columns(shards, cf0 + n_head, in_cols)
            + global_columns(shards, cf0, cf0 + n_head) + [jnp.zeros((d, LANES - n_head), BF16)], axis=1))

    piece_order = [(0, 3 * aw, 0), (3 * aw, 7 * aw, 1), (7 * aw, cf0, 2), (cf0, cf0 + n_head, 6),
                   (cf0 + n_head, 11 * aw + n_head, 3), (11 * aw + n_head, 11 * aw + n_head + 2 * mw, 4),
                   (11 * aw + n_head + 2 * mw, in_cols, 5)]

    def own_columns(parts, j):
        lo, hi = j * shard_w, (j + 1) * shard_w
        return jnp.concatenate([parts[k][:, max(lo, a) - a:min(hi, b) - a]
                                for a, b, k in piece_order if max(lo, a) < min(hi, b)], axis=1)
    pcat = jnp.concatenate([cols(p_a_g), cols(p_b_g), cols(p_c_g), cols(p_m_g)], axis=1)
    w_kv = rows(w_kv_g)
    wout = rows(w_out_g)
    conv_a = jnp.pad(cols(conv_a_g), ((0, 0), (0, HALO_A - CONV_A), (0, 0)))
    conv_b = jnp.pad(cols(conv_b_g), ((0, 0), (0, HALO_B - CONV_B), (0, 0)))
    bf_pad = jnp.pad(b_forget, ((0, 0), (0, LANES - n_head)))
    pieces = [(0, 3 * aw), (3 * aw, 7 * aw), (7 * aw, 10 * aw), (10 * aw, 11 * aw),
              (11 * aw, 11 * aw + 2 * mw), (11 * aw + 2 * mw, n_main), (n_main, n_pad)]

    mem_n = _ln_rows(mem[0], mem_ln_g[None], mem_ln_b[None], "mem_ln")

    xs, saved = [x0], []
    x_ops = [_matmul_copies(x0, "matmul_copies")]
    for l in range(depth):
        xl = xs[-1]
        proj = _mm(x_ops[l][0], w_pad[l], name=f"proj_{l}")
        za, conv_out = _branch_a_fwd(proj, conv_a[l], conv_a_b[l][None], ln_a_g[l][None], ln_a_b[l][None], f"a_fwd_{l}")
        zb = _branch_b_fwd(proj, conv_b[l], f"b_fwd_{l}")
        cum, cumt, sgt = _forget_prep(proj, bf_pad[l][None], fblk, f"forget_prep_{l}")
        o_c, lse = _fox_fwd(proj, cum, cumt, qblk, kblk, vblk, f"fox_fwd_{l}")
        zc = _gate_mul(proj, o_c, 10, f"c_gate_{l}")
        mkv = _mm(mem_n, w_kv[l], name=f"mkv_{l}")
        zm = _branch_m_fwd(proj, mkv, mblk, f"m_fwd_{l}")
        zs = [za, zb, zc, zm]
        y, y_b, y_bt = _merge_fwd(zs, proj, gblk, pcat[l], wout[l], xl, ln_g[l][None], ln_b[l][None], alpha,
                                  f"merge_fwd_{l}")
        xs.append(y)
        x_ops.append((y_b, y_bt))
        saved.append((proj, zs, conv_out, cum, cumt, sgt, o_c, lse, mkv))

    dy, loss_part = _loss_head(xs[-1], loss_target[0], "loss_head")

    g_w_in, g_conv_a, g_conv_b, g_w_kv, g_pcat, g_wout = [], [], [], [], [], []
    small = []
    dmem_n = None
    for l in reversed(range(depth)):
        proj, zs, conv_out, cum, cumt, sgt, o_c, lse, mkv = saved[l]
        xl = xs[l]
        (dx, d_g, dza, dzb, dzc, dzm, dpcat, dwout, dlng, dlnb) = _merge_bwd(
            zs, proj, gblk, pcat[l], wout[l], xl, ln_g[l][None], ln_b[l][None], dy, alpha, f"merge_bwd_{l}")
        d_m, dmkv = _branch_m_bwd(proj, mkv, dzm, mblk, f"m_bwd_{l}")
        g_w_kv.append(_mm(mem_n, dmkv, ta=True, name=f"dwkv_{l}"))
        dmem_n = _mm(dmkv, w_kv[l], tb=True, add=dmem_n, name=f"dmem_{l}")
        do, d_cg, delta = _fox_bwd_prep(proj, dzc, o_c, 10, f"fox_bwd_prep_{l}")
        dq, dk, dv, dcumt4, dcumq4 = _fox_bwd(proj, do, lse, delta, cum, cumt, qblk, kblk, vblk, f"fox_bwd_{l}")
        d_f, dbf = _forget_bwd(dcumt4, dcumq4, sgt, f"forget_bwd_{l}")
        d_b, dconv_b = _branch_b_bwd(proj, dzb, conv_b[l], f"b_bwd_{l}")
        d_a, dconv_a, dconv_ab, dlag, dlab = _branch_a_bwd(
            proj, conv_out, dza, conv_a[l], ln_a_g[l][None], ln_a_b[l][None], f"a_bwd_{l}")
        dparts = [d_a, d_b, jnp.concatenate([dq, dk, dv], axis=1), d_cg, d_m, d_g, d_f]
        dx = _input_grad(dparts, [w_pad[l][:, lo:hi] for lo, hi in pieces], dx, f"dx_{l}")
        dw_parts = [_mm(x_ops[l][1], dp, tm=d, tk=_pick(s, (T_DW_K,)), name=f"dw_{l}_{k}")
                    for k, dp in enumerate(dparts)]
        g_w_in.append(dw_parts)
        g_conv_a.append(dconv_a)
        g_conv_b.append(dconv_b)
        g_pcat.append(dpcat)
        g_wout.append(dwout)
        small.append([dbf[:, 0], dconv_ab[0], dlag[0], dlab[0], dlng[0], dlnb[0]])
        dy = dx
    grad_x = dy
    dmlg, dmlb = _ln_rows_param_grads(mem[0], dmem_n, "mem_ln_grads")
    for lst in (g_w_in, g_conv_a, g_conv_b, g_w_kv, g_pcat, g_wout, small):
        lst.reverse()

    pa_end, pb_end, pc_end = aw, 2 * aw, 3 * aw
    rs_in = [
        jnp.stack([jnp.concatenate([own_columns(parts, j) for parts in g_w_in], axis=0) for j in range(4)]),
        _shard_cols(jnp.concatenate(g_conv_a, axis=0)),
        _shard_cols(jnp.concatenate([_pad_rows(g, 2 * HALO_B) for g in g_conv_b], axis=0)),
        jnp.concatenate([g.reshape(4, g.shape[0] // 4, g.shape[1]) for g in g_w_kv], axis=1),
        _shard_cols(jnp.concatenate([g[:pa_end] for g in g_pcat], axis=0)),
        _shard_cols(jnp.concatenate([g[pa_end:pb_end] for g in g_pcat], axis=0)),
        _shard_cols(jnp.concatenate([g[pb_end:pc_end] for g in g_pcat], axis=0)),
        _shard_cols(jnp.concatenate([g[pc_end:] for g in g_pcat], axis=0)),
        jnp.concatenate([g.reshape(4, g.shape[0] // 4, g.shape[1]) for g in g_wout], axis=1),
    ]
    rs_out = _reduce_scatter(rs_in)
    gw_in = rs_out[0].reshape(depth, d, -1)
    g_ca = rs_out[1].reshape(depth, HALO_A, -1)[:, :CONV_A]
    g_cb = rs_out[2].reshape(depth, 2 * HALO_B, -1)[:, :CONV_B]
    gw_kv = rs_out[3].reshape(depth, -1, 2 * mw)
    gp_a = rs_out[4].reshape(depth, aw, -1)
    gp_b = rs_out[5].reshape(depth, aw, -1)
    gp_c = rs_out[6].reshape(depth, aw, -1)
    gp_m = rs_out[7].reshape(depth, mw, -1)
    gw_out = rs_out[8].reshape(depth, -1, d)

    flat = jnp.concatenate([jnp.concatenate(p) for p in small] + [dmlg[0], dmlb[0], loss_part[0, 0:1]])
    n_small = flat.shape[0]
    n_rows = -(-n_small // (8 * LANES)) * 8
    vec = jnp.pad(flat, (0, n_rows * LANES - n_small)).reshape(n_rows, LANES)
    tot = _sum_leading(_allgather_all(vec, "gather_small"), F32, "sum_small").reshape(-1)
    per_layer = n_head + 3 * aw + 2 * d
    tl = tot[:depth * per_layer].reshape(depth, per_layer)
    offs = np.cumsum([0, n_head, aw, aw, aw, d, d])
    g_bf, g_cab, g_lag, g_lab, g_lg, g_lb = [tl[:, offs[k]:offs[k + 1]] for k in range(6)]
    base = depth * per_layer
    g_mlg, g_mlb = tot[base:base + d], tot[base + d:base + 2 * d]
    loss = tot[base + 2 * d]

    grads = [gw_in, g_bf, g_ca, g_cab, g_lag, g_lab, g_cb, gw_kv, g_mlg, g_mlb, gp_a, gp_b, gp_c, gp_m, gw_out, g_lg, g_lb]
    ws = [w_in, b_forget, conv_a_w, conv_a_b, ln_a_g, ln_a_b, conv_b_w, w_kv_mem, mem_ln_g, mem_ln_b, p_a, p_b, p_c, p_m, w_out, ln_g, ln_b]
    ms = [m_w_in, m_b_forget, m_conv_a_w, m_conv_a_b, m_ln_a_g, m_ln_a_b, m_conv_b_w, m_w_kv_mem, m_mem_ln_g, m_mem_ln_b, m_p_a, m_p_b, m_p_c, m_p_m, m_w_out, m_ln_g, m_ln_b]
    vs = [v_w_in, v_b_forget, v_conv_a_w, v_conv_a_b, v_ln_a_g, v_ln_a_b, v_conv_b_w, v_w_kv_mem, v_mem_ln_g, v_mem_ln_b, v_p_a, v_p_b, v_p_c, v_p_m, v_w_out, v_ln_g, v_ln_b]
    deltas, new_ms, new_vs = [], [], []
    for k, (wk, gk, mk, vk) in enumerate(zip(ws, grads, ms, vs)):
        shape = wk.shape
        as_3d = (1,) * (3 - wk.ndim) + shape
        dk_, nm_, nv_ = _adamw(wk.reshape(as_3d), gk.reshape(as_3d), mk.reshape(as_3d), vk.reshape(as_3d), f"adamw_{k}")
        deltas.append(dk_.reshape(shape))
        new_ms.append(nm_.reshape(shape))
        new_vs.append(nv_.reshape(shape))
        grads[k] = gk.reshape(shape)
    return (loss, grad_x[None], *grads, *deltas, *new_ms, *new_vs)


def _gate_mul(proj, o, gblk, name):
    s, w = o.shape
    t = T_ELEM

    def body(g_ref, o_ref, z_ref):
        z_ref[...] = (o_ref[...] * _silu(g_ref[...])).astype(BF16)

    row = pl.BlockSpec((t, w), lambda i: (i, 0))
    return pl.pallas_call(body, name=name, grid=(s // t,), in_specs=[pl.BlockSpec((t, w), lambda i: (i, gblk)), row],
                          out_specs=row, out_shape=jax.ShapeDtypeStruct((s, w), BF16),
                          compiler_params=_params(("parallel",)))(proj, o)
```

```python
import functools
import math

import numpy as np
import jax
import jax.numpy as jnp
from jax import lax
from jax.experimental import pallas as pl
from jax.experimental.pallas import tpu as pltpu

F32 = jnp.float32
BF16 = jnp.bfloat16
MESH = pl.DeviceIdType.MESH
ANY = pl.BlockSpec(memory_space=pl.ANY)

LN_EPS = 1e-5
NEG_BIG = -1e30
HEAD_DIM = 64
LANES = 128
CONV_A = 31
CONV_B = 3
HALO_A = 32
HALO_B = 8
CHUNK = 32
VMEM_LIMIT = 60 * 1024 * 1024

ADAM_LR, ADAM_B1, ADAM_B2, ADAM_EPS, ADAM_WD, ADAM_STEP = 0.001, 0.9, 0.999, 1e-08, 0.01, 10

T_MM = 512
T_A = 128
T_B = 256
T_ATT_Q = 512
T_ATT_K = 1024
T_ATT_K_FWD = 2048
T_CUM = 512
T_M = 512
T_MERGE = 256
T_DX = 256
T_DW_K = 2048
T_ELEM = 512


def _pick(n, prefs):
    for p in prefs:
        if n % p == 0:
            return p
    return n


def _params(sem=None):
    return pltpu.CompilerParams(dimension_semantics=sem, vmem_limit_bytes=VMEM_LIMIT)


def _sigmoid(x):
    return jax.nn.sigmoid(x)


def _silu(x):
    return x * _sigmoid(x)


def _dsilu(x):
    s = _sigmoid(x)
    return s * (1.0 + x * (1.0 - s))


def _dot(a, b, dims, precision=None):
    return lax.dot_general(a, b, (dims, ((), ())), preferred_element_type=F32, precision=precision)


NN = ((1,), (0,))
NT = ((1,), (1,))
TN = ((0,), (0,))


def _iota(shape, dim):
    return lax.broadcasted_iota(jnp.int32, shape, dim)


def _mm(a, b, *, ta=False, tb=False, add=None, out_dtype=F32, tm=None, tk=None, name):
    m = a.shape[1] if ta else a.shape[0]
    k = a.shape[0] if ta else a.shape[1]
    n = b.shape[0] if tb else b.shape[1]
    if tm is None:
        tm = _pick(m, (1024, 512, 256)) if ta else _pick(m, (T_MM, 256))
    tn = _pick(n, (3456, 1152, 1024, 768, 512, 384, 256, 128))
    if tk is None:
        tk = _pick(k, (1024, 512, 256))
    nk = k // tk
    dims = ((0,) if ta else (1,), (1,) if tb else (0,))

    def body(*refs):
        if add is None:
            a_ref, b_ref, o_ref, acc_ref = refs
        else:
            a_ref, b_ref, add_ref, o_ref, acc_ref = refs
        kk = pl.program_id(2)
        p = _dot(a_ref[...].astype(BF16), b_ref[...].astype(BF16), dims)

        @pl.when(kk == 0)
        def _():
            acc_ref[...] = p

        @pl.when(kk > 0)
        def _():
            acc_ref[...] += p

        @pl.when(kk == nk - 1)
        def _():
            r = acc_ref[...]
            if add is not None:
                r = r + add_ref[...]
            o_ref[...] = r.astype(out_dtype)

    a_spec = (pl.BlockSpec((tk, tm), lambda j, i, kk: (kk, i)) if ta
              else pl.BlockSpec((tm, tk), lambda j, i, kk: (i, kk)))
    b_spec = (pl.BlockSpec((tn, tk), lambda j, i, kk: (j, kk)) if tb
              else pl.BlockSpec((tk, tn), lambda j, i, kk: (kk, j)))
    o_spec = pl.BlockSpec((tm, tn), lambda j, i, kk: (i, j))
    in_specs = [a_spec, b_spec] + ([o_spec] if add is not None else [])
    args = (a, b) + ((add,) if add is not None else ())
    return pl.pallas_call(
        body, name=name, grid=(n // tn, m // tm, nk), in_specs=in_specs, out_specs=o_spec,
        out_shape=jax.ShapeDtypeStruct((m, n), out_dtype),
        scratch_shapes=[pltpu.VMEM((tm, tn), F32)],
        compiler_params=_params(("parallel", "parallel", "arbitrary")),
    )(*args)


def _input_grad(dparts, wparts, add, name):
    s, d = add.shape
    t = T_DX
    n_p = len(dparts)

    def body(*refs):
        d_refs, w_hbm = refs[:n_p], refs[n_p:2 * n_p]
        add_ref, o_ref = refs[2 * n_p], refs[2 * n_p + 1]
        w_v = refs[2 * n_p + 2:]

        @pl.when(pl.program_id(0) == 0)
        def _():
            for p in range(n_p):
                pltpu.sync_copy(w_hbm[p], w_v[p])

        acc = add_ref[...]
        for p in range(n_p):
            acc = acc + _dot(d_refs[p][...], w_v[p][...], NT)
        o_ref[...] = acc

    row = pl.BlockSpec((t, d), lambda i: (i, 0))
    in_specs = ([pl.BlockSpec((t, dp.shape[1]), lambda i: (i, 0)) for dp in dparts] + [ANY] * n_p + [row])
    return pl.pallas_call(
        body, name=name, grid=(s // t,), in_specs=in_specs, out_specs=row,
        out_shape=jax.ShapeDtypeStruct((s, d), F32),
        scratch_shapes=[pltpu.VMEM(w.shape, BF16) for w in wparts],
        compiler_params=_params(("arbitrary",)),
    )(*dparts, *wparts, add)


def _ln_rows(x, g, b, name):
    r, d = x.shape
    t = _pick(r, (256,))

    def body(x_ref, g_ref, b_ref, o_ref):
        xv = x_ref[...]
        mu = jnp.mean(xv, axis=-1, keepdims=True)
        dv = xv - mu
        var = jnp.mean(dv * dv, axis=-1, keepdims=True)
        o_ref[...] = dv * lax.rsqrt(var + LN_EPS) * g_ref[...] + b_ref[...]

    row = pl.BlockSpec((t, d), lambda i: (i, 0))
    vec = pl.BlockSpec((1, d), lambda i: (0, 0))
    return pl.pallas_call(body, name=name, grid=(r // t,), in_specs=[row, vec, vec], out_specs=row,
                          out_shape=jax.ShapeDtypeStruct((r, d), F32), compiler_params=_params(("parallel",)))(x, g, b)


def _ln_rows_param_grads(x, dy, name):
    r, d = x.shape

    def body(x_ref, dy_ref, dg_ref, db_ref):
        xv = x_ref[...]
        mu = jnp.mean(xv, axis=-1, keepdims=True)
        dv = xv - mu
        var = jnp.mean(dv * dv, axis=-1, keepdims=True)
        xh = dv * lax.rsqrt(var + LN_EPS)
        dg_ref[...] = jnp.sum(dy_ref[...] * xh, axis=0, keepdims=True)
        db_ref[...] = jnp.sum(dy_ref[...], axis=0, keepdims=True)

    full = pl.BlockSpec((r, d), lambda i: (0, 0))
    vec = pl.BlockSpec((1, d), lambda i: (0, 0))
    o = jax.ShapeDtypeStruct((1, d), F32)
    return pl.pallas_call(body, name=name, grid=(1,), in_specs=[full, full], out_specs=[vec, vec],
                          out_shape=[o, o], compiler_params=_params(("arbitrary",)))(x, dy)


def _conv_a_chunk(glu_ref, cw_ref, r0):
    acc = cw_ref[0:1, :] * glu_ref[pl.ds(r0 + 2, CHUNK), :]
    for k in range(1, CONV_A):
        acc = acc + cw_ref[k:k + 1, :] * glu_ref[pl.ds(r0 + 2 + k, CHUNK), :]
    return acc


def _ln_stats(c):
    mu = jnp.mean(c, axis=-1, keepdims=True)
    d = c - mu
    var = jnp.mean(d * d, axis=-1, keepdims=True)
    rstd = lax.rsqrt(var + LN_EPS)
    return d * rstd, rstd


def _branch_a_fwd(proj, cw, cb, lg, lb, name):
    s = proj.shape[0]
    t = T_A
    w = cw.shape[1]
    r = t // HALO_A

    def body(u_ref, v_ref, gt_ref, hu_ref, hv_ref, cw_ref, cb_ref, lg_ref, lb_ref, z_ref, conv_ref, glu):
        i = pl.program_id(0)
        hglu = hu_ref[...] * _sigmoid(hv_ref[...])
        glu[0:HALO_A, :] = jnp.where(i > 0, hglu, 0.0)
        glu[HALO_A:HALO_A + t, :] = u_ref[...] * _sigmoid(v_ref[...])
        for c in range(t // CHUNK):
            r0 = c * CHUNK
            conv = _conv_a_chunk(glu, cw_ref, r0) + cb_ref[...]
            conv_ref[r0:r0 + CHUNK, :] = conv
            xh, _ = _ln_stats(conv)
            a3 = _silu(xh * lg_ref[...] + lb_ref[...])
            z_ref[r0:r0 + CHUNK, :] = (a3 * _silu(gt_ref[r0:r0 + CHUNK, :])).astype(BF16)

    def cur(col):
        return pl.BlockSpec((t, w), lambda i, col=col: (i, col))

    def prev(col):
        return pl.BlockSpec((HALO_A, w), lambda i, col=col: (jnp.maximum(i * r - 1, 0), col))

    vec = pl.BlockSpec((1, w), lambda i: (0, 0))
    return pl.pallas_call(
        body, name=name, grid=(s // t,),
        in_specs=[cur(0), cur(1), cur(2), prev(0), prev(1), pl.BlockSpec((HALO_A, w), lambda i: (0, 0)), vec, vec, vec],
        out_specs=[pl.BlockSpec((t, w), lambda i: (i, 0)), pl.BlockSpec((t, w), lambda i: (i, 0))],
        out_shape=[jax.ShapeDtypeStruct((s, w), BF16), jax.ShapeDtypeStruct((s, w), F32)],
        scratch_shapes=[pltpu.VMEM((HALO_A + t, w), F32)],
        compiler_params=_params(("parallel",)),
    )(proj, proj, proj, proj, proj, cw, cb, lg, lb)


def _branch_a_bwd(proj, conv, dz, cw, lg, lb, name):
    s = proj.shape[0]
    t = T_A
    w = cw.shape[1]
    r = t // HALO_A
    n = s // t
    nblk = s // HALO_A
    ext = t + HALO_A

    def body(u_ref, v_ref, gt_ref, dz_ref, conv_ref, pu_ref, pv_ref, ngt_ref, ndz_ref, nconv_ref,
             cw_ref, lg_ref, lb_ref, da_ref, dw_ref, dcb_ref, dlg_ref, dlb_ref, glu, dc, dw8):
        i = pl.program_id(0)

        @pl.when(i == 0)
        def _():
            dw8[...] = jnp.zeros_like(dw8)
            dcb_ref[...] = jnp.zeros_like(dcb_ref)
            dlg_ref[...] = jnp.zeros_like(dlg_ref)
            dlb_ref[...] = jnp.zeros_like(dlb_ref)

        glu[0:HALO_A, :] = jnp.where(i > 0, pu_ref[...] * _sigmoid(pv_ref[...]), 0.0)
        glu[HALO_A:HALO_A + t, :] = u_ref[...] * _sigmoid(v_ref[...])
        has_next = i < n - 1
        dcb = jnp.zeros((1, w), F32)
        dlg = jnp.zeros((1, w), F32)
        dlb = jnp.zeros((1, w), F32)
        for c in range(ext // CHUNK):
            r0 = c * CHUNK
            own = r0 < t
            xh, rstd = _ln_stats(conv_ref[r0:r0 + CHUNK, :] if own else nconv_ref[...])
            a2 = xh * lg_ref[...] + lb_ref[...]
            if own:
                gt = gt_ref[r0:r0 + CHUNK, :]
                dzc = dz_ref[r0:r0 + CHUNK, :]
            else:
                gt = ngt_ref[...]
                dzc = ndz_ref[...]
            da2 = dzc * _silu(gt) * _dsilu(a2)
            dxh = da2 * lg_ref[...]
            dconv = rstd * (dxh - jnp.mean(dxh, axis=-1, keepdims=True)
                            - xh * jnp.mean(dxh * xh, axis=-1, keepdims=True))
            if own:
                dc[r0:r0 + CHUNK, :] = dconv
                da_ref[r0:r0 + CHUNK, 2 * w:3 * w] = (dzc * _silu(a2) * _dsilu(gt)).astype(BF16)
                dcb = dcb + jnp.sum(dconv, axis=0, keepdims=True)
                dlg = dlg + jnp.sum(da2 * xh, axis=0, keepdims=True)
                dlb = dlb + jnp.sum(da2, axis=0, keepdims=True)
            else:
                dc[r0:r0 + CHUNK, :] = jnp.where(has_next, dconv, 0.0)
        dcb_ref[...] += dcb
        dlg_ref[...] += dlg
        dlb_ref[...] += dlb
        for c in range(t // CHUNK):
            r0 = c * CHUNK
            dcc = dc[r0:r0 + CHUNK, :]
            dglu = cw_ref[0:1, :] * dc[pl.ds(r0 + CONV_A - 1, CHUNK), :]
            for k in range(1, CONV_A):
                dglu = dglu + cw_ref[k:k + 1, :] * dc[pl.ds(r0 + CONV_A - 1 - k, CHUNK), :]
            for k in range(CONV_A):
                prod = dcc * glu[pl.ds(r0 + 2 + k, CHUNK), :]
                dw8[k] += jnp.sum(prod.reshape(CHUNK // 8, 8, w), axis=0)
            sv = _sigmoid(v_ref[r0:r0 + CHUNK, :])
            da_ref[r0:r0 + CHUNK, 0:w] = (dglu * sv).astype(BF16)
            da_ref[r0:r0 + CHUNK, w:2 * w] = (dglu * u_ref[r0:r0 + CHUNK, :] * sv * (1.0 - sv)).astype(BF16)

        @pl.when(i == n - 1)
        def _():
            dw_ref[...] = jnp.sum(dw8[...], axis=1)

    def cur(col):
        return pl.BlockSpec((t, w), lambda i, col=col: (i, col))

    def prev(col):
        return pl.BlockSpec((HALO_A, w), lambda i, col=col: (jnp.maximum(i * r - 1, 0), col))

    def nxt(col):
        return pl.BlockSpec((HALO_A, w), lambda i, col=col: (jnp.minimum((i + 1) * r, nblk - 1), col))

    own_rows = pl.BlockSpec((t, w), lambda i: (i, 0))
    vec = pl.BlockSpec((1, w), lambda i: (0, 0))
    vo = jax.ShapeDtypeStruct((1, w), F32)
    return pl.pallas_call(
        body, name=name, grid=(n,),
        in_specs=[cur(0), cur(1), cur(2), own_rows, own_rows, prev(0), prev(1), nxt(2), nxt(0), nxt(0),
                  pl.BlockSpec((HALO_A, w), lambda i: (0, 0)), vec, vec],
        out_specs=[pl.BlockSpec((t, 3 * w), lambda i: (i, 0)), pl.BlockSpec((HALO_A, w), lambda i: (0, 0)), vec, vec, vec],
        out_shape=[jax.ShapeDtypeStruct((s, 3 * w), BF16), jax.ShapeDtypeStruct((HALO_A, w), F32), vo, vo, vo],
        scratch_shapes=[pltpu.VMEM((HALO_A + t, w), F32), pltpu.VMEM((ext, w), F32),
                        pltpu.VMEM((HALO_A, 8, w), F32)],
        compiler_params=_params(("arbitrary",)),
    )(proj, proj, proj, dz, conv, proj, proj, proj, dz, conv, cw, lg, lb)


def _conv_b(u_ext, cw_ref, t):
    acc = cw_ref[0:1, :] * u_ext[pl.ds(HALO_B - 2, t), :]
    for k in range(1, CONV_B):
        acc = acc + cw_ref[k:k + 1, :] * u_ext[pl.ds(HALO_B - 2 + k, t), :]
    return acc


def _branch_b_fwd(proj, cw, name):
    s = proj.shape[0]
    t = T_B
    w = cw.shape[1]
    r = t // HALO_B

    def body(h_ref, b_ref, c_ref, gt_ref, ph_ref, pc_ref, cw_ref, z_ref, u_ext):
        i = pl.program_id(0)
        u_ext[0:HALO_B, :] = jnp.where(i > 0, pc_ref[...] * ph_ref[...], 0.0)
        u_ext[HALO_B:HALO_B + t, :] = c_ref[...] * h_ref[...]
        cv = _conv_b(u_ext, cw_ref, t)
        z_ref[...] = (b_ref[...] * cv * _silu(gt_ref[...])).astype(BF16)

    def cur(col):
        return pl.BlockSpec((t, w), lambda i, col=col: (i, col))

    def prev(col):
        return pl.BlockSpec((HALO_B, w), lambda i, col=col: (jnp.maximum(i * r - 1, 0), col))

    return pl.pallas_call(
        body, name=name, grid=(s // t,),
        in_specs=[cur(3), cur(4), cur(5), cur(6), prev(3), prev(5), pl.BlockSpec((HALO_B, w), lambda i: (0, 0))],
        out_specs=pl.BlockSpec((t, w), lambda i: (i, 0)),
        out_shape=jax.ShapeDtypeStruct((s, w), BF16),
        scratch_shapes=[pltpu.VMEM((HALO_B + t, w), F32)],
        compiler_params=_params(("parallel",)),
    )(proj, proj, proj, proj, proj, proj, cw)


def _branch_b_bwd(proj, dz, cw, name):
    s = proj.shape[0]
    t = T_B
    w = cw.shape[1]
    r = t // HALO_B
    n = s // t
    nblk = s // HALO_B

    def body(h_ref, b_ref, c_ref, gt_ref, dz_ref, ph_ref, pc_ref, nb_ref, ngt_ref, ndz_ref, cw_ref,
             db_ref, dw_ref, u_ext, dcv_ext, dw8):
        i = pl.program_id(0)

        @pl.when(i == 0)
        def _():
            dw8[...] = jnp.zeros_like(dw8)

        u_ext[0:HALO_B, :] = jnp.where(i > 0, pc_ref[...] * ph_ref[...], 0.0)
        u_ext[HALO_B:HALO_B + t, :] = c_ref[...] * h_ref[...]
        cv = _conv_b(u_ext, cw_ref, t)
        gt = gt_ref[...]
        dhb = dz_ref[...] * _silu(gt)
        db_ref[:, 3 * w:4 * w] = (dz_ref[...] * b_ref[...] * cv * _dsilu(gt)).astype(BF16)
        db_ref[:, w:2 * w] = (dhb * cv).astype(BF16)
        dcv = dhb * b_ref[...]
        dcv_ext[0:t, :] = dcv
        ndcv = ndz_ref[...] * _silu(ngt_ref[...]) * nb_ref[...]
        dcv_ext[t:t + HALO_B, :] = jnp.where(i < n - 1, ndcv, 0.0)
        du = cw_ref[0:1, :] * dcv_ext[pl.ds(2, t), :]
        for k in range(1, CONV_B):
            du = du + cw_ref[k:k + 1, :] * dcv_ext[pl.ds(2 - k, t), :]
        db_ref[:, 2 * w:3 * w] = (du * h_ref[...]).astype(BF16)
        db_ref[:, 0:w] = (du * c_ref[...]).astype(BF16)
        for k in range(CONV_B):
            prod = dcv * u_ext[pl.ds(HALO_B - 2 + k, t), :]
            dw8[k] += jnp.sum(prod.reshape(t // 8, 8, w), axis=0)

        @pl.when(i == n - 1)
        def _():
            dw_ref[...] = jnp.sum(dw8[...], axis=1)

    def cur(col):
        return pl.BlockSpec((t, w), lambda i, col=col: (i, col))

    def prev(col):
        return pl.BlockSpec((HALO_B, w), lambda i, col=col: (jnp.maximum(i * r - 1, 0), col))

    def nxt(col):
        return pl.BlockSpec((HALO_B, w), lambda i, col=col: (jnp.minimum((i + 1) * r, nblk - 1), col))

    return pl.pallas_call(
        body, name=name, grid=(n,),
        in_specs=[cur(3), cur(4), cur(5), cur(6), pl.BlockSpec((t, w), lambda i: (i, 0)),
                  prev(3), prev(5), nxt(4), nxt(6),
                  pl.BlockSpec((HALO_B, w), lambda i: (jnp.minimum((i + 1) * r, nblk - 1), 0)),
                  pl.BlockSpec((HALO_B, w), lambda i: (0, 0))],
        out_specs=[pl.BlockSpec((t, 4 * w), lambda i: (i, 0)), pl.BlockSpec((HALO_B, w), lambda i: (0, 0))],
        out_shape=[jax.ShapeDtypeStruct((s, 4 * w), BF16), jax.ShapeDtypeStruct((HALO_B, w), F32)],
        scratch_shapes=[pltpu.VMEM((HALO_B + t, w), F32), pltpu.VMEM((t + HALO_B, w), F32),
                        pltpu.VMEM((HALO_B, 8, w), F32)],
        compiler_params=_params(("arbitrary",)),
    )(proj, proj, proj, proj, dz, proj, proj, proj, proj, dz, cw)


def _forget_prep(proj, bf, fblk, name):
    s = proj.shape[0]
    t = T_CUM

    def body(f_ref, bf_ref, cum_ref, cumt_ref, sgt_ref, carry):
        i = pl.program_id(0)

        @pl.when(i == 0)
        def _():
            carry[...] = jnp.zeros_like(carry)

        z = f_ref[...] + bf_ref[...]
        logf = jnp.minimum(z, 0.0) - jnp.log1p(jnp.exp(-jnp.abs(z)))
        tri = (_iota((t, t), 0) >= _iota((t, t), 1)).astype(F32)
        cum = _dot(tri, logf, NN, precision=lax.Precision.HIGHEST) + carry[0:1, :]
        carry[0:1, :] = cum[t - 1:t, :]
        cum_ref[...] = cum
        cumt_ref[...] = cum.T[0:8, :]
        sgt_ref[...] = _sigmoid(-z).T[0:8, :]

    return pl.pallas_call(
        body, name=name, grid=(s // t,),
        in_specs=[pl.BlockSpec((t, LANES), lambda i: (i, fblk)), pl.BlockSpec((1, LANES), lambda i: (0, 0))],
        out_specs=[pl.BlockSpec((t, LANES), lambda i: (i, 0)), pl.BlockSpec((8, t), lambda i: (0, i)),
                   pl.BlockSpec((8, t), lambda i: (0, i))],
        out_shape=[jax.ShapeDtypeStruct((s, LANES), F32), jax.ShapeDtypeStruct((8, s), F32),
                   jax.ShapeDtypeStruct((8, s), F32)],
        scratch_shapes=[pltpu.VMEM((8, LANES), F32)],
        compiler_params=_params(("arbitrary",)),
    )(proj, bf)


def _lane_pick(x, lane):
    return jnp.sum(jnp.where(_iota(x.shape, 1) == lane, x, 0.0), axis=1, keepdims=True)


def _sublane_pick(x, row):
    return jnp.sum(jnp.where(_iota(x.shape, 0) == row, x, 0.0), axis=0, keepdims=True)


def _head_mask(hh):
    lane = _iota((1, LANES), 1)
    return (lane >= HEAD_DIM * hh) & (lane < HEAD_DIM * (hh + 1))


def _causal_pairs(nq, ratio, kv_major):
    if kv_major:
        pairs = [(q, k) for k in range(nq // ratio) for q in range(k * ratio, nq)]
    else:
        pairs = [(q, k) for q in range(nq) for k in range(q // ratio + 1)]
    qs = np.asarray([p[0] for p in pairs], np.int32)
    ks = np.asarray([p[1] for p in pairs], np.int32)
    return qs, ks, np.where(ks == qs // ratio, qs % ratio + 1, 0).astype(np.int32)


def _fox_scores(qm, kb, cum_ref, cumt, h, row0, diag, tq, tk):
    cq0 = _lane_pick(cum_ref[0:1, :], h)
    sc = _dot(qm, kb, NT) + (cq0 - _sublane_pick(cumt, h))
    if not diag:
        return sc
    causal = (_iota((tq, tk), 0) + row0) >= _iota((tq, tk), 1)
    return jnp.where(causal, sc, NEG_BIG)


def _fox_fwd(proj, cum, cumt, qblk, kblk, vblk, gblk, name):
    s = proj.shape[0]
    tq, tk = T_ATT_Q, T_ATT_K_FWD
    n_pair = 4
    qi_np, ki_np, diag_np = _causal_pairs(s // tq, tk // tq, kv_major=False)
    scale = HEAD_DIM ** -0.5

    def body(qi_ref, ki_ref, diag_ref, q_ref, k_ref, v_ref, cum_ref, cumt_ref, gt_ref, o_ref, lse_ref, z_ref,
             m_s, acc_s):
        hp = pl.program_id(0)
        step = pl.program_id(1)
        qi, ki, diag = qi_ref[step], ki_ref[step], diag_ref[step]

        @pl.when(ki == 0)
        def _():
            m_s[...] = jnp.full_like(m_s, NEG_BIG)
            acc_s[...] = jnp.zeros_like(acc_s)

        def update(blocks):
            width = tk if blocks is None else blocks * tq
            q = q_ref[...] * scale
            kb = k_ref[0:width, :].astype(BF16)
            v = v_ref[0:width, :]
            cumt = cumt_ref[:, 0:width]
            m_old = [m_s[0], m_s[1]]
            acc_old = [acc_s[0], acc_s[1]]
            scores, values = [], []
            for hh in range(2):
                hm = _head_mask(hh)
                qm = jnp.where(hm, q, 0.0).astype(BF16)
                values.append(jnp.where(hm, v, 1.0).astype(BF16))
                scores.append(_fox_scores(qm, kb, cum_ref, cumt, 2 * hp + hh, qi * tq - ki * tk, blocks is not None,
                                          tq, width))
            m_new = [jnp.maximum(m_old[hh], jnp.max(scores[hh], axis=1, keepdims=True)) for hh in range(2)]
            probs = [jnp.exp(scores[hh] - m_new[hh]).astype(BF16) for hh in range(2)]
            acc_new = [jnp.exp(m_old[hh] - m_new[hh]) * acc_old[hh] + _dot(probs[hh], values[hh], NN)
                       for hh in range(2)]
            for hh in range(2):
                acc_s[hh] = acc_new[hh]
                m_s[hh] = m_new[hh]

        @pl.when(diag == 0)
        def _():
            update(None)

        for blocks in range(1, tk // tq + 1):
            pl.when(diag == blocks)(functools.partial(update, blocks))

        @pl.when(diag >= 1)
        def _():
            lane = _iota((tq, LANES), 1)
            a0, a1 = acc_s[0], acc_s[1]
            o = jnp.where(lane < HEAD_DIM, a0 / pltpu.roll(a0, HEAD_DIM, axis=1),
                          a1 / pltpu.roll(a1, HEAD_DIM, axis=1))
            o_ref[...] = o
            z_ref[...] = (o * _silu(gt_ref[...])).astype(BF16)
            lse0 = m_s[0] + jnp.log(a0[:, HEAD_DIM:HEAD_DIM + 1])
            lse1 = m_s[1] + jnp.log(a1[:, 0:1])
            lse_ref[0] = jnp.where(lane == 0, lse0, jnp.where(lane == 1, lse1, 0.0))

    grid_spec = pltpu.PrefetchScalarGridSpec(
        num_scalar_prefetch=3, grid=(n_pair, len(qi_np)),
        in_specs=[pl.BlockSpec((tq, LANES), lambda hp, st, qi, ki, dg: (qi[st], qblk + hp)),
                  pl.BlockSpec((tk, LANES), lambda hp, st, qi, ki, dg: (ki[st], kblk + hp)),
                  pl.BlockSpec((tk, LANES), lambda hp, st, qi, ki, dg: (ki[st], vblk + hp)),
                  pl.BlockSpec((tq, LANES), lambda hp, st, qi, ki, dg: (qi[st], 0)),
                  pl.BlockSpec((8, tk), lambda hp, st, qi, ki, dg: (0, ki[st])),
                  pl.BlockSpec((tq, LANES), lambda hp, st, qi, ki, dg: (qi[st], gblk + hp))],
        out_specs=[pl.BlockSpec((tq, LANES), lambda hp, st, qi, ki, dg: (qi[st], hp)),
                   pl.BlockSpec((1, tq, LANES), lambda hp, st, qi, ki, dg: (hp, qi[st], 0)),
                   pl.BlockSpec((tq, LANES), lambda hp, st, qi, ki, dg: (qi[st], hp))],
        scratch_shapes=[pltpu.VMEM((2, tq, 1), F32), pltpu.VMEM((2, tq, LANES), F32)])
    return pl.pallas_call(
        body, name=name, grid_spec=grid_spec,
        out_shape=[jax.ShapeDtypeStruct((s, n_pair * LANES), F32), jax.ShapeDtypeStruct((n_pair, s, LANES), F32),
                   jax.ShapeDtypeStruct((s, n_pair * LANES), BF16)],
        compiler_params=_params(("parallel", "arbitrary")),
    )(jnp.asarray(qi_np), jnp.asarray(ki_np), jnp.asarray(diag_np), proj, proj, proj, cum, cumt, proj)


def _fox_bwd_prep(proj, dz, o, gblk, name):
    s, w = o.shape
    t = T_ELEM
    n_head = w // HEAD_DIM

    def body(gt_ref, dz_ref, o_ref, do_ref, dg_ref, dl_ref):
        gt = gt_ref[...]
        do = dz_ref[...] * _silu(gt)
        do_ref[...] = do
        dg_ref[...] = (dz_ref[...] * o_ref[...] * _dsilu(gt)).astype(BF16)
        sel = (_iota((w, LANES), 0) // HEAD_DIM == _iota((w, LANES), 1)).astype(F32)
        dl_ref[...] = _dot(do * o_ref[...], sel, NN, precision=lax.Precision.HIGHEST)

    assert n_head <= LANES
    row = pl.BlockSpec((t, w), lambda i: (i, 0))
    return pl.pallas_call(
        body, name=name, grid=(s // t,),
        in_specs=[pl.BlockSpec((t, w), lambda i: (i, gblk)), row, row],
        out_specs=[row, row, pl.BlockSpec((t, LANES), lambda i: (i, 0))],
        out_shape=[jax.ShapeDtypeStruct((s, w), F32), jax.ShapeDtypeStruct((s, w), BF16),
                   jax.ShapeDtypeStruct((s, LANES), F32)],
        compiler_params=_params(("parallel",)),
    )(proj, dz, o)


def _fox_bwd(proj, do, lse, delta, cum, cumt, qblk, kblk, vblk, name):
    s = proj.shape[0]
    tq, tk = T_ATT_Q, T_ATT_K
    nq = s // tq
    n_pair = 4
    qi_np, ki_np, diag_np = _causal_pairs(nq, tk // tq, kv_major=True)
    n_step = len(qi_np)
    scale = HEAD_DIM ** -0.5

    def body(qi_ref, ki_ref, diag_ref, q_ref, k_ref, v_ref, do_ref, lse_ref, dl_ref, cum_ref, cumt_ref,
             dq_ref, dk_ref, dv_ref, dct_ref, dcq_ref, dq_s, dk_s, dv_s, dc_s, dcq_s):
        hp = pl.program_id(0)
        step = pl.program_id(1)
        qi, ki, diag = qi_ref[step], ki_ref[step], diag_ref[step]

        @pl.when(step == 0)
        def _():
            dq_s[...] = jnp.zeros_like(dq_s)
            dcq_s[...] = jnp.zeros_like(dcq_s)

        @pl.when(qi == ki * (tk // tq))
        def _():
            dk_s[...] = jnp.zeros_like(dk_s)
            dv_s[...] = jnp.zeros_like(dv_s)
            dc_s[...] = jnp.zeros_like(dc_s)

        def update(blocks):
            width = tk if blocks is None else blocks * tq
            q = q_ref[...] * scale
            do = do_ref[...]
            kb = k_ref[0:width, :].astype(BF16)
            vb = v_ref[0:width, :].astype(BF16)
            cumt = cumt_ref[:, 0:width]
            sub = _iota((8, width), 0)
            dq_new = jnp.zeros((tq, LANES), F32)
            dcq_new = jnp.zeros((tq, LANES), F32)
            lane = _iota((tq, LANES), 1)
            for hh in range(2):
                h = 2 * hp + hh
                hm = _head_mask(hh)
                qm = jnp.where(hm, q, 0.0).astype(BF16)
                dom = jnp.where(hm, do, 0.0).astype(BF16)
                sc = _fox_scores(qm, kb, cum_ref, cumt, h, qi * tq - ki * tk, blocks is not None, tq, width)
                p = jnp.exp(sc - _lane_pick(lse_ref[0], hh))
                dv_s[0:width, :] += _dot(p.astype(BF16), dom, TN)
                dp = _dot(dom, vb, NT)
                ds = p * (dp - _lane_pick(dl_ref[...], h))
                dc_s[:, 0:width] += jnp.where(sub == h, -jnp.sum(ds, axis=0, keepdims=True), 0.0)
                dcq_new = dcq_new + jnp.where(lane == h, jnp.sum(ds, axis=1, keepdims=True), 0.0)
                dsb = ds.astype(BF16)
                dk_s[0:width, :] += _dot(dsb, qm, TN)
                dq_new = dq_new + jnp.where(hm, _dot(dsb, kb, NN), 0.0)
            row0 = pl.multiple_of(qi * tq, tq)
            dq_s[pl.ds(row0, tq), :] += dq_new * scale
            dcq_s[pl.ds(row0, tq), :] += dcq_new

        @pl.when(diag == 0)
        def _():
            update(None)

        for blocks in range(1, tk // tq + 1):
            pl.when(diag == blocks)(functools.partial(update, blocks))

        @pl.when(qi == nq - 1)
        def _():
            dk_ref[...] = dk_s[...].astype(BF16)
            dv_ref[...] = dv_s[...].astype(BF16)
            dct_ref[0] = dc_s[...]

        @pl.when(step == n_step - 1)
        def _():
            dq_ref[...] = dq_s[...].astype(BF16)
            dcq_ref[0] = dcq_s[...]

    grid_spec = pltpu.PrefetchScalarGridSpec(
        num_scalar_prefetch=3, grid=(n_pair, n_step),
        in_specs=[pl.BlockSpec((tq, LANES), lambda hp, st, qi, ki, dg: (qi[st], qblk + hp)),
                  pl.BlockSpec((tk, LANES), lambda hp, st, qi, ki, dg: (ki[st], kblk + hp)),
                  pl.BlockSpec((tk, LANES), lambda hp, st, qi, ki, dg: (ki[st], vblk + hp)),
                  pl.BlockSpec((tq, LANES), lambda hp, st, qi, ki, dg: (qi[st], hp)),
                  pl.BlockSpec((1, tq, LANES), lambda hp, st, qi, ki, dg: (hp, qi[st], 0)),
                  pl.BlockSpec((tq, LANES), lambda hp, st, qi, ki, dg: (qi[st], 0)),
                  pl.BlockSpec((tq, LANES), lambda hp, st, qi, ki, dg: (qi[st], 0)),
                  pl.BlockSpec((8, tk), lambda hp, st, qi, ki, dg: (0, ki[st]))],
        out_specs=[pl.BlockSpec((s, LANES), lambda hp, st, qi, ki, dg: (0, hp)),
                   pl.BlockSpec((tk, LANES), lambda hp, st, qi, ki, dg: (ki[st], hp)),
                   pl.BlockSpec((tk, LANES), lambda hp, st, qi, ki, dg: (ki[st], hp)),
                   pl.BlockSpec((1, 8, tk), lambda hp, st, qi, ki, dg: (hp, 0, ki[st])),
                   pl.BlockSpec((1, s, LANES), lambda hp, st, qi, ki, dg: (hp, 0, 0))],
        scratch_shapes=[pltpu.VMEM((s, LANES), F32), pltpu.VMEM((tk, LANES), F32), pltpu.VMEM((tk, LANES), F32),
                        pltpu.VMEM((8, tk), F32), pltpu.VMEM((s, LANES), F32)])
    w = n_pair * LANES
    return pl.pallas_call(
        body, name=name, grid_spec=grid_spec,
        out_shape=[jax.ShapeDtypeStruct((s, w), BF16), jax.ShapeDtypeStruct((s, w), BF16),
                   jax.ShapeDtypeStruct((s, w), BF16), jax.ShapeDtypeStruct((n_pair, 8, s), F32),
                   jax.ShapeDtypeStruct((n_pair, s, LANES), F32)],
        compiler_params=_params(("parallel", "arbitrary")),
    )(jnp.asarray(qi_np), jnp.asarray(ki_np), jnp.asarray(diag_np), proj, proj, proj, do, lse, delta, cum, cumt)


def _forget_bwd(dcumt4, dcumq4, sgt, name):
    s = sgt.shape[1]
    t = T_CUM
    n = s // t

    def body(d_ref, dq_ref, sg_ref, df_ref, dbf_ref, carry):
        i = pl.program_id(0)

        @pl.when(i == 0)
        def _():
            carry[...] = jnp.zeros_like(carry)
            dbf_ref[...] = jnp.zeros_like(dbf_ref)

        dq = dq_ref[0] + dq_ref[1] + dq_ref[2] + dq_ref[3]
        d = d_ref[0] + d_ref[1] + d_ref[2] + d_ref[3] + dq.T[0:8, :]
        upper = (_iota((t, t), 0) >= _iota((t, t), 1)).astype(F32)
        dlog = _dot(d, upper, NN, precision=lax.Precision.HIGHEST) + carry[:, 0:1]
        carry[...] += jnp.sum(d, axis=1, keepdims=True)
        dzt = dlog * sg_ref[...]
        dbf_ref[...] += jnp.sum(dzt, axis=1, keepdims=True)
        padded = jnp.concatenate([dzt, jnp.zeros((LANES - 8, t), F32)], axis=0)
        df_ref[...] = padded.T.astype(BF16)

    return pl.pallas_call(
        body, name=name, grid=(n,),
        in_specs=[pl.BlockSpec((4, 8, t), lambda i: (0, 0, n - 1 - i)),
                  pl.BlockSpec((4, t, LANES), lambda i: (0, n - 1 - i, 0)),
                  pl.BlockSpec((8, t), lambda i: (0, n - 1 - i))],
        out_specs=[pl.BlockSpec((t, LANES), lambda i: (n - 1 - i, 0)), pl.BlockSpec((8, LANES), lambda i: (0, 0))],
        out_shape=[jax.ShapeDtypeStruct((s, LANES), BF16), jax.ShapeDtypeStruct((8, LANES), F32)],
        scratch_shapes=[pltpu.VMEM((8, LANES), F32)],
        compiler_params=_params(("arbitrary",)),
    )(dcumt4, dcumq4, sgt)


def _mem_softmax(qm, kp):
    sc = _dot(qm, kp, NT) * (HEAD_DIM ** -0.5)
    e = jnp.exp(sc - jnp.max(sc, axis=1, keepdims=True))
    return e / jnp.sum(e, axis=1, keepdims=True)


def _branch_m_fwd(proj, mkv, mblk, name):
    s = proj.shape[0]
    t = T_M
    mw = mkv.shape[1] // 2
    ml = mkv.shape[0]

    def body(m_ref, kv_ref, z_ref):
        outs = []
        for pr in range(mw // LANES):
            qp = m_ref[:, pr * LANES:(pr + 1) * LANES]
            kp = kv_ref[:, pr * LANES:(pr + 1) * LANES].astype(BF16)
            vp = kv_ref[:, mw + pr * LANES:mw + (pr + 1) * LANES].astype(BF16)
            oh = []
            for hh in range(2):
                qm = jnp.where(_head_mask(hh), qp, 0.0).astype(BF16)
                oh.append(_dot(_mem_softmax(qm, kp).astype(BF16), vp, NN))
            outs.append(jnp.where(_iota((t, LANES), 1) < HEAD_DIM, oh[0], oh[1]))
        o = jnp.concatenate(outs, axis=1)
        z_ref[...] = (o * _silu(m_ref[:, mw:2 * mw])).astype(BF16)

    return pl.pallas_call(
        body, name=name, grid=(s // t,),
        in_specs=[pl.BlockSpec((t, 2 * mw), lambda i: (i, mblk)), pl.BlockSpec((ml, 2 * mw), lambda i: (0, 0))],
        out_specs=pl.BlockSpec((t, mw), lambda i: (i, 0)),
        out_shape=jax.ShapeDtypeStruct((s, mw), BF16),
        compiler_params=_params(("parallel",)),
    )(proj, mkv)


def _branch_m_bwd(proj, mkv, dz, mblk, name):
    s = proj.shape[0]
    t = T_M
    mw = mkv.shape[1] // 2
    ml = mkv.shape[0]
    scale = HEAD_DIM ** -0.5

    def body(m_ref, kv_ref, dz_ref, dm_ref, dkv_ref):
        i = pl.program_id(0)

        @pl.when(i == 0)
        def _():
            dkv_ref[...] = jnp.zeros_like(dkv_ref)

        gt = m_ref[:, mw:2 * mw]
        dz = dz_ref[...]
        do = dz * _silu(gt)
        outs = []
        for pr in range(mw // LANES):
            cols = slice(pr * LANES, (pr + 1) * LANES)
            vcols = slice(mw + pr * LANES, mw + (pr + 1) * LANES)
            qp = m_ref[:, cols]
            kp = kv_ref[:, cols].astype(BF16)
            vp = kv_ref[:, vcols].astype(BF16)
            dop = do[:, cols]
            oh = []
            dq = jnp.zeros((t, LANES), F32)
            dk = jnp.zeros((ml, LANES), F32)
            dv = jnp.zeros((ml, LANES), F32)
            for hh in range(2):
                hm = _head_mask(hh)
                qm = jnp.where(hm, qp, 0.0).astype(BF16)
                dom = jnp.where(hm, dop, 0.0).astype(BF16)
                p = _mem_softmax(qm, kp)
                pb = p.astype(BF16)
                oh.append(_dot(pb, vp, NN))
                dv = dv + _dot(pb, dom, TN)
                dp = _dot(dom, vp, NT)
                ds = p * (dp - jnp.sum(dp * p, axis=1, keepdims=True))
                dsb = (ds * scale).astype(BF16)
                dq = dq + jnp.where(hm, _dot(dsb, kp, NN), 0.0)
                dk = dk + _dot(dsb, qm, TN)
            outs.append(jnp.where(_iota((t, LANES), 1) < HEAD_DIM, oh[0], oh[1]))
            dm_ref[:, cols] = dq.astype(BF16)
            dkv_ref[:, cols] += dk
            dkv_ref[:, vcols] += dv
        o = jnp.concatenate(outs, axis=1)
        dm_ref[:, mw:2 * mw] = (dz * o * _dsilu(gt)).astype(BF16)

    return pl.pallas_call(
        body, name=name, grid=(s // t,),
        in_specs=[pl.BlockSpec((t, 2 * mw), lambda i: (i, mblk)), pl.BlockSpec((ml, 2 * mw), lambda i: (0, 0)),
                  pl.BlockSpec((t, mw), lambda i: (i, 0))],
        out_specs=[pl.BlockSpec((t, 2 * mw), lambda i: (i, 0)), pl.BlockSpec((ml, 2 * mw), lambda i: (0, 0))],
        out_shape=[jax.ShapeDtypeStruct((s, 2 * mw), BF16), jax.ShapeDtypeStruct((ml, 2 * mw), F32)],
        compiler_params=_params(("arbitrary",)),
    )(proj, mkv, dz)


def _merge_parts(z_refs, g_refs, pcat, bounds):
    ys, sgs = [], []
    merged = None
    for zr, gr, (lo, hi) in zip(z_refs, g_refs, bounds):
        y = _dot(zr[...], pcat[lo:hi, :], NN)
        sg = _sigmoid(gr[...])
        ys.append(y)
        sgs.append(sg)
        merged = sg * y if merged is None else merged + sg * y
    return ys, sgs, merged


def _branch_bounds(zs):
    bounds, lo = [], 0
    for z in zs:
        bounds.append((lo, lo + z.shape[1]))
        lo += z.shape[1]
    return bounds


def _matmul_copies(x, name):
    s, d = x.shape
    t = T_MERGE

    def body(x_ref, b_ref, bt_ref):
        b_ref[...] = x_ref[...].astype(BF16)
        bt_ref[...] = x_ref[...].T.astype(BF16)

    return pl.pallas_call(
        body, name=name, grid=(s // t,), in_specs=[pl.BlockSpec((t, d), lambda i: (i, 0))],
        out_specs=[pl.BlockSpec((t, d), lambda i: (i, 0)), pl.BlockSpec((d, t), lambda i: (0, i))],
        out_shape=[jax.ShapeDtypeStruct((s, d), BF16), jax.ShapeDtypeStruct((d, s), BF16)],
        compiler_params=_params(("parallel",)))(x)


def _merge_fwd(zs, proj, gblk, pcat, wout, x, lng, lnb, alpha, name):
    s, d = x.shape
    t = T_MERGE
    bounds = _branch_bounds(zs)

    def body(*refs):
        z_refs, g_refs = refs[0:4], refs[4:8]
        pcat_hbm, wout_hbm, x_ref, lng_ref, lnb_ref, y_ref, yb_ref, ybt_ref, pcat_v, wout_v = refs[8:]

        @pl.when(pl.program_id(0) == 0)
        def _():
            pltpu.sync_copy(pcat_hbm, pcat_v)
            pltpu.sync_copy(wout_hbm, wout_v)

        _, _, merged = _merge_parts(z_refs, g_refs, pcat_v, bounds)
        h = alpha * x_ref[...] + _dot(merged.astype(BF16), wout_v[...], NN)
        xh, _ = _ln_stats(h)
        y = xh * lng_ref[...] + lnb_ref[...]
        y_ref[...] = y
        yb_ref[...] = y.astype(BF16)
        ybt_ref[...] = y.T.astype(BF16)

    row = pl.BlockSpec((t, d), lambda i: (i, 0))
    vec = pl.BlockSpec((1, d), lambda i: (0, 0))
    in_specs = ([pl.BlockSpec((t, z.shape[1]), lambda i: (i, 0)) for z in zs]
                + [pl.BlockSpec((t, d), lambda i, k=k: (i, gblk + k)) for k in range(4)]
                + [ANY, ANY, row, vec, vec])
    return pl.pallas_call(
        body, name=name, grid=(s // t,), in_specs=in_specs,
        out_specs=[row, row, pl.BlockSpec((d, t), lambda i: (0, i))],
        out_shape=[jax.ShapeDtypeStruct((s, d), F32), jax.ShapeDtypeStruct((s, d), BF16),
                   jax.ShapeDtypeStruct((d, s), BF16)],
        scratch_shapes=[pltpu.VMEM(pcat.shape, BF16), pltpu.VMEM(wout.shape, BF16)],
        compiler_params=_params(("arbitrary",)),
    )(*zs, proj, proj, proj, proj, pcat, wout, x, lng, lnb)


def _merge_bwd(zs, proj, gblk, pcat, wout, x, lng, lnb, dy, alpha, name):
    s, d = x.shape
    t = T_MERGE
    n = s // t
    bounds = _branch_bounds(zs)

    def body(*refs):
        z_refs, g_refs = refs[0:4], refs[4:8]
        pcat_hbm, wout_hbm, x_ref, lng_ref, lnb_ref, dy_ref = refs[8:14]
        dx_ref, dg_ref = refs[14:16]
        dz_refs = refs[16:20]
        dpcat_hbm, dwout_hbm, dlng_ref, dlnb_ref = refs[20:24]
        pcat_v, wout_v, dpcat_v, dwout_v = refs[24:]
        i = pl.program_id(0)

        @pl.when(i == 0)
        def _():
            pltpu.sync_copy(pcat_hbm, pcat_v)
            pltpu.sync_copy(wout_hbm, wout_v)
            dpcat_v[...] = jnp.zeros_like(dpcat_v)
            dwout_v[...] = jnp.zeros_like(dwout_v)
            dlng_ref[...] = jnp.zeros_like(dlng_ref)
            dlnb_ref[...] = jnp.zeros_like(dlnb_ref)

        ys, sgs, merged = _merge_parts(z_refs, g_refs, pcat_v, bounds)
        mb = merged.astype(BF16)
        h = alpha * x_ref[...] + _dot(mb, wout_v[...], NN)
        xh, rstd = _ln_stats(h)
        dyv = dy_ref[...]
        dxh = dyv * lng_ref[...]
        dh = rstd * (dxh - jnp.mean(dxh, axis=-1, keepdims=True) - xh * jnp.mean(dxh * xh, axis=-1, keepdims=True))
        dx_ref[...] = alpha * dh
        dhb = dh.astype(BF16)
        dmerged = _dot(dhb, wout_v[...], NT)
        dpcat_new = []
        for k, (zr, (lo, hi)) in enumerate(zip(z_refs, bounds)):
            sg = sgs[k]
            dg_ref[:, k * d:(k + 1) * d] = (dmerged * ys[k] * sg * (1.0 - sg)).astype(BF16)
            dyk = (dmerged * sg).astype(BF16)
            dpcat_new.append(_dot(zr[...], dyk, TN))
            dz_refs[k][...] = _dot(dyk, pcat_v[lo:hi, :], NT)
        for (lo, hi), upd in zip(bounds, dpcat_new):
            dpcat_v[lo:hi, :] += upd
        dwout_v[...] += _dot(mb, dhb, TN)
        dlng_ref[...] += jnp.sum(dyv * xh, axis=0, keepdims=True)
        dlnb_ref[...] += jnp.sum(dyv, axis=0, keepdims=True)

        @pl.when(i == n - 1)
        def _():
            pltpu.sync_copy(dpcat_v, dpcat_hbm)
            pltpu.sync_copy(dwout_v, dwout_hbm)

    row = pl.BlockSpec((t, d), lambda i: (i, 0))
    vec = pl.BlockSpec((1, d), lambda i: (0, 0))
    z_specs = [pl.BlockSpec((t, z.shape[1]), lambda i: (i, 0)) for z in zs]
    in_specs = (z_specs + [pl.BlockSpec((t, d), lambda i, k=k: (i, gblk + k)) for k in range(4)]
                + [ANY, ANY, row, vec, vec, row])
    out_specs = [row, pl.BlockSpec((t, 4 * d), lambda i: (i, 0))] + z_specs + [ANY, ANY, vec, vec]
    vo = jax.ShapeDtypeStruct((1, d), F32)
    out_shape = ([jax.ShapeDtypeStruct((s, d), F32), jax.ShapeDtypeStruct((s, 4 * d), BF16)]
                 + [jax.ShapeDtypeStruct(z.shape, F32) for z in zs]
                 + [jax.ShapeDtypeStruct(pcat.shape, F32), jax.ShapeDtypeStruct(wout.shape, F32), vo, vo])
    return pl.pallas_call(
        body, name=name, grid=(n,), in_specs=in_specs, out_specs=out_specs, out_shape=out_shape,
        scratch_shapes=[pltpu.VMEM(pcat.shape, BF16), pltpu.VMEM(wout.shape, BF16),
                        pltpu.VMEM(pcat.shape, F32), pltpu.VMEM(wout.shape, F32)],
        compiler_params=_params(("arbitrary",)),
    )(*zs, proj, proj, proj, proj, pcat, wout, x, lng, lnb, dy)


def _loss_head(y, target, name):
    s, d = y.shape
    t = T_ELEM

    def body(y_ref, t_ref, dy_ref, loss_ref):
        @pl.when(pl.program_id(0) == 0)
        def _():
            loss_ref[...] = jnp.zeros_like(loss_ref)

        e = y_ref[...] - t_ref[...]
        dy_ref[...] = e * (1.0 / d)
        loss_ref[...] += 0.5 * jnp.sum(jnp.mean(e * e, axis=-1, keepdims=True), axis=0, keepdims=True)

    row = pl.BlockSpec((t, d), lambda i: (i, 0))
    return pl.pallas_call(
        body, name=name, grid=(s // t,), in_specs=[row, row],
        out_specs=[row, pl.BlockSpec((8, LANES), lambda i: (0, 0))],
        out_shape=[jax.ShapeDtypeStruct((s, d), F32), jax.ShapeDtypeStruct((8, LANES), F32)],
        compiler_params=_params(("arbitrary",)),
    )(y, target)


def _adamw(w, g, m, v, name):
    n_l, r, c = w.shape
    t = _pick(r, (256, 128, 64, 32, 16, 8))

    def body(w_ref, g_ref, m_ref, v_ref, d_ref, nm_ref, nv_ref):
        gv = g_ref[...]
        nm = ADAM_B1 * m_ref[...] + (1.0 - ADAM_B1) * gv
        nv = ADAM_B2 * v_ref[...] + (1.0 - ADAM_B2) * (gv * gv)
        m_hat = nm / (1.0 - ADAM_B1 ** ADAM_STEP)
        v_hat = nv / (1.0 - ADAM_B2 ** ADAM_STEP)
        d_ref[...] = -ADAM_LR * (m_hat / (jnp.sqrt(v_hat) + ADAM_EPS) + ADAM_WD * w_ref[...])
        nm_ref[...] = nm
        nv_ref[...] = nv

    blk = pl.BlockSpec((1, t, c), lambda l, i: (l, i, 0))
    o = jax.ShapeDtypeStruct((n_l, r, c), F32)
    return pl.pallas_call(body, name=name, grid=(n_l, r // t), in_specs=[blk] * 4, out_specs=[blk] * 3,
                          out_shape=[o, o, o], compiler_params=_params(("parallel", "parallel")))(w, g, m, v)


def _sum_leading(x, out_dtype, name):
    k, r, c = x.shape
    t = _pick(r, (256, 128, 64, 32, 16, 8))

    def body(x_ref, o_ref):
        acc = x_ref[0].astype(F32)
        for j in range(1, k):
            acc = acc + x_ref[j].astype(F32)
        o_ref[...] = acc.astype(out_dtype)

    return pl.pallas_call(body, name=name, grid=(r // t,),
                          in_specs=[pl.BlockSpec((k, t, c), lambda i: (0, i, 0))],
                          out_specs=pl.BlockSpec((t, c), lambda i: (i, 0)),
                          out_shape=jax.ShapeDtypeStruct((r, c), out_dtype), compiler_params=_params(("parallel",)))(x)


def _position():
    return lax.axis_index("x"), lax.axis_index("y"), lax.axis_index("c")


def _chip_peers(x, y):
    return [(1 - x, y), (x, 1 - y), (1 - x, 1 - y)]


def _comm_call(body, n_in, out_shape, n_remote, n_local, name):
    return pl.pallas_call(
        body, name=name, in_specs=[ANY] * n_in, out_specs=[ANY] * len(out_shape), out_shape=out_shape,
        scratch_shapes=[pltpu.SemaphoreType.DMA((n_remote,)), pltpu.SemaphoreType.DMA((n_remote,)),
                        pltpu.SemaphoreType.DMA((max(n_local, 1),))])


def _run_copies(local, remote, send, recv, loc):
    copies = [pltpu.make_async_copy(src, dst, loc.at[k]) for k, (src, dst) in enumerate(local)]
    copies += [pltpu.make_async_remote_copy(src_ref=src, dst_ref=dst, send_sem=send.at[k], recv_sem=recv.at[k],
                                            device_id=peer, device_id_type=MESH)
               for k, (src, dst, peer) in enumerate(remote)]
    for cp in copies:
        cp.start()
    for cp in copies:
        cp.wait()


COPY_BYTES = 1024 * 1024


def _n_copies(rows, row_bytes, align):
    n = 8
    while n > 1 and (rows % (n * align) or rows // n * row_bytes < COPY_BYTES):
        n //= 2
    return n


def _allgather_chips(arrs, name):
    n = len(arrs)
    per_layer = [a.size * a.dtype.itemsize // a.shape[0] >= COPY_BYTES for a in arrs]
    n_each = [a.shape[0] if pl_ else 1 for a, pl_ in zip(arrs, per_layer)]

    def body(*refs):
        ins, outs = refs[:n], refs[n:2 * n]
        send, recv, loc = refs[2 * n:]
        x, y, c = _position()
        me = 2 * x + y
        local, remote = [], []
        for a in range(n):
            if per_layer[a]:
                parts = [(ins[a].at[l], outs[a].at[me, l]) for l in range(arrs[a].shape[0])]
            else:
                parts = [(ins[a], outs[a].at[me])]
            local += parts
            for px, py in _chip_peers(x, y):
                remote += [(src, dst, (px, py, c)) for src, dst in parts]
        _run_copies(local, remote, send, recv, loc)

    out_shape = [jax.ShapeDtypeStruct((4,) + a.shape, a.dtype) for a in arrs]
    return _comm_call(body, n, out_shape, 3 * sum(n_each), sum(n_each), name)(*arrs)


def _allgather_all(v, name):
    def body(v_ref, o_ref, send, recv, loc):
        x, y, c = _position()
        me = 4 * x + 2 * y + c
        remote = []
        for k in range(1, 8):
            fx, fy, fc = (k >> 2) & 1, (k >> 1) & 1, k & 1
            remote.append((v_ref, o_ref.at[me], (x ^ fx, y ^ fy, c ^ fc)))
        _run_copies([(v_ref, o_ref.at[me])], remote, send, recv, loc)

    return _comm_call(body, 1, [jax.ShapeDtypeStruct((8,) + v.shape, v.dtype)], 7, 1, name)(v)[0]


def _pair_exchange_sum(g, name):
    _, _, n, r, cc = g.shape

    def body(c_ref, mine_ref, send_ref, o_ref, land, send_sem, recv_sem):
        x, y, c = _position()
        slot = (pl.program_id(0) * n + pl.program_id(1)) % 2
        push = pltpu.make_async_remote_copy(
            src_ref=send_ref.at[0, 0, 0], dst_ref=land.at[slot], send_sem=send_sem.at[slot],
            recv_sem=recv_sem.at[slot], device_id=(x, y, 1 - c), device_id_type=MESH)
        push.start()
        push.wait_recv()
        o_ref[0, 0] = (mine_ref[0, 0, 0] + land[slot]).astype(BF16)
        push.wait_send()

    grid_spec = pltpu.PrefetchScalarGridSpec(
        num_scalar_prefetch=1, grid=(4, n),
        in_specs=[pl.BlockSpec((1, 1, 1, r, cc), lambda j, k, c: (j, c[0], k, 0, 0)),
                  pl.BlockSpec((1, 1, 1, r, cc), lambda j, k, c: (j, 1 - c[0], k, 0, 0))],
        out_specs=pl.BlockSpec((1, 1, r, cc), lambda j, k, c: (j, k, 0, 0)),
        scratch_shapes=[pltpu.VMEM((2, r, cc), F32), pltpu.SemaphoreType.DMA((2,)), pltpu.SemaphoreType.DMA((2,))])
    return pl.pallas_call(
        body, name=name, grid_spec=grid_spec, out_shape=jax.ShapeDtypeStruct((4, n, r, cc), BF16),
        compiler_params=_params(("arbitrary", "arbitrary")))(_scalar(lax.axis_index("c")), g, g)


def _scalar(v):
    return v.astype(jnp.int32).reshape(1)


def _chip_exchange_sum(p, name):
    _, n, r, cc = p.shape

    def body(i0, i1, i2, i3, own_ref, s0_ref, s1_ref, s2_ref, mine_ref, theirs_ref, land, total, land_pair,
             send_sem, recv_sem, pair_send, pair_recv):
        x, y, c = _position()
        slot = pl.program_id(0) % 2
        pushes = [pltpu.make_async_remote_copy(
            src_ref=src.at[0, 0], dst_ref=land.at[slot, j], send_sem=send_sem.at[slot, j], recv_sem=recv_sem.at[slot, j],
            device_id=(px, py, c), device_id_type=MESH)
            for j, (src, (px, py)) in enumerate(zip((s0_ref, s1_ref, s2_ref), _chip_peers(x, y)))]
        for cp in pushes:
            cp.start()
        acc = own_ref[0, 0].astype(F32)
        for j, cp in enumerate(pushes):
            cp.wait_recv()
            acc = acc + land[slot, j].astype(F32)
        mine_ref[0] = acc
        total[slot] = acc
        share = pltpu.make_async_remote_copy(
            src_ref=total.at[slot], dst_ref=land_pair.at[slot], send_sem=pair_send.at[slot], recv_sem=pair_recv.at[slot],
            device_id=(x, y, 1 - c), device_id_type=MESH)
        share.start()
        share.wait_recv()
        theirs_ref[0] = land_pair[slot]
        for cp in pushes:
            cp.wait_send()
        share.wait_send()

    def slot_spec(which):
        return pl.BlockSpec((1, 1, r, cc), lambda k, *idx, which=which: (idx[which][0], k, 0, 0))

    out_spec = pl.BlockSpec((1, r, cc), lambda k, *idx: (k, 0, 0))
    grid_spec = pltpu.PrefetchScalarGridSpec(
        num_scalar_prefetch=4, grid=(n,), in_specs=[slot_spec(0), slot_spec(1), slot_spec(2), slot_spec(3)],
        out_specs=[out_spec, out_spec],
        scratch_shapes=[pltpu.VMEM((2, 3, r, cc), BF16), pltpu.VMEM((2, r, cc), F32), pltpu.VMEM((2, r, cc), F32),
                        pltpu.SemaphoreType.DMA((2, 3)), pltpu.SemaphoreType.DMA((2, 3)),
                        pltpu.SemaphoreType.DMA((2,)), pltpu.SemaphoreType.DMA((2,))])
    o = jax.ShapeDtypeStruct((n, r, cc), F32)
    x, y, _ = _position()
    chips = [_scalar(2 * x + y)] + [_scalar(2 * px + py) for px, py in _chip_peers(x, y)]
    return pl.pallas_call(body, name=name, grid_spec=grid_spec, out_shape=[o, o],
                          compiler_params=_params(("arbitrary",)))(*chips, p, p, p, p)


def _reduce_scatter(gs):
    c = lax.axis_index("c")
    outs = []
    for a, g in enumerate(gs):
        _, rows, cc = g.shape
        k = _n_copies(rows // 2, cc * 4, 16)
        p = _pair_exchange_sum(g.reshape(4, 2, k, rows // (2 * k), cc), f"rs_pair_exchange_sum_{a}")
        mine, theirs = _chip_exchange_sum(p, f"rs_chip_exchange_sum_{a}")
        mine, theirs = mine.reshape(rows // 2, cc), theirs.reshape(rows // 2, cc)
        outs.append(jnp.where(c == 0, jnp.concatenate([mine, theirs]), jnp.concatenate([theirs, mine])))
    return outs


def _gather_shards(w, name):
    rows, cc = w.shape
    half = rows // 2
    n = _n_copies(half, cc * w.dtype.itemsize, 16)
    r = half // n

    def body(core_ref, mine_ref, other_ref, out_ref, land, land_pair, send_sem, recv_sem, pair_send, pair_recv, out_sem):
        x, y, c = _position()
        k = pl.program_id(0)
        slot = k % 2
        me = 2 * x + y
        chips = [2 * px + py for px, py in _chip_peers(x, y)]
        pushes = [pltpu.make_async_remote_copy(
            src_ref=mine_ref.at[0, 0], dst_ref=land.at[slot, j], send_sem=send_sem.at[slot, j],
            recv_sem=recv_sem.at[slot, j], device_id=(px, py, c), device_id_type=MESH)
            for j, (px, py) in enumerate(_chip_peers(x, y))]
        for cp in pushes:
            cp.start()
        writes = [pltpu.make_async_copy(mine_ref.at[0, 0], out_ref.at[me, c, k], out_sem.at[0]),
                  pltpu.make_async_copy(other_ref.at[0, 0], out_ref.at[me, 1 - c, k], out_sem.at[1])]
        for cp in writes:
            cp.start()
        passes = []
        for j, cp in enumerate(pushes):
            cp.wait_recv()
            passes.append(pltpu.make_async_remote_copy(
                src_ref=land.at[slot, j], dst_ref=land_pair.at[slot, j], send_sem=pair_send.at[slot, j],
                recv_sem=pair_recv.at[slot, j], device_id=(x, y, 1 - c), device_id_type=MESH))
            passes[j].start()
            writes.append(pltpu.make_async_copy(land.at[slot, j], out_ref.at[chips[j], c, k], out_sem.at[2 + j]))
            writes[-1].start()
        for j, cp in enumerate(passes):
            cp.wait_recv()
            writes.append(pltpu.make_async_copy(land_pair.at[slot, j], out_ref.at[chips[j], 1 - c, k], out_sem.at[5 + j]))
            writes[-1].start()
        for cp in writes:
            cp.wait()
        for cp in pushes + passes:
            cp.wait_send()

    grid_spec = pltpu.PrefetchScalarGridSpec(
        num_scalar_prefetch=1, grid=(n,),
        in_specs=[pl.BlockSpec((1, 1, r, cc), lambda k, core: (core[0], k, 0, 0)),
                  pl.BlockSpec((1, 1, r, cc), lambda k, core: (1 - core[0], k, 0, 0))],
        out_specs=ANY,
        scratch_shapes=[pltpu.VMEM((2, 3, r, cc), w.dtype), pltpu.VMEM((2, 3, r, cc), w.dtype),
                        pltpu.SemaphoreType.DMA((2, 3)), pltpu.SemaphoreType.DMA((2, 3)),
                        pltpu.SemaphoreType.DMA((2, 3)), pltpu.SemaphoreType.DMA((2, 3)),
                        pltpu.SemaphoreType.DMA((8,))])
    w4 = w.reshape(2, n, r, cc)
    out = pl.pallas_call(body, name=name, grid_spec=grid_spec,
                         out_shape=jax.ShapeDtypeStruct((4, 2, n, r, cc), w.dtype),
                         compiler_params=_params(("arbitrary",)))(_scalar(lax.axis_index("c")), w4, w4)
    return out.reshape(4, rows, cc)


def _pad_rows(a, rows):
    return jnp.pad(a, ((0, rows - a.shape[0]), (0, 0)))


def _shard_cols(a):
    r, c4 = a.shape
    return a.reshape(r, 4, c4 // 4).transpose(1, 0, 2)


def kernel(x, mem, w_in, b_forget, conv_a_w, conv_a_b, ln_a_g, ln_a_b, conv_b_w, w_kv_mem, mem_ln_g, mem_ln_b, p_a, p_b, p_c, p_m, w_out, ln_g, ln_b, loss_target, m_w_in, m_b_forget, m_conv_a_w, m_conv_a_b, m_ln_a_g, m_ln_a_b, m_conv_b_w, m_w_kv_mem, m_mem_ln_g, m_mem_ln_b, m_p_a, m_p_b, m_p_c, m_p_m, m_w_out, m_ln_g, m_ln_b, v_w_in, v_b_forget, v_conv_a_w, v_conv_a_b, v_ln_a_g, v_ln_a_b, v_conv_b_w, v_w_kv_mem, v_mem_ln_g, v_mem_ln_b, v_p_a, v_p_b, v_p_c, v_p_m, v_w_out, v_ln_g, v_ln_b):
    depth = w_in.shape[0]
    x0 = x[0]
    s, d = x0.shape
    aw = conv_a_w.shape[2] * 4
    mw = p_m.shape[1]
    n_head = b_forget.shape[1]
    alpha = (2.0 * depth) ** 0.25
    in_cols = w_in.shape[2] * 4
    assert aw == n_head * HEAD_DIM and mw % LANES == 0 and in_cols == 11 * aw + n_head + 2 * mw + 4 * d
    cf0 = 10 * aw
    n_main = in_cols - n_head
    fblk = n_main // LANES
    n_pad = n_main + LANES
    gblk = (11 * aw + 2 * mw) // d
    mblk = (11 * aw) // (2 * mw)
    qblk, kblk, vblk = 7 * aw // LANES, 8 * aw // LANES, 9 * aw // LANES
    assert (11 * aw + 2 * mw) % d == 0 and (11 * aw) % (2 * mw) == 0

    def gather(w, name):
        flat = _gather_shards(w.astype(BF16).reshape(-1, w.shape[-1]), name)
        return flat.reshape((4,) + w.shape)

    w_in_g, p_a_g, p_b_g, p_c_g = (gather(w, f"gather_{nm}") for w, nm in
                                   ((w_in, "w_in"), (p_a, "p_a"), (p_b, "p_b"), (p_c, "p_c")))
    p_m_g, w_kv_g, w_out_g = (gather(w, f"gather_{nm}") for w, nm in
                              ((p_m, "p_m"), (w_kv_mem, "w_kv"), (w_out, "w_out")))
    conv_a_g, conv_b_g = _allgather_chips([conv_a_w, conv_b_w], "gather_conv_taps")

    def cols(g):
        return jnp.concatenate([g[j] for j in range(4)], axis=-1)

    def rows(g):
        return jnp.concatenate([g[j] for j in range(4)], axis=-2)

    shard_w = in_cols // 4

    def global_columns(shards, lo, hi):
        return [shards[j][:, max(lo, j * shard_w) - j * shard_w:min(hi, (j + 1) * shard_w) - j * shard_w]
                for j in range(4) if max(lo, j * shard_w) < min(hi, (j + 1) * shard_w)]

    w_pad = []
    for l in range(depth):
        shards = [w_in_g[j, l] for j in range(4)]
        w_pad.append(jnp.concatenate(
            global_columns(shards, 0, cf0) + global_columns(shards, cf0 + n_head, in_cols)
            + global_columns(shards, cf0, cf0 + n_head) + [jnp.zeros((d, LANES - n_head), BF16)], axis=1))

    piece_order = [(0, 3 * aw, 0), (3 * aw, 7 * aw, 1), (7 * aw, cf0, 2), (cf0, cf0 + n_head, 6),
                   (cf0 + n_head, 11 * aw + n_head, 3), (11 * aw + n_head, 11 * aw + n_head + 2 * mw, 4),
                   (11 * aw + n_head + 2 * mw, in_cols, 5)]

    def own_columns(parts, j):
        lo, hi = j * shard_w, (j + 1) * shard_w
        return jnp.concatenate([parts[k][:, max(lo, a) - a:min(hi, b) - a]
                                for a, b, k in piece_order if max(lo, a) < min(hi, b)], axis=1)
    pcat = jnp.concatenate([cols(p_a_g), cols(p_b_g), cols(p_c_g), cols(p_m_g)], axis=1)
    w_kv = rows(w_kv_g)
    wout = rows(w_out_g)
    conv_a = jnp.pad(cols(conv_a_g), ((0, 0), (0, HALO_A - CONV_A), (0, 0)))
    conv_b = jnp.pad(cols(conv_b_g), ((0, 0), (0, HALO_B - CONV_B), (0, 0)))
    bf_pad = jnp.pad(b_forget, ((0, 0), (0, LANES - n_head)))
    pieces = [(0, 3 * aw), (3 * aw, 7 * aw), (7 * aw, 10 * aw), (10 * aw, 11 * aw),
              (11 * aw, 11 * aw + 2 * mw), (11 * aw + 2 * mw, n_main), (n_main, n_pad)]

    mem_n = _ln_rows(mem[0], mem_ln_g[None], mem_ln_b[None], "mem_ln")

    xs, saved = [x0], []
    x_ops = [_matmul_copies(x0, "matmul_copies")]
    for l in range(depth):
        xl = xs[-1]
        proj = _mm(x_ops[l][0], w_pad[l], name=f"proj_{l}")
        za, conv_out = _branch_a_fwd(proj, conv_a[l], conv_a_b[l][None], ln_a_g[l][None], ln_a_b[l][None], f"a_fwd_{l}")
        zb = _branch_b_fwd(proj, conv_b[l], f"b_fwd_{l}")
        cum, cumt, sgt = _forget_prep(proj, bf_pad[l][None], fblk, f"forget_prep_{l}")
        o_c, lse, zc = _fox_fwd(proj, cum, cumt, qblk, kblk, vblk, 10 * aw // LANES, f"fox_fwd_{l}")
        mkv = _mm(mem_n, w_kv[l], name=f"mkv_{l}")
        zm = _branch_m_fwd(proj, mkv, mblk, f"m_fwd_{l}")
        zs = [za, zb, zc, zm]
        y, y_b, y_bt = _merge_fwd(zs, proj, gblk, pcat[l], wout[l], xl, ln_g[l][None], ln_b[l][None], alpha,
                                  f"merge_fwd_{l}")
        xs.append(y)
        x_ops.append((y_b, y_bt))
        saved.append((proj, zs, conv_out, cum, cumt, sgt, o_c, lse, mkv))

    dy, loss_part = _loss_head(xs[-1], loss_target[0], "loss_head")

    g_w_in, g_conv_a, g_conv_b, g_w_kv, g_pcat, g_wout = [], [], [], [], [], []
    small = []
    dmem_n = None
    for l in reversed(range(depth)):
        proj, zs, conv_out, cum, cumt, sgt, o_c, lse, mkv = saved[l]
        xl = xs[l]
        (dx, d_g, dza, dzb, dzc, dzm, dpcat, dwout, dlng, dlnb) = _merge_bwd(
            zs, proj, gblk, pcat[l], wout[l], xl, ln_g[l][None], ln_b[l][None], dy, alpha, f"merge_bwd_{l}")
        d_m, dmkv = _branch_m_bwd(proj, mkv, dzm, mblk, f"m_bwd_{l}")
        g_w_kv.append(_mm(mem_n, dmkv, ta=True, name=f"dwkv_{l}"))
        dmem_n = _mm(dmkv, w_kv[l], tb=True, add=dmem_n, name=f"dmem_{l}")
        do, d_cg, delta = _fox_bwd_prep(proj, dzc, o_c, 10, f"fox_bwd_prep_{l}")
        dq, dk, dv, dcumt4, dcumq4 = _fox_bwd(proj, do, lse, delta, cum, cumt, qblk, kblk, vblk, f"fox_bwd_{l}")
        d_f, dbf = _forget_bwd(dcumt4, dcumq4, sgt, f"forget_bwd_{l}")
        d_b, dconv_b = _branch_b_bwd(proj, dzb, conv_b[l], f"b_bwd_{l}")
        d_a, dconv_a, dconv_ab, dlag, dlab = _branch_a_bwd(
            proj, conv_out, dza, conv_a[l], ln_a_g[l][None], ln_a_b[l][None], f"a_bwd_{l}")
        dparts = [d_a, d_b, jnp.concatenate([dq, dk, dv], axis=1), d_cg, d_m, d_g, d_f]
        dx = _input_grad(dparts, [w_pad[l][:, lo:hi] for lo, hi in pieces], dx, f"dx_{l}")
        dw_parts = [_mm(x_ops[l][1], dp, tm=d, tk=_pick(s, (T_DW_K,)), name=f"dw_{l}_{k}")
                    for k, dp in enumerate(dparts)]
        g_w_in.append(dw_parts)
        g_conv_a.append(dconv_a)
        g_conv_b.append(dconv_b)
        g_pcat.append(dpcat)
        g_wout.append(dwout)
        small.append([dbf[:, 0], dconv_ab[0], dlag[0], dlab[0], dlng[0], dlnb[0]])
        dy = dx
    grad_x = dy
    dmlg, dmlb = _ln_rows_param_grads(mem[0], dmem_n, "mem_ln_grads")
    for lst in (g_w_in, g_conv_a, g_conv_b, g_w_kv, g_pcat, g_wout, small):
        lst.reverse()

    pa_end, pb_end, pc_end = aw, 2 * aw, 3 * aw
    rs_in = [
        jnp.stack([jnp.concatenate([own_columns(parts, j) for parts in g_w_in], axis=0) for j in range(4)]),
        _shard_cols(jnp.concatenate(g_conv_a, axis=0)),
        _shard_cols(jnp.concatenate([_pad_rows(g, 2 * HALO_B) for g in g_conv_b], axis=0)),
        jnp.concatenate([g.reshape(4, g.shape[0] // 4, g.shape[1]) for g in g_w_kv], axis=1),
        _shard_cols(jnp.concatenate([g[:pa_end] for g in g_pcat], axis=0)),
        _shard_cols(jnp.concatenate([g[pa_end:pb_end] for g in g_pcat], axis=0)),
        _shard_cols(jnp.concatenate([g[pb_end:pc_end] for g in g_pcat], axis=0)),
        _shard_cols(jnp.concatenate([g[pc_end:] for g in g_pcat], axis=0)),
        jnp.concatenate([g.reshape(4, g.shape[0] // 4, g.shape[1]) for g in g_wout], axis=1),
    ]
    rs_out = _reduce_scatter(rs_in)
    gw_in = rs_out[0].reshape(depth, d, -1)
    g_ca = rs_out[1].reshape(depth, HALO_A, -1)[:, :CONV_A]
    g_cb = rs_out[2].reshape(depth, 2 * HALO_B, -1)[:, :CONV_B]
    gw_kv = rs_out[3].reshape(depth, -1, 2 * mw)
    gp_a = rs_out[4].reshape(depth, aw, -1)
    gp_b = rs_out[5].reshape(depth, aw, -1)
    gp_c = rs_out[6].reshape(depth, aw, -1)
    gp_m = rs_out[7].reshape(depth, mw, -1)
    gw_out = rs_out[8].reshape(depth, -1, d)

    flat = jnp.concatenate([jnp.concatenate(p) for p in small] + [dmlg[0], dmlb[0], loss_part[0, 0:1]])
    n_small = flat.shape[0]
    n_rows = -(-n_small // (8 * LANES)) * 8
    vec = jnp.pad(flat, (0, n_rows * LANES - n_small)).reshape(n_rows, LANES)
    tot = _sum_leading(_allgather_all(vec, "gather_small"), F32, "sum_small").reshape(-1)
    per_layer = n_head + 3 * aw + 2 * d
    tl = tot[:depth * per_layer].reshape(depth, per_layer)
    offs = np.cumsum([0, n_head, aw, aw, aw, d, d])
    g_bf, g_cab, g_lag, g_lab, g_lg, g_lb = [tl[:, offs[k]:offs[k + 1]] for k in range(6)]
    base = depth * per_layer
    g_mlg, g_mlb = tot[base:base + d], tot[base + d:base + 2 * d]
    loss = tot[base + 2 * d]

    grads = [gw_in, g_bf, g_ca, g_cab, g_lag, g_lab, g_cb, gw_kv, g_mlg, g_mlb, gp_a, gp_b, gp_c, gp_m, gw_out, g_lg, g_lb]
    ws = [w_in, b_forget, conv_a_w, conv_a_b, ln_a_g, ln_a_b, conv_b_w, w_kv_mem, mem_ln_g, mem_ln_b, p_a, p_b, p_c, p_m, w_out, ln_g, ln_b]
    ms = [m_w_in, m_b_forget, m_conv_a_w, m_conv_a_b, m_ln_a_g, m_ln_a_b, m_conv_b_w, m_w_kv_mem, m_mem_ln_g, m_mem_ln_b, m_p_a, m_p_b, m_p_c, m_p_m, m_w_out, m_ln_g, m_ln_b]
    vs = [v_w_in, v_b_forget, v_conv_a_w, v_conv_a_b, v_ln_a_g, v_ln_a_b, v_conv_b_w, v_w_kv_mem, v_mem_ln_g, v_mem_ln_b, v_p_a, v_p_b, v_p_c, v_p_m, v_w_out, v_ln_g, v_ln_b]
    deltas, new_ms, new_vs = [], [], []
    for k, (wk, gk, mk, vk) in enumerate(zip(ws, grads, ms, vs)):
        shape = wk.shape
        as_3d = (1,) * (3 - wk.ndim) + shape
        dk_, nm_, nv_ = _adamw(wk.reshape(as_3d), gk.reshape(as_3d), mk.reshape(as_3d), vk.reshape(as_3d), f"adamw_{k}")
        deltas.append(dk_.reshape(shape))
        new_ms.append(nm_.reshape(shape))
        new_vs.append(nv_.reshape(shape))
        grads[k] = gk.reshape(shape)
    return (loss, grad_x[None], *grads, *deltas, *new_ms, *new_vs)


def _gate_mul(proj, o, gblk, name):
    s, w = o.shape
    t = T_ELEM

    def body(g_ref, o_ref, z_ref):
        z_ref[...] = (o_ref[...] * _silu(g_ref[...])).astype(BF16)

    row = pl.BlockSpec((t, w), lambda i: (i, 0))
    return pl.pallas_call(body, name=name, grid=(s // t,), in_specs=[pl.BlockSpec((t, w), lambda i: (i, gblk)), row],
                          out_specs=row, out_shape=jax.ShapeDtypeStruct((s, w), BF16),
                          compiler_params=_params(("parallel",)))(proj, o)
```
